```python
import math
import jax
import jax.numpy as jnp
from jax import lax
import numpy as np

D_MODEL = 1024
BATCH = 8
SEQ = 4096
DEPTH = 2
DEC_BATCH = 128
DEC_SEQ = 1
PAST_LEN = 16384
PAGE_SIZE = 128

D_SSM = D_MODEL // 4
SSM_GROUP = 16
SSM_GROUPS = D_SSM // SSM_GROUP
SSM_STATE = 64
HEAD_DIM = 64
N_HEADS = (D_MODEL // 2) // HEAD_DIM
N_KV_HEADS = 2
GQA = N_HEADS // N_KV_HEADS
D_ATTN = N_HEADS * HEAD_DIM
D_KV = N_KV_HEADS * HEAD_DIM
WINDOW = 128
ATTN_BLOCK = 128
ATTN_SCALE = HEAD_DIM ** -0.5
NEG_INF = -1e30
D_POOL = D_MODEL // 4
POOL_WINDOWS = (2, 4, 8, 16)
POOL_GROUPS = len(POOL_WINDOWS)
POOL_GROUP_DIM = D_POOL // POOL_GROUPS
POOL_BUF = max(POOL_WINDOWS) - 1
N_BRANCHES = 3
D_IN = D_SSM + D_ATTN + 2 * D_KV + D_POOL + N_BRANCHES * D_MODEL
SPLITS = (D_SSM, D_SSM + D_ATTN, D_SSM + D_ATTN + D_KV, D_SSM + D_ATTN + 2 * D_KV, D_SSM + D_ATTN + 2 * D_KV + D_POOL)
D_FF = 2816
N_EXPERTS = 8
TOP_K = 2
D_EXPERT = 1024
N_DENSE = (DEPTH + 1) // 2
N_MOE = DEPTH // 2
N_MOD = 6
RMS_EPS = 1e-6

kernel_name = 'hybrid_s5_swa_pool_step'


def rms_norm(x, gain):
    xf = x.astype(jnp.float32)
    xf = xf * lax.rsqrt(jnp.mean(jnp.square(xf), axis=-1, keepdims=True) + RMS_EPS)
    return (xf * gain.astype(jnp.float32)).astype(x.dtype)


def adaln(c, w_ada, b_ada):
    mod = jax.nn.silu(c) @ w_ada + b_ada
    return [m[:, None, :] for m in jnp.split(mod, N_MOD, axis=-1)]


def _complex_affine_combine(e1, e2):
    a1r, a1i, b1r, b1i = e1
    a2r, a2i, b2r, b2i = e2
    return (a2r * a1r - a2i * a1i, a2r * a1i + a2i * a1r,
            a2r * b1r - a2i * b1i + b2r, a2r * b1i + a2i * b1r + b2i)


def ssm_branch(u, h0_re, h0_im, a_re, a_im, log_dt, b_re, b_im, c_re, c_im, d_skip, w_glu, b_glu):
    bsz, t, _ = u.shape
    uf = u.astype(jnp.float32)
    ug = uf.reshape(bsz, t, SSM_GROUPS, SSM_GROUP)
    ar = a_re.astype(jnp.float32)
    ai = a_im.astype(jnp.float32)
    dt = jnp.exp(log_dt.astype(jnp.float32))[:, None]
    mag = jnp.exp(ar * dt)
    ab_re = mag * jnp.cos(ai * dt)
    ab_im = mag * jnp.sin(ai * dt)
    den = ar * ar + ai * ai
    n_re = ab_re - 1.0
    f_re = (n_re * ar + ab_im * ai) / den
    f_im = (ab_im * ar - n_re * ai) / den
    br = b_re.astype(jnp.float32)
    bi = b_im.astype(jnp.float32)
    bb_re = f_re[..., None] * br - f_im[..., None] * bi
    bb_im = f_re[..., None] * bi + f_im[..., None] * br
    bu_re = jnp.einsum('gpc,btgc->btgp', bb_re, ug)
    bu_im = jnp.einsum('gpc,btgc->btgp', bb_im, ug)
    h0r = h0_re.astype(jnp.float32)
    h0i = h0_im.astype(jnp.float32)
    bu_re = bu_re.at[:, 0].add(ab_re * h0r - ab_im * h0i)
    bu_im = bu_im.at[:, 0].add(ab_re * h0i + ab_im * h0r)
    a_sc_re = jnp.broadcast_to(ab_re, bu_re.shape)
    a_sc_im = jnp.broadcast_to(ab_im, bu_im.shape)
    _, _, h_re, h_im = lax.associative_scan(_complex_affine_combine, (a_sc_re, a_sc_im, bu_re, bu_im), axis=1)
    y = (jnp.einsum('gcp,btgp->btgc', c_re.astype(jnp.float32), h_re)
         - jnp.einsum('gcp,btgp->btgc', c_im.astype(jnp.float32), h_im))
    y = y.reshape(bsz, t, D_SSM) + d_skip.astype(jnp.float32) * uf
    z = jax.nn.gelu(y)
    out = z * jax.nn.sigmoid(z @ w_glu.astype(jnp.float32) + b_glu.astype(jnp.float32))
    return out.astype(u.dtype), h_re[:, -1], h_im[:, -1]


def _alibi_slopes():
    slopes = [2.0 ** (-8.0 * (h + 1) / N_HEADS) for h in range(N_HEADS)]
    return jnp.asarray(slopes, dtype=jnp.float32).reshape(N_KV_HEADS, GQA)


def _sink_softmax(s, sinks):
    sk = sinks.astype(jnp.float32).reshape(N_KV_HEADS, GQA)[:, :, None, None]
    m = jnp.maximum(jnp.max(s, axis=-1, keepdims=True), sk)
    p = jnp.exp(s - m)
    return p / (jnp.sum(p, axis=-1, keepdims=True) + jnp.exp(sk - m))


def swa_prompt(q, k, v, sinks):
    bsz, t = q.shape[:2]
    blk = ATTN_BLOCK
    nb = t // blk
    qb = q.reshape(bsz, nb, blk, N_KV_HEADS, GQA, HEAD_DIM)
    pad = ((0, 0), (blk, 0), (0, 0), (0, 0))
    kp = jnp.pad(k, pad).reshape(bsz, nb + 1, blk, N_KV_HEADS, HEAD_DIM)
    vp = jnp.pad(v, pad).reshape(bsz, nb + 1, blk, N_KV_HEADS, HEAD_DIM)
    kb = jnp.concatenate([kp[:, :-1], kp[:, 1:]], axis=2)
    vb = jnp.concatenate([vp[:, :-1], vp[:, 1:]], axis=2)
    s = jnp.einsum('bnqkgd,bnskd->bnkgqs', qb, kb, preferred_element_type=jnp.float32) * ATTN_SCALE
    qpos = jnp.arange(nb)[:, None] * blk + jnp.arange(blk)[None, :]
    kpos = jnp.arange(nb)[:, None] * blk - blk + jnp.arange(2 * blk)[None, :]
    dist = qpos[:, :, None] - kpos[:, None, :]
    valid = (dist >= 0) & (dist <= WINDOW) & (kpos[:, None, :] >= 0)
    slopes = _alibi_slopes()
    s = s - slopes[None, None, :, :, None, None] * dist[None, :, None, None].astype(jnp.float32)
    s = jnp.where(valid[None, :, None, None], s, NEG_INF)
    p = _sink_softmax(s, sinks)
    o = jnp.einsum('bnkgqs,bnskd->bnqkgd', p.astype(vb.dtype), vb)
    return o.reshape(bsz, t, D_ATTN)


def swa_sample(q, k, v, k_buf, v_buf, sinks):
    bsz, sq = q.shape[:2]
    w = k_buf.shape[1]
    keys = jnp.concatenate([k_buf.astype(k.dtype), k], axis=1)
    vals = jnp.concatenate([v_buf.astype(v.dtype), v], axis=1)
    qg = q.reshape(bsz, sq, N_KV_HEADS, GQA, HEAD_DIM)
    s = jnp.einsum('bqkgd,bskd->bkgqs', qg, keys, preferred_element_type=jnp.float32) * ATTN_SCALE
    dist = (w + jnp.arange(sq))[:, None] - jnp.arange(w + sq)[None, :]
    valid = (dist >= 0) & (dist <= WINDOW)
    slopes = _alibi_slopes()
    s = s - slopes[None, :, :, None, None] * dist.astype(jnp.float32)
    s = jnp.where(valid, s, NEG_INF)
    p = _sink_softmax(s, sinks)
    o = jnp.einsum('bkgqs,bskd->bqkgd', p.astype(vals.dtype), vals)
    return o.reshape(bsz, sq, D_ATTN), keys[:, -w:], vals[:, -w:]


def pool_branch(p_new, hist, pos0, pool_w, pool_scale):
    bsz, t, _ = p_new.shape
    pf = p_new.astype(jnp.float32)
    ext = jnp.concatenate([hist.astype(jnp.float32), pf], axis=1)
    cs = jnp.cumsum(jnp.pad(ext, ((0, 0), (1, 0), (0, 0))), axis=1)
    end = cs[:, POOL_BUF + 1:]
    pos = pos0 + jnp.arange(t)
    means = []
    for g, w in enumerate(POOL_WINDOWS):
        sl = slice(g * POOL_GROUP_DIM, (g + 1) * POOL_GROUP_DIM)
        start = cs[:, POOL_BUF + 1 - w: POOL_BUF + 1 - w + t, sl]
        cnt = jnp.minimum(w, pos + 1).astype(jnp.float32)[None, :, None]
        means.append((end[..., sl] - start) / cnt)
    mixed = jnp.concatenate(means, axis=-1) - pf
    y = jnp.einsum('btgc,gcd->btgd', mixed.reshape(bsz, t, POOL_GROUPS, POOL_GROUP_DIM), pool_w.astype(jnp.float32))
    y = y.reshape(bsz, t, D_POOL) * pool_scale.astype(jnp.float32)
    return y.astype(p_new.dtype), ext[:, -POOL_BUF:].astype(p_new.dtype)


def swiglu(h, w_gate, w_up, w_down):
    return (jax.nn.silu(h @ w_gate) * (h @ w_up)) @ w_down


def moe_swiglu(h, w_router, b_router, w_gate, w_up, w_down):
    bsz, t, d = h.shape
    hf = h.reshape(-1, d)
    logits = (hf @ w_router).astype(jnp.float32) + b_router.astype(jnp.float32)
    top_v, top_i = lax.top_k(logits, TOP_K)
    top_w = jax.nn.softmax(top_v, axis=-1)
    gates = jnp.sum(jax.nn.one_hot(top_i, N_EXPERTS, dtype=jnp.float32) * top_w[..., None], axis=1)
    out = jnp.zeros(hf.shape, jnp.float32)
    for e in range(N_EXPERTS):
        out = out + gates[:, e:e + 1] * swiglu(hf, w_gate[e], w_up[e], w_down[e]).astype(jnp.float32)
    return out.reshape(bsz, t, d).astype(h.dtype)


def trunk_layer(x, c, lp, is_moe, st, pos0):
    bsz, t = x.shape[:2]
    sh1, sc1, g1, sh2, sc2, g2 = adaln(c, lp['w_ada'], lp['b_ada'])
    h = rms_norm(x, lp['norm_mix_pre']) * (1.0 + sc1) + sh1
    proj = h @ lp['w_in']
    u, q, k, v, pin, gates = jnp.split(proj, SPLITS, axis=-1)
    q = q.reshape(bsz, t, N_HEADS, HEAD_DIM)
    k = k.reshape(bsz, t, N_KV_HEADS, HEAD_DIM)
    v = v.reshape(bsz, t, N_KV_HEADS, HEAD_DIM)
    if st is None:
        h0_re = jnp.zeros((bsz, SSM_GROUPS, SSM_STATE), jnp.float32)
        h0_im = jnp.zeros((bsz, SSM_GROUPS, SSM_STATE), jnp.float32)
        pool_hist = jnp.zeros((bsz, POOL_BUF, D_POOL), x.dtype)
    else:
        h0_re, h0_im, k_buf, v_buf, pool_hist = st
    y_ssm, h_re, h_im = ssm_branch(u, h0_re, h0_im, lp['ssm_a_re'], lp['ssm_a_im'], lp['ssm_log_dt'],
                                   lp['ssm_b_re'], lp['ssm_b_im'], lp['ssm_c_re'], lp['ssm_c_im'],
                                   lp['ssm_d'], lp['ssm_w_glu'], lp['ssm_b_glu'])
    if st is None:
        y_att = swa_prompt(q, k, v, lp['attn_sinks'])
        k_win, v_win = k[:, -WINDOW:], v[:, -WINDOW:]
    else:
        y_att, k_win, v_win = swa_sample(q, k, v, k_buf, v_buf, lp['attn_sinks'])
    y_pool, new_hist = pool_branch(pin, pool_hist, pos0, lp['pool_w'], lp['pool_scale'])
    gs, ga, gp = jnp.split(gates, N_BRANCHES, axis=-1)
    merged = (jax.nn.sigmoid(gs) * (y_ssm @ lp['w_up_ssm'])
              + jax.nn.sigmoid(ga) * (y_att @ lp['w_up_attn'])
              + jax.nn.sigmoid(gp) * (y_pool @ lp['w_up_pool']))
    out = merged @ lp['w_out']
    x = x + g1 * rms_norm(out, lp['norm_mix_post'])
    h = rms_norm(x, lp['norm_ffn_pre']) * (1.0 + sc2) + sh2
    if is_moe:
        f = moe_swiglu(h, lp['moe_w_router'], lp['moe_b_router'], lp['moe_w_gate'], lp['moe_w_up'], lp['moe_w_down'])
    else:
        f = swiglu(h, lp['ffn_w_gate'], lp['ffn_w_up'], lp['ffn_w_down'])
    x = x + g2 * rms_norm(f, lp['norm_ffn_post'])
    return x, (h_re, h_im, k_win, v_win, new_hist)


def setup_inputs(seed: int = 0) -> dict:
    key = jax.random.key(seed)
    ks = iter(jax.random.split(key, 64))
    nrm = lambda shape, s=1.0: jax.random.normal(next(ks), shape, jnp.float32) * s
    win_buf = min(WINDOW, PAST_LEN)
    n_idx = jnp.arange(SSM_STATE, dtype=jnp.float32)
    inp = {}
    inp['x_prompt'] = nrm((BATCH, SEQ, D_MODEL))
    inp['x_sample'] = nrm((DEC_BATCH, DEC_SEQ, D_MODEL))
    inp['state_ssm_re'] = nrm((DEPTH, DEC_BATCH, SSM_GROUPS, SSM_STATE), 0.3)
    inp['state_ssm_im'] = nrm((DEPTH, DEC_BATCH, SSM_GROUPS, SSM_STATE), 0.3)
    inp['cache_win_k'] = nrm((DEPTH, DEC_BATCH, win_buf, N_KV_HEADS, HEAD_DIM))
    inp['cache_win_v'] = nrm((DEPTH, DEC_BATCH, win_buf, N_KV_HEADS, HEAD_DIM))
    inp['state_pool'] = nrm((DEPTH, DEC_BATCH, POOL_BUF, D_POOL))
    inp['c_prompt'] = nrm((BATCH, D_MODEL))
    inp['c_sample'] = nrm((DEC_BATCH, D_MODEL))
    inp['w_ada'] = nrm((DEPTH, D_MODEL, N_MOD * D_MODEL), 0.5 * D_MODEL ** -0.5)
    inp['b_ada'] = nrm((DEPTH, N_MOD * D_MODEL), 0.02)
    inp['norm_mix_pre'] = 1.0 + nrm((DEPTH, D_MODEL), 0.05)
    inp['norm_mix_post'] = 1.0 + nrm((DEPTH, D_MODEL), 0.05)
    inp['norm_ffn_pre'] = 1.0 + nrm((DEPTH, D_MODEL), 0.05)
    inp['norm_ffn_post'] = 1.0 + nrm((DEPTH, D_MODEL), 0.05)
    inp['w_in'] = nrm((DEPTH, D_MODEL, D_IN), D_MODEL ** -0.5)
    inp['ssm_a_re'] = -0.5 + nrm((DEPTH, SSM_GROUPS, SSM_STATE), 0.01)
    inp['ssm_a_im'] = math.pi * n_idx + nrm((DEPTH, SSM_GROUPS, SSM_STATE), 0.01)
    inp['ssm_log_dt'] = jax.random.uniform(next(ks), (DEPTH, SSM_GROUPS), jnp.float32, math.log(1e-3), math.log(1e-1))
    inp['ssm_b_re'] = nrm((DEPTH, SSM_GROUPS, SSM_STATE, SSM_GROUP), (2 * SSM_GROUP) ** -0.5)
    inp['ssm_b_im'] = nrm((DEPTH, SSM_GROUPS, SSM_STATE, SSM_GROUP), (2 * SSM_GROUP) ** -0.5)
    inp['ssm_c_re'] = nrm((DEPTH, SSM_GROUPS, SSM_GROUP, SSM_STATE), SSM_STATE ** -0.5)
    inp['ssm_c_im'] = nrm((DEPTH, SSM_GROUPS, SSM_GROUP, SSM_STATE), SSM_STATE ** -0.5)
    inp['ssm_d'] = nrm((DEPTH, D_SSM))
    inp['ssm_w_glu'] = nrm((DEPTH, D_SSM, D_SSM), D_SSM ** -0.5)
    inp['ssm_b_glu'] = nrm((DEPTH, D_SSM), 0.02)
    inp['attn_sinks'] = nrm((DEPTH, N_HEADS))
    inp['pool_w'] = nrm((DEPTH, POOL_GROUPS, POOL_GROUP_DIM, POOL_GROUP_DIM), POOL_GROUP_DIM ** -0.5)
    inp['pool_scale'] = 1.0 + nrm((DEPTH, D_POOL), 0.1)
    inp['w_up_ssm'] = nrm((DEPTH, D_SSM, D_MODEL), D_SSM ** -0.5)
    inp['w_up_attn'] = nrm((DEPTH, D_ATTN, D_MODEL), D_ATTN ** -0.5)
    inp['w_up_pool'] = nrm((DEPTH, D_POOL, D_MODEL), D_POOL ** -0.5)
    inp['w_out'] = nrm((DEPTH, D_MODEL, D_MODEL), D_MODEL ** -0.5)
    inp['ffn_w_gate'] = nrm((N_DENSE, D_MODEL, D_FF), D_MODEL ** -0.5)
    inp['ffn_w_up'] = nrm((N_DENSE, D_MODEL, D_FF), D_MODEL ** -0.5)
    inp['ffn_w_down'] = nrm((N_DENSE, D_FF, D_MODEL), D_FF ** -0.5)
    inp['moe_w_router'] = nrm((N_MOE, D_MODEL, N_EXPERTS), D_MODEL ** -0.5)
    inp['moe_b_router'] = nrm((N_MOE, N_EXPERTS), 0.01)
    inp['moe_w_gate'] = nrm((N_MOE, N_EXPERTS, D_MODEL, D_EXPERT), D_MODEL ** -0.5)
    inp['moe_w_up'] = nrm((N_MOE, N_EXPERTS, D_MODEL, D_EXPERT), D_MODEL ** -0.5)
    inp['moe_w_down'] = nrm((N_MOE, N_EXPERTS, D_EXPERT, D_MODEL), D_EXPERT ** -0.5)
    return inp


def reference(x_prompt, x_sample, state_ssm_re, state_ssm_im, cache_win_k, cache_win_v, state_pool,
              c_prompt, c_sample, w_ada, b_ada, norm_mix_pre, norm_mix_post, norm_ffn_pre, norm_ffn_post,
              w_in, ssm_a_re, ssm_a_im, ssm_log_dt, ssm_b_re, ssm_b_im, ssm_c_re, ssm_c_im, ssm_d,
              ssm_w_glu, ssm_b_glu, attn_sinks, pool_w, pool_scale, w_up_ssm, w_up_attn, w_up_pool, w_out,
              ffn_w_gate, ffn_w_up, ffn_w_down, moe_w_router, moe_b_router, moe_w_gate, moe_w_up, moe_w_down):
    xp, xs = x_prompt, x_sample
    p_new = ([], [], [], [], [])
    s_new = ([], [], [], [], [])
    for l in range(DEPTH):
        is_moe = (l % 2 == 1)
        j = l // 2
        lp = {'w_ada': w_ada[l], 'b_ada': b_ada[l],
              'norm_mix_pre': norm_mix_pre[l], 'norm_mix_post': norm_mix_post[l],
              'norm_ffn_pre': norm_ffn_pre[l], 'norm_ffn_post': norm_ffn_post[l],
              'w_in': w_in[l], 'ssm_a_re': ssm_a_re[l], 'ssm_a_im': ssm_a_im[l], 'ssm_log_dt': ssm_log_dt[l],
              'ssm_b_re': ssm_b_re[l], 'ssm_b_im': ssm_b_im[l], 'ssm_c_re': ssm_c_re[l], 'ssm_c_im': ssm_c_im[l],
              'ssm_d': ssm_d[l], 'ssm_w_glu': ssm_w_glu[l], 'ssm_b_glu': ssm_b_glu[l],
              'attn_sinks': attn_sinks[l], 'pool_w': pool_w[l], 'pool_scale': pool_scale[l],
              'w_up_ssm': w_up_ssm[l], 'w_up_attn': w_up_attn[l], 'w_up_pool': w_up_pool[l], 'w_out': w_out[l]}
        if is_moe:
            lp['moe_w_router'] = moe_w_router[j]
            lp['moe_b_router'] = moe_b_router[j]
            lp['moe_w_gate'] = moe_w_gate[j]
            lp['moe_w_up'] = moe_w_up[j]
            lp['moe_w_down'] = moe_w_down[j]
        else:
            lp['ffn_w_gate'] = ffn_w_gate[j]
            lp['ffn_w_up'] = ffn_w_up[j]
            lp['ffn_w_down'] = ffn_w_down[j]
        xp, sp = trunk_layer(xp, c_prompt, lp, is_moe, None, 0)
        st = (state_ssm_re[l], state_ssm_im[l], cache_win_k[l], cache_win_v[l], state_pool[l])
        xs, ss = trunk_layer(xs, c_sample, lp, is_moe, st, PAST_LEN)
        for i in range(5):
            p_new[i].append(sp[i])
            s_new[i].append(ss[i])
    p_ssm_re, p_ssm_im, p_win_k, p_win_v, p_pool = [jnp.stack(a, axis=0) for a in p_new]
    s_ssm_re, s_ssm_im, s_win_k, s_win_v, s_pool = [jnp.stack(a, axis=0) for a in s_new]
    return (xp, xs, p_ssm_re, p_ssm_im, p_win_k, p_win_v, p_pool, s_ssm_re, s_ssm_im, s_win_k, s_win_v, s_pool)
```

```python
import functools

import jax
import jax.numpy as jnp
from jax import lax
from jax.experimental import pallas as pl
from jax.experimental.pallas import tpu as pltpu

F32 = jnp.float32
BF16 = jnp.bfloat16

D_MODEL = 1024
D_SSM = 256
SSM_GROUP = 16
SSM_GROUPS = 16
SSM_STATE = 64
D_STATE = SSM_GROUPS * SSM_STATE
HEAD_DIM = 64
N_HEADS = 8
N_KV_HEADS = 2
GQA = N_HEADS // N_KV_HEADS
D_ATTN = N_HEADS * HEAD_DIM
D_KV = N_KV_HEADS * HEAD_DIM
WINDOW = 128
ATTN_SCALE = HEAD_DIM ** -0.5
NEG_INF = -1e30
D_POOL = 256
POOL_WINDOWS = (2, 4, 8, 16)
POOL_BUF = 15
POOL_HIST = 16
N_MOD = 6
D_FF = 2816
N_EXPERTS = 8
D_EXPERT = 1024
RMS_EPS = 1e-6
PAST_LEN = 16384
D_SMALL = D_SSM + D_ATTN + 2 * D_KV + D_POOL
D_GATES = 3 * D_MODEL
SLOPES = tuple(2.0 ** (-8.0 * (h + 1) / N_HEADS) for h in range(N_HEADS))

SUBLANES = 8
LANES = 128
VMEM_LIMIT = 56 * 1024 * 1024

TT = 512
SCAN_LEN = TT // SUBLANES
Q_BLK = 128
FFN_TM = 512
MOE_TM = 1024
SAMPLE_BC = 32
ADA_TN = 1536
FF_CHUNKS = ((0, 1024), (1024, 2048), (2048, D_FF))


def _rms(x, gain):
    return x * lax.rsqrt(jnp.mean(x * x, axis=-1, keepdims=True) + RMS_EPS) * gain


def _sigmoid(x):
    return 1.0 / (1.0 + jnp.exp(-x))


def _silu(x):
    return x * _sigmoid(x)


def _dot(a, b):
    return jnp.dot(a, b, preferred_element_type=F32)


def _dot_t(a, b):
    return lax.dot_general(a, b, (((1,), (1,)), ((), ())), preferred_element_type=F32)


def _const_spec(shape):
    nd = len(shape)
    return pl.BlockSpec(shape, lambda *_: (0,) * nd)


def _params(sem):
    return pltpu.CompilerParams(dimension_semantics=sem, vmem_limit_bytes=VMEM_LIMIT)


def _ada_kernel(c_ref, w_ref, b_ref, o_ref):
    c = c_ref[...]
    s = _silu(c).astype(BF16)
    o_ref[0] = _dot(s, w_ref[0].astype(BF16)) + b_ref[0]


def _ada_call(c_all, w_ada, b_ada):
    depth = w_ada.shape[0]
    rows = c_all.shape[0]
    n = N_MOD * D_MODEL
    return pl.pallas_call(
        _ada_kernel,
        out_shape=jax.ShapeDtypeStruct((depth, rows, n), F32),
        grid=(depth, n // ADA_TN),
        in_specs=[
            pl.BlockSpec((rows, D_MODEL), lambda l, j: (0, 0)),
            pl.BlockSpec((1, D_MODEL, ADA_TN), lambda l, j: (l, 0, j)),
            pl.BlockSpec((1, 1, ADA_TN), lambda l, j: (l, 0, j)),
        ],
        out_specs=pl.BlockSpec((1, rows, ADA_TN), lambda l, j: (l, 0, j)),
        compiler_params=_params(("arbitrary", "arbitrary")),
        name="ada_mod",
    )(c_all, w_ada, b_ada.reshape(depth, 1, n))


def _ssm_prep_kernel(are_ref, aim_ref, ldt_ref, bre_ref, bim_ref,
                     abre_o, abim_o, bbre_o, bbim_o, pwre_o, pwim_o):
    ar = are_ref[0]
    ai = aim_ref[0]
    dt = jnp.exp(ldt_ref[0])
    mag = jnp.exp(ar * dt)
    ab_re = mag * jnp.cos(ai * dt)
    ab_im = mag * jnp.sin(ai * dt)
    den = ar * ar + ai * ai
    n_re = ab_re - 1.0
    f_re = (n_re * ar + ab_im * ai) / den
    f_im = (ab_im * ar - n_re * ai) / den
    br = bre_ref[0]
    bi = bim_ref[0]
    bbre_o[0] = f_re * br - f_im * bi
    bbim_o[0] = f_re * bi + f_im * br
    abre_o[0] = ab_re
    abim_o[0] = ab_im
    cr, ci = ab_re, ab_im
    for t in range(SCAN_LEN):
        pwre_o[0, t] = cr
        pwim_o[0, t] = ci
        cr, ci = cr * ab_re - ci * ab_im, cr * ab_im + ci * ab_re


def _ssm_prep_call(a_re, a_im, log_dt, b_re, b_im):
    depth = a_re.shape[0]
    g, p, c = SSM_GROUPS, SSM_STATE, SSM_GROUP
    a4 = lambda a: a.reshape(depth, g, 1, p)
    spec_a = pl.BlockSpec((1, g, 1, p), lambda l: (l, 0, 0, 0))
    spec_b = pl.BlockSpec((1, g, c, p), lambda l: (l, 0, 0, 0))
    spec_pw = pl.BlockSpec((1, SCAN_LEN, g, 1, p), lambda l: (l, 0, 0, 0, 0))
    return pl.pallas_call(
        _ssm_prep_kernel,
        out_shape=(jax.ShapeDtypeStruct((depth, g, 1, p), F32),) * 2
        + (jax.ShapeDtypeStruct((depth, g, c, p), F32),) * 2
        + (jax.ShapeDtypeStruct((depth, SCAN_LEN, g, 1, p), F32),) * 2,
        grid=(depth,),
        in_specs=[spec_a, spec_a, pl.BlockSpec((1, g, 1, 1), lambda l: (l, 0, 0, 0)), spec_b, spec_b],
        out_specs=(spec_a, spec_a, spec_b, spec_b, spec_pw, spec_pw),
        compiler_params=_params(("arbitrary",)),
        name="ssm_prep",
    )(a4(a_re), a4(a_im), log_dt.reshape(depth, g, 1, 1),
      jnp.swapaxes(b_re, 2, 3), jnp.swapaxes(b_im, 2, 3))


def _ssm_out(y, u, d_skip, w_glu, b_glu):
    z = jax.nn.gelu(y + d_skip * u)
    return z * _sigmoid(_dot(z.astype(BF16), w_glu) + b_glu)


def _lane_lo(shape):
    return (lax.broadcasted_iota(jnp.int32, shape, len(shape) - 1) % LANES) < HEAD_DIM


def _mixer_kernel(x_ref, sh_ref, sc_ref, g_ref, npre_ref, npost_ref, wa_ref, wg_ref,
                  wbu_ref, abre_ref, abim_ref, pwre_ref, pwim_ref, cm_ref, dsk_ref, wglu_ref, bglu_ref,
                  sinks_ref, wpool_ref, pscale_ref, wus_ref, wua_ref, wup_ref, wout_ref,
                  xo_ref, hre_o, him_o, kwin_o, vwin_o, plast_o,
                  hb_s, proj_s, bu_s, unat_s, uperm_s, operm_s, yssm_s, yatt_s, kbuf, vbuf, pbuf, hc_s):
    t = pl.program_id(1)

    @pl.when(t == 0)
    def _():
        hc_s[...] = jnp.zeros_like(hc_s)
        kbuf[0:WINDOW, :] = jnp.zeros((WINDOW, D_KV), BF16)
        vbuf[0:WINDOW, :] = jnp.zeros((WINDOW, D_KV), BF16)
        pbuf[0:POOL_HIST, :] = jnp.zeros((POOL_HIST, D_POOL), F32)

    x = x_ref[0]
    h = _rms(x, npre_ref[...]) * (1.0 + sc_ref[0]) + sh_ref[0]
    hb_s[...] = h.astype(BF16)
    proj_s[...] = _dot(hb_s[...], wa_ref[...])

    for c in range(D_SSM // LANES):
        unat_s[c] = proj_s[:, c * LANES:(c + 1) * LANES]
        for i in range(SCAN_LEN):
            uperm_s[i * SUBLANES:(i + 1) * SUBLANES, c * LANES:(c + 1) * LANES] = (
                unat_s.at[c][pl.ds(i, SUBLANES, stride=SCAN_LEN), :])
    bu_s[...] = _dot(uperm_s[...].astype(BF16), wbu_ref[...])

    abr = jnp.broadcast_to(abre_ref[...], (SUBLANES, D_STATE))
    abi = jnp.broadcast_to(abim_ref[...], (SUBLANES, D_STATE))

    def local_step(i, carry):
        hr, hi = carry
        r0 = pl.multiple_of(i * SUBLANES, SUBLANES)
        nr = abr * hr - abi * hi + bu_s[pl.ds(r0, SUBLANES), 0:D_STATE]
        ni = abr * hi + abi * hr + bu_s[pl.ds(r0, SUBLANES), D_STATE:2 * D_STATE]
        bu_s[pl.ds(r0, SUBLANES), 0:D_STATE] = nr
        bu_s[pl.ds(r0, SUBLANES), D_STATE:2 * D_STATE] = ni
        return nr, ni

    zero = jnp.zeros((SUBLANES, D_STATE), F32)
    er, ei = lax.fori_loop(0, SCAN_LEN, local_step, (zero, zero), unroll=2)

    alr = pwre_ref[SCAN_LEN - 1:SCAN_LEN, :]
    ali = pwim_ref[SCAN_LEN - 1:SCAN_LEN, :]
    cr = hc_s[0:1, :]
    ci = hc_s[1:2, :]
    row = lax.broadcasted_iota(jnp.int32, (SUBLANES, D_STATE), 0)
    hin_r = zero
    hin_i = zero
    for j in range(SUBLANES):
        hin_r = jnp.where(row == j, cr, hin_r)
        hin_i = jnp.where(row == j, ci, hin_i)
        cr, ci = (alr * cr - ali * ci + er[j:j + 1, :], alr * ci + ali * cr + ei[j:j + 1, :])
    hc_s[0:1, :] = cr
    hc_s[1:2, :] = ci
    hre_o[0] = cr
    him_o[0] = ci

    def fix_step(i, _):
        r0 = pl.multiple_of(i * SUBLANES, SUBLANES)
        pr = pwre_ref[pl.ds(i, 1), :]
        pi = pwim_ref[pl.ds(i, 1), :]
        bu_s[pl.ds(r0, SUBLANES), 0:D_STATE] += pr * hin_r - pi * hin_i
        bu_s[pl.ds(r0, SUBLANES), D_STATE:2 * D_STATE] += pr * hin_i + pi * hin_r
        return 0

    lax.fori_loop(0, SCAN_LEN, fix_step, 0, unroll=2)

    y = _dot(bu_s[...].astype(BF16), cm_ref[...])
    operm_s[...] = _ssm_out(y, uperm_s[...], dsk_ref[...], wglu_ref[...], bglu_ref[...])
    for c in range(D_SSM // LANES):
        for i in range(SCAN_LEN):
            yssm_s.at[c][pl.ds(i, SUBLANES, stride=SCAN_LEN), :] = (
                operm_s[i * SUBLANES:(i + 1) * SUBLANES, c * LANES:(c + 1) * LANES])

    kbuf[WINDOW:WINDOW + TT, :] = proj_s[:, D_SSM + D_ATTN:D_SSM + D_ATTN + D_KV].astype(BF16)
    vbuf[WINDOW:WINDOW + TT, :] = proj_s[:, D_SSM + D_ATTN + D_KV:D_SSM + D_ATTN + 2 * D_KV].astype(BF16)
    lo = _lane_lo((Q_BLK, LANES))
    r_i = lax.broadcasted_iota(jnp.int32, (Q_BLK, 2 * Q_BLK), 0)
    c_i = lax.broadcasted_iota(jnp.int32, (Q_BLK, 2 * Q_BLK), 1)
    dist = r_i - c_i + Q_BLK
    valid = (dist >= 0) & (dist <= WINDOW)
    distf = dist.astype(F32)
    first_key = jnp.where(t == 0, Q_BLK, 0)
    for blk in range(TT // Q_BLK):
        r0 = blk * Q_BLK
        kk = kbuf[r0:r0 + 2 * Q_BLK, :]
        vv = vbuf[r0:r0 + 2 * Q_BLK, :]
        pieces = []
        for hh in range(N_HEADS):
            j = hh % GQA
            qp = proj_s[r0:r0 + Q_BLK, D_SSM + j * LANES:D_SSM + (j + 1) * LANES] * ATTN_SCALE
            keep = lo if hh < GQA else jnp.logical_not(lo)
            pieces.append(jnp.where(keep, qp, 0.0).astype(BF16))
        s_all = _dot_t(jnp.concatenate(pieces, axis=0), kk)
        vmask = valid & (c_i >= first_key) if blk == 0 else valid
        outs = []
        for hh in range(N_HEADS):
            s = s_all[hh * Q_BLK:(hh + 1) * Q_BLK, :] - SLOPES[hh] * distf
            s = jnp.where(vmask, s, NEG_INF)
            snk = sinks_ref[hh]
            m = jnp.maximum(jnp.max(s, axis=-1, keepdims=True), snk)
            p = jnp.exp(s - m)
            den = jnp.sum(p, axis=-1, keepdims=True) + jnp.exp(snk - m)
            outs.append(_dot(p.astype(BF16), vv) / den)
        for j in range(GQA):
            yatt_s[r0:r0 + Q_BLK, j * LANES:(j + 1) * LANES] = jnp.where(lo, outs[j], outs[GQA + j]).astype(BF16)
    kwin_o[0] = proj_s[TT - WINDOW:TT, D_SSM + D_ATTN:D_SSM + D_ATTN + D_KV]
    vwin_o[0] = proj_s[TT - WINDOW:TT, D_SSM + D_ATTN + D_KV:D_SSM + D_ATTN + 2 * D_KV]
    kbuf[0:WINDOW, :] = kbuf[TT:TT + WINDOW, :]
    vbuf[0:WINDOW, :] = vbuf[TT:TT + WINDOW, :]

    p0 = D_SMALL - D_POOL
    pbuf[POOL_HIST:POOL_HIST + TT, :] = proj_s[:, p0:D_SMALL]
    plast_o[0] = proj_s[TT - POOL_HIST:TT, p0:D_SMALL]
    pos1 = (t * TT + 1 + lax.broadcasted_iota(jnp.int32, (TT, LANES), 0)).astype(F32)
    lo_t = _lane_lo((TT, LANES))
    mixed = []
    for col, (w_lo, w_hi) in enumerate(((POOL_WINDOWS[0], POOL_WINDOWS[1]), (POOL_WINDOWS[2], POOL_WINDOWS[3]))):
        cs = slice(col * LANES, (col + 1) * LANES)
        pf = pbuf[POOL_HIST:POOL_HIST + TT, cs]
        acc = pf
        for k in range(1, w_lo):
            acc = acc + pbuf[POOL_HIST - k:POOL_HIST - k + TT, cs]
        acc_lo = acc
        for k in range(w_lo, w_hi):
            acc = acc + pbuf[POOL_HIST - k:POOL_HIST - k + TT, cs]
        cnt = jnp.where(lo_t, jnp.minimum(float(w_lo), pos1), jnp.minimum(float(w_hi), pos1))
        mixed.append(jnp.where(lo_t, acc_lo, acc) / cnt - pf)
    ypool = _dot(jnp.concatenate(mixed, axis=1).astype(BF16), wpool_ref[...]) * pscale_ref[...]
    pbuf[0:POOL_HIST, :] = pbuf[TT:TT + POOL_HIST, :]

    hb = hb_s[...]
    yssm = jnp.concatenate([yssm_s[c] for c in range(D_SSM // LANES)], axis=1)
    merged = _sigmoid(_dot(hb, wg_ref[:, 0:D_MODEL])) * _dot(yssm.astype(BF16), wus_ref[...])
    merged += _sigmoid(_dot(hb, wg_ref[:, D_MODEL:2 * D_MODEL])) * _dot(yatt_s[...], wua_ref[...])
    merged += _sigmoid(_dot(hb, wg_ref[:, 2 * D_MODEL:3 * D_MODEL])) * _dot(ypool.astype(BF16), wup_ref[...])
    out = _dot(merged.astype(BF16), wout_ref[...])
    xo_ref[0] = x_ref[0] + g_ref[0] * _rms(out, npost_ref[...])


def _mixer_call(x, mods3, lp):
    b, t_len, d = x.shape
    nt = t_len // TT
    nsamp = mods3.shape[0] - b
    mod_spec = lambda col: pl.BlockSpec((1, 1, D_MODEL), lambda i, j: (nsamp + i, 0, col))
    x_spec = pl.BlockSpec((1, TT, d), lambda i, j: (i, j, 0))
    consts = [lp['norm_mix_pre'], lp['norm_mix_post'], lp['w_in_a'], lp['w_in_g'],
              lp['w_bu'], lp['ab_re'], lp['ab_im'], lp['pw_re'], lp['pw_im'], lp['c_mat'],
              lp['ssm_d'], lp['ssm_w_glu'], lp['ssm_b_glu']]
    consts2 = [lp['pool_wbd'], lp['pool_scale'], lp['w_up_ssm'], lp['w_up_attn'], lp['w_up_pool'], lp['w_out']]
    in_specs = ([x_spec, mod_spec(0), mod_spec(1), mod_spec(2)]
                + [_const_spec(a.shape) for a in consts]
                + [pl.BlockSpec(memory_space=pltpu.SMEM)]
                + [_const_spec(a.shape) for a in consts2])
    per_b = lambda r, c: pl.BlockSpec((1, r, c), lambda i, j: (i, 0, 0))
    out_shape = (jax.ShapeDtypeStruct((b, t_len, d), F32),
                 jax.ShapeDtypeStruct((b, 1, D_STATE), F32), jax.ShapeDtypeStruct((b, 1, D_STATE), F32),
                 jax.ShapeDtypeStruct((b, WINDOW, D_KV), F32), jax.ShapeDtypeStruct((b, WINDOW, D_KV), F32),
                 jax.ShapeDtypeStruct((b, POOL_HIST, D_POOL), F32))
    out_specs = (x_spec, per_b(1, D_STATE), per_b(1, D_STATE), per_b(WINDOW, D_KV), per_b(WINDOW, D_KV),
                 per_b(POOL_HIST, D_POOL))
    scratch = [pltpu.VMEM((TT, d), BF16), pltpu.VMEM((TT, D_SMALL), F32), pltpu.VMEM((TT, 2 * D_STATE), F32),
               pltpu.VMEM((D_SSM // LANES, TT, LANES), F32), pltpu.VMEM((TT, D_SSM), F32),
               pltpu.VMEM((TT, D_SSM), F32), pltpu.VMEM((D_SSM // LANES, TT, LANES), F32),
               pltpu.VMEM((TT, D_ATTN), BF16),
               pltpu.VMEM((WINDOW + TT, D_KV), BF16), pltpu.VMEM((WINDOW + TT, D_KV), BF16),
               pltpu.VMEM((POOL_HIST + TT, D_POOL), F32), pltpu.VMEM((SUBLANES, D_STATE), F32)]
    return pl.pallas_call(
        _mixer_kernel,
        out_shape=out_shape,
        grid=(b, nt),
        in_specs=in_specs,
        out_specs=out_specs,
        scratch_shapes=scratch,
        compiler_params=_params(("arbitrary", "arbitrary")),
        name="prompt_mixer",
    )(x, mods3, mods3, mods3, *consts, lp['attn_sinks'], *consts2)


def _mod_rows(ref, per_row):
    return ref[...] if per_row else ref[0]


def _ffn_kernel(per_row, x_ref, sh_ref, sc_ref, g_ref, npre_ref, npost_ref, wg_ref, wu_ref, wd_ref, o_ref):
    x = x_ref[...]
    h = (_rms(x, npre_ref[...]) * (1.0 + _mod_rows(sc_ref, per_row)) + _mod_rows(sh_ref, per_row)).astype(BF16)
    f = None
    for c0, c1 in FF_CHUNKS:
        act = _silu(_dot(h, wg_ref[:, c0:c1])) * _dot(h, wu_ref[:, c0:c1])
        part = _dot(act.astype(BF16), wd_ref[c0:c1, :])
        f = part if f is None else f + part
    o_ref[...] = x + _mod_rows(g_ref, per_row) * _rms(f, npost_ref[...])


def _mod_specs(mods3, mods2, n_rows, tm, rows_per_seq, per_row):
    nsamp = mods2.shape[0] - (n_rows // rows_per_seq if not per_row else 0)
    if per_row:
        specs = [pl.BlockSpec((tm, D_MODEL), functools.partial(lambda col, i, *_: (i, col), col)) for col in (3, 4, 5)]
        return specs, [mods2] * 3
    tiles_per_seq = rows_per_seq // tm
    specs = [pl.BlockSpec((1, 1, D_MODEL),
                          functools.partial(lambda col, i, *_: (nsamp + i // tiles_per_seq, 0, col), col))
             for col in (3, 4, 5)]
    return specs, [mods3] * 3


def _ffn_call(x2, mods3, mods2, lp, rows_per_seq, per_row):
    n = x2.shape[0]
    tm = n if per_row else FFN_TM
    mspecs, mops = _mod_specs(mods3, mods2, n, tm, rows_per_seq, per_row)
    consts = [lp['norm_ffn_pre'], lp['norm_ffn_post'], lp['ffn_w_gate'], lp['ffn_w_up'], lp['ffn_w_down']]
    x_spec = pl.BlockSpec((tm, D_MODEL), lambda i: (i, 0))
    return pl.pallas_call(
        functools.partial(_ffn_kernel, per_row),
        out_shape=jax.ShapeDtypeStruct(x2.shape, F32),
        grid=(n // tm,),
        in_specs=[x_spec] + mspecs + [_const_spec(a.shape) for a in consts],
        out_specs=x_spec,
        compiler_params=_params(("arbitrary",)),
        name="ffn_rows" if per_row else "ffn_seq",
    )(x2, *mops, *consts)


def _moe_kernel(per_row, x_ref, sh_ref, sc_ref, g_ref, npre_ref, npost_ref, wr_ref, br_ref,
                wg_ref, wu_ref, wd_ref, o_ref, h_s, gates_s, acc_s):
    e = pl.program_id(1)
    lane_e = lax.broadcasted_iota(jnp.int32, gates_s.shape, 1)

    @pl.when(e == 0)
    def _():
        x = x_ref[...]
        h = _rms(x, npre_ref[...]) * (1.0 + _mod_rows(sc_ref, per_row)) + _mod_rows(sh_ref, per_row)
        hb = h.astype(BF16)
        h_s[...] = hb
        logits = _dot(hb, wr_ref[...]) + br_ref[...]
        lane_f = lane_e.astype(F32)
        v1 = jnp.max(logits, axis=-1, keepdims=True)
        i1 = jnp.min(jnp.where(logits == v1, lane_f, float(N_EXPERTS)), axis=-1, keepdims=True)
        rest = jnp.where(lane_f == i1, -jnp.inf, logits)
        v2 = jnp.max(rest, axis=-1, keepdims=True)
        i2 = jnp.min(jnp.where(rest == v2, lane_f, float(N_EXPERTS)), axis=-1, keepdims=True)
        e2 = jnp.exp(v2 - v1)
        den = 1.0 + e2
        gates_s[...] = jnp.where(lane_f == i1, 1.0 / den, 0.0) + jnp.where(lane_f == i2, e2 / den, 0.0)
        acc_s[...] = jnp.zeros_like(acc_s)

    h = h_s[...]
    act = _silu(_dot(h, wg_ref[0, 0])) * _dot(h, wu_ref[0, 0])
    y = _dot(act.astype(BF16), wd_ref[0, 0])
    ge = jnp.sum(jnp.where(lane_e == e, gates_s[...], 0.0), axis=-1, keepdims=True)
    acc_s[...] += ge * y

    @pl.when(e == N_EXPERTS - 1)
    def _():
        o_ref[...] = x_ref[...] + _mod_rows(g_ref, per_row) * _rms(acc_s[...], npost_ref[...])


def _moe_call(x2, mods3, mods2, lp, rows_per_seq, per_row):
    n = x2.shape[0]
    tm = n if per_row else MOE_TM
    mspecs, mops = _mod_specs(mods3, mods2, n, tm, rows_per_seq, per_row)
    consts = [lp['norm_ffn_pre'], lp['norm_ffn_post'], lp['moe_w_router'], lp['moe_b_router']]
    x_spec = pl.BlockSpec((tm, D_MODEL), lambda i, e: (i, 0))
    w_spec = lambda a: pl.BlockSpec((1, 1) + a.shape[2:], lambda i, e: (0, e, 0, 0))
    experts = [lp['moe_w_gate'], lp['moe_w_up'], lp['moe_w_down']]
    return pl.pallas_call(
        functools.partial(_moe_kernel, per_row),
        out_shape=jax.ShapeDtypeStruct(x2.shape, F32),
        grid=(n // tm, N_EXPERTS),
        in_specs=[x_spec] + mspecs + [_const_spec(a.shape) for a in consts] + [w_spec(a) for a in experts],
        out_specs=x_spec,
        scratch_shapes=[pltpu.VMEM((tm, D_MODEL), BF16), pltpu.VMEM((tm, N_EXPERTS), F32),
                        pltpu.VMEM((tm, D_MODEL), F32)],
        compiler_params=_params(("arbitrary", "arbitrary")),
        name="moe_rows" if per_row else "moe_seq",
    )(x2, *mops, *consts, *experts)


def _sproj_kernel(x_ref, sh_ref, sc_ref, npre_ref, wa_ref, wg_ref, pa_o, pg_o):
    h = (_rms(x_ref[...], npre_ref[...]) * (1.0 + sc_ref[...]) + sh_ref[...]).astype(BF16)
    pa_o[...] = _dot(h, wa_ref[...])
    pg_o[...] = _dot(h, wg_ref[...])


def _sproj_call(xs, mods2, lp):
    n = xs.shape[0]
    mod_spec = lambda col: pl.BlockSpec((n, D_MODEL), lambda i: (0, col))
    consts = [lp['norm_mix_pre'], lp['w_in_a'], lp['w_in_g']]
    return pl.pallas_call(
        _sproj_kernel,
        out_shape=(jax.ShapeDtypeStruct((n, D_SMALL), F32), jax.ShapeDtypeStruct((n, D_GATES), F32)),
        grid=(1,),
        in_specs=[_const_spec(xs.shape), mod_spec(0), mod_spec(1)] + [_const_spec(a.shape) for a in consts],
        out_specs=(_const_spec((n, D_SMALL)), _const_spec((n, D_GATES))),
        compiler_params=_params(("arbitrary",)),
        name="sample_proj",
    )(xs, mods2, mods2, *consts)


def _smix_kernel(u_ref, q_ref, kn_ref, vn_ref, pin_ref, h0r_ref, h0i_ref, ck_ref, cv_ref, hist_ref,
                 wbu_ref, abre_ref, abim_ref, cm_ref, dsk_ref, wglu_ref, bglu_ref, sinks_ref,
                 wpool_ref, pscale_ref, yssm_o, yatt_o, ypool_o, hr_o, hi_o):
    u = u_ref[...]
    bu = _dot(u.astype(BF16), wbu_ref[...])
    abr, abi = abre_ref[...], abim_ref[...]
    h0r, h0i = h0r_ref[...], h0i_ref[...]
    hr = bu[:, 0:D_STATE] + (abr * h0r - abi * h0i)
    hi = bu[:, D_STATE:2 * D_STATE] + (abr * h0i + abi * h0r)
    hr_o[...] = hr
    hi_o[...] = hi
    y = _dot(jnp.concatenate([hr, hi], axis=1).astype(BF16), cm_ref[...])
    yssm_o[...] = _ssm_out(y, u, dsk_ref[...], wglu_ref[...], bglu_ref[...])

    q = q_ref[...] * ATTN_SCALE
    kn, vn = kn_ref[...], vn_ref[...]
    kc = ck_ref[...].astype(BF16)
    vc = cv_ref[...].astype(BF16)
    lo = _lane_lo(q.shape)
    j_i = lax.broadcasted_iota(jnp.int32, (1, GQA, 1), 1)
    dist_c = (WINDOW - lax.broadcasted_iota(jnp.int32, (1, 1, WINDOW), 2)).astype(F32)
    halves = []
    for half in range(N_KV_HEADS):
        keep = lo if half == 0 else jnp.logical_not(lo)
        qm = jnp.where(keep, q, 0.0)
        slope = jnp.zeros((1, GQA, 1), F32)
        snk = jnp.zeros((1, GQA, 1), F32)
        for j in range(GQA):
            slope = jnp.where(j_i == j, SLOPES[half * GQA + j], slope)
            snk = jnp.where(j_i == j, sinks_ref[half * GQA + j], snk)
        s = jnp.einsum('bhd,bsd->bhs', qm.astype(BF16), kc, preferred_element_type=F32) - slope * dist_c
        s_new = jnp.sum(qm * kn, axis=-1, keepdims=True)
        m = jnp.maximum(jnp.maximum(jnp.max(s, axis=-1, keepdims=True), s_new), snk)
        p = jnp.exp(s - m)
        p_new = jnp.exp(s_new - m)
        den = jnp.sum(p, axis=-1, keepdims=True) + p_new + jnp.exp(snk - m)
        o = jnp.einsum('bhs,bsd->bhd', p.astype(BF16), vc, preferred_element_type=F32) + p_new * vn
        halves.append(o / den)
    yatt_o[...] = jnp.where(lo, halves[0], halves[1])

    pin = pin_ref[...]
    lo2 = _lane_lo((pin.shape[0], LANES))
    mixed = []
    for col, (w_lo, w_hi) in enumerate(((POOL_WINDOWS[0], POOL_WINDOWS[1]), (POOL_WINDOWS[2], POOL_WINDOWS[3]))):
        cs = slice(col * LANES, (col + 1) * LANES)
        pf = pin[:, cs]
        acc = pf
        for k in range(1, w_lo):
            acc = acc + hist_ref[POOL_BUF - k, :, cs]
        acc_lo = acc
        for k in range(w_lo, w_hi):
            acc = acc + hist_ref[POOL_BUF - k, :, cs]
        cnt_lo = float(min(w_lo, PAST_LEN + 1))
        cnt_hi = float(min(w_hi, PAST_LEN + 1))
        mixed.append(jnp.where(lo2, acc_lo / cnt_lo, acc / cnt_hi) - pf)
    ypool_o[...] = _dot(jnp.concatenate(mixed, axis=1).astype(BF16), wpool_ref[...]) * pscale_ref[...]


def _smix_call(u, q3, kn3, vn3, pin, h0r, h0i, ck, cv, hist_t, lp):
    n = u.shape[0]
    bc = SAMPLE_BC
    rows2 = lambda c: pl.BlockSpec((bc, c), lambda i: (i, 0))
    rows3 = lambda r, c: pl.BlockSpec((bc, r, c), lambda i: (i, 0, 0))
    consts = [lp['w_bu'], lp['ab_re'], lp['ab_im'], lp['c_mat'], lp['ssm_d'], lp['ssm_w_glu'], lp['ssm_b_glu']]
    consts2 = [lp['pool_wbd'], lp['pool_scale']]
    in_specs = ([rows2(D_SSM), rows3(GQA, LANES), rows3(1, LANES), rows3(1, LANES), rows2(D_POOL),
                 rows2(D_STATE), rows2(D_STATE), rows3(WINDOW, D_KV), rows3(WINDOW, D_KV),
                 pl.BlockSpec((POOL_BUF, bc, D_POOL), lambda i: (0, i, 0))]
                + [_const_spec(a.shape) for a in consts]
                + [pl.BlockSpec(memory_space=pltpu.SMEM)]
                + [_const_spec(a.shape) for a in consts2])
    return pl.pallas_call(
        _smix_kernel,
        out_shape=(jax.ShapeDtypeStruct((n, D_SSM), F32), jax.ShapeDtypeStruct((n, GQA, LANES), F32),
                   jax.ShapeDtypeStruct((n, D_POOL), F32),
                   jax.ShapeDtypeStruct((n, D_STATE), F32), jax.ShapeDtypeStruct((n, D_STATE), F32)),
        grid=(n // bc,),
        in_specs=in_specs,
        out_specs=(rows2(D_SSM), rows3(GQA, LANES), rows2(D_POOL), rows2(D_STATE), rows2(D_STATE)),
        compiler_params=_params(("arbitrary",)),
        name="sample_mix",
    )(u, q3, kn3, vn3, pin, h0r, h0i, ck, cv, hist_t, *consts, lp['attn_sinks'], *consts2)


def _smerge_kernel(x_ref, g_ref, pg_ref, yssm_ref, yatt_ref, ypool_ref, npost_ref,
                   wus_ref, wua_ref, wup_ref, wout_ref, o_ref):
    merged = _sigmoid(pg_ref[:, 0:D_MODEL]) * _dot(yssm_ref[...].astype(BF16), wus_ref[...])
    merged += _sigmoid(pg_ref[:, D_MODEL:2 * D_MODEL]) * _dot(yatt_ref[...].astype(BF16), wua_ref[...])
    merged += _sigmoid(pg_ref[:, 2 * D_MODEL:3 * D_MODEL]) * _dot(ypool_ref[...].astype(BF16), wup_ref[...])
    out = _dot(merged.astype(BF16), wout_ref[...])
    o_ref[...] = x_ref[...] + g_ref[...] * _rms(out, npost_ref[...])


def _smerge_call(xs, mods2, pg, yssm, yatt, ypool, lp):
    n = xs.shape[0]
    ops = [xs, mods2, pg, yssm, yatt, ypool, lp['norm_mix_post'],
           lp['w_up_ssm'], lp['w_up_attn'], lp['w_up_pool'], lp['w_out']]
    in_specs = [_const_spec(a.shape) for a in ops]
    in_specs[1] = pl.BlockSpec((n, D_MODEL), lambda i: (0, 2))
    return pl.pallas_call(
        _smerge_kernel,
        out_shape=jax.ShapeDtypeStruct(xs.shape, F32),
        grid=(1,),
        in_specs=in_specs,
        out_specs=_const_spec(xs.shape),
        compiler_params=_params(("arbitrary",)),
        name="sample_merge",
    )(*ops)


def _block_diag(w):
    g, a, b = w.shape
    eye = jnp.eye(g, dtype=w.dtype)
    return jnp.einsum('gab,gh->gahb', w, eye).reshape(g * a, g * b)


def _pair_perm():
    idx = []
    for j in range(GQA):
        idx += list(range(j * HEAD_DIM, (j + 1) * HEAD_DIM))
        idx += list(range((GQA + j) * HEAD_DIM, (GQA + j + 1) * HEAD_DIM))
    return jnp.asarray(idx, dtype=jnp.int32)


def _layer_params(l, p, prep):
    ab_re, ab_im, bb_re, bb_im, pw_re, pw_im = prep
    perm = _pair_perm()
    row = lambda a: a.reshape(1, -1)
    w_in = p['w_in'][l]
    q_cols = w_in[:, D_SSM:D_SSM + D_ATTN][:, perm]
    w_in_a = jnp.concatenate([w_in[:, 0:D_SSM], q_cols, w_in[:, D_SSM + D_ATTN:D_SMALL]], axis=1)
    lp = {
        'norm_mix_pre': row(p['norm_mix_pre'][l]), 'norm_mix_post': row(p['norm_mix_post'][l]),
        'norm_ffn_pre': row(p['norm_ffn_pre'][l]), 'norm_ffn_post': row(p['norm_ffn_post'][l]),
        'w_in_a': w_in_a.astype(BF16), 'w_in_g': w_in[:, D_SMALL:].astype(BF16),
        'w_bu': jnp.concatenate([_block_diag(bb_re[l]), _block_diag(bb_im[l])], axis=1).astype(BF16),
        'ab_re': ab_re[l].reshape(1, D_STATE), 'ab_im': ab_im[l].reshape(1, D_STATE),
        'pw_re': pw_re[l].reshape(SCAN_LEN, D_STATE), 'pw_im': pw_im[l].reshape(SCAN_LEN, D_STATE),
        'c_mat': jnp.concatenate([_block_diag(jnp.swapaxes(p['ssm_c_re'][l], 1, 2)),
                                  _block_diag(jnp.swapaxes(-p['ssm_c_im'][l], 1, 2))], axis=0).astype(BF16),
        'ssm_d': row(p['ssm_d'][l]), 'ssm_w_glu': p['ssm_w_glu'][l].astype(BF16), 'ssm_b_glu': row(p['ssm_b_glu'][l]),
        'attn_sinks': p['attn_sinks'][l],
        'pool_wbd': _block_diag(p['pool_w'][l]).astype(BF16), 'pool_scale': row(p['pool_scale'][l]),
        'w_up_ssm': p['w_up_ssm'][l].astype(BF16), 'w_up_attn': p['w_up_attn'][l][perm, :].astype(BF16),
        'w_up_pool': p['w_up_pool'][l].astype(BF16), 'w_out': p['w_out'][l].astype(BF16),
    }
    j = l // 2
    if l % 2 == 1:
        lp.update({'moe_w_router': p['moe_w_router'][j].astype(BF16), 'moe_b_router': row(p['moe_b_router'][j]),
                   'moe_w_gate': p['moe_w_gate'][j:j + 1].astype(BF16), 'moe_w_up': p['moe_w_up'][j:j + 1].astype(BF16),
                   'moe_w_down': p['moe_w_down'][j:j + 1].astype(BF16)})
    else:
        lp.update({'ffn_w_gate': p['ffn_w_gate'][j].astype(BF16), 'ffn_w_up': p['ffn_w_up'][j].astype(BF16),
                   'ffn_w_down': p['ffn_w_down'][j].astype(BF16)})
    return lp


def kernel(x_prompt, x_sample, state_ssm_re, state_ssm_im, cache_win_k, cache_win_v, state_pool,
           c_prompt, c_sample, w_ada, b_ada, norm_mix_pre, norm_mix_post, norm_ffn_pre, norm_ffn_post,
           w_in, ssm_a_re, ssm_a_im, ssm_log_dt, ssm_b_re, ssm_b_im, ssm_c_re, ssm_c_im, ssm_d,
           ssm_w_glu, ssm_b_glu, attn_sinks, pool_w, pool_scale, w_up_ssm, w_up_attn, w_up_pool, w_out,
           ffn_w_gate, ffn_w_up, ffn_w_down, moe_w_router, moe_b_router, moe_w_gate, moe_w_up, moe_w_down):
    p = dict(norm_mix_pre=norm_mix_pre, norm_mix_post=norm_mix_post, norm_ffn_pre=norm_ffn_pre,
             norm_ffn_post=norm_ffn_post, w_in=w_in, ssm_c_re=ssm_c_re, ssm_c_im=ssm_c_im, ssm_d=ssm_d,
             ssm_w_glu=ssm_w_glu, ssm_b_glu=ssm_b_glu, attn_sinks=attn_sinks, pool_w=pool_w, pool_scale=pool_scale,
             w_up_ssm=w_up_ssm, w_up_attn=w_up_attn, w_up_pool=w_up_pool, w_out=w_out,
             ffn_w_gate=ffn_w_gate, ffn_w_up=ffn_w_up, ffn_w_down=ffn_w_down, moe_w_router=moe_w_router,
             moe_b_router=moe_b_router, moe_w_gate=moe_w_gate, moe_w_up=moe_w_up, moe_w_down=moe_w_down)
    depth = w_in.shape[0]
    b, t_len, d = x_prompt.shape
    ns = x_sample.shape[0]

    mods = _ada_call(jnp.concatenate([c_sample, c_prompt], axis=0), w_ada, b_ada)
    prep = _ssm_prep_call(ssm_a_re, ssm_a_im, ssm_log_dt, ssm_b_re, ssm_b_im)

    xp = x_prompt
    xs = x_sample.reshape(ns, d)
    p_out = ([], [], [], [], [])
    s_out = ([], [], [], [], [])
    for l in range(depth):
        lp = _layer_params(l, p, prep)
        mods2 = mods[l]
        mods3 = mods2.reshape(ns + b, 1, N_MOD * d)

        xp, hre, him, kwin, vwin, plast = _mixer_call(xp, mods3, lp)
        x2 = xp.reshape(b * t_len, d)
        x2 = (_moe_call if l % 2 == 1 else _ffn_call)(x2, mods3, mods2, lp, t_len, False)
        xp = x2.reshape(b, t_len, d)
        p_out[0].append(hre.reshape(b, SSM_GROUPS, SSM_STATE))
        p_out[1].append(him.reshape(b, SSM_GROUPS, SSM_STATE))
        p_out[2].append(kwin.reshape(b, WINDOW, N_KV_HEADS, HEAD_DIM))
        p_out[3].append(vwin.reshape(b, WINDOW, N_KV_HEADS, HEAD_DIM))
        p_out[4].append(plast[:, POOL_HIST - POOL_BUF:, :])

        pa, pg = _sproj_call(xs, mods2, lp)
        u = pa[:, 0:D_SSM]
        q3 = pa[:, D_SSM:D_SSM + D_ATTN].reshape(ns, GQA, LANES)
        k_new = pa[:, D_SSM + D_ATTN:D_SSM + D_ATTN + D_KV]
        v_new = pa[:, D_SSM + D_ATTN + D_KV:D_SSM + D_ATTN + 2 * D_KV]
        pin = pa[:, D_SMALL - D_POOL:]
        ck = cache_win_k[l].reshape(ns, -1, D_KV)
        cv = cache_win_v[l].reshape(ns, -1, D_KV)
        yssm, yatt3, ypool, hr, hi = _smix_call(
            u, q3, k_new.reshape(ns, 1, D_KV), v_new.reshape(ns, 1, D_KV), pin,
            state_ssm_re[l].reshape(ns, D_STATE), state_ssm_im[l].reshape(ns, D_STATE),
            ck, cv, jnp.swapaxes(state_pool[l], 0, 1), lp)
        xs = _smerge_call(xs, mods2, pg, yssm, yatt3.reshape(ns, D_ATTN), ypool, lp)
        xs = (_moe_call if l % 2 == 1 else _ffn_call)(xs, mods3, mods2, lp, 1, True)
        s_out[0].append(hr.reshape(ns, SSM_GROUPS, SSM_STATE))
        s_out[1].append(hi.reshape(ns, SSM_GROUPS, SSM_STATE))
        s_out[2].append(jnp.concatenate([ck[:, 1:], k_new[:, None, :]], axis=1).reshape(cache_win_k.shape[1:]))
        s_out[3].append(jnp.concatenate([cv[:, 1:], v_new[:, None, :]], axis=1).reshape(cache_win_v.shape[1:]))
        s_out[4].append(jnp.concatenate([state_pool[l][:, 1:], pin[:, None, :]], axis=1))

    stack = lambda xs_: [jnp.stack(a, axis=0) for a in xs_]
    p_ssm_re, p_ssm_im, p_win_k, p_win_v, p_pool = stack(p_out)
    s_ssm_re, s_ssm_im, s_win_k, s_win_v, s_pool = stack(s_out)
    return (xp, xs.reshape(x_sample.shape), p_ssm_re, p_ssm_im, p_win_k, p_win_v, p_pool,
            s_ssm_re, s_ssm_im, s_win_k, s_win_v, s_pool)
```

```python
import functools
import math

import jax
import jax.numpy as jnp
from jax import lax
from jax.experimental import pallas as pl
from jax.experimental.pallas import tpu as pltpu

F32 = jnp.float32
BF16 = jnp.bfloat16

D_MODEL = 1024
D_SSM = 256
SSM_GROUP = 16
SSM_GROUPS = 16
SSM_STATE = 64
D_STATE = SSM_GROUPS * SSM_STATE
HEAD_DIM = 64
N_HEADS = 8
N_KV_HEADS = 2
GQA = N_HEADS // N_KV_HEADS
D_ATTN = N_HEADS * HEAD_DIM
D_KV = N_KV_HEADS * HEAD_DIM
WINDOW = 128
ATTN_SCALE = HEAD_DIM ** -0.5
NEG_INF = -1e30
LOG2E = math.log2(math.e)
D_POOL = 256
POOL_WINDOWS = (2, 4, 8, 16)
POOL_BUF = 15
POOL_HIST = 16
N_MOD = 6
D_FF = 2816
N_EXPERTS = 8
D_EXPERT = 1024
RMS_EPS = 1e-6
PAST_LEN = 16384
D_SMALL = D_SSM + D_ATTN + 2 * D_KV + D_POOL
D_GATES = 3 * D_MODEL
SLOPES = tuple(2.0 ** (-8.0 * (h + 1) / N_HEADS) for h in range(N_HEADS))

SUBLANES = 8
LANES = 128
VMEM_LIMIT = 56 * 1024 * 1024

TT = 512
SCAN_LEN = TT // SUBLANES
Q_BLK = 128
GATE_CHUNK = 256
GATE_LOOP_A = 8
GATE_LOOP_B = D_GATES // GATE_CHUNK - GATE_LOOP_A
FFN_TM = 512
MOE_TM = 1024
SAMPLE_BC = 32
ADA_TN = 1536
FF_CHUNKS = ((0, 1024), (1024, 2048), (2048, D_FF))


def _rms(x, gain):
    return x * lax.rsqrt(jnp.mean(x * x, axis=-1, keepdims=True) + RMS_EPS) * gain


def _sigmoid(x):
    return 1.0 / (1.0 + jnp.exp(-x))


def _silu(x):
    return x * _sigmoid(x)


def _dot(a, b):
    return jnp.dot(a, b, preferred_element_type=F32)


def _dot_t(a, b):
    return lax.dot_general(a, b, (((1,), (1,)), ((), ())), preferred_element_type=F32)


def _const_spec(shape):
    nd = len(shape)
    return pl.BlockSpec(shape, lambda *_: (0,) * nd)


def _params(sem):
    return pltpu.CompilerParams(dimension_semantics=sem, vmem_limit_bytes=VMEM_LIMIT)


def _ada_kernel(c_ref, w_ref, b_ref, o_ref):
    c = c_ref[...]
    s = _silu(c).astype(BF16)
    o_ref[0] = _dot(s, w_ref[0].astype(BF16)) + b_ref[0]


def _ada_call(c_all, w_ada, b_ada):
    depth = w_ada.shape[0]
    rows = c_all.shape[0]
    n = N_MOD * D_MODEL
    return pl.pallas_call(
        _ada_kernel,
        out_shape=jax.ShapeDtypeStruct((depth, rows, n), F32),
        grid=(depth, n // ADA_TN),
        in_specs=[
            pl.BlockSpec((rows, D_MODEL), lambda l, j: (0, 0)),
            pl.BlockSpec((1, D_MODEL, ADA_TN), lambda l, j: (l, 0, j)),
            pl.BlockSpec((1, 1, ADA_TN), lambda l, j: (l, 0, j)),
        ],
        out_specs=pl.BlockSpec((1, rows, ADA_TN), lambda l, j: (l, 0, j)),
        compiler_params=_params(("arbitrary", "arbitrary")),
        name="ada_mod",
    )(c_all, w_ada, b_ada.reshape(depth, 1, n))


def _ssm_prep_kernel(are_ref, aim_ref, ldt_ref, bre_ref, bim_ref,
                     abre_o, abim_o, bbre_o, bbim_o, pwre_o, pwim_o):
    ar = are_ref[0]
    ai = aim_ref[0]
    dt = jnp.exp(ldt_ref[0])
    mag = jnp.exp(ar * dt)
    ab_re = mag * jnp.cos(ai * dt)
    ab_im = mag * jnp.sin(ai * dt)
    den = ar * ar + ai * ai
    n_re = ab_re - 1.0
    f_re = (n_re * ar + ab_im * ai) / den
    f_im = (ab_im * ar - n_re * ai) / den
    br = bre_ref[0]
    bi = bim_ref[0]
    bbre_o[0] = f_re * br - f_im * bi
    bbim_o[0] = f_re * bi + f_im * br
    abre_o[0] = ab_re
    abim_o[0] = ab_im
    cr, ci = ab_re, ab_im
    for t in range(SCAN_LEN):
        pwre_o[0, t] = cr
        pwim_o[0, t] = ci
        cr, ci = cr * ab_re - ci * ab_im, cr * ab_im + ci * ab_re


def _ssm_prep_call(a_re, a_im, log_dt, b_re, b_im):
    depth = a_re.shape[0]
    g, p, c = SSM_GROUPS, SSM_STATE, SSM_GROUP
    a4 = lambda a: a.reshape(depth, g, 1, p)
    spec_a = pl.BlockSpec((1, g, 1, p), lambda l: (l, 0, 0, 0))
    spec_b = pl.BlockSpec((1, g, c, p), lambda l: (l, 0, 0, 0))
    spec_pw = pl.BlockSpec((1, SCAN_LEN, g, 1, p), lambda l: (l, 0, 0, 0, 0))
    return pl.pallas_call(
        _ssm_prep_kernel,
        out_shape=(jax.ShapeDtypeStruct((depth, g, 1, p), F32),) * 2
        + (jax.ShapeDtypeStruct((depth, g, c, p), F32),) * 2
        + (jax.ShapeDtypeStruct((depth, SCAN_LEN, g, 1, p), F32),) * 2,
        grid=(depth,),
        in_specs=[spec_a, spec_a, pl.BlockSpec((1, g, 1, 1), lambda l: (l, 0, 0, 0)), spec_b, spec_b],
        out_specs=(spec_a, spec_a, spec_b, spec_b, spec_pw, spec_pw),
        compiler_params=_params(("arbitrary",)),
        name="ssm_prep",
    )(a4(a_re), a4(a_im), log_dt.reshape(depth, g, 1, 1),
      jnp.swapaxes(b_re, 2, 3), jnp.swapaxes(b_im, 2, 3))


def _ssm_out(y, u, d_skip, w_glu, b_glu):
    z = jax.nn.gelu(y + d_skip * u)
    return z * _sigmoid(_dot(z.astype(BF16), w_glu) + b_glu)


def _lane_lo(shape):
    return (lax.broadcasted_iota(jnp.int32, shape, len(shape) - 1) % LANES) < HEAD_DIM


def _mixer_kernel(x_ref, sh_ref, sc_ref, g_ref, npre_ref, npost_ref, wa_ref, wg_ref,
                  wbu_ref, abre_ref, abim_ref, pwre_ref, pwim_ref, cm_ref, dsk_ref, wglu_ref, bglu_ref,
                  sinks_ref, wpool_ref, pscale_ref, wus_ref, wua_ref, wup_ref, wout_ref,
                  xo_ref, hre_o, him_o, kwin_o, vwin_o, plast_o,
                  hb_s, proj_s, bu_s, hst_s, sg_s, unat_s, uperm_s, operm_s, yssm_s, yatt_s, kbuf, vbuf, pbuf, hc_s,
                  bias_s):
    t = pl.program_id(1)

    @pl.when(t == 0)
    def _():
        hc_s[...] = jnp.zeros_like(hc_s)
        kbuf[0:WINDOW, :] = jnp.zeros((WINDOW, D_KV), BF16)
        vbuf[0:WINDOW, :] = jnp.zeros((WINDOW, D_KV), BF16)
        pbuf[0:POOL_HIST, :] = jnp.zeros((POOL_HIST, D_POOL), F32)

    x = x_ref[0]
    h = _rms(x, npre_ref[...]) * (1.0 + sc_ref[0]) + sh_ref[0]
    hb_s[...] = h.astype(BF16)
    proj_s[...] = _dot(hb_s[...], wa_ref[...])

    for c in range(D_SSM // LANES):
        unat_s[c] = proj_s[:, c * LANES:(c + 1) * LANES]
        for i in range(SCAN_LEN):
            uperm_s[i * SUBLANES:(i + 1) * SUBLANES, c * LANES:(c + 1) * LANES] = (
                unat_s.at[c][pl.ds(i, SUBLANES, stride=SCAN_LEN), :])
    bu_s[...] = _dot(uperm_s[...].astype(BF16), wbu_ref[...])

    abr = jnp.broadcast_to(abre_ref[...], (SUBLANES, D_STATE))
    abi = jnp.broadcast_to(abim_ref[...], (SUBLANES, D_STATE))

    def gate_chunk(c):
        sg_s[c] = _sigmoid(_dot(hb_s[...], wg_ref[c]))

    zero = jnp.zeros((SUBLANES, D_STATE), F32)
    steps_a = SCAN_LEN // GATE_LOOP_A

    def scan_loop(i, carry):
        er, ei = carry
        gate_chunk(i)
        for s in range(steps_a):
            rows = pl.ds(pl.multiple_of((i * steps_a + s) * SUBLANES, SUBLANES), SUBLANES)
            er, ei = (abr * er - abi * ei + bu_s[rows, 0:D_STATE],
                      abr * ei + abi * er + bu_s[rows, D_STATE:2 * D_STATE])
            bu_s[rows, 0:D_STATE] = er
            bu_s[rows, D_STATE:2 * D_STATE] = ei
        return er, ei

    er, ei = zero, zero
    for i in range(GATE_LOOP_A):
        er, ei = scan_loop(i, (er, ei))

    alr = pwre_ref[SCAN_LEN - 1:SCAN_LEN, :]
    ali = pwim_ref[SCAN_LEN - 1:SCAN_LEN, :]
    cr = hc_s[0:1, :]
    ci = hc_s[1:2, :]
    row = lax.broadcasted_iota(jnp.int32, (SUBLANES, D_STATE), 0)
    hin_r = zero
    hin_i = zero
    for j in range(SUBLANES):
        hin_r = jnp.where(row == j, cr, hin_r)
        hin_i = jnp.where(row == j, ci, hin_i)
        cr, ci = (alr * cr - ali * ci + er[j:j + 1, :], alr * ci + ali * cr + ei[j:j + 1, :])
    hc_s[0:1, :] = cr
    hc_s[1:2, :] = ci
    hre_o[0] = cr
    him_o[0] = ci

    pack = 2 * SUBLANES
    packs_b = TT // pack // GATE_LOOP_B

    def fix_loop(i, _):
        gate_chunk(GATE_LOOP_A + i)
        for s in range(packs_b):
            k = i * packs_b + s
            fixed_r, fixed_i = [], []
            for half in range(2):
                slab = 2 * k + half
                rows = pl.ds(pl.multiple_of(slab * SUBLANES, SUBLANES), SUBLANES)
                pr = pwre_ref[pl.ds(slab, 1), :]
                pi = pwim_ref[pl.ds(slab, 1), :]
                fixed_r.append(bu_s[rows, 0:D_STATE] + (pr * hin_r - pi * hin_i))
                fixed_i.append(bu_s[rows, D_STATE:2 * D_STATE] + (pr * hin_i + pi * hin_r))
            prow = pl.ds(pl.multiple_of(k * pack, pack), pack)
            hst_s[prow, 0:D_STATE] = jnp.concatenate(fixed_r, axis=0).astype(BF16)
            hst_s[prow, D_STATE:2 * D_STATE] = jnp.concatenate(fixed_i, axis=0).astype(BF16)
        return 0

    for i in range(GATE_LOOP_B):
        fix_loop(i, 0)

    y = _dot(hst_s[...], cm_ref[...])
    operm_s[...] = _ssm_out(y, uperm_s[...], dsk_ref[...], wglu_ref[...], bglu_ref[...])
    for c in range(D_SSM // LANES):
        for i in range(SCAN_LEN):
            yssm_s.at[c][pl.ds(i, SUBLANES, stride=SCAN_LEN), :] = (
                operm_s[i * SUBLANES:(i + 1) * SUBLANES, c * LANES:(c + 1) * LANES])

    kbuf[WINDOW:WINDOW + TT, :] = proj_s[:, D_SSM + D_ATTN:D_SSM + D_ATTN + D_KV].astype(BF16)
    vbuf[WINDOW:WINDOW + TT, :] = proj_s[:, D_SSM + D_ATTN + D_KV:D_SSM + D_ATTN + 2 * D_KV].astype(BF16)
    lo = _lane_lo((Q_BLK, LANES))
    r_i = lax.broadcasted_iota(jnp.int32, (Q_BLK, 2 * Q_BLK), 0)
    c_i = lax.broadcasted_iota(jnp.int32, (Q_BLK, 2 * Q_BLK), 1)
    @pl.when((pl.program_id(0) == 0) & (t == 0))
    def _():
        dist = r_i - c_i + Q_BLK
        valid = (dist >= 0) & (dist <= WINDOW)
        for hh in range(N_HEADS):
            bias_s[hh] = jnp.where(valid, (-SLOPES[hh] * LOG2E) * dist.astype(F32), NEG_INF)

    first_key = jnp.where(t == 0, Q_BLK, 0)
    for blk in range(TT // Q_BLK):
        r0 = blk * Q_BLK
        kk = kbuf[r0:r0 + 2 * Q_BLK, :]
        vv = vbuf[r0:r0 + 2 * Q_BLK, :]
        pieces = []
        for hh in range(N_HEADS):
            j = hh % GQA
            qp = proj_s[r0:r0 + Q_BLK, D_SSM + j * LANES:D_SSM + (j + 1) * LANES] * (ATTN_SCALE * LOG2E)
            keep = lo if hh < GQA else jnp.logical_not(lo)
            pieces.append(jnp.where(keep, qp, 0.0).astype(BF16))
        s_all = _dot_t(jnp.concatenate(pieces, axis=0), kk)
        outs = []
        for hh in range(N_HEADS):
            s = s_all[hh * Q_BLK:(hh + 1) * Q_BLK, :] + bias_s[hh]
            if blk == 0:
                s = jnp.where(c_i >= first_key, s, NEG_INF)
            snk = sinks_ref[hh] * LOG2E
            m = jnp.maximum(jnp.max(s, axis=-1, keepdims=True), snk)
            p = jnp.exp2(s - m)
            den = jnp.sum(p, axis=-1, keepdims=True) + jnp.exp2(snk - m)
            outs.append(_dot(p.astype(BF16), vv) / den)
        for j in range(GQA):
            yatt_s[r0:r0 + Q_BLK, j * LANES:(j + 1) * LANES] = jnp.where(lo, outs[j], outs[GQA + j]).astype(BF16)
    kwin_o[0] = proj_s[TT - WINDOW:TT, D_SSM + D_ATTN:D_SSM + D_ATTN + D_KV]
    vwin_o[0] = proj_s[TT - WINDOW:TT, D_SSM + D_ATTN + D_KV:D_SSM + D_ATTN + 2 * D_KV]
    kbuf[0:WINDOW, :] = kbuf[TT:TT + WINDOW, :]
    vbuf[0:WINDOW, :] = vbuf[TT:TT + WINDOW, :]

    p0 = D_SMALL - D_POOL
    pbuf[POOL_HIST:POOL_HIST + TT, :] = proj_s[:, p0:D_SMALL]
    plast_o[0] = proj_s[TT - POOL_HIST:TT, p0:D_SMALL]
    pos1 = (t * TT + 1 + lax.broadcasted_iota(jnp.int32, (TT, LANES), 0)).astype(F32)
    lo_t = _lane_lo((TT, LANES))
    mixed = []
    for col, (w_lo, w_hi) in enumerate(((POOL_WINDOWS[0], POOL_WINDOWS[1]), (POOL_WINDOWS[2], POOL_WINDOWS[3]))):
        cs = slice(col * LANES, (col + 1) * LANES)
        pf = pbuf[POOL_HIST:POOL_HIST + TT, cs]
        acc = pf
        for k in range(1, w_lo):
            acc = acc + pbuf[POOL_HIST - k:POOL_HIST - k + TT, cs]
        acc_lo = acc
        for k in range(w_lo, w_hi):
            acc = acc + pbuf[POOL_HIST - k:POOL_HIST - k + TT, cs]
        cnt = jnp.where(lo_t, jnp.minimum(float(w_lo), pos1), jnp.minimum(float(w_hi), pos1))
        mixed.append(jnp.where(lo_t, acc_lo, acc) / cnt - pf)
    ypool = _dot(jnp.concatenate(mixed, axis=1).astype(BF16), wpool_ref[...]) * pscale_ref[...]
    pbuf[0:POOL_HIST, :] = pbuf[TT:TT + POOL_HIST, :]

    per_branch = D_MODEL // GATE_CHUNK
    gate = lambda b: jnp.concatenate([sg_s[b * per_branch + c] for c in range(per_branch)], axis=1)
    yssm = jnp.concatenate([yssm_s[c] for c in range(D_SSM // LANES)], axis=1)
    merged = gate(0) * _dot(yssm.astype(BF16), wus_ref[...])
    merged += gate(1) * _dot(yatt_s[...], wua_ref[...])
    merged += gate(2) * _dot(ypool.astype(BF16), wup_ref[...])
    out = _dot(merged.astype(BF16), wout_ref[...])
    xo_ref[0] = x_ref[0] + g_ref[0] * _rms(out, npost_ref[...])


def _mixer_call(x, mods3, lp):
    b, t_len, d = x.shape
    nt = t_len // TT
    nsamp = mods3.shape[0] - b
    mod_spec = lambda col: pl.BlockSpec((1, 1, D_MODEL), lambda i, j: (nsamp + i, 0, col))
    x_spec = pl.BlockSpec((1, TT, d), lambda i, j: (i, j, 0))
    consts = [lp['norm_mix_pre'], lp['norm_mix_post'], lp['w_in_a'], lp['w_in_g'],
              lp['w_bu'], lp['ab_re'], lp['ab_im'], lp['pw_re'], lp['pw_im'], lp['c_mat'],
              lp['ssm_d'], lp['ssm_w_glu'], lp['ssm_b_glu']]
    consts2 = [lp['pool_wbd'], lp['pool_scale'], lp['w_up_ssm'], lp['w_up_attn'], lp['w_up_pool'], lp['w_out']]
    in_specs = ([x_spec, mod_spec(0), mod_spec(1), mod_spec(2)]
                + [_const_spec(a.shape) for a in consts]
                + [pl.BlockSpec(memory_space=pltpu.SMEM)]
                + [_const_spec(a.shape) for a in consts2])
    per_b = lambda r, c: pl.BlockSpec((1, r, c), lambda i, j: (i, 0, 0))
    out_shape = (jax.ShapeDtypeStruct((b, t_len, d), F32),
                 jax.ShapeDtypeStruct((b, 1, D_STATE), F32), jax.ShapeDtypeStruct((b, 1, D_STATE), F32),
                 jax.ShapeDtypeStruct((b, WINDOW, D_KV), F32), jax.ShapeDtypeStruct((b, WINDOW, D_KV), F32),
                 jax.ShapeDtypeStruct((b, POOL_HIST, D_POOL), F32))
    out_specs = (x_spec, per_b(1, D_STATE), per_b(1, D_STATE), per_b(WINDOW, D_KV), per_b(WINDOW, D_KV),
                 per_b(POOL_HIST, D_POOL))
    scratch = [pltpu.VMEM((TT, d), BF16), pltpu.VMEM((TT, D_SMALL), F32), pltpu.VMEM((TT, 2 * D_STATE), F32),
               pltpu.VMEM((TT, 2 * D_STATE), BF16), pltpu.VMEM((D_GATES // GATE_CHUNK, TT, GATE_CHUNK), F32),
               pltpu.VMEM((D_SSM // LANES, TT, LANES), F32), pltpu.VMEM((TT, D_SSM), F32),
               pltpu.VMEM((TT, D_SSM), F32), pltpu.VMEM((D_SSM // LANES, TT, LANES), F32),
               pltpu.VMEM((TT, D_ATTN), BF16),
               pltpu.VMEM((WINDOW + TT, D_KV), BF16), pltpu.VMEM((WINDOW + TT, D_KV), BF16),
               pltpu.VMEM((POOL_HIST + TT, D_POOL), F32), pltpu.VMEM((SUBLANES, D_STATE), F32),
               pltpu.VMEM((N_HEADS, Q_BLK, 2 * Q_BLK), F32)]
    return pl.pallas_call(
        _mixer_kernel,
        out_shape=out_shape,
        grid=(b, nt),
        in_specs=in_specs,
        out_specs=out_specs,
        scratch_shapes=scratch,
        compiler_params=_params(("arbitrary", "arbitrary")),
        name="prompt_mixer",
    )(x, mods3, mods3, mods3, *consts, lp['attn_sinks'], *consts2)


def _mod_rows(ref, per_row):
    return ref[...] if per_row else ref[0]


def _ffn_kernel(per_row, x_ref, sh_ref, sc_ref, g_ref, npre_ref, npost_ref, wg_ref, wu_ref, wd_ref, o_ref):
    x = x_ref[...]
    h = (_rms(x, npre_ref[...]) * (1.0 + _mod_rows(sc_ref, per_row)) + _mod_rows(sh_ref, per_row)).astype(BF16)
    f = None
    for c0, c1 in FF_CHUNKS:
        act = _silu(_dot(h, wg_ref[:, c0:c1])) * _dot(h, wu_ref[:, c0:c1])
        part = _dot(act.astype(BF16), wd_ref[c0:c1, :])
        f = part if f is None else f + part
    o_ref[...] = x + _mod_rows(g_ref, per_row) * _rms(f, npost_ref[...])


def _mod_specs(mods3, mods2, n_rows, tm, rows_per_seq, per_row):
    nsamp = mods2.shape[0] - (n_rows // rows_per_seq if not per_row else 0)
    if per_row:
        specs = [pl.BlockSpec((tm, D_MODEL), functools.partial(lambda col, i, *_: (i, col), col)) for col in (3, 4, 5)]
        return specs, [mods2] * 3
    tiles_per_seq = rows_per_seq // tm
    specs = [pl.BlockSpec((1, 1, D_MODEL),
                          functools.partial(lambda col, i, *_: (nsamp + i // tiles_per_seq, 0, col), col))
             for col in (3, 4, 5)]
    return specs, [mods3] * 3


def _ffn_call(x2, mods3, mods2, lp, rows_per_seq, per_row):
    n = x2.shape[0]
    tm = n if per_row else FFN_TM
    mspecs, mops = _mod_specs(mods3, mods2, n, tm, rows_per_seq, per_row)
    consts = [lp['norm_ffn_pre'], lp['norm_ffn_post'], lp['ffn_w_gate'], lp['ffn_w_up'], lp['ffn_w_down']]
    x_spec = pl.BlockSpec((tm, D_MODEL), lambda i: (i, 0))
    return pl.pallas_call(
        functools.partial(_ffn_kernel, per_row),
        out_shape=jax.ShapeDtypeStruct(x2.shape, F32),
        grid=(n // tm,),
        in_specs=[x_spec] + mspecs + [_const_spec(a.shape) for a in consts],
        out_specs=x_spec,
        compiler_params=_params(("arbitrary",)),
        name="ffn_rows" if per_row else "ffn_seq",
    )(x2, *mops, *consts)


def _moe_kernel(per_row, x_ref, sh_ref, sc_ref, g_ref, npre_ref, npost_ref, wr_ref, br_ref,
                wg_ref, wu_ref, wd_ref, o_ref, h_s, gates_s, acc_s):
    e = pl.program_id(1)
    lane_e = lax.broadcasted_iota(jnp.int32, gates_s.shape, 1)

    @pl.when(e == 0)
    def _():
        x = x_ref[...]
        h = _rms(x, npre_ref[...]) * (1.0 + _mod_rows(sc_ref, per_row)) + _mod_rows(sh_ref, per_row)
        hb = h.astype(BF16)
        h_s[...] = hb
        logits = _dot(hb, wr_ref[...]) + br_ref[...]
        lane_f = lane_e.astype(F32)
        v1 = jnp.max(logits, axis=-1, keepdims=True)
        i1 = jnp.min(jnp.where(logits == v1, lane_f, float(N_EXPERTS)), axis=-1, keepdims=True)
        rest = jnp.where(lane_f == i1, -jnp.inf, logits)
        v2 = jnp.max(rest, axis=-1, keepdims=True)
        i2 = jnp.min(jnp.where(rest == v2, lane_f, float(N_EXPERTS)), axis=-1, keepdims=True)
        e2 = jnp.exp(v2 - v1)
        den = 1.0 + e2
        gates_s[...] = jnp.where(lane_f == i1, 1.0 / den, 0.0) + jnp.where(lane_f == i2, e2 / den, 0.0)
        acc_s[...] = jnp.zeros_like(acc_s)

    h = h_s[...]
    act = _silu(_dot(h, wg_ref[0, 0])) * _dot(h, wu_ref[0, 0])
    y = _dot(act.astype(BF16), wd_ref[0, 0])
    ge = jnp.sum(jnp.where(lane_e == e, gates_s[...], 0.0), axis=-1, keepdims=True)
    acc_s[...] += ge * y

    @pl.when(e == N_EXPERTS - 1)
    def _():
        o_ref[...] = x_ref[...] + _mod_rows(g_ref, per_row) * _rms(acc_s[...], npost_ref[...])


def _moe_call(x2, mods3, mods2, lp, rows_per_seq, per_row):
    n = x2.shape[0]
    tm = n if per_row else MOE_TM
    mspecs, mops = _mod_specs(mods3, mods2, n, tm, rows_per_seq, per_row)
    consts = [lp['norm_ffn_pre'], lp['norm_ffn_post'], lp['moe_w_router'], lp['moe_b_router']]
    x_spec = pl.BlockSpec((tm, D_MODEL), lambda i, e: (i, 0))
    w_spec = lambda a: pl.BlockSpec((1, 1) + a.shape[2:], lambda i, e: (0, e, 0, 0))
    experts = [lp['moe_w_gate'], lp['moe_w_up'], lp['moe_w_down']]
    return pl.pallas_call(
        functools.partial(_moe_kernel, per_row),
        out_shape=jax.ShapeDtypeStruct(x2.shape, F32),
        grid=(n // tm, N_EXPERTS),
        in_specs=[x_spec] + mspecs + [_const_spec(a.shape) for a in consts] + [w_spec(a) for a in experts],
        out_specs=x_spec,
        scratch_shapes=[pltpu.VMEM((tm, D_MODEL), BF16), pltpu.VMEM((tm, N_EXPERTS), F32),
                        pltpu.VMEM((tm, D_MODEL), F32)],
        compiler_params=_params(("arbitrary", "arbitrary")),
        name="moe_rows" if per_row else "moe_seq",
    )(x2, *mops, *consts, *experts)


def _sproj_kernel(x_ref, sh_ref, sc_ref, npre_ref, wa_ref, wg_ref, pa_o, pg_o):
    h = (_rms(x_ref[...], npre_ref[...]) * (1.0 + sc_ref[...]) + sh_ref[...]).astype(BF16)
    pa_o[...] = _dot(h, wa_ref[...])
    for c in range(D_GATES // GATE_CHUNK):
        pg_o[:, c * GATE_CHUNK:(c + 1) * GATE_CHUNK] = _dot(h, wg_ref[c])


def _sproj_call(xs, mods2, lp):
    n = xs.shape[0]
    mod_spec = lambda col: pl.BlockSpec((n, D_MODEL), lambda i: (0, col))
    consts = [lp['norm_mix_pre'], lp['w_in_a'], lp['w_in_g']]
    return pl.pallas_call(
        _sproj_kernel,
        out_shape=(jax.ShapeDtypeStruct((n, D_SMALL), F32), jax.ShapeDtypeStruct((n, D_GATES), F32)),
        grid=(1,),
        in_specs=[_const_spec(xs.shape), mod_spec(0), mod_spec(1)] + [_const_spec(a.shape) for a in consts],
        out_specs=(_const_spec((n, D_SMALL)), _const_spec((n, D_GATES))),
        compiler_params=_params(("arbitrary",)),
        name="sample_proj",
    )(xs, mods2, mods2, *consts)


def _smix_kernel(u_ref, q_ref, kn_ref, vn_ref, pin_ref, h0r_ref, h0i_ref, ck_ref, cv_ref, hist_ref,
                 wbu_ref, abre_ref, abim_ref, cm_ref, dsk_ref, wglu_ref, bglu_ref, sinks_ref,
                 wpool_ref, pscale_ref, yssm_o, yatt_o, ypool_o, hr_o, hi_o):
    u = u_ref[...]
    bu = _dot(u.astype(BF16), wbu_ref[...])
    abr, abi = abre_ref[...], abim_ref[...]
    h0r, h0i = h0r_ref[...], h0i_ref[...]
    hr = bu[:, 0:D_STATE] + (abr * h0r - abi * h0i)
    hi = bu[:, D_STATE:2 * D_STATE] + (abr * h0i + abi * h0r)
    hr_o[...] = hr
    hi_o[...] = hi
    y = _dot(jnp.concatenate([hr, hi], axis=1).astype(BF16), cm_ref[...])
    yssm_o[...] = _ssm_out(y, u, dsk_ref[...], wglu_ref[...], bglu_ref[...])

    q = q_ref[...] * ATTN_SCALE
    kn, vn = kn_ref[...], vn_ref[...]
    kc = ck_ref[...].astype(BF16)
    vc = cv_ref[...].astype(BF16)
    lo = _lane_lo(q.shape)
    j_i = lax.broadcasted_iota(jnp.int32, (1, GQA, 1), 1)
    dist_c = (WINDOW - lax.broadcasted_iota(jnp.int32, (1, 1, WINDOW), 2)).astype(F32)
    halves = []
    for half in range(N_KV_HEADS):
        keep = lo if half == 0 else jnp.logical_not(lo)
        qm = jnp.where(keep, q, 0.0)
        slope = jnp.zeros((1, GQA, 1), F32)
        snk = jnp.zeros((1, GQA, 1), F32)
        for j in range(GQA):
            slope = jnp.where(j_i == j, SLOPES[half * GQA + j], slope)
            snk = jnp.where(j_i == j, sinks_ref[half * GQA + j], snk)
        s = jnp.einsum('bhd,bsd->bhs', qm.astype(BF16), kc, preferred_element_type=F32) - slope * dist_c
        s_new = jnp.sum(qm * kn, axis=-1, keepdims=True)
        m = jnp.maximum(jnp.maximum(jnp.max(s, axis=-1, keepdims=True), s_new), snk)
        p = jnp.exp(s - m)
        p_new = jnp.exp(s_new - m)
        den = jnp.sum(p, axis=-1, keepdims=True) + p_new + jnp.exp(snk - m)
        o = jnp.einsum('bhs,bsd->bhd', p.astype(BF16), vc, preferred_element_type=F32) + p_new * vn
        halves.append(o / den)
    yatt_o[...] = jnp.where(lo, halves[0], halves[1])

    pin = pin_ref[...]
    lo2 = _lane_lo((pin.shape[0], LANES))
    mixed = []
    for col, (w_lo, w_hi) in enumerate(((POOL_WINDOWS[0], POOL_WINDOWS[1]), (POOL_WINDOWS[2], POOL_WINDOWS[3]))):
        cs = slice(col * LANES, (col + 1) * LANES)
        pf = pin[:, cs]
        acc = pf
        for k in range(1, w_lo):
            acc = acc + hist_ref[POOL_BUF - k, :, cs]
        acc_lo = acc
        for k in range(w_lo, w_hi):
            acc = acc + hist_ref[POOL_BUF - k, :, cs]
        cnt_lo = float(min(w_lo, PAST_LEN + 1))
        cnt_hi = float(min(w_hi, PAST_LEN + 1))
        mixed.append(jnp.where(lo2, acc_lo / cnt_lo, acc / cnt_hi) - pf)
    ypool_o[...] = _dot(jnp.concatenate(mixed, axis=1).astype(BF16), wpool_ref[...]) * pscale_ref[...]


def _smix_call(u, q3, kn3, vn3, pin, h0r, h0i, ck, cv, hist_t, lp):
    n = u.shape[0]
    bc = SAMPLE_BC
    rows2 = lambda c: pl.BlockSpec((bc, c), lambda i: (i, 0))
    rows3 = lambda r, c: pl.BlockSpec((bc, r, c), lambda i: (i, 0, 0))
    consts = [lp['w_bu'], lp['ab_re'], lp['ab_im'], lp['c_mat'], lp['ssm_d'], lp['ssm_w_glu'], lp['ssm_b_glu']]
    consts2 = [lp['pool_wbd'], lp['pool_scale']]
    in_specs = ([rows2(D_SSM), rows3(GQA, LANES), rows3(1, LANES), rows3(1, LANES), rows2(D_POOL),
                 rows2(D_STATE), rows2(D_STATE), rows3(WINDOW, D_KV), rows3(WINDOW, D_KV),
                 pl.BlockSpec((POOL_BUF, bc, D_POOL), lambda i: (0, i, 0))]
                + [_const_spec(a.shape) for a in consts]
                + [pl.BlockSpec(memory_space=pltpu.SMEM)]
                + [_const_spec(a.shape) for a in consts2])
    return pl.pallas_call(
        _smix_kernel,
        out_shape=(jax.ShapeDtypeStruct((n, D_SSM), F32), jax.ShapeDtypeStruct((n, GQA, LANES), F32),
                   jax.ShapeDtypeStruct((n, D_POOL), F32),
                   jax.ShapeDtypeStruct((n, D_STATE), F32), jax.ShapeDtypeStruct((n, D_STATE), F32)),
        grid=(n // bc,),
        in_specs=in_specs,
        out_specs=(rows2(D_SSM), rows3(GQA, LANES), rows2(D_POOL), rows2(D_STATE), rows2(D_STATE)),
        compiler_params=_params(("arbitrary",)),
        name="sample_mix",
    )(u, q3, kn3, vn3, pin, h0r, h0i, ck, cv, hist_t, *consts, lp['attn_sinks'], *consts2)


def _smerge_kernel(x_ref, g_ref, pg_ref, yssm_ref, yatt_ref, ypool_ref, npost_ref,
                   wus_ref, wua_ref, wup_ref, wout_ref, o_ref):
    merged = _sigmoid(pg_ref[:, 0:D_MODEL]) * _dot(yssm_ref[...].astype(BF16), wus_ref[...])
    merged += _sigmoid(pg_ref[:, D_MODEL:2 * D_MODEL]) * _dot(yatt_ref[...].astype(BF16), wua_ref[...])
    merged += _sigmoid(pg_ref[:, 2 * D_MODEL:3 * D_MODEL]) * _dot(ypool_ref[...].astype(BF16), wup_ref[...])
    out = _dot(merged.astype(BF16), wout_ref[...])
    o_ref[...] = x_ref[...] + g_ref[...] * _rms(out, npost_ref[...])


def _smerge_call(xs, mods2, pg, yssm, yatt, ypool, lp):
    n = xs.shape[0]
    ops = [xs, mods2, pg, yssm, yatt, ypool, lp['norm_mix_post'],
           lp['w_up_ssm'], lp['w_up_attn'], lp['w_up_pool'], lp['w_out']]
    in_specs = [_const_spec(a.shape) for a in ops]
    in_specs[1] = pl.BlockSpec((n, D_MODEL), lambda i: (0, 2))
    return pl.pallas_call(
        _smerge_kernel,
        out_shape=jax.ShapeDtypeStruct(xs.shape, F32),
        grid=(1,),
        in_specs=in_specs,
        out_specs=_const_spec(xs.shape),
        compiler_params=_params(("arbitrary",)),
        name="sample_merge",
    )(*ops)


def _block_diag(w):
    g, a, b = w.shape
    eye = jnp.eye(g, dtype=w.dtype)
    return jnp.einsum('gab,gh->gahb', w, eye).reshape(g * a, g * b)


def _pair_perm():
    idx = []
    for j in range(GQA):
        idx += list(range(j * HEAD_DIM, (j + 1) * HEAD_DIM))
        idx += list(range((GQA + j) * HEAD_DIM, (GQA + j + 1) * HEAD_DIM))
    return jnp.asarray(idx, dtype=jnp.int32)


def _layer_params(l, p, prep):
    ab_re, ab_im, bb_re, bb_im, pw_re, pw_im = prep
    perm = _pair_perm()
    row = lambda a: a.reshape(1, -1)
    w_in = p['w_in'][l]
    q_cols = w_in[:, D_SSM:D_SSM + D_ATTN][:, perm]
    w_in_a = jnp.concatenate([w_in[:, 0:D_SSM], q_cols, w_in[:, D_SSM + D_ATTN:D_SMALL]], axis=1)
    lp = {
        'norm_mix_pre': row(p['norm_mix_pre'][l]), 'norm_mix_post': row(p['norm_mix_post'][l]),
        'norm_ffn_pre': row(p['norm_ffn_pre'][l]), 'norm_ffn_post': row(p['norm_ffn_post'][l]),
        'w_in_a': w_in_a.astype(BF16),
        'w_in_g': jnp.swapaxes(w_in[:, D_SMALL:].reshape(D_MODEL, D_GATES // GATE_CHUNK, GATE_CHUNK), 0, 1).astype(BF16),
        'w_bu': jnp.concatenate([_block_diag(bb_re[l]), _block_diag(bb_im[l])], axis=1).astype(BF16),
        'ab_re': ab_re[l].reshape(1, D_STATE), 'ab_im': ab_im[l].reshape(1, D_STATE),
        'pw_re': pw_re[l].reshape(SCAN_LEN, D_STATE), 'pw_im': pw_im[l].reshape(SCAN_LEN, D_STATE),
        'c_mat': jnp.concatenate([_block_diag(jnp.swapaxes(p['ssm_c_re'][l], 1, 2)),
                                  _block_diag(jnp.swapaxes(-p['ssm_c_im'][l], 1, 2))], axis=0).astype(BF16),
        'ssm_d': row(p['ssm_d'][l]), 'ssm_w_glu': p['ssm_w_glu'][l].astype(BF16), 'ssm_b_glu': row(p['ssm_b_glu'][l]),
        'attn_sinks': p['attn_sinks'][l],
        'pool_wbd': _block_diag(p['pool_w'][l]).astype(BF16), 'pool_scale': row(p['pool_scale'][l]),
        'w_up_ssm': p['w_up_ssm'][l].astype(BF16), 'w_up_attn': p['w_up_attn'][l][perm, :].astype(BF16),
        'w_up_pool': p['w_up_pool'][l].astype(BF16), 'w_out': p['w_out'][l].astype(BF16),
    }
    j = l // 2
    if l % 2 == 1:
        lp.update({'moe_w_router': p['moe_w_router'][j].astype(BF16), 'moe_b_router': row(p['moe_b_router'][j]),
                   'moe_w_gate': p['moe_w_gate'][j:j + 1].astype(BF16), 'moe_w_up': p['moe_w_up'][j:j + 1].astype(BF16),
                   'moe_w_down': p['moe_w_down'][j:j + 1].astype(BF16)})
    else:
        lp.update({'ffn_w_gate': p['ffn_w_gate'][j].astype(BF16), 'ffn_w_up': p['ffn_w_up'][j].astype(BF16),
                   'ffn_w_down': p['ffn_w_down'][j].astype(BF16)})
    return lp


def kernel(x_prompt, x_sample, state_ssm_re, state_ssm_im, cache_win_k, cache_win_v, state_pool,
           c_prompt, c_sample, w_ada, b_ada, norm_mix_pre, norm_mix_post, norm_ffn_pre, norm_ffn_post,
           w_in, ssm_a_re, ssm_a_im, ssm_log_dt, ssm_b_re, ssm_b_im, ssm_c_re, ssm_c_im, ssm_d,
           ssm_w_glu, ssm_b_glu, attn_sinks, pool_w, pool_scale, w_up_ssm, w_up_attn, w_up_pool, w_out,
           ffn_w_gate, ffn_w_up, ffn_w_down, moe_w_router, moe_b_router, moe_w_gate, moe_w_up, moe_w_down):
    p = dict(norm_mix_pre=norm_mix_pre, norm_mix_post=norm_mix_post, norm_ffn_pre=norm_ffn_pre,
             norm_ffn_post=norm_ffn_post, w_in=w_in, ssm_c_re=ssm_c_re, ssm_c_im=ssm_c_im, ssm_d=ssm_d,
             ssm_w_glu=ssm_w_glu, ssm_b_glu=ssm_b_glu, attn_sinks=attn_sinks, pool_w=pool_w, pool_scale=pool_scale,
             w_up_ssm=w_up_ssm, w_up_attn=w_up_attn, w_up_pool=w_up_pool, w_out=w_out,
             ffn_w_gate=ffn_w_gate, ffn_w_up=ffn_w_up, ffn_w_down=ffn_w_down, moe_w_router=moe_w_router,
             moe_b_router=moe_b_router, moe_w_gate=moe_w_gate, moe_w_up=moe_w_up, moe_w_down=moe_w_down)
    depth = w_in.shape[0]
    b, t_len, d = x_prompt.shape
    ns = x_sample.shape[0]

    mods = _ada_call(jnp.concatenate([c_sample, c_prompt], axis=0), w_ada, b_ada)
    prep = _ssm_prep_call(ssm_a_re, ssm_a_im, ssm_log_dt, ssm_b_re, ssm_b_im)

    xp = x_prompt
    xs = x_sample.reshape(ns, d)
    p_out = ([], [], [], [], [])
    s_out = ([], [], [], [], [])
    for l in range(depth):
        lp = _layer_params(l, p, prep)
        mods2 = mods[l]
        mods3 = mods2.reshape(ns + b, 1, N_MOD * d)

        xp, hre, him, kwin, vwin, plast = _mixer_call(xp, mods3, lp)
        x2 = xp.reshape(b * t_len, d)
        x2 = (_moe_call if l % 2 == 1 else _ffn_call)(x2, mods3, mods2, lp, t_len, False)
        xp = x2.reshape(b, t_len, d)
        p_out[0].append(hre.reshape(b, SSM_GROUPS, SSM_STATE))
        p_out[1].append(him.reshape(b, SSM_GROUPS, SSM_STATE))
        p_out[2].append(kwin.reshape(b, WINDOW, N_KV_HEADS, HEAD_DIM))
        p_out[3].append(vwin.reshape(b, WINDOW, N_KV_HEADS, HEAD_DIM))
        p_out[4].append(plast[:, POOL_HIST - POOL_BUF:, :])

        pa, pg = _sproj_call(xs, mods2, lp)
        u = pa[:, 0:D_SSM]
        q3 = pa[:, D_SSM:D_SSM + D_ATTN].reshape(ns, GQA, LANES)
        k_new = pa[:, D_SSM + D_ATTN:D_SSM + D_ATTN + D_KV]
        v_new = pa[:, D_SSM + D_ATTN + D_KV:D_SSM + D_ATTN + 2 * D_KV]
        pin = pa[:, D_SMALL - D_POOL:]
        ck = cache_win_k[l].reshape(ns, -1, D_KV)
        cv = cache_win_v[l].reshape(ns, -1, D_KV)
        yssm, yatt3, ypool, hr, hi = _smix_call(
            u, q3, k_new.reshape(ns, 1, D_KV), v_new.reshape(ns, 1, D_KV), pin,
            state_ssm_re[l].reshape(ns, D_STATE), state_ssm_im[l].reshape(ns, D_STATE),
            ck, cv, jnp.swapaxes(state_pool[l], 0, 1), lp)
        xs = _smerge_call(xs, mods2, pg, yssm, yatt3.reshape(ns, D_ATTN), ypool, lp)
        xs = (_moe_call if l % 2 == 1 else _ffn_call)(xs, mods3, mods2, lp, 1, True)
        s_out[0].append(hr.reshape(ns, SSM_GROUPS, SSM_STATE))
        s_out[1].append(hi.reshape(ns, SSM_GROUPS, SSM_STATE))
        s_out[2].append(jnp.concatenate([ck[:, 1:], k_new[:, None, :]], axis=1).reshape(cache_win_k.shape[1:]))
        s_out[3].append(jnp.concatenate([cv[:, 1:], v_new[:, None, :]], axis=1).reshape(cache_win_v.shape[1:]))
        s_out[4].append(jnp.concatenate([state_pool[l][:, 1:], pin[:, None, :]], axis=1))

    stack = lambda xs_: [jnp.stack(a, axis=0) for a in xs_]
    p_ssm_re, p_ssm_im, p_win_k, p_win_v, p_pool = stack(p_out)
    s_ssm_re, s_ssm_im, s_win_k, s_win_v, s_pool = stack(s_out)
    return (xp, xs.reshape(x_sample.shape), p_ssm_re, p_ssm_im, p_win_k, p_win_v, p_pool,
            s_ssm_re, s_ssm_im, s_win_k, s_win_v, s_pool)
```

```python
import functools
import math

import jax
import jax.numpy as jnp
from jax import lax
from jax.experimental import pallas as pl
from jax.experimental.pallas import tpu as pltpu

F32 = jnp.float32
BF16 = jnp.bfloat16

D_MODEL = 1024
D_SSM = 256
SSM_GROUP = 16
SSM_GROUPS = 16
SSM_STATE = 64
D_STATE = SSM_GROUPS * SSM_STATE
HEAD_DIM = 64
N_HEADS = 8
N_KV_HEADS = 2
GQA = N_HEADS // N_KV_HEADS
D_ATTN = N_HEADS * HEAD_DIM
D_KV = N_KV_HEADS * HEAD_DIM
WINDOW = 128
ATTN_SCALE = HEAD_DIM ** -0.5
NEG_INF = -1e30
LOG2E = math.log2(math.e)
D_POOL = 256
POOL_WINDOWS = (2, 4, 8, 16)
POOL_BUF = 15
POOL_HIST = 16
N_MOD = 6
D_FF = 2816
N_EXPERTS = 8
D_EXPERT = 1024
RMS_EPS = 1e-6
PAST_LEN = 16384
D_SMALL = D_SSM + D_ATTN + 2 * D_KV + D_POOL
D_GATES = 3 * D_MODEL
SLOPES = tuple(2.0 ** (-8.0 * (h + 1) / N_HEADS) for h in range(N_HEADS))

SUBLANES = 8
LANES = 128
VMEM_LIMIT = 56 * 1024 * 1024

TT = 512
SCAN_LEN = TT // SUBLANES
Q_BLK = 128
GATE_CHUNK = 256
GATE_LOOP_A = 8
GATE_LOOP_B = D_GATES // GATE_CHUNK - GATE_LOOP_A
FFN_TM = 512
MOE_TM = 1024
MOE_SP_TM = 896
MOE_CAP = 256
SAMPLE_BC = 32
ADA_TN = 1536
FF_CHUNKS = ((0, 1024), (1024, 2048), (2048, D_FF))


def _rms(x, gain):
    return x * lax.rsqrt(jnp.mean(x * x, axis=-1, keepdims=True) + RMS_EPS) * gain


def _sigmoid(x):
    return 1.0 / (1.0 + jnp.exp(-x))


def _silu(x):
    return x * _sigmoid(x)


def _dot(a, b):
    return jnp.dot(a, b, preferred_element_type=F32)


def _dot_t(a, b):
    return lax.dot_general(a, b, (((1,), (1,)), ((), ())), preferred_element_type=F32)


def _const_spec(shape):
    nd = len(shape)
    return pl.BlockSpec(shape, lambda *_: (0,) * nd)


def _params(sem):
    return pltpu.CompilerParams(dimension_semantics=sem, vmem_limit_bytes=VMEM_LIMIT)


def _ada_kernel(c_ref, w_ref, b_ref, o_ref):
    c = c_ref[...]
    s = _silu(c).astype(BF16)
    o_ref[0] = _dot(s, w_ref[0].astype(BF16)) + b_ref[0]


def _ada_call(c_all, w_ada, b_ada):
    depth = w_ada.shape[0]
    rows = c_all.shape[0]
    n = N_MOD * D_MODEL
    return pl.pallas_call(
        _ada_kernel,
        out_shape=jax.ShapeDtypeStruct((depth, rows, n), F32),
        grid=(depth, n // ADA_TN),
        in_specs=[
            pl.BlockSpec((rows, D_MODEL), lambda l, j: (0, 0)),
            pl.BlockSpec((1, D_MODEL, ADA_TN), lambda l, j: (l, 0, j)),
            pl.BlockSpec((1, 1, ADA_TN), lambda l, j: (l, 0, j)),
        ],
        out_specs=pl.BlockSpec((1, rows, ADA_TN), lambda l, j: (l, 0, j)),
        compiler_params=_params(("arbitrary", "arbitrary")),
        name="ada_mod",
    )(c_all, w_ada, b_ada.reshape(depth, 1, n))


def _ssm_prep_kernel(are_ref, aim_ref, ldt_ref, bre_ref, bim_ref,
                     abre_o, abim_o, bbre_o, bbim_o, pwre_o, pwim_o):
    ar = are_ref[0]
    ai = aim_ref[0]
    dt = jnp.exp(ldt_ref[0])
    mag = jnp.exp(ar * dt)
    ab_re = mag * jnp.cos(ai * dt)
    ab_im = mag * jnp.sin(ai * dt)
    den = ar * ar + ai * ai
    n_re = ab_re - 1.0
    f_re = (n_re * ar + ab_im * ai) / den
    f_im = (ab_im * ar - n_re * ai) / den
    br = bre_ref[0]
    bi = bim_ref[0]
    bbre_o[0] = f_re * br - f_im * bi
    bbim_o[0] = f_re * bi + f_im * br
    abre_o[0] = ab_re
    abim_o[0] = ab_im
    cr, ci = ab_re, ab_im
    for t in range(SCAN_LEN):
        pwre_o[0, t] = cr
        pwim_o[0, t] = ci
        cr, ci = cr * ab_re - ci * ab_im, cr * ab_im + ci * ab_re


def _ssm_prep_call(a_re, a_im, log_dt, b_re, b_im):
    depth = a_re.shape[0]
    g, p, c = SSM_GROUPS, SSM_STATE, SSM_GROUP
    a4 = lambda a: a.reshape(depth, g, 1, p)
    spec_a = pl.BlockSpec((1, g, 1, p), lambda l: (l, 0, 0, 0))
    spec_b = pl.BlockSpec((1, g, c, p), lambda l: (l, 0, 0, 0))
    spec_pw = pl.BlockSpec((1, SCAN_LEN, g, 1, p), lambda l: (l, 0, 0, 0, 0))
    return pl.pallas_call(
        _ssm_prep_kernel,
        out_shape=(jax.ShapeDtypeStruct((depth, g, 1, p), F32),) * 2
        + (jax.ShapeDtypeStruct((depth, g, c, p), F32),) * 2
        + (jax.ShapeDtypeStruct((depth, SCAN_LEN, g, 1, p), F32),) * 2,
        grid=(depth,),
        in_specs=[spec_a, spec_a, pl.BlockSpec((1, g, 1, 1), lambda l: (l, 0, 0, 0)), spec_b, spec_b],
        out_specs=(spec_a, spec_a, spec_b, spec_b, spec_pw, spec_pw),
        compiler_params=_params(("arbitrary",)),
        name="ssm_prep",
    )(a4(a_re), a4(a_im), log_dt.reshape(depth, g, 1, 1),
      jnp.swapaxes(b_re, 2, 3), jnp.swapaxes(b_im, 2, 3))


def _ssm_out(y, u, d_skip, w_glu, b_glu):
    z = jax.nn.gelu(y + d_skip * u)
    return z * _sigmoid(_dot(z.astype(BF16), w_glu) + b_glu)


def _lane_lo(shape):
    return (lax.broadcasted_iota(jnp.int32, shape, len(shape) - 1) % LANES) < HEAD_DIM


def _mixer_kernel(x_ref, sh_ref, sc_ref, g_ref, npre_ref, npost_ref, wa_ref, wg_ref,
                  wbu_ref, abre_ref, abim_ref, pwre_ref, pwim_ref, cm_ref, dsk_ref, wglu_ref, bglu_ref,
                  sinks_ref, wpool_ref, pscale_ref, wus_ref, wua_ref, wup_ref, wout_ref,
                  xo_ref, hre_o, him_o, kwin_o, vwin_o, plast_o,
                  hb_s, proj_s, bu_s, hst_s, sg_s, unat_s, uperm_s, operm_s, yssm_s, yatt_s, kbuf, vbuf, pbuf, hc_s,
                  bias_s):
    t = pl.program_id(1)

    @pl.when(t == 0)
    def _():
        hc_s[...] = jnp.zeros_like(hc_s)
        kbuf[0:WINDOW, :] = jnp.zeros((WINDOW, D_KV), BF16)
        vbuf[0:WINDOW, :] = jnp.zeros((WINDOW, D_KV), BF16)
        pbuf[0:POOL_HIST, :] = jnp.zeros((POOL_HIST, D_POOL), F32)

    x = x_ref[0]
    h = _rms(x, npre_ref[...]) * (1.0 + sc_ref[0]) + sh_ref[0]
    hb_s[...] = h.astype(BF16)
    proj_s[...] = _dot(hb_s[...], wa_ref[...])

    for c in range(D_SSM // LANES):
        unat_s[c] = proj_s[:, c * LANES:(c + 1) * LANES]
        for i in range(SCAN_LEN):
            uperm_s[i * SUBLANES:(i + 1) * SUBLANES, c * LANES:(c + 1) * LANES] = (
                unat_s.at[c][pl.ds(i, SUBLANES, stride=SCAN_LEN), :])
    bu_s[...] = _dot(uperm_s[...].astype(BF16), wbu_ref[...])

    abr = jnp.broadcast_to(abre_ref[...], (SUBLANES, D_STATE))
    abi = jnp.broadcast_to(abim_ref[...], (SUBLANES, D_STATE))

    def gate_chunk(c):
        sg_s[c] = _sigmoid(_dot(hb_s[...], wg_ref[c]))

    zero = jnp.zeros((SUBLANES, D_STATE), F32)
    steps_a = SCAN_LEN // GATE_LOOP_A

    def scan_loop(i, carry):
        er, ei = carry
        gate_chunk(i)
        for s in range(steps_a):
            rows = pl.ds(pl.multiple_of((i * steps_a + s) * SUBLANES, SUBLANES), SUBLANES)
            er, ei = (abr * er - abi * ei + bu_s[rows, 0:D_STATE],
                      abr * ei + abi * er + bu_s[rows, D_STATE:2 * D_STATE])
            bu_s[rows, 0:D_STATE] = er
            bu_s[rows, D_STATE:2 * D_STATE] = ei
        return er, ei

    er, ei = zero, zero
    for i in range(GATE_LOOP_A):
        er, ei = scan_loop(i, (er, ei))

    alr = pwre_ref[SCAN_LEN - 1:SCAN_LEN, :]
    ali = pwim_ref[SCAN_LEN - 1:SCAN_LEN, :]
    cr = hc_s[0:1, :]
    ci = hc_s[1:2, :]
    row = lax.broadcasted_iota(jnp.int32, (SUBLANES, D_STATE), 0)
    hin_r = zero
    hin_i = zero
    for j in range(SUBLANES):
        hin_r = jnp.where(row == j, cr, hin_r)
        hin_i = jnp.where(row == j, ci, hin_i)
        cr, ci = (alr * cr - ali * ci + er[j:j + 1, :], alr * ci + ali * cr + ei[j:j + 1, :])
    hc_s[0:1, :] = cr
    hc_s[1:2, :] = ci
    hre_o[0] = cr
    him_o[0] = ci

    pack = 2 * SUBLANES
    packs_b = TT // pack // GATE_LOOP_B

    def fix_loop(i, _):
        gate_chunk(GATE_LOOP_A + i)
        for s in range(packs_b):
            k = i * packs_b + s
            fixed_r, fixed_i = [], []
            for half in range(2):
                slab = 2 * k + half
                rows = pl.ds(pl.multiple_of(slab * SUBLANES, SUBLANES), SUBLANES)
                pr = pwre_ref[pl.ds(slab, 1), :]
                pi = pwim_ref[pl.ds(slab, 1), :]
                fixed_r.append(bu_s[rows, 0:D_STATE] + (pr * hin_r - pi * hin_i))
                fixed_i.append(bu_s[rows, D_STATE:2 * D_STATE] + (pr * hin_i + pi * hin_r))
            prow = pl.ds(pl.multiple_of(k * pack, pack), pack)
            hst_s[prow, 0:D_STATE] = jnp.concatenate(fixed_r, axis=0).astype(BF16)
            hst_s[prow, D_STATE:2 * D_STATE] = jnp.concatenate(fixed_i, axis=0).astype(BF16)
        return 0

    for i in range(GATE_LOOP_B):
        fix_loop(i, 0)

    y = _dot(hst_s[...], cm_ref[...])
    operm_s[...] = _ssm_out(y, uperm_s[...], dsk_ref[...], wglu_ref[...], bglu_ref[...])
    for c in range(D_SSM // LANES):
        for i in range(SCAN_LEN):
            yssm_s.at[c][pl.ds(i, SUBLANES, stride=SCAN_LEN), :] = (
                operm_s[i * SUBLANES:(i + 1) * SUBLANES, c * LANES:(c + 1) * LANES])

    kbuf[WINDOW:WINDOW + TT, :] = proj_s[:, D_SSM + D_ATTN:D_SSM + D_ATTN + D_KV].astype(BF16)
    vbuf[WINDOW:WINDOW + TT, :] = proj_s[:, D_SSM + D_ATTN + D_KV:D_SSM + D_ATTN + 2 * D_KV].astype(BF16)
    lo = _lane_lo((Q_BLK, LANES))
    r_i = lax.broadcasted_iota(jnp.int32, (Q_BLK, 2 * Q_BLK), 0)
    c_i = lax.broadcasted_iota(jnp.int32, (Q_BLK, 2 * Q_BLK), 1)
    @pl.when((pl.program_id(0) == 0) & (t == 0))
    def _():
        dist = r_i - c_i + Q_BLK
        valid = (dist >= 0) & (dist <= WINDOW)
        for hh in range(N_HEADS):
            bias_s[hh] = jnp.where(valid, (-SLOPES[hh] * LOG2E) * dist.astype(F32), NEG_INF)

    first_key = jnp.where(t == 0, Q_BLK, 0)
    for blk in range(TT // Q_BLK):
        r0 = blk * Q_BLK
        kk = kbuf[r0:r0 + 2 * Q_BLK, :]
        vv = vbuf[r0:r0 + 2 * Q_BLK, :]
        pieces = []
        for hh in range(N_HEADS):
            j = hh % GQA
            qp = proj_s[r0:r0 + Q_BLK, D_SSM + j * LANES:D_SSM + (j + 1) * LANES] * (ATTN_SCALE * LOG2E)
            keep = lo if hh < GQA else jnp.logical_not(lo)
            pieces.append(jnp.where(keep, qp, 0.0).astype(BF16))
        s_all = _dot_t(jnp.concatenate(pieces, axis=0), kk)
        outs = []
        for hh in range(N_HEADS):
            s = s_all[hh * Q_BLK:(hh + 1) * Q_BLK, :] + bias_s[hh]
            if blk == 0:
                s = jnp.where(c_i >= first_key, s, NEG_INF)
            snk = sinks_ref[hh] * LOG2E
            m = jnp.maximum(jnp.max(s, axis=-1, keepdims=True), snk)
            p = jnp.exp2(s - m)
            den = jnp.sum(p, axis=-1, keepdims=True) + jnp.exp2(snk - m)
            outs.append(_dot(p.astype(BF16), vv) / den)
        for j in range(GQA):
            yatt_s[r0:r0 + Q_BLK, j * LANES:(j + 1) * LANES] = jnp.where(lo, outs[j], outs[GQA + j]).astype(BF16)
    kwin_o[0] = proj_s[TT - WINDOW:TT, D_SSM + D_ATTN:D_SSM + D_ATTN + D_KV]
    vwin_o[0] = proj_s[TT - WINDOW:TT, D_SSM + D_ATTN + D_KV:D_SSM + D_ATTN + 2 * D_KV]
    kbuf[0:WINDOW, :] = kbuf[TT:TT + WINDOW, :]
    vbuf[0:WINDOW, :] = vbuf[TT:TT + WINDOW, :]

    p0 = D_SMALL - D_POOL
    pbuf[POOL_HIST:POOL_HIST + TT, :] = proj_s[:, p0:D_SMALL]
    plast_o[0] = proj_s[TT - POOL_HIST:TT, p0:D_SMALL]
    pos1 = (t * TT + 1 + lax.broadcasted_iota(jnp.int32, (TT, LANES), 0)).astype(F32)
    lo_t = _lane_lo((TT, LANES))
    mixed = []
    for col, (w_lo, w_hi) in enumerate(((POOL_WINDOWS[0], POOL_WINDOWS[1]), (POOL_WINDOWS[2], POOL_WINDOWS[3]))):
        cs = slice(col * LANES, (col + 1) * LANES)
        pf = pbuf[POOL_HIST:POOL_HIST + TT, cs]
        acc = pf
        for k in range(1, w_lo):
            acc = acc + pbuf[POOL_HIST - k:POOL_HIST - k + TT, cs]
        acc_lo = acc
        for k in range(w_lo, w_hi):
            acc = acc + pbuf[POOL_HIST - k:POOL_HIST - k + TT, cs]
        cnt = jnp.where(lo_t, jnp.minimum(float(w_lo), pos1), jnp.minimum(float(w_hi), pos1))
        mixed.append(jnp.where(lo_t, acc_lo, acc) / cnt - pf)
    ypool = _dot(jnp.concatenate(mixed, axis=1).astype(BF16), wpool_ref[...]) * pscale_ref[...]
    pbuf[0:POOL_HIST, :] = pbuf[TT:TT + POOL_HIST, :]

    per_branch = D_MODEL // GATE_CHUNK
    gate = lambda b: jnp.concatenate([sg_s[b * per_branch + c] for c in range(per_branch)], axis=1)
    yssm = jnp.concatenate([yssm_s[c] for c in range(D_SSM // LANES)], axis=1)
    merged = gate(0) * _dot(yssm.astype(BF16), wus_ref[...])
    merged += gate(1) * _dot(yatt_s[...], wua_ref[...])
    merged += gate(2) * _dot(ypool.astype(BF16), wup_ref[...])
    out = _dot(merged.astype(BF16), wout_ref[...])
    xo_ref[0] = x_ref[0] + g_ref[0] * _rms(out, npost_ref[...])


def _mixer_call(x, mods3, lp):
    b, t_len, d = x.shape
    nt = t_len // TT
    nsamp = mods3.shape[0] - b
    mod_spec = lambda col: pl.BlockSpec((1, 1, D_MODEL), lambda i, j: (nsamp + i, 0, col))
    x_spec = pl.BlockSpec((1, TT, d), lambda i, j: (i, j, 0))
    consts = [lp['norm_mix_pre'], lp['norm_mix_post'], lp['w_in_a'], lp['w_in_g'],
              lp['w_bu'], lp['ab_re'], lp['ab_im'], lp['pw_re'], lp['pw_im'], lp['c_mat'],
              lp['ssm_d'], lp['ssm_w_glu'], lp['ssm_b_glu']]
    consts2 = [lp['pool_wbd'], lp['pool_scale'], lp['w_up_ssm'], lp['w_up_attn'], lp['w_up_pool'], lp['w_out']]
    in_specs = ([x_spec, mod_spec(0), mod_spec(1), mod_spec(2)]
                + [_const_spec(a.shape) for a in consts]
                + [pl.BlockSpec(memory_space=pltpu.SMEM)]
                + [_const_spec(a.shape) for a in consts2])
    per_b = lambda r, c: pl.BlockSpec((1, r, c), lambda i, j: (i, 0, 0))
    out_shape = (jax.ShapeDtypeStruct((b, t_len, d), F32),
                 jax.ShapeDtypeStruct((b, 1, D_STATE), F32), jax.ShapeDtypeStruct((b, 1, D_STATE), F32),
                 jax.ShapeDtypeStruct((b, WINDOW, D_KV), F32), jax.ShapeDtypeStruct((b, WINDOW, D_KV), F32),
                 jax.ShapeDtypeStruct((b, POOL_HIST, D_POOL), F32))
    out_specs = (x_spec, per_b(1, D_STATE), per_b(1, D_STATE), per_b(WINDOW, D_KV), per_b(WINDOW, D_KV),
                 per_b(POOL_HIST, D_POOL))
    scratch = [pltpu.VMEM((TT, d), BF16), pltpu.VMEM((TT, D_SMALL), F32), pltpu.VMEM((TT, 2 * D_STATE), F32),
               pltpu.VMEM((TT, 2 * D_STATE), BF16), pltpu.VMEM((D_GATES // GATE_CHUNK, TT, GATE_CHUNK), F32),
               pltpu.VMEM((D_SSM // LANES, TT, LANES), F32), pltpu.VMEM((TT, D_SSM), F32),
               pltpu.VMEM((TT, D_SSM), F32), pltpu.VMEM((D_SSM // LANES, TT, LANES), F32),
               pltpu.VMEM((TT, D_ATTN), BF16),
               pltpu.VMEM((WINDOW + TT, D_KV), BF16), pltpu.VMEM((WINDOW + TT, D_KV), BF16),
               pltpu.VMEM((POOL_HIST + TT, D_POOL), F32), pltpu.VMEM((SUBLANES, D_STATE), F32),
               pltpu.VMEM((N_HEADS, Q_BLK, 2 * Q_BLK), F32)]
    return pl.pallas_call(
        _mixer_kernel,
        out_shape=out_shape,
        grid=(b, nt),
        in_specs=in_specs,
        out_specs=out_specs,
        scratch_shapes=scratch,
        compiler_params=_params(("arbitrary", "arbitrary")),
        name="prompt_mixer",
    )(x, mods3, mods3, mods3, *consts, lp['attn_sinks'], *consts2)


def _mod_rows(ref, per_row):
    return ref[...] if per_row else ref[0]


def _ffn_kernel(per_row, x_ref, sh_ref, sc_ref, g_ref, npre_ref, npost_ref, wg_ref, wu_ref, wd_ref, o_ref):
    x = x_ref[...]
    h = (_rms(x, npre_ref[...]) * (1.0 + _mod_rows(sc_ref, per_row)) + _mod_rows(sh_ref, per_row)).astype(BF16)
    f = None
    for c0, c1 in FF_CHUNKS:
        act = _silu(_dot(h, wg_ref[:, c0:c1])) * _dot(h, wu_ref[:, c0:c1])
        part = _dot(act.astype(BF16), wd_ref[c0:c1, :])
        f = part if f is None else f + part
    o_ref[...] = x + _mod_rows(g_ref, per_row) * _rms(f, npost_ref[...])


def _mod_specs(mods3, mods2, n_rows, tm, rows_per_seq, per_row):
    nsamp = mods2.shape[0] - (n_rows // rows_per_seq if not per_row else 0)
    if per_row:
        specs = [pl.BlockSpec((tm, D_MODEL), functools.partial(lambda col, i, *_: (i, col), col)) for col in (3, 4, 5)]
        return specs, [mods2] * 3
    tiles_per_seq = rows_per_seq // tm
    specs = [pl.BlockSpec((1, 1, D_MODEL),
                          functools.partial(lambda col, i, *_: (nsamp + i // tiles_per_seq, 0, col), col))
             for col in (3, 4, 5)]
    return specs, [mods3] * 3


def _ffn_call(x2, mods3, mods2, lp, rows_per_seq, per_row):
    n = x2.shape[0]
    tm = n if per_row else FFN_TM
    mspecs, mops = _mod_specs(mods3, mods2, n, tm, rows_per_seq, per_row)
    consts = [lp['norm_ffn_pre'], lp['norm_ffn_post'], lp['ffn_w_gate'], lp['ffn_w_up'], lp['ffn_w_down']]
    x_spec = pl.BlockSpec((tm, D_MODEL), lambda i: (i, 0))
    return pl.pallas_call(
        functools.partial(_ffn_kernel, per_row),
        out_shape=jax.ShapeDtypeStruct(x2.shape, F32),
        grid=(n // tm,),
        in_specs=[x_spec] + mspecs + [_const_spec(a.shape) for a in consts],
        out_specs=x_spec,
        compiler_params=_params(("arbitrary",)),
        name="ffn_rows" if per_row else "ffn_seq",
    )(x2, *mops, *consts)


def _moe_kernel(per_row, x_ref, sh_ref, sc_ref, g_ref, npre_ref, npost_ref, wr_ref, br_ref,
                wg_ref, wu_ref, wd_ref, o_ref, h_s, gates_s, acc_s):
    e = pl.program_id(1)
    lane_e = lax.broadcasted_iota(jnp.int32, gates_s.shape, 1)

    @pl.when(e == 0)
    def _():
        x = x_ref[...]
        h = _rms(x, npre_ref[...]) * (1.0 + _mod_rows(sc_ref, per_row)) + _mod_rows(sh_ref, per_row)
        hb = h.astype(BF16)
        h_s[...] = hb
        logits = _dot(hb, wr_ref[...]) + br_ref[...]
        gates_s[...], _ = _top2_gates(logits, lane_e.astype(F32))
        acc_s[...] = jnp.zeros_like(acc_s)

    h = h_s[...]
    act = _silu(_dot(h, wg_ref[0, 0])) * _dot(h, wu_ref[0, 0])
    y = _dot(act.astype(BF16), wd_ref[0, 0])
    ge = jnp.sum(jnp.where(lane_e == e, gates_s[...], 0.0), axis=-1, keepdims=True)
    acc_s[...] += ge * y

    @pl.when(e == N_EXPERTS - 1)
    def _():
        o_ref[...] = x_ref[...] + _mod_rows(g_ref, per_row) * _rms(acc_s[...], npost_ref[...])


def _moe_call(x2, mods3, mods2, lp, rows_per_seq, per_row):
    n = x2.shape[0]
    tm = n if per_row else MOE_TM
    mspecs, mops = _mod_specs(mods3, mods2, n, tm, rows_per_seq, per_row)
    consts = [lp['norm_ffn_pre'], lp['norm_ffn_post'], lp['moe_w_router'], lp['moe_b_router']]
    x_spec = pl.BlockSpec((tm, D_MODEL), lambda i, e: (i, 0))
    w_spec = lambda a: pl.BlockSpec((1, 1) + a.shape[2:], lambda i, e: (0, e, 0, 0))
    experts = [lp['moe_w_gate'], lp['moe_w_up'], lp['moe_w_down']]
    return pl.pallas_call(
        functools.partial(_moe_kernel, per_row),
        out_shape=jax.ShapeDtypeStruct(x2.shape, F32),
        grid=(n // tm, N_EXPERTS),
        in_specs=[x_spec] + mspecs + [_const_spec(a.shape) for a in consts] + [w_spec(a) for a in experts],
        out_specs=x_spec,
        scratch_shapes=[pltpu.VMEM((tm, D_MODEL), BF16), pltpu.VMEM((tm, N_EXPERTS), F32),
                        pltpu.VMEM((tm, D_MODEL), F32)],
        compiler_params=_params(("arbitrary", "arbitrary")),
        name="moe_rows" if per_row else "moe_seq",
    )(x2, *mops, *consts, *experts)


def _top2_gates(logits, lane_f):
    n_lanes = float(logits.shape[-1])
    v1 = jnp.max(logits, axis=-1, keepdims=True)
    i1 = jnp.min(jnp.where(logits == v1, lane_f, n_lanes), axis=-1, keepdims=True)
    rest = jnp.where(lane_f == i1, -jnp.inf, logits)
    v2 = jnp.max(rest, axis=-1, keepdims=True)
    i2 = jnp.min(jnp.where(rest == v2, lane_f, n_lanes), axis=-1, keepdims=True)
    e2 = jnp.exp(v2 - v1)
    den = 1.0 + e2
    gates = jnp.where(lane_f == i1, 1.0 / den, 0.0) + jnp.where(lane_f == i2, e2 / den, 0.0)
    return gates, (lane_f == i1) | (lane_f == i2)


def _moe_routed_kernel(seq_len, n_rows, x_ref, sh0_ref, sh1_ref, sc0_ref, sc1_ref, g0_ref, g1_ref,
                       npre_ref, npost_ref, wr_ref, br_ref, wg_ref, wu_ref, wd_ref, o_ref,
                       h_s, before_s, dest_s, gate_s, acc_s, cnt_s):
    i = pl.program_id(0)
    e = pl.program_id(1)
    tm = x_ref.shape[0]
    row = lax.broadcasted_iota(jnp.int32, (tm, 1), 0) + i * tm
    valid = row < n_rows
    second = row >= ((i * tm) // seq_len + 1) * seq_len
    pick = lambda a_ref, b_ref: jnp.where(second, b_ref[0], a_ref[0])

    @pl.when((i == 0) & (e == 0))
    def _():
        r_i = lax.broadcasted_iota(jnp.int32, (tm, tm), 0)
        c_i = lax.broadcasted_iota(jnp.int32, (tm, tm), 1)
        before_s[...] = jnp.where(c_i < r_i, 1.0, 0.0).astype(BF16)

    @pl.when(e == 0)
    def _():
        x = jnp.where(valid, x_ref[...], 0.0)
        h = _rms(x, npre_ref[...]) * (1.0 + pick(sc0_ref, sc1_ref)) + pick(sh0_ref, sh1_ref)
        hb = h.astype(BF16)
        h_s[...] = hb
        lane_f = lax.broadcasted_iota(jnp.int32, (tm, LANES), 1).astype(F32)
        logits = jnp.where(lane_f < N_EXPERTS, _dot(hb, wr_ref[...]) + br_ref[...], -jnp.inf)
        gates, sel = _top2_gates(logits, lane_f)
        self = jnp.where(sel & valid, 1.0, 0.0)
        dest = jnp.where(self > 0.0, _dot(before_s[...], self.astype(BF16)), -1.0)
        dest_s[...] = dest.T[0:SUBLANES, :]
        gate_s[...] = gates.T[0:SUBLANES, :]
        counts = jnp.sum(self, axis=0, keepdims=True)
        for j in range(N_EXPERTS):
            cnt_s[j] = counts[0, j].astype(jnp.int32)
        acc_s[...] = jnp.zeros_like(acc_s)

    dest_row = dest_s[pl.ds(e, 1), :]
    gate_row = gate_s[pl.ds(e, 1), :]
    n_chunks = lax.shift_right_logical(cnt_s[e] + (MOE_CAP - 1), MOE_CAP.bit_length() - 1)

    def chunk(c, _):
        slot = lax.broadcasted_iota(jnp.int32, (MOE_CAP, tm), 0).astype(F32) + (c * MOE_CAP).astype(F32)
        onehot = jnp.where(slot == dest_row, 1.0, 0.0)
        gate_packed = jnp.sum(onehot * gate_row, axis=-1, keepdims=True)
        pack = onehot.astype(BF16)
        xg = _dot(pack, h_s[...]).astype(BF16)
        act = _silu(_dot(xg, wg_ref[0, 0])) * _dot(xg, wu_ref[0, 0])
        y = _dot(act.astype(BF16), wd_ref[0, 0]) * gate_packed
        acc_s[...] += lax.dot_general(pack, y.astype(BF16), (((0,), (0,)), ((), ())), preferred_element_type=F32)
        return 0

    lax.fori_loop(0, n_chunks, chunk, 0)

    @pl.when(e == N_EXPERTS - 1)
    def _():
        x = jnp.where(valid, x_ref[...], 0.0)
        o_ref[...] = x + pick(g0_ref, g1_ref) * _rms(acc_s[...], npost_ref[...])


def _moe_routed_call(x2, mods3, lp, seq_len):
    n = x2.shape[0]
    tm = MOE_SP_TM
    n_seq = n // seq_len
    nsamp = mods3.shape[0] - n_seq

    def mod_spec(col, nxt):
        return pl.BlockSpec((1, 1, D_MODEL),
                            lambda i, e: (nsamp + jnp.minimum((i * tm) // seq_len + nxt, n_seq - 1), 0, col))

    mspecs = [mod_spec(col, nxt) for col in (3, 4, 5) for nxt in (0, 1)]
    pad = LANES - N_EXPERTS
    consts = [lp['norm_ffn_pre'], lp['norm_ffn_post'],
              jnp.pad(lp['moe_w_router'], ((0, 0), (0, pad))), jnp.pad(lp['moe_b_router'], ((0, 0), (0, pad)))]
    x_spec = pl.BlockSpec((tm, D_MODEL), lambda i, e: (i, 0))
    w_spec = lambda a: pl.BlockSpec((1, 1) + a.shape[2:], lambda i, e: (0, e, 0, 0))
    experts = [lp['moe_w_gate'], lp['moe_w_up'], lp['moe_w_down']]
    return pl.pallas_call(
        functools.partial(_moe_routed_kernel, seq_len, n),
        out_shape=jax.ShapeDtypeStruct(x2.shape, F32),
        grid=(pl.cdiv(n, tm), N_EXPERTS),
        in_specs=[x_spec] + mspecs + [_const_spec(a.shape) for a in consts] + [w_spec(a) for a in experts],
        out_specs=x_spec,
        scratch_shapes=[pltpu.VMEM((tm, D_MODEL), BF16), pltpu.VMEM((tm, tm), BF16),
                        pltpu.VMEM((SUBLANES, tm), F32), pltpu.VMEM((SUBLANES, tm), F32),
                        pltpu.VMEM((tm, D_MODEL), F32), pltpu.SMEM((N_EXPERTS,), jnp.int32)],
        compiler_params=_params(("arbitrary", "arbitrary")),
        name="moe_routed",
    )(x2, *([mods3] * 6), *consts, *experts)


def _sproj_kernel(x_ref, sh_ref, sc_ref, npre_ref, wa_ref, wg_ref, pa_o, pg_o):
    h = (_rms(x_ref[...], npre_ref[...]) * (1.0 + sc_ref[...]) + sh_ref[...]).astype(BF16)
    pa_o[...] = _dot(h, wa_ref[...])
    for c in range(D_GATES // GATE_CHUNK):
        pg_o[:, c * GATE_CHUNK:(c + 1) * GATE_CHUNK] = _dot(h, wg_ref[c])


def _sproj_call(xs, mods2, lp):
    n = xs.shape[0]
    mod_spec = lambda col: pl.BlockSpec((n, D_MODEL), lambda i: (0, col))
    consts = [lp['norm_mix_pre'], lp['w_in_a'], lp['w_in_g']]
    return pl.pallas_call(
        _sproj_kernel,
        out_shape=(jax.ShapeDtypeStruct((n, D_SMALL), F32), jax.ShapeDtypeStruct((n, D_GATES), F32)),
        grid=(1,),
        in_specs=[_const_spec(xs.shape), mod_spec(0), mod_spec(1)] + [_const_spec(a.shape) for a in consts],
        out_specs=(_const_spec((n, D_SMALL)), _const_spec((n, D_GATES))),
        compiler_params=_params(("arbitrary",)),
        name="sample_proj",
    )(xs, mods2, mods2, *consts)


def _smix_kernel(u_ref, q_ref, kn_ref, vn_ref, pin_ref, h0r_ref, h0i_ref, ck_ref, cv_ref, hist_ref,
                 wbu_ref, abre_ref, abim_ref, cm_ref, dsk_ref, wglu_ref, bglu_ref, sinks_ref,
                 wpool_ref, pscale_ref, yssm_o, yatt_o, ypool_o, hr_o, hi_o):
    u = u_ref[...]
    bu = _dot(u.astype(BF16), wbu_ref[...])
    abr, abi = abre_ref[...], abim_ref[...]
    h0r, h0i = h0r_ref[...], h0i_ref[...]
    hr = bu[:, 0:D_STATE] + (abr * h0r - abi * h0i)
    hi = bu[:, D_STATE:2 * D_STATE] + (abr * h0i + abi * h0r)
    hr_o[...] = hr
    hi_o[...] = hi
    y = _dot(jnp.concatenate([hr, hi], axis=1).astype(BF16), cm_ref[...])
    yssm_o[...] = _ssm_out(y, u, dsk_ref[...], wglu_ref[...], bglu_ref[...])

    q = q_ref[...] * ATTN_SCALE
    kn, vn = kn_ref[...], vn_ref[...]
    kc = ck_ref[...].astype(BF16)
    vc = cv_ref[...].astype(BF16)
    lo = _lane_lo(q.shape)
    j_i = lax.broadcasted_iota(jnp.int32, (1, GQA, 1), 1)
    dist_c = (WINDOW - lax.broadcasted_iota(jnp.int32, (1, 1, WINDOW), 2)).astype(F32)
    halves = []
    for half in range(N_KV_HEADS):
        keep = lo if half == 0 else jnp.logical_not(lo)
        qm = jnp.where(keep, q, 0.0)
        slope = jnp.zeros((1, GQA, 1), F32)
        snk = jnp.zeros((1, GQA, 1), F32)
        for j in range(GQA):
            slope = jnp.where(j_i == j, SLOPES[half * GQA + j], slope)
            snk = jnp.where(j_i == j, sinks_ref[half * GQA + j], snk)
        s = jnp.einsum('bhd,bsd->bhs', qm.astype(BF16), kc, preferred_element_type=F32) - slope * dist_c
        s_new = jnp.sum(qm * kn, axis=-1, keepdims=True)
        m = jnp.maximum(jnp.maximum(jnp.max(s, axis=-1, keepdims=True), s_new), snk)
        p = jnp.exp(s - m)
        p_new = jnp.exp(s_new - m)
        den = jnp.sum(p, axis=-1, keepdims=True) + p_new + jnp.exp(snk - m)
        o = jnp.einsum('bhs,bsd->bhd', p.astype(BF16), vc, preferred_element_type=F32) + p_new * vn
        halves.append(o / den)
    yatt_o[...] = jnp.where(lo, halves[0], halves[1])

    pin = pin_ref[...]
    lo2 = _lane_lo((pin.shape[0], LANES))
    mixed = []
    for col, (w_lo, w_hi) in enumerate(((POOL_WINDOWS[0], POOL_WINDOWS[1]), (POOL_WINDOWS[2], POOL_WINDOWS[3]))):
        cs = slice(col * LANES, (col + 1) * LANES)
        pf = pin[:, cs]
        acc = pf
        for k in range(1, w_lo):
            acc = acc + hist_ref[POOL_BUF - k, :, cs]
        acc_lo = acc
        for k in range(w_lo, w_hi):
            acc = acc + hist_ref[POOL_BUF - k, :, cs]
        cnt_lo = float(min(w_lo, PAST_LEN + 1))
        cnt_hi = float(min(w_hi, PAST_LEN + 1))
        mixed.append(jnp.where(lo2, acc_lo / cnt_lo, acc / cnt_hi) - pf)
    ypool_o[...] = _dot(jnp.concatenate(mixed, axis=1).astype(BF16), wpool_ref[...]) * pscale_ref[...]


def _smix_call(u, q3, kn3, vn3, pin, h0r, h0i, ck, cv, hist_t, lp):
    n = u.shape[0]
    bc = SAMPLE_BC
    rows2 = lambda c: pl.BlockSpec((bc, c), lambda i: (i, 0))
    rows3 = lambda r, c: pl.BlockSpec((bc, r, c), lambda i: (i, 0, 0))
    consts = [lp['w_bu'], lp['ab_re'], lp['ab_im'], lp['c_mat'], lp['ssm_d'], lp['ssm_w_glu'], lp['ssm_b_glu']]
    consts2 = [lp['pool_wbd'], lp['pool_scale']]
    in_specs = ([rows2(D_SSM), rows3(GQA, LANES), rows3(1, LANES), rows3(1, LANES), rows2(D_POOL),
                 rows2(D_STATE), rows2(D_STATE), rows3(WINDOW, D_KV), rows3(WINDOW, D_KV),
                 pl.BlockSpec((POOL_BUF, bc, D_POOL), lambda i: (0, i, 0))]
                + [_const_spec(a.shape) for a in consts]
                + [pl.BlockSpec(memory_space=pltpu.SMEM)]
                + [_const_spec(a.shape) for a in consts2])
    return pl.pallas_call(
        _smix_kernel,
        out_shape=(jax.ShapeDtypeStruct((n, D_SSM), F32), jax.ShapeDtypeStruct((n, GQA, LANES), F32),
                   jax.ShapeDtypeStruct((n, D_POOL), F32),
                   jax.ShapeDtypeStruct((n, D_STATE), F32), jax.ShapeDtypeStruct((n, D_STATE), F32)),
        grid=(n // bc,),
        in_specs=in_specs,
        out_specs=(rows2(D_SSM), rows3(GQA, LANES), rows2(D_POOL), rows2(D_STATE), rows2(D_STATE)),
        compiler_params=_params(("arbitrary",)),
        name="sample_mix",
    )(u, q3, kn3, vn3, pin, h0r, h0i, ck, cv, hist_t, *consts, lp['attn_sinks'], *consts2)


def _smerge_kernel(x_ref, g_ref, pg_ref, yssm_ref, yatt_ref, ypool_ref, npost_ref,
                   wus_ref, wua_ref, wup_ref, wout_ref, o_ref):
    merged = _sigmoid(pg_ref[:, 0:D_MODEL]) * _dot(yssm_ref[...].astype(BF16), wus_ref[...])
    merged += _sigmoid(pg_ref[:, D_MODEL:2 * D_MODEL]) * _dot(yatt_ref[...].astype(BF16), wua_ref[...])
    merged += _sigmoid(pg_ref[:, 2 * D_MODEL:3 * D_MODEL]) * _dot(ypool_ref[...].astype(BF16), wup_ref[...])
    out = _dot(merged.astype(BF16), wout_ref[...])
    o_ref[...] = x_ref[...] + g_ref[...] * _rms(out, npost_ref[...])


def _smerge_call(xs, mods2, pg, yssm, yatt, ypool, lp):
    n = xs.shape[0]
    ops = [xs, mods2, pg, yssm, yatt, ypool, lp['norm_mix_post'],
           lp['w_up_ssm'], lp['w_up_attn'], lp['w_up_pool'], lp['w_out']]
    in_specs = [_const_spec(a.shape) for a in ops]
    in_specs[1] = pl.BlockSpec((n, D_MODEL), lambda i: (0, 2))
    return pl.pallas_call(
        _smerge_kernel,
        out_shape=jax.ShapeDtypeStruct(xs.shape, F32),
        grid=(1,),
        in_specs=in_specs,
        out_specs=_const_spec(xs.shape),
        compiler_params=_params(("arbitrary",)),
        name="sample_merge",
    )(*ops)


def _block_diag(w):
    g, a, b = w.shape
    eye = jnp.eye(g, dtype=w.dtype)
    return jnp.einsum('gab,gh->gahb', w, eye).reshape(g * a, g * b)


def _pair_perm():
    idx = []
    for j in range(GQA):
        idx += list(range(j * HEAD_DIM, (j + 1) * HEAD_DIM))
        idx += list(range((GQA + j) * HEAD_DIM, (GQA + j + 1) * HEAD_DIM))
    return jnp.asarray(idx, dtype=jnp.int32)


def _layer_params(l, p, prep):
    ab_re, ab_im, bb_re, bb_im, pw_re, pw_im = prep
    perm = _pair_perm()
    row = lambda a: a.reshape(1, -1)
    w_in = p['w_in'][l]
    q_cols = w_in[:, D_SSM:D_SSM + D_ATTN][:, perm]
    w_in_a = jnp.concatenate([w_in[:, 0:D_SSM], q_cols, w_in[:, D_SSM + D_ATTN:D_SMALL]], axis=1)
    lp = {
        'norm_mix_pre': row(p['norm_mix_pre'][l]), 'norm_mix_post': row(p['norm_mix_post'][l]),
        'norm_ffn_pre': row(p['norm_ffn_pre'][l]), 'norm_ffn_post': row(p['norm_ffn_post'][l]),
        'w_in_a': w_in_a.astype(BF16),
        'w_in_g': jnp.swapaxes(w_in[:, D_SMALL:].reshape(D_MODEL, D_GATES // GATE_CHUNK, GATE_CHUNK), 0, 1).astype(BF16),
        'w_bu': jnp.concatenate([_block_diag(bb_re[l]), _block_diag(bb_im[l])], axis=1).astype(BF16),
        'ab_re': ab_re[l].reshape(1, D_STATE), 'ab_im': ab_im[l].reshape(1, D_STATE),
        'pw_re': pw_re[l].reshape(SCAN_LEN, D_STATE), 'pw_im': pw_im[l].reshape(SCAN_LEN, D_STATE),
        'c_mat': jnp.concatenate([_block_diag(jnp.swapaxes(p['ssm_c_re'][l], 1, 2)),
                                  _block_diag(jnp.swapaxes(-p['ssm_c_im'][l], 1, 2))], axis=0).astype(BF16),
        'ssm_d': row(p['ssm_d'][l]), 'ssm_w_glu': p['ssm_w_glu'][l].astype(BF16), 'ssm_b_glu': row(p['ssm_b_glu'][l]),
        'attn_sinks': p['attn_sinks'][l],
        'pool_wbd': _block_diag(p['pool_w'][l]).astype(BF16), 'pool_scale': row(p['pool_scale'][l]),
        'w_up_ssm': p['w_up_ssm'][l].astype(BF16), 'w_up_attn': p['w_up_attn'][l][perm, :].astype(BF16),
        'w_up_pool': p['w_up_pool'][l].astype(BF16), 'w_out': p['w_out'][l].astype(BF16),
    }
    j = l // 2
    if l % 2 == 1:
        lp.update({'moe_w_router': p['moe_w_router'][j].astype(BF16), 'moe_b_router': row(p['moe_b_router'][j]),
                   'moe_w_gate': p['moe_w_gate'][j:j + 1].astype(BF16), 'moe_w_up': p['moe_w_up'][j:j + 1].astype(BF16),
                   'moe_w_down': p['moe_w_down'][j:j + 1].astype(BF16)})
    else:
        lp.update({'ffn_w_gate': p['ffn_w_gate'][j].astype(BF16), 'ffn_w_up': p['ffn_w_up'][j].astype(BF16),
                   'ffn_w_down': p['ffn_w_down'][j].astype(BF16)})
    return lp


def kernel(x_prompt, x_sample, state_ssm_re, state_ssm_im, cache_win_k, cache_win_v, state_pool,
           c_prompt, c_sample, w_ada, b_ada, norm_mix_pre, norm_mix_post, norm_ffn_pre, norm_ffn_post,
           w_in, ssm_a_re, ssm_a_im, ssm_log_dt, ssm_b_re, ssm_b_im, ssm_c_re, ssm_c_im, ssm_d,
           ssm_w_glu, ssm_b_glu, attn_sinks, pool_w, pool_scale, w_up_ssm, w_up_attn, w_up_pool, w_out,
           ffn_w_gate, ffn_w_up, ffn_w_down, moe_w_router, moe_b_router, moe_w_gate, moe_w_up, moe_w_down):
    p = dict(norm_mix_pre=norm_mix_pre, norm_mix_post=norm_mix_post, norm_ffn_pre=norm_ffn_pre,
             norm_ffn_post=norm_ffn_post, w_in=w_in, ssm_c_re=ssm_c_re, ssm_c_im=ssm_c_im, ssm_d=ssm_d,
             ssm_w_glu=ssm_w_glu, ssm_b_glu=ssm_b_glu, attn_sinks=attn_sinks, pool_w=pool_w, pool_scale=pool_scale,
             w_up_ssm=w_up_ssm, w_up_attn=w_up_attn, w_up_pool=w_up_pool, w_out=w_out,
             ffn_w_gate=ffn_w_gate, ffn_w_up=ffn_w_up, ffn_w_down=ffn_w_down, moe_w_router=moe_w_router,
             moe_b_router=moe_b_router, moe_w_gate=moe_w_gate, moe_w_up=moe_w_up, moe_w_down=moe_w_down)
    depth = w_in.shape[0]
    b, t_len, d = x_prompt.shape
    ns = x_sample.shape[0]

    mods = _ada_call(jnp.concatenate([c_sample, c_prompt], axis=0), w_ada, b_ada)
    prep = _ssm_prep_call(ssm_a_re, ssm_a_im, ssm_log_dt, ssm_b_re, ssm_b_im)

    xp = x_prompt
    xs = x_sample.reshape(ns, d)
    p_out = ([], [], [], [], [])
    s_out = ([], [], [], [], [])
    for l in range(depth):
        lp = _layer_params(l, p, prep)
        mods2 = mods[l]
        mods3 = mods2.reshape(ns + b, 1, N_MOD * d)

        xp, hre, him, kwin, vwin, plast = _mixer_call(xp, mods3, lp)
        x2 = xp.reshape(b * t_len, d)
        if l % 2 == 1:
            x2 = _moe_routed_call(x2, mods3, lp, t_len)
        else:
            x2 = _ffn_call(x2, mods3, mods2, lp, t_len, False)
        xp = x2.reshape(b, t_len, d)
        p_out[0].append(hre.reshape(b, SSM_GROUPS, SSM_STATE))
        p_out[1].append(him.reshape(b, SSM_GROUPS, SSM_STATE))
        p_out[2].append(kwin.reshape(b, WINDOW, N_KV_HEADS, HEAD_DIM))
        p_out[3].append(vwin.reshape(b, WINDOW, N_KV_HEADS, HEAD_DIM))
        p_out[4].append(plast[:, POOL_HIST - POOL_BUF:, :])

        pa, pg = _sproj_call(xs, mods2, lp)
        u = pa[:, 0:D_SSM]
        q3 = pa[:, D_SSM:D_SSM + D_ATTN].reshape(ns, GQA, LANES)
        k_new = pa[:, D_SSM + D_ATTN:D_SSM + D_ATTN + D_KV]
        v_new = pa[:, D_SSM + D_ATTN + D_KV:D_SSM + D_ATTN + 2 * D_KV]
        pin = pa[:, D_SMALL - D_POOL:]
        ck = cache_win_k[l].reshape(ns, -1, D_KV)
        cv = cache_win_v[l].reshape(ns, -1, D_KV)
        yssm, yatt3, ypool, hr, hi = _smix_call(
            u, q3, k_new.reshape(ns, 1, D_KV), v_new.reshape(ns, 1, D_KV), pin,
            state_ssm_re[l].reshape(ns, D_STATE), state_ssm_im[l].reshape(ns, D_STATE),
            ck, cv, jnp.swapaxes(state_pool[l], 0, 1), lp)
        xs = _smerge_call(xs, mods2, pg, yssm, yatt3.reshape(ns, D_ATTN), ypool, lp)
        xs = (_moe_call if l % 2 == 1 else _ffn_call)(xs, mods3, mods2, lp, 1, True)
        s_out[0].append(hr.reshape(ns, SSM_GROUPS, SSM_STATE))
        s_out[1].append(hi.reshape(ns, SSM_GROUPS, SSM_STATE))
        s_out[2].append(jnp.concatenate([ck[:, 1:], k_new[:, None, :]], axis=1).reshape(cache_win_k.shape[1:]))
        s_out[3].append(jnp.concatenate([cv[:, 1:], v_new[:, None, :]], axis=1).reshape(cache_win_v.shape[1:]))
        s_out[4].append(jnp.concatenate([state_pool[l][:, 1:], pin[:, None, :]], axis=1))

    stack = lambda xs_: [jnp.stack(a, axis=0) for a in xs_]
    p_ssm_re, p_ssm_im, p_win_k, p_win_v, p_pool = stack(p_out)
    s_ssm_re, s_ssm_im, s_win_k, s_win_v, s_pool = stack(s_out)
    return (xp, xs.reshape(x_sample.shape), p_ssm_re, p_ssm_im, p_win_k, p_win_v, p_pool,
            s_ssm_re, s_ssm_im, s_win_k, s_win_v, s_pool)
```

```python
import functools
import math
import types

import jax
import jax.numpy as jnp
from jax import lax
from jax.experimental import pallas as pl
from jax.experimental.pallas import tpu as pltpu

F32 = jnp.float32
BF16 = jnp.bfloat16

D_MODEL = 1024
D_SSM = 256
SSM_GROUP = 16
SSM_GROUPS = 16
SSM_STATE = 64
D_STATE = SSM_GROUPS * SSM_STATE
HEAD_DIM = 64
N_HEADS = 8
N_KV_HEADS = 2
GQA = N_HEADS // N_KV_HEADS
D_ATTN = N_HEADS * HEAD_DIM
D_KV = N_KV_HEADS * HEAD_DIM
WINDOW = 128
ATTN_SCALE = HEAD_DIM ** -0.5
NEG_INF = -1e30
LOG2E = math.log2(math.e)
D_POOL = 256
POOL_WINDOWS = (2, 4, 8, 16)
POOL_BUF = 15
POOL_HIST = 16
N_MOD = 6
D_FF = 2816
N_EXPERTS = 8
D_EXPERT = 1024
RMS_EPS = 1e-6
PAST_LEN = 16384
D_SMALL = D_SSM + D_ATTN + 2 * D_KV + D_POOL
D_GATES = 3 * D_MODEL
SLOPES = tuple(2.0 ** (-8.0 * (h + 1) / N_HEADS) for h in range(N_HEADS))

SUBLANES = 8
LANES = 128
VMEM_LIMIT = 56 * 1024 * 1024

TT = 512
SEQ_PER_STEP = 1
SCAN_LEN = TT // SUBLANES
Q_BLK = 128
GATE_CHUNK = 256
BU_CHUNK = 512
MIXER_LAG = 0
GATE_LOOP_A = 8
GATE_LOOP_B = D_GATES // GATE_CHUNK - GATE_LOOP_A
FFN_TM = 512
MOE_TM = 1024
MOE_SP_TM = 896
MOE_CAP = 256
SAMPLE_BC = 32
ADA_TN = 1536
FF_CHUNKS = ((0, 1024), (1024, 2048), (2048, D_FF))


def _rms(x, gain):
    return x * lax.rsqrt(jnp.mean(x * x, axis=-1, keepdims=True) + RMS_EPS) * gain


def _sigmoid(x):
    return 1.0 / (1.0 + jnp.exp(-x))


def _silu(x):
    return x * _sigmoid(x)


def _dot(a, b):
    return jnp.dot(a, b, preferred_element_type=F32)


def _dot_t(a, b):
    return lax.dot_general(a, b, (((1,), (1,)), ((), ())), preferred_element_type=F32)


def _const_spec(shape):
    nd = len(shape)
    return pl.BlockSpec(shape, lambda *_: (0,) * nd)


def _params(sem):
    return pltpu.CompilerParams(dimension_semantics=sem, vmem_limit_bytes=VMEM_LIMIT)


def _ada_kernel(c_ref, w_ref, b_ref, o_ref):
    c = c_ref[...]
    s = _silu(c).astype(BF16)
    o_ref[0] = _dot(s, w_ref[0].astype(BF16)) + b_ref[0]


def _ada_call(c_all, w_ada, b_ada):
    depth = w_ada.shape[0]
    rows = c_all.shape[0]
    n = N_MOD * D_MODEL
    return pl.pallas_call(
        _ada_kernel,
        out_shape=jax.ShapeDtypeStruct((depth, rows, n), F32),
        grid=(depth, n // ADA_TN),
        in_specs=[
            pl.BlockSpec((rows, D_MODEL), lambda l, j: (0, 0)),
            pl.BlockSpec((1, D_MODEL, ADA_TN), lambda l, j: (l, 0, j)),
            pl.BlockSpec((1, 1, ADA_TN), lambda l, j: (l, 0, j)),
        ],
        out_specs=pl.BlockSpec((1, rows, ADA_TN), lambda l, j: (l, 0, j)),
        compiler_params=_params(("arbitrary", "arbitrary")),
        name="ada_mod",
    )(c_all, w_ada, b_ada.reshape(depth, 1, n))


def _ssm_prep_kernel(are_ref, aim_ref, ldt_ref, bre_ref, bim_ref,
                     abre_o, abim_o, bbre_o, bbim_o, pwre_o, pwim_o):
    ar = are_ref[0]
    ai = aim_ref[0]
    dt = jnp.exp(ldt_ref[0])
    mag = jnp.exp(ar * dt)
    ab_re = mag * jnp.cos(ai * dt)
    ab_im = mag * jnp.sin(ai * dt)
    den = ar * ar + ai * ai
    n_re = ab_re - 1.0
    f_re = (n_re * ar + ab_im * ai) / den
    f_im = (ab_im * ar - n_re * ai) / den
    br = bre_ref[0]
    bi = bim_ref[0]
    bbre_o[0] = f_re * br - f_im * bi
    bbim_o[0] = f_re * bi + f_im * br
    abre_o[0] = ab_re
    abim_o[0] = ab_im
    cr, ci = ab_re, ab_im
    for t in range(SCAN_LEN):
        pwre_o[0, t] = cr
        pwim_o[0, t] = ci
        cr, ci = cr * ab_re - ci * ab_im, cr * ab_im + ci * ab_re


def _ssm_prep_call(a_re, a_im, log_dt, b_re, b_im):
    depth = a_re.shape[0]
    g, p, c = SSM_GROUPS, SSM_STATE, SSM_GROUP
    a4 = lambda a: a.reshape(depth, g, 1, p)
    spec_a = pl.BlockSpec((1, g, 1, p), lambda l: (l, 0, 0, 0))
    spec_b = pl.BlockSpec((1, g, c, p), lambda l: (l, 0, 0, 0))
    spec_pw = pl.BlockSpec((1, SCAN_LEN, g, 1, p), lambda l: (l, 0, 0, 0, 0))
    return pl.pallas_call(
        _ssm_prep_kernel,
        out_shape=(jax.ShapeDtypeStruct((depth, g, 1, p), F32),) * 2
        + (jax.ShapeDtypeStruct((depth, g, c, p), F32),) * 2
        + (jax.ShapeDtypeStruct((depth, SCAN_LEN, g, 1, p), F32),) * 2,
        grid=(depth,),
        in_specs=[spec_a, spec_a, pl.BlockSpec((1, g, 1, 1), lambda l: (l, 0, 0, 0)), spec_b, spec_b],
        out_specs=(spec_a, spec_a, spec_b, spec_b, spec_pw, spec_pw),
        compiler_params=_params(("arbitrary",)),
        name="ssm_prep",
    )(a4(a_re), a4(a_im), log_dt.reshape(depth, g, 1, 1),
      jnp.swapaxes(b_re, 2, 3), jnp.swapaxes(b_im, 2, 3))


def _ssm_out(y, u, d_skip, w_glu, b_glu):
    z = jax.nn.gelu(y + d_skip * u)
    return z * _sigmoid(_dot(z.astype(BF16), w_glu) + b_glu)


def _lane_lo(shape):
    return (lax.broadcasted_iota(jnp.int32, shape, len(shape) - 1) % LANES) < HEAD_DIM


N_MIXER_SEQ_IN = 4
N_MIXER_IN = 24
N_MIXER_OUT = 6


def _mixer_kernel(*refs):
    ins = refs[:N_MIXER_IN]
    outs = refs[N_MIXER_IN:N_MIXER_IN + N_MIXER_OUT]
    scratch = refs[N_MIXER_IN + N_MIXER_OUT:]
    bias_s = scratch[-1]
    t = pl.program_id(1)

    @pl.when((pl.program_id(0) == 0) & (t == 0))
    def _():
        r_i = lax.broadcasted_iota(jnp.int32, (Q_BLK, 2 * Q_BLK), 0)
        c_i = lax.broadcasted_iota(jnp.int32, (Q_BLK, 2 * Q_BLK), 1)
        dist = r_i - c_i + Q_BLK
        valid = (dist >= 0) & (dist <= WINDOW)
        for hh in range(N_HEADS):
            bias_s[hh] = jnp.where(valid, (-SLOPES[hh] * LOG2E) * dist.astype(F32), NEG_INF)

    kbuf, vbuf, pbuf, hc_s = scratch[-5:-1]

    @pl.when(t == 0)
    def _():
        hc_s[...] = jnp.zeros_like(hc_s)
        kbuf[:, 0:WINDOW, :] = jnp.zeros((SEQ_PER_STEP, WINDOW, D_KV), BF16)
        vbuf[:, 0:WINDOW, :] = jnp.zeros((SEQ_PER_STEP, WINDOW, D_KV), BF16)
        pbuf[:, 0:POOL_HIST, :] = jnp.zeros((SEQ_PER_STEP, POOL_HIST, D_POOL), F32)

    chains = []
    for s in range(SEQ_PER_STEP):
        one = functools.partial(lambda s, r: r.at[s:s + 1], s)
        chains.append(_mixer_body(t, *[one(r) for r in ins[:N_MIXER_SEQ_IN]], *ins[N_MIXER_SEQ_IN:],
                                  *[one(r) for r in outs], *[r.at[s] for r in scratch[:-1]], bias_s))
    for _ in range(MIXER_LAG):
        next(chains[0])
    while chains:
        for chain in list(chains):
            if next(chain, StopIteration) is StopIteration:
                chains.remove(chain)


MIXER_REF_NAMES = (
    'x_ref sh_ref sc_ref g_ref npre_ref npost_ref wa_ref wg_ref '
    'wbu_ref abre_ref abim_ref pwre_ref pwim_ref cm_ref dsk_ref wglu_ref bglu_ref '
    'sinks_ref wpool_ref pscale_ref wus_ref wua_ref wup_ref wout_ref '
    'xo_ref hre_o him_o kwin_o vwin_o plast_o '
    'hb_s proj_s bu_s hst_s sg_s unat_s uperm_s operm_s yssm_s yatt_s kbuf vbuf pbuf hc_s bias_s').split()


def _mixer_body(t, *refs):
    r = types.SimpleNamespace(**dict(zip(MIXER_REF_NAMES, refs, strict=True)))
    x = r.x_ref[0]
    h = _rms(x, r.npre_ref[...]) * (1.0 + r.sc_ref[0]) + r.sh_ref[0]
    r.hb_s[...] = h.astype(BF16)
    yield
    for c in range(D_SMALL // GATE_CHUNK):
        cols = slice(c * GATE_CHUNK, (c + 1) * GATE_CHUNK)
        r.proj_s[:, cols] = _dot(r.hb_s[...], r.wa_ref[:, cols])
        yield
    res = {}
    yield from _ssm_chain(r)
    yield from _attn_chain(t, r)
    yield from _pool_chain(t, r, res)
    yield from _merge_tail(r, res)


def _ssm_chain(r):
    (proj_s, hb_s, bu_s, hst_s, sg_s, unat_s, uperm_s, operm_s, yssm_s, hc_s, wg_ref, wbu_ref, abre_ref, abim_ref,
     pwre_ref, pwim_ref, cm_ref, dsk_ref, wglu_ref, bglu_ref, hre_o, him_o) = (
        r.proj_s, r.hb_s, r.bu_s, r.hst_s, r.sg_s, r.unat_s, r.uperm_s, r.operm_s, r.yssm_s, r.hc_s, r.wg_ref,
        r.wbu_ref, r.abre_ref, r.abim_ref, r.pwre_ref, r.pwim_ref, r.cm_ref, r.dsk_ref, r.wglu_ref, r.bglu_ref,
        r.hre_o, r.him_o)
    for c in range(D_SSM // LANES):
        unat_s[c] = proj_s[:, c * LANES:(c + 1) * LANES]
        for i in range(SCAN_LEN):
            uperm_s[i * SUBLANES:(i + 1) * SUBLANES, c * LANES:(c + 1) * LANES] = (
                unat_s.at[c][pl.ds(i, SUBLANES, stride=SCAN_LEN), :])
        yield
    for c in range(2 * D_STATE // BU_CHUNK):
        cols = slice(c * BU_CHUNK, (c + 1) * BU_CHUNK)
        bu_s[:, cols] = _dot(uperm_s[...].astype(BF16), wbu_ref[:, cols])
        yield

    abr = jnp.broadcast_to(abre_ref[...], (SUBLANES, D_STATE))
    abi = jnp.broadcast_to(abim_ref[...], (SUBLANES, D_STATE))

    def gate_chunk(c):
        sg_s[c] = _sigmoid(_dot(hb_s[...], wg_ref[c]))

    zero = jnp.zeros((SUBLANES, D_STATE), F32)
    steps_a = SCAN_LEN // GATE_LOOP_A

    er, ei = zero, zero
    for i in range(GATE_LOOP_A):
        gate_chunk(i)
        yield
        for s in range(steps_a):
            rows = slice((i * steps_a + s) * SUBLANES, (i * steps_a + s + 1) * SUBLANES)
            er, ei = (abr * er - abi * ei + bu_s[rows, 0:D_STATE],
                      abr * ei + abi * er + bu_s[rows, D_STATE:2 * D_STATE])
            bu_s[rows, 0:D_STATE] = er
            bu_s[rows, D_STATE:2 * D_STATE] = ei
        yield

    alr = pwre_ref[SCAN_LEN - 1:SCAN_LEN, :]
    ali = pwim_ref[SCAN_LEN - 1:SCAN_LEN, :]
    cr = hc_s[0:1, :]
    ci = hc_s[1:2, :]
    row = lax.broadcasted_iota(jnp.int32, (SUBLANES, D_STATE), 0)
    hin_r = zero
    hin_i = zero
    for j in range(SUBLANES):
        hin_r = jnp.where(row == j, cr, hin_r)
        hin_i = jnp.where(row == j, ci, hin_i)
        cr, ci = (alr * cr - ali * ci + er[j:j + 1, :], alr * ci + ali * cr + ei[j:j + 1, :])
    hc_s[0:1, :] = cr
    hc_s[1:2, :] = ci
    hre_o[0] = cr
    him_o[0] = ci
    yield

    pack = 2 * SUBLANES
    packs_b = TT // pack // GATE_LOOP_B

    for i in range(GATE_LOOP_B):
        gate_chunk(GATE_LOOP_A + i)
        yield
        for s in range(packs_b):
            k = i * packs_b + s
            fixed_r, fixed_i = [], []
            for slab in (2 * k, 2 * k + 1):
                rows = slice(slab * SUBLANES, (slab + 1) * SUBLANES)
                pr = pwre_ref[slab:slab + 1, :]
                pi = pwim_ref[slab:slab + 1, :]
                fixed_r.append(bu_s[rows, 0:D_STATE] + (pr * hin_r - pi * hin_i))
                fixed_i.append(bu_s[rows, D_STATE:2 * D_STATE] + (pr * hin_i + pi * hin_r))
            prow = slice(k * pack, (k + 1) * pack)
            hst_s[prow, 0:D_STATE] = jnp.concatenate(fixed_r, axis=0).astype(BF16)
            hst_s[prow, D_STATE:2 * D_STATE] = jnp.concatenate(fixed_i, axis=0).astype(BF16)
        yield

    y = _dot(hst_s[...], cm_ref[...])
    yield
    operm_s[...] = _ssm_out(y, uperm_s[...], dsk_ref[...], wglu_ref[...], bglu_ref[...])
    yield
    for c in range(D_SSM // LANES):
        for i in range(SCAN_LEN):
            yssm_s.at[c][pl.ds(i, SUBLANES, stride=SCAN_LEN), :] = (
                operm_s[i * SUBLANES:(i + 1) * SUBLANES, c * LANES:(c + 1) * LANES])
        yield

def _attn_chain(t, r):
    proj_s, kbuf, vbuf, yatt_s, bias_s, sinks_ref, kwin_o, vwin_o = (
        r.proj_s, r.kbuf, r.vbuf, r.yatt_s, r.bias_s, r.sinks_ref, r.kwin_o, r.vwin_o)
    kbuf[WINDOW:WINDOW + TT, :] = proj_s[:, D_SSM + D_ATTN:D_SSM + D_ATTN + D_KV].astype(BF16)
    vbuf[WINDOW:WINDOW + TT, :] = proj_s[:, D_SSM + D_ATTN + D_KV:D_SSM + D_ATTN + 2 * D_KV].astype(BF16)
    lo = _lane_lo((Q_BLK, LANES))
    r_i = lax.broadcasted_iota(jnp.int32, (Q_BLK, 2 * Q_BLK), 0)
    c_i = lax.broadcasted_iota(jnp.int32, (Q_BLK, 2 * Q_BLK), 1)
    first_key = jnp.where(t == 0, Q_BLK, 0)
    for blk in range(TT // Q_BLK):
        r0 = blk * Q_BLK
        kk = kbuf[r0:r0 + 2 * Q_BLK, :]
        vv = vbuf[r0:r0 + 2 * Q_BLK, :]
        pieces = []
        for hh in range(N_HEADS):
            j = hh % GQA
            qp = proj_s[r0:r0 + Q_BLK, D_SSM + j * LANES:D_SSM + (j + 1) * LANES] * (ATTN_SCALE * LOG2E)
            keep = lo if hh < GQA else jnp.logical_not(lo)
            pieces.append(jnp.where(keep, qp, 0.0).astype(BF16))
        s_all = _dot_t(jnp.concatenate(pieces, axis=0), kk)
        yield
        outs = []
        for hh in range(N_HEADS):
            s = s_all[hh * Q_BLK:(hh + 1) * Q_BLK, :] + bias_s[hh]
            if blk == 0:
                s = jnp.where(c_i >= first_key, s, NEG_INF)
            snk = sinks_ref[hh] * LOG2E
            m = jnp.maximum(jnp.max(s, axis=-1, keepdims=True), snk)
            p = jnp.exp2(s - m)
            den = jnp.sum(p, axis=-1, keepdims=True) + jnp.exp2(snk - m)
            outs.append(_dot(p.astype(BF16), vv) / den)
            yield
        for j in range(GQA):
            yatt_s[r0:r0 + Q_BLK, j * LANES:(j + 1) * LANES] = jnp.where(lo, outs[j], outs[GQA + j]).astype(BF16)
    kwin_o[0] = proj_s[TT - WINDOW:TT, D_SSM + D_ATTN:D_SSM + D_ATTN + D_KV]
    vwin_o[0] = proj_s[TT - WINDOW:TT, D_SSM + D_ATTN + D_KV:D_SSM + D_ATTN + 2 * D_KV]
    kbuf[0:WINDOW, :] = kbuf[TT:TT + WINDOW, :]
    vbuf[0:WINDOW, :] = vbuf[TT:TT + WINDOW, :]
    yield

def _pool_chain(t, r, res):
    proj_s, pbuf, plast_o, wpool_ref, pscale_ref = r.proj_s, r.pbuf, r.plast_o, r.wpool_ref, r.pscale_ref
    p0 = D_SMALL - D_POOL
    pbuf[POOL_HIST:POOL_HIST + TT, :] = proj_s[:, p0:D_SMALL]
    plast_o[0] = proj_s[TT - POOL_HIST:TT, p0:D_SMALL]
    pos1 = (t * TT + 1 + lax.broadcasted_iota(jnp.int32, (TT, LANES), 0)).astype(F32)
    lo_t = _lane_lo((TT, LANES))
    mixed = []
    for col, (w_lo, w_hi) in enumerate(((POOL_WINDOWS[0], POOL_WINDOWS[1]), (POOL_WINDOWS[2], POOL_WINDOWS[3]))):
        cs = slice(col * LANES, (col + 1) * LANES)
        pf = pbuf[POOL_HIST:POOL_HIST + TT, cs]
        acc = pf
        for k in range(1, w_lo):
            acc = acc + pbuf[POOL_HIST - k:POOL_HIST - k + TT, cs]
        acc_lo = acc
        for k in range(w_lo, w_hi):
            acc = acc + pbuf[POOL_HIST - k:POOL_HIST - k + TT, cs]
        cnt = jnp.where(lo_t, jnp.minimum(float(w_lo), pos1), jnp.minimum(float(w_hi), pos1))
        mixed.append(jnp.where(lo_t, acc_lo, acc) / cnt - pf)
        yield
    ypool = _dot(jnp.concatenate(mixed, axis=1).astype(BF16), wpool_ref[...]) * pscale_ref[...]
    pbuf[0:POOL_HIST, :] = pbuf[TT:TT + POOL_HIST, :]
    res['ypool'] = ypool
    yield


def _merge_tail(r, res):
    sg_s, yssm_s, yatt_s, wus_ref, wua_ref, wup_ref, wout_ref, x_ref, g_ref, npost_ref, xo_ref = (
        r.sg_s, r.yssm_s, r.yatt_s, r.wus_ref, r.wua_ref, r.wup_ref, r.wout_ref, r.x_ref, r.g_ref, r.npost_ref,
        r.xo_ref)
    ypool = res['ypool']
    per_branch = D_MODEL // GATE_CHUNK
    gate = lambda b: jnp.concatenate([sg_s[b * per_branch + c] for c in range(per_branch)], axis=1)
    yssm = jnp.concatenate([yssm_s[c] for c in range(D_SSM // LANES)], axis=1)
    merged = gate(0) * _dot(yssm.astype(BF16), wus_ref[...])
    yield
    merged += gate(1) * _dot(yatt_s[...], wua_ref[...])
    yield
    merged += gate(2) * _dot(ypool.astype(BF16), wup_ref[...])
    yield
    out = _dot(merged.astype(BF16), wout_ref[...])
    yield
    xo_ref[0] = x_ref[0] + g_ref[0] * _rms(out, npost_ref[...])


def _mixer_call(x, mods3, lp):
    b, t_len, d = x.shape
    nt = t_len // TT
    nsamp = mods3.shape[0] - b
    sps = SEQ_PER_STEP
    mod_spec = lambda col: pl.BlockSpec((sps, 1, D_MODEL), lambda i, j: (nsamp // sps + i, 0, col))
    x_spec = pl.BlockSpec((sps, TT, d), lambda i, j: (i, j, 0))
    consts = [lp['norm_mix_pre'], lp['norm_mix_post'], lp['w_in_a'], lp['w_in_g'],
              lp['w_bu'], lp['ab_re'], lp['ab_im'], lp['pw_re'], lp['pw_im'], lp['c_mat'],
              lp['ssm_d'], lp['ssm_w_glu'], lp['ssm_b_glu']]
    consts2 = [lp['pool_wbd'], lp['pool_scale'], lp['w_up_ssm'], lp['w_up_attn'], lp['w_up_pool'], lp['w_out']]
    in_specs = ([x_spec, mod_spec(0), mod_spec(1), mod_spec(2)]
                + [_const_spec(a.shape) for a in consts]
                + [pl.BlockSpec(memory_space=pltpu.SMEM)]
                + [_const_spec(a.shape) for a in consts2])
    per_b = lambda r, c: pl.BlockSpec((sps, r, c), lambda i, j: (i, 0, 0))
    out_shape = (jax.ShapeDtypeStruct((b, t_len, d), F32),
                 jax.ShapeDtypeStruct((b, 1, D_STATE), F32), jax.ShapeDtypeStruct((b, 1, D_STATE), F32),
                 jax.ShapeDtypeStruct((b, WINDOW, D_KV), F32), jax.ShapeDtypeStruct((b, WINDOW, D_KV), F32),
                 jax.ShapeDtypeStruct((b, POOL_HIST, D_POOL), F32))
    out_specs = (x_spec, per_b(1, D_STATE), per_b(1, D_STATE), per_b(WINDOW, D_KV), per_b(WINDOW, D_KV),
                 per_b(POOL_HIST, D_POOL))
    per_seq = lambda shape, dtype: pltpu.VMEM((sps,) + shape, dtype)
    scratch = [per_seq((TT, d), BF16), per_seq((TT, D_SMALL), F32), per_seq((TT, 2 * D_STATE), F32),
               per_seq((TT, 2 * D_STATE), BF16), per_seq((D_GATES // GATE_CHUNK, TT, GATE_CHUNK), F32),
               per_seq((D_SSM // LANES, TT, LANES), F32), per_seq((TT, D_SSM), F32),
               per_seq((TT, D_SSM), F32), per_seq((D_SSM // LANES, TT, LANES), F32),
               per_seq((TT, D_ATTN), BF16),
               per_seq((WINDOW + TT, D_KV), BF16), per_seq((WINDOW + TT, D_KV), BF16),
               per_seq((POOL_HIST + TT, D_POOL), F32), per_seq((SUBLANES, D_STATE), F32),
               pltpu.VMEM((N_HEADS, Q_BLK, 2 * Q_BLK), F32)]
    return pl.pallas_call(
        _mixer_kernel,
        out_shape=out_shape,
        grid=(b // sps, nt),
        in_specs=in_specs,
        out_specs=out_specs,
        scratch_shapes=scratch,
        compiler_params=_params(("arbitrary", "arbitrary")),
        name="prompt_mixer",
    )(x, mods3, mods3, mods3, *consts, lp['attn_sinks'], *consts2)


def _mod_rows(ref, per_row):
    return ref[...] if per_row else ref[0]


def _ffn_kernel(per_row, x_ref, sh_ref, sc_ref, g_ref, npre_ref, npost_ref, wg_ref, wu_ref, wd_ref, o_ref):
    x = x_ref[...]
    h = (_rms(x, npre_ref[...]) * (1.0 + _mod_rows(sc_ref, per_row)) + _mod_rows(sh_ref, per_row)).astype(BF16)
    f = None
    for c0, c1 in FF_CHUNKS:
        act = _silu(_dot(h, wg_ref[:, c0:c1])) * _dot(h, wu_ref[:, c0:c1])
        part = _dot(act.astype(BF16), wd_ref[c0:c1, :])
        f = part if f is None else f + part
    o_ref[...] = x + _mod_rows(g_ref, per_row) * _rms(f, npost_ref[...])


def _mod_specs(mods3, mods2, n_rows, tm, rows_per_seq, per_row):
    nsamp = mods2.shape[0] - (n_rows // rows_per_seq if not per_row else 0)
    if per_row:
        specs = [pl.BlockSpec((tm, D_MODEL), functools.partial(lambda col, i, *_: (i, col), col)) for col in (3, 4, 5)]
        return specs, [mods2] * 3
    tiles_per_seq = rows_per_seq // tm
    specs = [pl.BlockSpec((1, 1, D_MODEL),
                          functools.partial(lambda col, i, *_: (nsamp + i // tiles_per_seq, 0, col), col))
             for col in (3, 4, 5)]
    return specs, [mods3] * 3


def _ffn_call(x2, mods3, mods2, lp, rows_per_seq, per_row):
    n = x2.shape[0]
    tm = n if per_row else FFN_TM
    mspecs, mops = _mod_specs(mods3, mods2, n, tm, rows_per_seq, per_row)
    consts = [lp['norm_ffn_pre'], lp['norm_ffn_post'], lp['ffn_w_gate'], lp['ffn_w_up'], lp['ffn_w_down']]
    x_spec = pl.BlockSpec((tm, D_MODEL), lambda i: (i, 0))
    return pl.pallas_call(
        functools.partial(_ffn_kernel, per_row),
        out_shape=jax.ShapeDtypeStruct(x2.shape, F32),
        grid=(n // tm,),
        in_specs=[x_spec] + mspecs + [_const_spec(a.shape) for a in consts],
        out_specs=x_spec,
        compiler_params=_params(("arbitrary",)),
        name="ffn_rows" if per_row else "ffn_seq",
    )(x2, *mops, *consts)


def _moe_kernel(per_row, x_ref, sh_ref, sc_ref, g_ref, npre_ref, npost_ref, wr_ref, br_ref,
                wg_ref, wu_ref, wd_ref, o_ref, h_s, gates_s, acc_s):
    e = pl.program_id(1)
    lane_e = lax.broadcasted_iota(jnp.int32, gates_s.shape, 1)

    @pl.when(e == 0)
    def _():
        x = x_ref[...]
        h = _rms(x, npre_ref[...]) * (1.0 + _mod_rows(sc_ref, per_row)) + _mod_rows(sh_ref, per_row)
        hb = h.astype(BF16)
        h_s[...] = hb
        logits = _dot(hb, wr_ref[...]) + br_ref[...]
        gates_s[...], _ = _top2_gates(logits, lane_e.astype(F32))
        acc_s[...] = jnp.zeros_like(acc_s)

    h = h_s[...]
    act = _silu(_dot(h, wg_ref[0, 0])) * _dot(h, wu_ref[0, 0])
    y = _dot(act.astype(BF16), wd_ref[0, 0])
    ge = jnp.sum(jnp.where(lane_e == e, gates_s[...], 0.0), axis=-1, keepdims=True)
    acc_s[...] += ge * y

    @pl.when(e == N_EXPERTS - 1)
    def _():
        o_ref[...] = x_ref[...] + _mod_rows(g_ref, per_row) * _rms(acc_s[...], npost_ref[...])


def _moe_call(x2, mods3, mods2, lp, rows_per_seq, per_row):
    n = x2.shape[0]
    tm = n if per_row else MOE_TM
    mspecs, mops = _mod_specs(mods3, mods2, n, tm, rows_per_seq, per_row)
    consts = [lp['norm_ffn_pre'], lp['norm_ffn_post'], lp['moe_w_router'], lp['moe_b_router']]
    x_spec = pl.BlockSpec((tm, D_MODEL), lambda i, e: (i, 0))
    w_spec = lambda a: pl.BlockSpec((1, 1) + a.shape[2:], lambda i, e: (0, e, 0, 0))
    experts = [lp['moe_w_gate'], lp['moe_w_up'], lp['moe_w_down']]
    return pl.pallas_call(
        functools.partial(_moe_kernel, per_row),
        out_shape=jax.ShapeDtypeStruct(x2.shape, F32),
        grid=(n // tm, N_EXPERTS),
        in_specs=[x_spec] + mspecs + [_const_spec(a.shape) for a in consts] + [w_spec(a) for a in experts],
        out_specs=x_spec,
        scratch_shapes=[pltpu.VMEM((tm, D_MODEL), BF16), pltpu.VMEM((tm, N_EXPERTS), F32),
                        pltpu.VMEM((tm, D_MODEL), F32)],
        compiler_params=_params(("arbitrary", "arbitrary")),
        name="moe_rows" if per_row else "moe_seq",
    )(x2, *mops, *consts, *experts)


def _top2_gates(logits, lane_f):
    n_lanes = float(logits.shape[-1])
    v1 = jnp.max(logits, axis=-1, keepdims=True)
    i1 = jnp.min(jnp.where(logits == v1, lane_f, n_lanes), axis=-1, keepdims=True)
    rest = jnp.where(lane_f == i1, -jnp.inf, logits)
    v2 = jnp.max(rest, axis=-1, keepdims=True)
    i2 = jnp.min(jnp.where(rest == v2, lane_f, n_lanes), axis=-1, keepdims=True)
    e2 = jnp.exp(v2 - v1)
    den = 1.0 + e2
    gates = jnp.where(lane_f == i1, 1.0 / den, 0.0) + jnp.where(lane_f == i2, e2 / den, 0.0)
    return gates, (lane_f == i1) | (lane_f == i2)


def _moe_routed_kernel(seq_len, n_rows, x_ref, sh0_ref, sh1_ref, sc0_ref, sc1_ref, g0_ref, g1_ref,
                       npre_ref, npost_ref, wr_ref, br_ref, wg_ref, wu_ref, wd_ref, o_ref,
                       h_s, before_s, dest_s, gate_s, acc_s, cnt_s):
    i = pl.program_id(0)
    e = pl.program_id(1)
    tm = x_ref.shape[0]
    row = lax.broadcasted_iota(jnp.int32, (tm, 1), 0) + i * tm
    valid = row < n_rows
    second = row >= ((i * tm) // seq_len + 1) * seq_len
    pick = lambda a_ref, b_ref: jnp.where(second, b_ref[0], a_ref[0])

    @pl.when((i == 0) & (e == 0))
    def _():
        r_i = lax.broadcasted_iota(jnp.int32, (tm, tm), 0)
        c_i = lax.broadcasted_iota(jnp.int32, (tm, tm), 1)
        before_s[...] = jnp.where(c_i < r_i, 1.0, 0.0).astype(BF16)

    @pl.when(e == 0)
    def _():
        x = jnp.where(valid, x_ref[...], 0.0)
        h = _rms(x, npre_ref[...]) * (1.0 + pick(sc0_ref, sc1_ref)) + pick(sh0_ref, sh1_ref)
        hb = h.astype(BF16)
        h_s[...] = hb
        lane_f = lax.broadcasted_iota(jnp.int32, (tm, LANES), 1).astype(F32)
        logits = jnp.where(lane_f < N_EXPERTS, _dot(hb, wr_ref[...]) + br_ref[...], -jnp.inf)
        gates, sel = _top2_gates(logits, lane_f)
        self = jnp.where(sel & valid, 1.0, 0.0)
        dest = jnp.where(self > 0.0, _dot(before_s[...], self.astype(BF16)), -1.0)
        dest_s[...] = dest.T[0:SUBLANES, :]
        gate_s[...] = gates.T[0:SUBLANES, :]
        counts = jnp.sum(self, axis=0, keepdims=True)
        for j in range(N_EXPERTS):
            cnt_s[j] = counts[0, j].astype(jnp.int32)
        acc_s[...] = jnp.zeros_like(acc_s)

    dest_row = dest_s[pl.ds(e, 1), :]
    gate_row = gate_s[pl.ds(e, 1), :]
    n_chunks = lax.shift_right_logical(cnt_s[e] + (MOE_CAP - 1), MOE_CAP.bit_length() - 1)

    def chunk(c, _):
        slot = lax.broadcasted_iota(jnp.int32, (MOE_CAP, tm), 0).astype(F32) + (c * MOE_CAP).astype(F32)
        onehot = jnp.where(slot == dest_row, 1.0, 0.0)
        gate_packed = jnp.sum(onehot * gate_row, axis=-1, keepdims=True)
        pack = onehot.astype(BF16)
        xg = _dot(pack, h_s[...]).astype(BF16)
        act = _silu(_dot(xg, wg_ref[0, 0])) * _dot(xg, wu_ref[0, 0])
        y = _dot(act.astype(BF16), wd_ref[0, 0]) * gate_packed
        acc_s[...] += lax.dot_general(pack, y.astype(BF16), (((0,), (0,)), ((), ())), preferred_element_type=F32)
        return 0

    lax.fori_loop(0, n_chunks, chunk, 0)

    @pl.when(e == N_EXPERTS - 1)
    def _():
        x = jnp.where(valid, x_ref[...], 0.0)
        o_ref[...] = x + pick(g0_ref, g1_ref) * _rms(acc_s[...], npost_ref[...])


def _moe_routed_call(x2, mods3, lp, seq_len):
    n = x2.shape[0]
    tm = MOE_SP_TM
    n_seq = n // seq_len
    nsamp = mods3.shape[0] - n_seq

    def mod_spec(col, nxt):
        return pl.BlockSpec((1, 1, D_MODEL),
                            lambda i, e: (nsamp + jnp.minimum((i * tm) // seq_len + nxt, n_seq - 1), 0, col))

    mspecs = [mod_spec(col, nxt) for col in (3, 4, 5) for nxt in (0, 1)]
    pad = LANES - N_EXPERTS
    consts = [lp['norm_ffn_pre'], lp['norm_ffn_post'],
              jnp.pad(lp['moe_w_router'], ((0, 0), (0, pad))), jnp.pad(lp['moe_b_router'], ((0, 0), (0, pad)))]
    x_spec = pl.BlockSpec((tm, D_MODEL), lambda i, e: (i, 0))
    w_spec = lambda a: pl.BlockSpec((1, 1) + a.shape[2:], lambda i, e: (0, e, 0, 0))
    experts = [lp['moe_w_gate'], lp['moe_w_up'], lp['moe_w_down']]
    return pl.pallas_call(
        functools.partial(_moe_routed_kernel, seq_len, n),
        out_shape=jax.ShapeDtypeStruct(x2.shape, F32),
        grid=(pl.cdiv(n, tm), N_EXPERTS),
        in_specs=[x_spec] + mspecs + [_const_spec(a.shape) for a in consts] + [w_spec(a) for a in experts],
        out_specs=x_spec,
        scratch_shapes=[pltpu.VMEM((tm, D_MODEL), BF16), pltpu.VMEM((tm, tm), BF16),
                        pltpu.VMEM((SUBLANES, tm), F32), pltpu.VMEM((SUBLANES, tm), F32),
                        pltpu.VMEM((tm, D_MODEL), F32), pltpu.SMEM((N_EXPERTS,), jnp.int32)],
        compiler_params=_params(("arbitrary", "arbitrary")),
        name="moe_routed",
    )(x2, *([mods3] * 6), *consts, *experts)


def _sproj_kernel(x_ref, sh_ref, sc_ref, npre_ref, wa_ref, wg_ref, pa_o, pg_o):
    h = (_rms(x_ref[...], npre_ref[...]) * (1.0 + sc_ref[...]) + sh_ref[...]).astype(BF16)
    pa_o[...] = _dot(h, wa_ref[...])
    for c in range(D_GATES // GATE_CHUNK):
        pg_o[:, c * GATE_CHUNK:(c + 1) * GATE_CHUNK] = _dot(h, wg_ref[c])


def _sproj_call(xs, mods2, lp):
    n = xs.shape[0]
    mod_spec = lambda col: pl.BlockSpec((n, D_MODEL), lambda i: (0, col))
    consts = [lp['norm_mix_pre'], lp['w_in_a'], lp['w_in_g']]
    return pl.pallas_call(
        _sproj_kernel,
        out_shape=(jax.ShapeDtypeStruct((n, D_SMALL), F32), jax.ShapeDtypeStruct((n, D_GATES), F32)),
        grid=(1,),
        in_specs=[_const_spec(xs.shape), mod_spec(0), mod_spec(1)] + [_const_spec(a.shape) for a in consts],
        out_specs=(_const_spec((n, D_SMALL)), _const_spec((n, D_GATES))),
        compiler_params=_params(("arbitrary",)),
        name="sample_proj",
    )(xs, mods2, mods2, *consts)


def _smix_kernel(u_ref, q_ref, kn_ref, vn_ref, knt_ref, vnt_ref, pin_ref, h0r_ref, h0i_ref, ck_ref, cv_ref, hist_ref,
                 wbu_ref, abre_ref, abim_ref, cm_ref, dsk_ref, wglu_ref, bglu_ref, sinks_ref,
                 wpool_ref, pscale_ref, yssm_o, yatt_o, ypool_o, hr_o, hi_o, ck_o, cv_o, hist_o):
    u = u_ref[...]
    bu = _dot(u.astype(BF16), wbu_ref[...])
    abr, abi = abre_ref[...], abim_ref[...]
    h0r, h0i = h0r_ref[...], h0i_ref[...]
    hr = bu[:, 0:D_STATE] + (abr * h0r - abi * h0i)
    hi = bu[:, D_STATE:2 * D_STATE] + (abr * h0i + abi * h0r)
    hr_o[...] = hr
    hi_o[...] = hi
    y = _dot(jnp.concatenate([hr, hi], axis=1).astype(BF16), cm_ref[...])
    yssm_o[...] = _ssm_out(y, u, dsk_ref[...], wglu_ref[...], bglu_ref[...])

    q = q_ref[...] * ATTN_SCALE
    kn, vn = kn_ref[...], vn_ref[...]
    kc = ck_ref[...].astype(BF16)
    vc = cv_ref[...].astype(BF16)
    lo = _lane_lo(q.shape)
    j_i = lax.broadcasted_iota(jnp.int32, (1, GQA, 1), 1)
    dist_c = (WINDOW - lax.broadcasted_iota(jnp.int32, (1, 1, WINDOW), 2)).astype(F32)
    halves = []
    for half in range(N_KV_HEADS):
        keep = lo if half == 0 else jnp.logical_not(lo)
        qm = jnp.where(keep, q, 0.0)
        slope = jnp.zeros((1, GQA, 1), F32)
        snk = jnp.zeros((1, GQA, 1), F32)
        for j in range(GQA):
            slope = jnp.where(j_i == j, SLOPES[half * GQA + j], slope)
            snk = jnp.where(j_i == j, sinks_ref[half * GQA + j], snk)
        s = jnp.einsum('bhd,bds->bhs', qm.astype(BF16), kc, preferred_element_type=F32) - slope * dist_c
        s_new = jnp.sum(qm * kn, axis=-1, keepdims=True)
        m = jnp.maximum(jnp.maximum(jnp.max(s, axis=-1, keepdims=True), s_new), snk)
        p = jnp.exp(s - m)
        p_new = jnp.exp(s_new - m)
        den = jnp.sum(p, axis=-1, keepdims=True) + p_new + jnp.exp(snk - m)
        o = jnp.einsum('bhs,bds->bhd', p.astype(BF16), vc, preferred_element_type=F32) + p_new * vn
        halves.append(o / den)
    yatt_o[...] = jnp.where(lo, halves[0], halves[1])

    bc = ck_ref.shape[0]
    lane_w = lax.broadcasted_iota(jnp.int32, (D_KV, WINDOW), 1)
    lane_b = lax.broadcasted_iota(jnp.int32, knt_ref.shape, 1)
    for src, new_t, dst in ((ck_ref, knt_ref, ck_o), (cv_ref, vnt_ref, cv_o)):
        for b in range(bc):
            col = jnp.sum(jnp.where(lane_b == pl.program_id(0) * bc + b, new_t[...], 0.0), axis=-1, keepdims=True)
            dst[b] = jnp.where(lane_w == WINDOW - 1, col, pltpu.roll(src[b], WINDOW - 1, axis=1))

    pin = pin_ref[...]
    lo2 = _lane_lo((pin.shape[0], LANES))
    mixed = []
    for col, (w_lo, w_hi) in enumerate(((POOL_WINDOWS[0], POOL_WINDOWS[1]), (POOL_WINDOWS[2], POOL_WINDOWS[3]))):
        cs = slice(col * LANES, (col + 1) * LANES)
        pf = pin[:, cs]
        acc = pf
        for k in range(1, w_lo):
            acc = acc + hist_ref[POOL_BUF - k, :, cs]
        acc_lo = acc
        for k in range(w_lo, w_hi):
            acc = acc + hist_ref[POOL_BUF - k, :, cs]
        cnt_lo = float(min(w_lo, PAST_LEN + 1))
        cnt_hi = float(min(w_hi, PAST_LEN + 1))
        mixed.append(jnp.where(lo2, acc_lo / cnt_lo, acc / cnt_hi) - pf)
    ypool_o[...] = _dot(jnp.concatenate(mixed, axis=1).astype(BF16), wpool_ref[...]) * pscale_ref[...]
    hist_o[0:POOL_BUF - 1] = hist_ref[1:POOL_BUF]
    hist_o[POOL_BUF - 1] = pin


def _smix_call(u, q3, kn3, vn3, knt, vnt, pin, h0r, h0i, ck, cv, hist_t, lp):
    n = u.shape[0]
    bc = SAMPLE_BC
    rows2 = lambda c: pl.BlockSpec((bc, c), lambda i: (i, 0))
    rows3 = lambda r, c: pl.BlockSpec((bc, r, c), lambda i: (i, 0, 0))
    consts = [lp['w_bu'], lp['ab_re'], lp['ab_im'], lp['c_mat'], lp['ssm_d'], lp['ssm_w_glu'], lp['ssm_b_glu']]
    consts2 = [lp['pool_wbd'], lp['pool_scale']]
    hist_spec = pl.BlockSpec((POOL_BUF, bc, D_POOL), lambda i: (0, i, 0))
    in_specs = ([rows2(D_SSM), rows3(GQA, LANES), rows3(1, LANES), rows3(1, LANES),
                 _const_spec(knt.shape), _const_spec(vnt.shape), rows2(D_POOL),
                 rows2(D_STATE), rows2(D_STATE), rows3(D_KV, WINDOW), rows3(D_KV, WINDOW), hist_spec]
                + [_const_spec(a.shape) for a in consts]
                + [pl.BlockSpec(memory_space=pltpu.SMEM)]
                + [_const_spec(a.shape) for a in consts2])
    return pl.pallas_call(
        _smix_kernel,
        out_shape=(jax.ShapeDtypeStruct((n, D_SSM), F32), jax.ShapeDtypeStruct((n, GQA, LANES), F32),
                   jax.ShapeDtypeStruct((n, D_POOL), F32),
                   jax.ShapeDtypeStruct((n, D_STATE), F32), jax.ShapeDtypeStruct((n, D_STATE), F32),
                   jax.ShapeDtypeStruct(ck.shape, F32), jax.ShapeDtypeStruct(cv.shape, F32),
                   jax.ShapeDtypeStruct(hist_t.shape, F32)),
        grid=(n // bc,),
        in_specs=in_specs,
        out_specs=(rows2(D_SSM), rows3(GQA, LANES), rows2(D_POOL), rows2(D_STATE), rows2(D_STATE),
                   rows3(D_KV, WINDOW), rows3(D_KV, WINDOW), hist_spec),
        compiler_params=_params(("arbitrary",)),
        name="sample_mix",
    )(u, q3, kn3, vn3, knt, vnt, pin, h0r, h0i, ck, cv, hist_t, *consts, lp['attn_sinks'], *consts2)


def _smerge_kernel(x_ref, g_ref, pg_ref, yssm_ref, yatt_ref, ypool_ref, npost_ref,
                   wus_ref, wua_ref, wup_ref, wout_ref, o_ref):
    merged = _sigmoid(pg_ref[:, 0:D_MODEL]) * _dot(yssm_ref[...].astype(BF16), wus_ref[...])
    merged += _sigmoid(pg_ref[:, D_MODEL:2 * D_MODEL]) * _dot(yatt_ref[...].astype(BF16), wua_ref[...])
    merged += _sigmoid(pg_ref[:, 2 * D_MODEL:3 * D_MODEL]) * _dot(ypool_ref[...].astype(BF16), wup_ref[...])
    out = _dot(merged.astype(BF16), wout_ref[...])
    o_ref[...] = x_ref[...] + g_ref[...] * _rms(out, npost_ref[...])


def _smerge_call(xs, mods2, pg, yssm, yatt, ypool, lp):
    n = xs.shape[0]
    ops = [xs, mods2, pg, yssm, yatt, ypool, lp['norm_mix_post'],
           lp['w_up_ssm'], lp['w_up_attn'], lp['w_up_pool'], lp['w_out']]
    in_specs = [_const_spec(a.shape) for a in ops]
    in_specs[1] = pl.BlockSpec((n, D_MODEL), lambda i: (0, 2))
    return pl.pallas_call(
        _smerge_kernel,
        out_shape=jax.ShapeDtypeStruct(xs.shape, F32),
        grid=(1,),
        in_specs=in_specs,
        out_specs=_const_spec(xs.shape),
        compiler_params=_params(("arbitrary",)),
        name="sample_merge",
    )(*ops)


def _block_diag(w):
    g, a, b = w.shape
    eye = jnp.eye(g, dtype=w.dtype)
    return jnp.einsum('gab,gh->gahb', w, eye).reshape(g * a, g * b)


def _pair_perm():
    idx = []
    for j in range(GQA):
        idx += list(range(j * HEAD_DIM, (j + 1) * HEAD_DIM))
        idx += list(range((GQA + j) * HEAD_DIM, (GQA + j + 1) * HEAD_DIM))
    return jnp.asarray(idx, dtype=jnp.int32)


def _layer_params(l, p, prep):
    ab_re, ab_im, bb_re, bb_im, pw_re, pw_im = prep
    perm = _pair_perm()
    row = lambda a: a.reshape(1, -1)
    w_in = p['w_in'][l]
    q_cols = w_in[:, D_SSM:D_SSM + D_ATTN][:, perm]
    w_in_a = jnp.concatenate([w_in[:, 0:D_SSM], q_cols, w_in[:, D_SSM + D_ATTN:D_SMALL]], axis=1)
    lp = {
        'norm_mix_pre': row(p['norm_mix_pre'][l]), 'norm_mix_post': row(p['norm_mix_post'][l]),
        'norm_ffn_pre': row(p['norm_ffn_pre'][l]), 'norm_ffn_post': row(p['norm_ffn_post'][l]),
        'w_in_a': w_in_a.astype(BF16),
        'w_in_g': jnp.swapaxes(w_in[:, D_SMALL:].reshape(D_MODEL, D_GATES // GATE_CHUNK, GATE_CHUNK), 0, 1).astype(BF16),
        'w_bu': jnp.concatenate([_block_diag(bb_re[l]), _block_diag(bb_im[l])], axis=1).astype(BF16),
        'ab_re': ab_re[l].reshape(1, D_STATE), 'ab_im': ab_im[l].reshape(1, D_STATE),
        'pw_re': pw_re[l].reshape(SCAN_LEN, D_STATE), 'pw_im': pw_im[l].reshape(SCAN_LEN, D_STATE),
        'c_mat': jnp.concatenate([_block_diag(jnp.swapaxes(p['ssm_c_re'][l], 1, 2)),
                                  _block_diag(jnp.swapaxes(-p['ssm_c_im'][l], 1, 2))], axis=0).astype(BF16),
        'ssm_d': row(p['ssm_d'][l]), 'ssm_w_glu': p['ssm_w_glu'][l].astype(BF16), 'ssm_b_glu': row(p['ssm_b_glu'][l]),
        'attn_sinks': p['attn_sinks'][l],
        'pool_wbd': _block_diag(p['pool_w'][l]).astype(BF16), 'pool_scale': row(p['pool_scale'][l]),
        'w_up_ssm': p['w_up_ssm'][l].astype(BF16), 'w_up_attn': p['w_up_attn'][l][perm, :].astype(BF16),
        'w_up_pool': p['w_up_pool'][l].astype(BF16), 'w_out': p['w_out'][l].astype(BF16),
    }
    j = l // 2
    if l % 2 == 1:
        lp.update({'moe_w_router': p['moe_w_router'][j].astype(BF16), 'moe_b_router': row(p['moe_b_router'][j]),
                   'moe_w_gate': p['moe_w_gate'][j:j + 1].astype(BF16), 'moe_w_up': p['moe_w_up'][j:j + 1].astype(BF16),
                   'moe_w_down': p['moe_w_down'][j:j + 1].astype(BF16)})
    else:
        lp.update({'ffn_w_gate': p['ffn_w_gate'][j].astype(BF16), 'ffn_w_up': p['ffn_w_up'][j].astype(BF16),
                   'ffn_w_down': p['ffn_w_down'][j].astype(BF16)})
    return lp


def kernel(x_prompt, x_sample, state_ssm_re, state_ssm_im, cache_win_k, cache_win_v, state_pool,
           c_prompt, c_sample, w_ada, b_ada, norm_mix_pre, norm_mix_post, norm_ffn_pre, norm_ffn_post,
           w_in, ssm_a_re, ssm_a_im, ssm_log_dt, ssm_b_re, ssm_b_im, ssm_c_re, ssm_c_im, ssm_d,
           ssm_w_glu, ssm_b_glu, attn_sinks, pool_w, pool_scale, w_up_ssm, w_up_attn, w_up_pool, w_out,
           ffn_w_gate, ffn_w_up, ffn_w_down, moe_w_router, moe_b_router, moe_w_gate, moe_w_up, moe_w_down):
    p = dict(norm_mix_pre=norm_mix_pre, norm_mix_post=norm_mix_post, norm_ffn_pre=norm_ffn_pre,
             norm_ffn_post=norm_ffn_post, w_in=w_in, ssm_c_re=ssm_c_re, ssm_c_im=ssm_c_im, ssm_d=ssm_d,
             ssm_w_glu=ssm_w_glu, ssm_b_glu=ssm_b_glu, attn_sinks=attn_sinks, pool_w=pool_w, pool_scale=pool_scale,
             w_up_ssm=w_up_ssm, w_up_attn=w_up_attn, w_up_pool=w_up_pool, w_out=w_out,
             ffn_w_gate=ffn_w_gate, ffn_w_up=ffn_w_up, ffn_w_down=ffn_w_down, moe_w_router=moe_w_router,
             moe_b_router=moe_b_router, moe_w_gate=moe_w_gate, moe_w_up=moe_w_up, moe_w_down=moe_w_down)
    depth = w_in.shape[0]
    b, t_len, d = x_prompt.shape
    ns = x_sample.shape[0]

    mods = _ada_call(jnp.concatenate([c_sample, c_prompt], axis=0), w_ada, b_ada)
    prep = _ssm_prep_call(ssm_a_re, ssm_a_im, ssm_log_dt, ssm_b_re, ssm_b_im)

    xp = x_prompt
    xs = x_sample.reshape(ns, d)
    p_out = ([], [], [], [], [])
    s_out = ([], [], [], [], [])
    for l in range(depth):
        lp = _layer_params(l, p, prep)
        mods2 = mods[l]
        mods3 = mods2.reshape(ns + b, 1, N_MOD * d)

        xp, hre, him, kwin, vwin, plast = _mixer_call(xp, mods3, lp)
        x2 = xp.reshape(b * t_len, d)
        if l % 2 == 1:
            x2 = _moe_routed_call(x2, mods3, lp, t_len)
        else:
            x2 = _ffn_call(x2, mods3, mods2, lp, t_len, False)
        xp = x2.reshape(b, t_len, d)
        p_out[0].append(hre.reshape(b, SSM_GROUPS, SSM_STATE))
        p_out[1].append(him.reshape(b, SSM_GROUPS, SSM_STATE))
        p_out[2].append(kwin.reshape(b, WINDOW, N_KV_HEADS, HEAD_DIM))
        p_out[3].append(vwin.reshape(b, WINDOW, N_KV_HEADS, HEAD_DIM))
        p_out[4].append(plast[:, POOL_HIST - POOL_BUF:, :])

        pa, pg = _sproj_call(xs, mods2, lp)
        u = pa[:, 0:D_SSM]
        q3 = pa[:, D_SSM:D_SSM + D_ATTN].reshape(ns, GQA, LANES)
        k_new = pa[:, D_SSM + D_ATTN:D_SSM + D_ATTN + D_KV]
        v_new = pa[:, D_SSM + D_ATTN + D_KV:D_SSM + D_ATTN + 2 * D_KV]
        pin = pa[:, D_SMALL - D_POOL:]
        win_minor = lambda c: jnp.transpose(c, (0, 2, 3, 1)).reshape(ns, D_KV, -1)
        win_major = lambda c: jnp.transpose(c.reshape(ns, N_KV_HEADS, HEAD_DIM, -1), (0, 3, 1, 2))
        yssm, yatt3, ypool, hr, hi, ck_new, cv_new, hist_new = _smix_call(
            u, q3, k_new.reshape(ns, 1, D_KV), v_new.reshape(ns, 1, D_KV), k_new.T, v_new.T, pin,
            state_ssm_re[l].reshape(ns, D_STATE), state_ssm_im[l].reshape(ns, D_STATE),
            win_minor(cache_win_k[l]), win_minor(cache_win_v[l]), jnp.swapaxes(state_pool[l], 0, 1), lp)
        xs = _smerge_call(xs, mods2, pg, yssm, yatt3.reshape(ns, D_ATTN), ypool, lp)
        xs = (_moe_call if l % 2 == 1 else _ffn_call)(xs, mods3, mods2, lp, 1, True)
        s_out[0].append(hr.reshape(ns, SSM_GROUPS, SSM_STATE))
        s_out[1].append(hi.reshape(ns, SSM_GROUPS, SSM_STATE))
        s_out[2].append(win_major(ck_new))
        s_out[3].append(win_major(cv_new))
        s_out[4].append(jnp.swapaxes(hist_new, 0, 1))

    stack = lambda xs_: [jnp.stack(a, axis=0) for a in xs_]
    p_ssm_re, p_ssm_im, p_win_k, p_win_v, p_pool = stack(p_out)
    s_ssm_re, s_ssm_im, s_win_k, s_win_v, s_pool = stack(s_out)
    return (xp, xs.reshape(x_sample.shape), p_ssm_re, p_ssm_im, p_win_k, p_win_v, p_pool,
            s_ssm_re, s_ssm_im, s_win_k, s_win_v, s_pool)
```

```python
import functools
import math
import types

import jax
import jax.numpy as jnp
from jax import lax
from jax.experimental import pallas as pl
from jax.experimental.pallas import tpu as pltpu

F32 = jnp.float32
BF16 = jnp.bfloat16

D_MODEL = 1024
D_SSM = 256
SSM_GROUP = 16
SSM_GROUPS = 16
SSM_STATE = 64
D_STATE = SSM_GROUPS * SSM_STATE
HEAD_DIM = 64
N_HEADS = 8
N_KV_HEADS = 2
GQA = N_HEADS // N_KV_HEADS
D_ATTN = N_HEADS * HEAD_DIM
D_KV = N_KV_HEADS * HEAD_DIM
WINDOW = 128
ATTN_SCALE = HEAD_DIM ** -0.5
NEG_INF = -1e30
LOG2E = math.log2(math.e)
D_POOL = 256
POOL_WINDOWS = (2, 4, 8, 16)
POOL_BUF = 15
POOL_HIST = 16
N_MOD = 6
D_FF = 2816
N_EXPERTS = 8
D_EXPERT = 1024
RMS_EPS = 1e-6
PAST_LEN = 16384
D_SMALL = D_SSM + D_ATTN + 2 * D_KV + D_POOL
D_GATES = 3 * D_MODEL
SLOPES = tuple(2.0 ** (-8.0 * (h + 1) / N_HEADS) for h in range(N_HEADS))

SUBLANES = 8
LANES = 128
VMEM_LIMIT = 56 * 1024 * 1024

TT = 512
SEQ_PER_STEP = 1
SCAN_LEN = TT // SUBLANES
Q_BLK = 128
GATE_CHUNK = 256
BU_CHUNK = 512
MIXER_LAG = 0
GATE_LOOP_A = 8
GATE_LOOP_B = D_GATES // GATE_CHUNK - GATE_LOOP_A
FFN_TM = 512
MOE_TM = 1024
MOE_SP_TM = 896
MOE_CAP = 256
MOE_TAIL = 128
SAMPLE_BC = 32
ADA_TN = 1536
FF_CHUNKS = ((0, 1024), (1024, 2048), (2048, D_FF))


def _rms(x, gain):
    return x * lax.rsqrt(jnp.mean(x * x, axis=-1, keepdims=True) + RMS_EPS) * gain


def _sigmoid(x):
    return 1.0 / (1.0 + jnp.exp(-x))


def _silu(x):
    return x * _sigmoid(x)


def _dot(a, b):
    return jnp.dot(a, b, preferred_element_type=F32)


def _dot_t(a, b):
    return lax.dot_general(a, b, (((1,), (1,)), ((), ())), preferred_element_type=F32)


def _const_spec(shape):
    nd = len(shape)
    return pl.BlockSpec(shape, lambda *_: (0,) * nd)


def _params(sem):
    return pltpu.CompilerParams(dimension_semantics=sem, vmem_limit_bytes=VMEM_LIMIT)


def _ada_kernel(c_ref, w_ref, b_ref, o_ref):
    c = c_ref[...]
    s = _silu(c).astype(BF16)
    o_ref[0] = _dot(s, w_ref[0].astype(BF16)) + b_ref[0]


def _ada_call(c_all, w_ada, b_ada):
    depth = w_ada.shape[0]
    rows = c_all.shape[0]
    n = N_MOD * D_MODEL
    return pl.pallas_call(
        _ada_kernel,
        out_shape=jax.ShapeDtypeStruct((depth, rows, n), F32),
        grid=(depth, n // ADA_TN),
        in_specs=[
            pl.BlockSpec((rows, D_MODEL), lambda l, j: (0, 0)),
            pl.BlockSpec((1, D_MODEL, ADA_TN), lambda l, j: (l, 0, j)),
            pl.BlockSpec((1, 1, ADA_TN), lambda l, j: (l, 0, j)),
        ],
        out_specs=pl.BlockSpec((1, rows, ADA_TN), lambda l, j: (l, 0, j)),
        compiler_params=_params(("arbitrary", "arbitrary")),
        name="ada_mod",
    )(c_all, w_ada, b_ada.reshape(depth, 1, n))


def _ssm_prep_kernel(are_ref, aim_ref, ldt_ref, bre_ref, bim_ref,
                     abre_o, abim_o, bbre_o, bbim_o, pwre_o, pwim_o):
    ar = are_ref[0]
    ai = aim_ref[0]
    dt = jnp.exp(ldt_ref[0])
    mag = jnp.exp(ar * dt)
    ab_re = mag * jnp.cos(ai * dt)
    ab_im = mag * jnp.sin(ai * dt)
    den = ar * ar + ai * ai
    n_re = ab_re - 1.0
    f_re = (n_re * ar + ab_im * ai) / den
    f_im = (ab_im * ar - n_re * ai) / den
    br = bre_ref[0]
    bi = bim_ref[0]
    bbre_o[0] = f_re * br - f_im * bi
    bbim_o[0] = f_re * bi + f_im * br
    abre_o[0] = ab_re
    abim_o[0] = ab_im
    cr, ci = ab_re, ab_im
    for t in range(SCAN_LEN):
        pwre_o[0, t] = cr
        pwim_o[0, t] = ci
        cr, ci = cr * ab_re - ci * ab_im, cr * ab_im + ci * ab_re


def _ssm_prep_call(a_re, a_im, log_dt, b_re, b_im):
    depth = a_re.shape[0]
    g, p, c = SSM_GROUPS, SSM_STATE, SSM_GROUP
    a4 = lambda a: a.reshape(depth, g, 1, p)
    spec_a = pl.BlockSpec((1, g, 1, p), lambda l: (l, 0, 0, 0))
    spec_b = pl.BlockSpec((1, g, c, p), lambda l: (l, 0, 0, 0))
    spec_pw = pl.BlockSpec((1, SCAN_LEN, g, 1, p), lambda l: (l, 0, 0, 0, 0))
    return pl.pallas_call(
        _ssm_prep_kernel,
        out_shape=(jax.ShapeDtypeStruct((depth, g, 1, p), F32),) * 2
        + (jax.ShapeDtypeStruct((depth, g, c, p), F32),) * 2
        + (jax.ShapeDtypeStruct((depth, SCAN_LEN, g, 1, p), F32),) * 2,
        grid=(depth,),
        in_specs=[spec_a, spec_a, pl.BlockSpec((1, g, 1, 1), lambda l: (l, 0, 0, 0)), spec_b, spec_b],
        out_specs=(spec_a, spec_a, spec_b, spec_b, spec_pw, spec_pw),
        compiler_params=_params(("arbitrary",)),
        name="ssm_prep",
    )(a4(a_re), a4(a_im), log_dt.reshape(depth, g, 1, 1),
      jnp.swapaxes(b_re, 2, 3), jnp.swapaxes(b_im, 2, 3))


def _ssm_out(y, u, d_skip, w_glu, b_glu):
    z = jax.nn.gelu(y + d_skip * u)
    return z * _sigmoid(_dot(z.astype(BF16), w_glu) + b_glu)


def _lane_lo(shape):
    return (lax.broadcasted_iota(jnp.int32, shape, len(shape) - 1) % LANES) < HEAD_DIM


N_MIXER_SEQ_IN = 4
N_MIXER_IN = 24
N_MIXER_OUT = 6


def _mixer_kernel(*refs):
    ins = refs[:N_MIXER_IN]
    outs = refs[N_MIXER_IN:N_MIXER_IN + N_MIXER_OUT]
    scratch = refs[N_MIXER_IN + N_MIXER_OUT:]
    bias_s = scratch[-1]
    t = pl.program_id(1)

    @pl.when((pl.program_id(0) == 0) & (t == 0))
    def _():
        r_i = lax.broadcasted_iota(jnp.int32, (Q_BLK, 2 * Q_BLK), 0)
        c_i = lax.broadcasted_iota(jnp.int32, (Q_BLK, 2 * Q_BLK), 1)
        dist = r_i - c_i + Q_BLK
        valid = (dist >= 0) & (dist <= WINDOW)
        for hh in range(N_HEADS):
            bias_s[hh] = jnp.where(valid, (-SLOPES[hh] * LOG2E) * dist.astype(F32), NEG_INF)

    kbuf, vbuf, pbuf, hc_s = scratch[-5:-1]

    @pl.when(t == 0)
    def _():
        hc_s[...] = jnp.zeros_like(hc_s)
        kbuf[:, 0:WINDOW, :] = jnp.zeros((SEQ_PER_STEP, WINDOW, D_KV), BF16)
        vbuf[:, 0:WINDOW, :] = jnp.zeros((SEQ_PER_STEP, WINDOW, D_KV), BF16)
        pbuf[:, 0:POOL_HIST, :] = jnp.zeros((SEQ_PER_STEP, POOL_HIST, D_POOL), F32)

    chains = []
    for s in range(SEQ_PER_STEP):
        one = functools.partial(lambda s, r: r.at[s:s + 1], s)
        chains.append(_mixer_body(t, *[one(r) for r in ins[:N_MIXER_SEQ_IN]], *ins[N_MIXER_SEQ_IN:],
                                  *[one(r) for r in outs], *[r.at[s] for r in scratch[:-1]], bias_s))
    for _ in range(MIXER_LAG):
        next(chains[0])
    while chains:
        for chain in list(chains):
            if next(chain, StopIteration) is StopIteration:
                chains.remove(chain)


MIXER_REF_NAMES = (
    'x_ref sh_ref sc_ref g_ref npre_ref npost_ref wa_ref wg_ref '
    'wbu_ref abre_ref abim_ref pwre_ref pwim_ref cm_ref dsk_ref wglu_ref bglu_ref '
    'sinks_ref wpool_ref pscale_ref wus_ref wua_ref wup_ref wout_ref '
    'xo_ref hre_o him_o kwin_o vwin_o plast_o '
    'hb_s proj_s bu_s hst_s sg_s unat_s uperm_s operm_s yssm_s yatt_s kbuf vbuf pbuf hc_s bias_s').split()


def _mixer_body(t, *refs):
    r = types.SimpleNamespace(**dict(zip(MIXER_REF_NAMES, refs, strict=True)))
    x = r.x_ref[0]
    h = _rms(x, r.npre_ref[...]) * (1.0 + r.sc_ref[0]) + r.sh_ref[0]
    r.hb_s[...] = h.astype(BF16)
    yield
    for c in range(D_SMALL // GATE_CHUNK):
        cols = slice(c * GATE_CHUNK, (c + 1) * GATE_CHUNK)
        r.proj_s[:, cols] = _dot(r.hb_s[...], r.wa_ref[:, cols])
        yield
    res = {}
    yield from _ssm_chain(r)
    yield from _attn_chain(t, r)
    yield from _pool_chain(t, r, res)
    yield from _merge_tail(r, res)


def _ssm_chain(r):
    (proj_s, hb_s, bu_s, hst_s, sg_s, unat_s, uperm_s, operm_s, yssm_s, hc_s, wg_ref, wbu_ref, abre_ref, abim_ref,
     pwre_ref, pwim_ref, cm_ref, dsk_ref, wglu_ref, bglu_ref, hre_o, him_o) = (
        r.proj_s, r.hb_s, r.bu_s, r.hst_s, r.sg_s, r.unat_s, r.uperm_s, r.operm_s, r.yssm_s, r.hc_s, r.wg_ref,
        r.wbu_ref, r.abre_ref, r.abim_ref, r.pwre_ref, r.pwim_ref, r.cm_ref, r.dsk_ref, r.wglu_ref, r.bglu_ref,
        r.hre_o, r.him_o)
    for c in range(D_SSM // LANES):
        unat_s[c] = proj_s[:, c * LANES:(c + 1) * LANES]
        for i in range(SCAN_LEN):
            uperm_s[i * SUBLANES:(i + 1) * SUBLANES, c * LANES:(c + 1) * LANES] = (
                unat_s.at[c][pl.ds(i, SUBLANES, stride=SCAN_LEN), :])
        yield
    for c in range(2 * D_STATE // BU_CHUNK):
        cols = slice(c * BU_CHUNK, (c + 1) * BU_CHUNK)
        bu_s[:, cols] = _dot(uperm_s[...].astype(BF16), wbu_ref[:, cols])
        yield

    abr = jnp.broadcast_to(abre_ref[...], (SUBLANES, D_STATE))
    abi = jnp.broadcast_to(abim_ref[...], (SUBLANES, D_STATE))

    def gate_chunk(c):
        sg_s[c] = _sigmoid(_dot(hb_s[...], wg_ref[c]))

    zero = jnp.zeros((SUBLANES, D_STATE), F32)
    steps_a = SCAN_LEN // GATE_LOOP_A

    er, ei = zero, zero
    for i in range(GATE_LOOP_A):
        gate_chunk(i)
        yield
        for s in range(steps_a):
            rows = slice((i * steps_a + s) * SUBLANES, (i * steps_a + s + 1) * SUBLANES)
            er, ei = (abr * er - abi * ei + bu_s[rows, 0:D_STATE],
                      abr * ei + abi * er + bu_s[rows, D_STATE:2 * D_STATE])
            bu_s[rows, 0:D_STATE] = er
            bu_s[rows, D_STATE:2 * D_STATE] = ei
        yield

    alr = pwre_ref[SCAN_LEN - 1:SCAN_LEN, :]
    ali = pwim_ref[SCAN_LEN - 1:SCAN_LEN, :]
    cr = hc_s[0:1, :]
    ci = hc_s[1:2, :]
    row = lax.broadcasted_iota(jnp.int32, (SUBLANES, D_STATE), 0)
    hin_r = zero
    hin_i = zero
    for j in range(SUBLANES):
        hin_r = jnp.where(row == j, cr, hin_r)
        hin_i = jnp.where(row == j, ci, hin_i)
        cr, ci = (alr * cr - ali * ci + er[j:j + 1, :], alr * ci + ali * cr + ei[j:j + 1, :])
    hc_s[0:1, :] = cr
    hc_s[1:2, :] = ci
    hre_o[0] = cr
    him_o[0] = ci
    yield

    pack = 2 * SUBLANES
    packs_b = TT // pack // GATE_LOOP_B

    for i in range(GATE_LOOP_B):
        gate_chunk(GATE_LOOP_A + i)
        yield
        for s in range(packs_b):
            k = i * packs_b + s
            fixed_r, fixed_i = [], []
            for slab in (2 * k, 2 * k + 1):
                rows = slice(slab * SUBLANES, (slab + 1) * SUBLANES)
                pr = pwre_ref[slab:slab + 1, :]
                pi = pwim_ref[slab:slab + 1, :]
                fixed_r.append(bu_s[rows, 0:D_STATE] + (pr * hin_r - pi * hin_i))
                fixed_i.append(bu_s[rows, D_STATE:2 * D_STATE] + (pr * hin_i + pi * hin_r))
            prow = slice(k * pack, (k + 1) * pack)
            hst_s[prow, 0:D_STATE] = jnp.concatenate(fixed_r, axis=0).astype(BF16)
            hst_s[prow, D_STATE:2 * D_STATE] = jnp.concatenate(fixed_i, axis=0).astype(BF16)
        yield

    y = _dot(hst_s[...], cm_ref[...])
    yield
    operm_s[...] = _ssm_out(y, uperm_s[...], dsk_ref[...], wglu_ref[...], bglu_ref[...])
    yield
    for c in range(D_SSM // LANES):
        for i in range(SCAN_LEN):
            yssm_s.at[c][pl.ds(i, SUBLANES, stride=SCAN_LEN), :] = (
                operm_s[i * SUBLANES:(i + 1) * SUBLANES, c * LANES:(c + 1) * LANES])
        yield

def _attn_chain(t, r):
    proj_s, kbuf, vbuf, yatt_s, bias_s, sinks_ref, kwin_o, vwin_o = (
        r.proj_s, r.kbuf, r.vbuf, r.yatt_s, r.bias_s, r.sinks_ref, r.kwin_o, r.vwin_o)
    kbuf[WINDOW:WINDOW + TT, :] = proj_s[:, D_SSM + D_ATTN:D_SSM + D_ATTN + D_KV].astype(BF16)
    vbuf[WINDOW:WINDOW + TT, :] = proj_s[:, D_SSM + D_ATTN + D_KV:D_SSM + D_ATTN + 2 * D_KV].astype(BF16)
    lo = _lane_lo((Q_BLK, LANES))
    r_i = lax.broadcasted_iota(jnp.int32, (Q_BLK, 2 * Q_BLK), 0)
    c_i = lax.broadcasted_iota(jnp.int32, (Q_BLK, 2 * Q_BLK), 1)
    first_key = jnp.where(t == 0, Q_BLK, 0)
    for blk in range(TT // Q_BLK):
        r0 = blk * Q_BLK
        kk = kbuf[r0:r0 + 2 * Q_BLK, :]
        vv = vbuf[r0:r0 + 2 * Q_BLK, :]
        pieces = []
        for hh in range(N_HEADS):
            j = hh % GQA
            qp = proj_s[r0:r0 + Q_BLK, D_SSM + j * LANES:D_SSM + (j + 1) * LANES] * (ATTN_SCALE * LOG2E)
            keep = lo if hh < GQA else jnp.logical_not(lo)
            pieces.append(jnp.where(keep, qp, 0.0).astype(BF16))
        s_all = _dot_t(jnp.concatenate(pieces, axis=0), kk)
        yield
        outs = []
        for hh in range(N_HEADS):
            s = s_all[hh * Q_BLK:(hh + 1) * Q_BLK, :] + bias_s[hh]
            if blk == 0:
                s = jnp.where(c_i >= first_key, s, NEG_INF)
            snk = sinks_ref[hh] * LOG2E
            m = jnp.maximum(jnp.max(s, axis=-1, keepdims=True), snk)
            p = jnp.exp2(s - m)
            den = jnp.sum(p, axis=-1, keepdims=True) + jnp.exp2(snk - m)
            outs.append(_dot(p.astype(BF16), vv) / den)
            yield
        for j in range(GQA):
            yatt_s[r0:r0 + Q_BLK, j * LANES:(j + 1) * LANES] = jnp.where(lo, outs[j], outs[GQA + j]).astype(BF16)
    kwin_o[0] = proj_s[TT - WINDOW:TT, D_SSM + D_ATTN:D_SSM + D_ATTN + D_KV]
    vwin_o[0] = proj_s[TT - WINDOW:TT, D_SSM + D_ATTN + D_KV:D_SSM + D_ATTN + 2 * D_KV]
    kbuf[0:WINDOW, :] = kbuf[TT:TT + WINDOW, :]
    vbuf[0:WINDOW, :] = vbuf[TT:TT + WINDOW, :]
    yield

def _pool_chain(t, r, res):
    proj_s, pbuf, plast_o, wpool_ref, pscale_ref = r.proj_s, r.pbuf, r.plast_o, r.wpool_ref, r.pscale_ref
    p0 = D_SMALL - D_POOL
    pbuf[POOL_HIST:POOL_HIST + TT, :] = proj_s[:, p0:D_SMALL]
    plast_o[0] = proj_s[TT - POOL_HIST:TT, p0:D_SMALL]
    pos1 = (t * TT + 1 + lax.broadcasted_iota(jnp.int32, (TT, LANES), 0)).astype(F32)
    lo_t = _lane_lo((TT, LANES))
    mixed = []
    for col, (w_lo, w_hi) in enumerate(((POOL_WINDOWS[0], POOL_WINDOWS[1]), (POOL_WINDOWS[2], POOL_WINDOWS[3]))):
        cs = slice(col * LANES, (col + 1) * LANES)
        pf = pbuf[POOL_HIST:POOL_HIST + TT, cs]
        acc = pf
        for k in range(1, w_lo):
            acc = acc + pbuf[POOL_HIST - k:POOL_HIST - k + TT, cs]
        acc_lo = acc
        for k in range(w_lo, w_hi):
            acc = acc + pbuf[POOL_HIST - k:POOL_HIST - k + TT, cs]
        cnt = jnp.where(lo_t, jnp.minimum(float(w_lo), pos1), jnp.minimum(float(w_hi), pos1))
        mixed.append(jnp.where(lo_t, acc_lo, acc) / cnt - pf)
        yield
    ypool = _dot(jnp.concatenate(mixed, axis=1).astype(BF16), wpool_ref[...]) * pscale_ref[...]
    pbuf[0:POOL_HIST, :] = pbuf[TT:TT + POOL_HIST, :]
    res['ypool'] = ypool
    yield


def _merge_tail(r, res):
    sg_s, yssm_s, yatt_s, wus_ref, wua_ref, wup_ref, wout_ref, x_ref, g_ref, npost_ref, xo_ref = (
        r.sg_s, r.yssm_s, r.yatt_s, r.wus_ref, r.wua_ref, r.wup_ref, r.wout_ref, r.x_ref, r.g_ref, r.npost_ref,
        r.xo_ref)
    ypool = res['ypool']
    per_branch = D_MODEL // GATE_CHUNK
    gate = lambda b: jnp.concatenate([sg_s[b * per_branch + c] for c in range(per_branch)], axis=1)
    yssm = jnp.concatenate([yssm_s[c] for c in range(D_SSM // LANES)], axis=1)
    merged = gate(0) * _dot(yssm.astype(BF16), wus_ref[...])
    yield
    merged += gate(1) * _dot(yatt_s[...], wua_ref[...])
    yield
    merged += gate(2) * _dot(ypool.astype(BF16), wup_ref[...])
    yield
    out = _dot(merged.astype(BF16), wout_ref[...])
    yield
    xo_ref[0] = x_ref[0] + g_ref[0] * _rms(out, npost_ref[...])


def _mixer_call(x, mods3, lp):
    b, t_len, d = x.shape
    nt = t_len // TT
    nsamp = mods3.shape[0] - b
    sps = SEQ_PER_STEP
    mod_spec = lambda col: pl.BlockSpec((sps, 1, D_MODEL), lambda i, j: (nsamp // sps + i, 0, col))
    x_spec = pl.BlockSpec((sps, TT, d), lambda i, j: (i, j, 0))
    consts = [lp['norm_mix_pre'], lp['norm_mix_post'], lp['w_in_a'], lp['w_in_g'],
              lp['w_bu'], lp['ab_re'], lp['ab_im'], lp['pw_re'], lp['pw_im'], lp['c_mat'],
              lp['ssm_d'], lp['ssm_w_glu'], lp['ssm_b_glu']]
    consts2 = [lp['pool_wbd'], lp['pool_scale'], lp['w_up_ssm'], lp['w_up_attn'], lp['w_up_pool'], lp['w_out']]
    in_specs = ([x_spec, mod_spec(0), mod_spec(1), mod_spec(2)]
                + [_const_spec(a.shape) for a in consts]
                + [pl.BlockSpec(memory_space=pltpu.SMEM)]
                + [_const_spec(a.shape) for a in consts2])
    per_b = lambda r, c: pl.BlockSpec((sps, r, c), lambda i, j: (i, 0, 0))
    out_shape = (jax.ShapeDtypeStruct((b, t_len, d), F32),
                 jax.ShapeDtypeStruct((b, 1, D_STATE), F32), jax.ShapeDtypeStruct((b, 1, D_STATE), F32),
                 jax.ShapeDtypeStruct((b, WINDOW, D_KV), F32), jax.ShapeDtypeStruct((b, WINDOW, D_KV), F32),
                 jax.ShapeDtypeStruct((b, POOL_HIST, D_POOL), F32))
    out_specs = (x_spec, per_b(1, D_STATE), per_b(1, D_STATE), per_b(WINDOW, D_KV), per_b(WINDOW, D_KV),
                 per_b(POOL_HIST, D_POOL))
    per_seq = lambda shape, dtype: pltpu.VMEM((sps,) + shape, dtype)
    scratch = [per_seq((TT, d), BF16), per_seq((TT, D_SMALL), F32), per_seq((TT, 2 * D_STATE), F32),
               per_seq((TT, 2 * D_STATE), BF16), per_seq((D_GATES // GATE_CHUNK, TT, GATE_CHUNK), F32),
               per_seq((D_SSM // LANES, TT, LANES), F32), per_seq((TT, D_SSM), F32),
               per_seq((TT, D_SSM), F32), per_seq((D_SSM // LANES, TT, LANES), F32),
               per_seq((TT, D_ATTN), BF16),
               per_seq((WINDOW + TT, D_KV), BF16), per_seq((WINDOW + TT, D_KV), BF16),
               per_seq((POOL_HIST + TT, D_POOL), F32), per_seq((SUBLANES, D_STATE), F32),
               pltpu.VMEM((N_HEADS, Q_BLK, 2 * Q_BLK), F32)]
    return pl.pallas_call(
        _mixer_kernel,
        out_shape=out_shape,
        grid=(b // sps, nt),
        in_specs=in_specs,
        out_specs=out_specs,
        scratch_shapes=scratch,
        compiler_params=_params(("arbitrary", "arbitrary")),
        name="prompt_mixer",
    )(x, mods3, mods3, mods3, *consts, lp['attn_sinks'], *consts2)


def _mod_rows(ref, per_row):
    return ref[...] if per_row else ref[0]


def _ffn_kernel(per_row, x_ref, sh_ref, sc_ref, g_ref, npre_ref, npost_ref, wg_ref, wu_ref, wd_ref, o_ref):
    x = x_ref[...]
    h = (_rms(x, npre_ref[...]) * (1.0 + _mod_rows(sc_ref, per_row)) + _mod_rows(sh_ref, per_row)).astype(BF16)
    f = None
    for c0, c1 in FF_CHUNKS:
        act = _silu(_dot(h, wg_ref[:, c0:c1])) * _dot(h, wu_ref[:, c0:c1])
        part = _dot(act.astype(BF16), wd_ref[c0:c1, :])
        f = part if f is None else f + part
    o_ref[...] = x + _mod_rows(g_ref, per_row) * _rms(f, npost_ref[...])


def _mod_specs(mods3, mods2, n_rows, tm, rows_per_seq, per_row):
    nsamp = mods2.shape[0] - (n_rows // rows_per_seq if not per_row else 0)
    if per_row:
        specs = [pl.BlockSpec((tm, D_MODEL), functools.partial(lambda col, i, *_: (i, col), col)) for col in (3, 4, 5)]
        return specs, [mods2] * 3
    tiles_per_seq = rows_per_seq // tm
    specs = [pl.BlockSpec((1, 1, D_MODEL),
                          functools.partial(lambda col, i, *_: (nsamp + i // tiles_per_seq, 0, col), col))
             for col in (3, 4, 5)]
    return specs, [mods3] * 3


def _ffn_call(x2, mods3, mods2, lp, rows_per_seq, per_row):
    n = x2.shape[0]
    tm = n if per_row else FFN_TM
    mspecs, mops = _mod_specs(mods3, mods2, n, tm, rows_per_seq, per_row)
    consts = [lp['norm_ffn_pre'], lp['norm_ffn_post'], lp['ffn_w_gate'], lp['ffn_w_up'], lp['ffn_w_down']]
    x_spec = pl.BlockSpec((tm, D_MODEL), lambda i: (i, 0))
    return pl.pallas_call(
        functools.partial(_ffn_kernel, per_row),
        out_shape=jax.ShapeDtypeStruct(x2.shape, F32),
        grid=(n // tm,),
        in_specs=[x_spec] + mspecs + [_const_spec(a.shape) for a in consts],
        out_specs=x_spec,
        compiler_params=_params(("arbitrary",)),
        name="ffn_rows" if per_row else "ffn_seq",
    )(x2, *mops, *consts)


def _moe_kernel(per_row, x_ref, sh_ref, sc_ref, g_ref, npre_ref, npost_ref, wr_ref, br_ref,
                wg_ref, wu_ref, wd_ref, o_ref, h_s, gates_s, acc_s):
    e = pl.program_id(1)
    lane_e = lax.broadcasted_iota(jnp.int32, gates_s.shape, 1)

    @pl.when(e == 0)
    def _():
        x = x_ref[...]
        h = _rms(x, npre_ref[...]) * (1.0 + _mod_rows(sc_ref, per_row)) + _mod_rows(sh_ref, per_row)
        hb = h.astype(BF16)
        h_s[...] = hb
        logits = _dot(hb, wr_ref[...]) + br_ref[...]
        gates_s[...], _ = _top2_gates(logits, lane_e.astype(F32))
        acc_s[...] = jnp.zeros_like(acc_s)

    h = h_s[...]
    act = _silu(_dot(h, wg_ref[0, 0])) * _dot(h, wu_ref[0, 0])
    y = _dot(act.astype(BF16), wd_ref[0, 0])
    ge = jnp.sum(jnp.where(lane_e == e, gates_s[...], 0.0), axis=-1, keepdims=True)
    acc_s[...] += ge * y

    @pl.when(e == N_EXPERTS - 1)
    def _():
        o_ref[...] = x_ref[...] + _mod_rows(g_ref, per_row) * _rms(acc_s[...], npost_ref[...])


def _moe_call(x2, mods3, mods2, lp, rows_per_seq, per_row):
    n = x2.shape[0]
    tm = n if per_row else MOE_TM
    mspecs, mops = _mod_specs(mods3, mods2, n, tm, rows_per_seq, per_row)
    consts = [lp['norm_ffn_pre'], lp['norm_ffn_post'], lp['moe_w_router'], lp['moe_b_router']]
    x_spec = pl.BlockSpec((tm, D_MODEL), lambda i, e: (i, 0))
    w_spec = lambda a: pl.BlockSpec((1, 1) + a.shape[2:], lambda i, e: (0, e, 0, 0))
    experts = [lp['moe_w_gate'], lp['moe_w_up'], lp['moe_w_down']]
    return pl.pallas_call(
        functools.partial(_moe_kernel, per_row),
        out_shape=jax.ShapeDtypeStruct(x2.shape, F32),
        grid=(n // tm, N_EXPERTS),
        in_specs=[x_spec] + mspecs + [_const_spec(a.shape) for a in consts] + [w_spec(a) for a in experts],
        out_specs=x_spec,
        scratch_shapes=[pltpu.VMEM((tm, D_MODEL), BF16), pltpu.VMEM((tm, N_EXPERTS), F32),
                        pltpu.VMEM((tm, D_MODEL), F32)],
        compiler_params=_params(("arbitrary", "arbitrary")),
        name="moe_rows" if per_row else "moe_seq",
    )(x2, *mops, *consts, *experts)


def _top2_gates(logits, lane_f):
    n_lanes = float(logits.shape[-1])
    v1 = jnp.max(logits, axis=-1, keepdims=True)
    i1 = jnp.min(jnp.where(logits == v1, lane_f, n_lanes), axis=-1, keepdims=True)
    rest = jnp.where(lane_f == i1, -jnp.inf, logits)
    v2 = jnp.max(rest, axis=-1, keepdims=True)
    i2 = jnp.min(jnp.where(rest == v2, lane_f, n_lanes), axis=-1, keepdims=True)
    e2 = jnp.exp(v2 - v1)
    den = 1.0 + e2
    gates = jnp.where(lane_f == i1, 1.0 / den, 0.0) + jnp.where(lane_f == i2, e2 / den, 0.0)
    return gates, (lane_f == i1) | (lane_f == i2)


def _moe_routed_kernel(seq_len, n_rows, x_ref, sh0_ref, sh1_ref, sc0_ref, sc1_ref, g0_ref, g1_ref,
                       npre_ref, npost_ref, wr_ref, br_ref, wg_ref, wu_ref, wd_ref, o_ref,
                       h_s, before_s, dest_s, gate_s, acc_s, cnt_s):
    i = pl.program_id(0)
    e = pl.program_id(1)
    tm = x_ref.shape[0]
    row = lax.broadcasted_iota(jnp.int32, (tm, 1), 0) + i * tm
    valid = row < n_rows
    second = row >= ((i * tm) // seq_len + 1) * seq_len
    pick = lambda a_ref, b_ref: jnp.where(second, b_ref[0], a_ref[0])

    @pl.when((i == 0) & (e == 0))
    def _():
        r_i = lax.broadcasted_iota(jnp.int32, (tm, tm), 0)
        c_i = lax.broadcasted_iota(jnp.int32, (tm, tm), 1)
        before_s[...] = jnp.where(c_i < r_i, 1.0, 0.0).astype(BF16)

    @pl.when(e == 0)
    def _():
        x = jnp.where(valid, x_ref[...], 0.0)
        h = _rms(x, npre_ref[...]) * (1.0 + pick(sc0_ref, sc1_ref)) + pick(sh0_ref, sh1_ref)
        hb = h.astype(BF16)
        h_s[...] = hb
        lane_f = lax.broadcasted_iota(jnp.int32, (tm, LANES), 1).astype(F32)
        logits = jnp.where(lane_f < N_EXPERTS, _dot(hb, wr_ref[...]) + br_ref[...], -jnp.inf)
        gates, sel = _top2_gates(logits, lane_f)
        self = jnp.where(sel & valid, 1.0, 0.0)
        dest = jnp.where(self > 0.0, _dot(before_s[...], self.astype(BF16)), -1.0)
        dest_s[...] = dest.T[0:SUBLANES, :]
        gate_s[...] = gates.T[0:SUBLANES, :]
        counts = jnp.sum(self, axis=0, keepdims=True)
        for j in range(N_EXPERTS):
            cnt_s[j] = counts[0, j].astype(jnp.int32)
        acc_s[...] = jnp.zeros_like(acc_s)

    dest_row = dest_s[pl.ds(e, 1), :]
    gate_row = gate_s[pl.ds(e, 1), :]
    def chunk(base, size):
        slot = lax.broadcasted_iota(jnp.int32, (size, tm), 0).astype(F32) + base.astype(F32)
        onehot = jnp.where(slot == dest_row, 1.0, 0.0)
        gate_packed = jnp.sum(onehot * gate_row, axis=-1, keepdims=True)
        pack = onehot.astype(BF16)
        xg = _dot(pack, h_s[...]).astype(BF16)
        act = _silu(_dot(xg, wg_ref[0, 0])) * _dot(xg, wu_ref[0, 0])
        y = _dot(act.astype(BF16), wd_ref[0, 0]) * gate_packed
        acc_s[...] += lax.dot_general(pack, y.astype(BF16), (((0,), (0,)), ((), ())), preferred_element_type=F32)

    shift = MOE_CAP.bit_length() - 1
    cnt = cnt_s[e]
    n_full = lax.shift_right_logical(cnt, shift)
    rem = cnt - lax.shift_left(n_full, shift)
    long_last = (n_full >= 1) & (rem >= 1) & (rem <= MOE_TAIL)
    n_plain = jnp.where(long_last, n_full - 1, n_full + jnp.where(rem >= 1, 1, 0))

    def plain_chunk(c, _):
        chunk(c * MOE_CAP, MOE_CAP)
        return 0

    lax.fori_loop(0, n_plain, plain_chunk, 0)

    @pl.when(long_last)
    def _():
        chunk(n_plain * MOE_CAP, MOE_CAP + MOE_TAIL)

    @pl.when(e == N_EXPERTS - 1)
    def _():
        x = jnp.where(valid, x_ref[...], 0.0)
        o_ref[...] = x + pick(g0_ref, g1_ref) * _rms(acc_s[...], npost_ref[...])


def _moe_routed_call(x2, mods3, lp, seq_len):
    n = x2.shape[0]
    tm = MOE_SP_TM
    n_seq = n // seq_len
    nsamp = mods3.shape[0] - n_seq

    def mod_spec(col, nxt):
        return pl.BlockSpec((1, 1, D_MODEL),
                            lambda i, e: (nsamp + jnp.minimum((i * tm) // seq_len + nxt, n_seq - 1), 0, col))

    mspecs = [mod_spec(col, nxt) for col in (3, 4, 5) for nxt in (0, 1)]
    pad = LANES - N_EXPERTS
    consts = [lp['norm_ffn_pre'], lp['norm_ffn_post'],
              jnp.pad(lp['moe_w_router'], ((0, 0), (0, pad))), jnp.pad(lp['moe_b_router'], ((0, 0), (0, pad)))]
    x_spec = pl.BlockSpec((tm, D_MODEL), lambda i, e: (i, 0))
    w_spec = lambda a: pl.BlockSpec((1, 1) + a.shape[2:], lambda i, e: (0, e, 0, 0))
    experts = [lp['moe_w_gate'], lp['moe_w_up'], lp['moe_w_down']]
    return pl.pallas_call(
        functools.partial(_moe_routed_kernel, seq_len, n),
        out_shape=jax.ShapeDtypeStruct(x2.shape, F32),
        grid=(pl.cdiv(n, tm), N_EXPERTS),
        in_specs=[x_spec] + mspecs + [_const_spec(a.shape) for a in consts] + [w_spec(a) for a in experts],
        out_specs=x_spec,
        scratch_shapes=[pltpu.VMEM((tm, D_MODEL), BF16), pltpu.VMEM((tm, tm), BF16),
                        pltpu.VMEM((SUBLANES, tm), F32), pltpu.VMEM((SUBLANES, tm), F32),
                        pltpu.VMEM((tm, D_MODEL), F32), pltpu.SMEM((N_EXPERTS,), jnp.int32)],
        compiler_params=_params(("arbitrary", "arbitrary")),
        name="moe_routed",
    )(x2, *([mods3] * 6), *consts, *experts)


def _sproj_kernel(x_ref, sh_ref, sc_ref, npre_ref, wa_ref, wg_ref, pa_o, pg_o):
    h = (_rms(x_ref[...], npre_ref[...]) * (1.0 + sc_ref[...]) + sh_ref[...]).astype(BF16)
    pa_o[...] = _dot(h, wa_ref[...])
    for c in range(D_GATES // GATE_CHUNK):
        pg_o[:, c * GATE_CHUNK:(c + 1) * GATE_CHUNK] = _dot(h, wg_ref[c])


def _sproj_call(xs, mods2, lp):
    n = xs.shape[0]
    mod_spec = lambda col: pl.BlockSpec((n, D_MODEL), lambda i: (0, col))
    consts = [lp['norm_mix_pre'], lp['w_in_a'], lp['w_in_g']]
    return pl.pallas_call(
        _sproj_kernel,
        out_shape=(jax.ShapeDtypeStruct((n, D_SMALL), F32), jax.ShapeDtypeStruct((n, D_GATES), F32)),
        grid=(1,),
        in_specs=[_const_spec(xs.shape), mod_spec(0), mod_spec(1)] + [_const_spec(a.shape) for a in consts],
        out_specs=(_const_spec((n, D_SMALL)), _const_spec((n, D_GATES))),
        compiler_params=_params(("arbitrary",)),
        name="sample_proj",
    )(xs, mods2, mods2, *consts)


N_SMIX_IN = 22


def _smix_kernel(*refs):
    (u_ref, q_ref, kn_ref, vn_ref, knt_ref, vnt_ref, pin_ref, h0r_ref, h0i_ref, ck_ref, cv_ref, hist_ref,
     wbu_ref, abre_ref, abim_ref, cm_ref, dsk_ref, wglu_ref, bglu_ref, sinks_ref,
     wpool_ref, pscale_ref) = refs[:N_SMIX_IN]
    yssm_o, yatt_o, ypool_o, hr_o, hi_o, ck_o, cv_o, hist_o = refs[-8:]
    u = u_ref[...]
    bu = _dot(u.astype(BF16), wbu_ref[...])
    abr, abi = abre_ref[...], abim_ref[...]
    h0r, h0i = h0r_ref[...], h0i_ref[...]
    hr = bu[:, 0:D_STATE] + (abr * h0r - abi * h0i)
    hi = bu[:, D_STATE:2 * D_STATE] + (abr * h0i + abi * h0r)
    hr_o[...] = hr
    hi_o[...] = hi
    y = _dot(jnp.concatenate([hr, hi], axis=1).astype(BF16), cm_ref[...])
    yssm_o[...] = _ssm_out(y, u, dsk_ref[...], wglu_ref[...], bglu_ref[...])

    q = q_ref[...] * ATTN_SCALE
    kn, vn = kn_ref[...], vn_ref[...]
    kc = ck_ref[...].astype(BF16)
    vc = cv_ref[...].astype(BF16)
    lo = _lane_lo(q.shape)
    j_i = lax.broadcasted_iota(jnp.int32, (1, GQA, 1), 1)
    dist_c = (WINDOW - lax.broadcasted_iota(jnp.int32, (1, 1, WINDOW), 2)).astype(F32)
    halves = []
    for half in range(N_KV_HEADS):
        keep = lo if half == 0 else jnp.logical_not(lo)
        qm = jnp.where(keep, q, 0.0)
        slope = jnp.zeros((1, GQA, 1), F32)
        snk = jnp.zeros((1, GQA, 1), F32)
        for j in range(GQA):
            slope = jnp.where(j_i == j, SLOPES[half * GQA + j], slope)
            snk = jnp.where(j_i == j, sinks_ref[half * GQA + j], snk)
        s = jnp.einsum('bhd,bds->bhs', qm.astype(BF16), kc, preferred_element_type=F32) - slope * dist_c
        s_new = jnp.sum(qm * kn, axis=-1, keepdims=True)
        m = jnp.maximum(jnp.maximum(jnp.max(s, axis=-1, keepdims=True), s_new), snk)
        p = jnp.exp(s - m)
        p_new = jnp.exp(s_new - m)
        den = jnp.sum(p, axis=-1, keepdims=True) + p_new + jnp.exp(snk - m)
        o = jnp.einsum('bhs,bds->bhd', p.astype(BF16), vc, preferred_element_type=F32) + p_new * vn
        halves.append(o / den)
    yatt_o[...] = jnp.where(lo, halves[0], halves[1])

    bc = ck_ref.shape[0]
    lane_w = lax.broadcasted_iota(jnp.int32, (D_KV, WINDOW), 1)
    lane_b = lax.broadcasted_iota(jnp.int32, knt_ref.shape, 1)
    for src, new_t, dst in ((ck_ref, knt_ref, ck_o), (cv_ref, vnt_ref, cv_o)):
        for b in range(bc):
            col = jnp.sum(jnp.where(lane_b == pl.program_id(0) * bc + b, new_t[...], 0.0), axis=-1, keepdims=True)
            dst[b] = jnp.where(lane_w == WINDOW - 1, col, pltpu.roll(src[b], WINDOW - 1, axis=1))

    pin = pin_ref[...]
    lo2 = _lane_lo((pin.shape[0], LANES))
    mixed = []
    for col, (w_lo, w_hi) in enumerate(((POOL_WINDOWS[0], POOL_WINDOWS[1]), (POOL_WINDOWS[2], POOL_WINDOWS[3]))):
        cs = slice(col * LANES, (col + 1) * LANES)
        pf = pin[:, cs]
        acc = pf
        for k in range(1, w_lo):
            acc = acc + hist_ref[POOL_BUF - k, :, cs]
        acc_lo = acc
        for k in range(w_lo, w_hi):
            acc = acc + hist_ref[POOL_BUF - k, :, cs]
        cnt_lo = float(min(w_lo, PAST_LEN + 1))
        cnt_hi = float(min(w_hi, PAST_LEN + 1))
        mixed.append(jnp.where(lo2, acc_lo / cnt_lo, acc / cnt_hi) - pf)
    ypool_o[...] = _dot(jnp.concatenate(mixed, axis=1).astype(BF16), wpool_ref[...]) * pscale_ref[...]
    hist_o[0:POOL_BUF - 1] = hist_ref[1:POOL_BUF]
    hist_o[POOL_BUF - 1] = pin


def _smix_call(layer, u, q3, kn3, vn3, knt, vnt, pin, h0r, h0i, ck, cv, prev_windows, hist_t, lp):
    n = u.shape[0]
    bc = SAMPLE_BC
    rows2 = lambda c: pl.BlockSpec((bc, c), lambda i: (i, 0))
    rows3 = lambda r, c: pl.BlockSpec((bc, r, c), lambda i: (i, 0, 0))
    win_spec = pl.BlockSpec((None, bc, D_KV, WINDOW), lambda i: (layer, i, 0, 0))
    consts = [lp['w_bu'], lp['ab_re'], lp['ab_im'], lp['c_mat'], lp['ssm_d'], lp['ssm_w_glu'], lp['ssm_b_glu']]
    consts2 = [lp['pool_wbd'], lp['pool_scale']]
    hist_spec = pl.BlockSpec((POOL_BUF, bc, D_POOL), lambda i: (0, i, 0))
    in_specs = ([rows2(D_SSM), rows3(GQA, LANES), rows3(1, LANES), rows3(1, LANES),
                 _const_spec(knt.shape), _const_spec(vnt.shape), rows2(D_POOL),
                 rows2(D_STATE), rows2(D_STATE), win_spec, win_spec, hist_spec]
                + [_const_spec(a.shape) for a in consts]
                + [pl.BlockSpec(memory_space=pltpu.SMEM)]
                + [_const_spec(a.shape) for a in consts2]
                + [pl.BlockSpec(memory_space=pl.ANY) for _ in prev_windows])
    assert len(in_specs) == N_SMIX_IN + len(prev_windows)
    aliases = {N_SMIX_IN + k: 5 + k for k in range(len(prev_windows))}
    return pl.pallas_call(
        _smix_kernel,
        out_shape=(jax.ShapeDtypeStruct((n, D_SSM), F32), jax.ShapeDtypeStruct((n, GQA, LANES), F32),
                   jax.ShapeDtypeStruct((n, D_POOL), F32),
                   jax.ShapeDtypeStruct((n, D_STATE), F32), jax.ShapeDtypeStruct((n, D_STATE), F32),
                   jax.ShapeDtypeStruct(ck.shape, F32), jax.ShapeDtypeStruct(cv.shape, F32),
                   jax.ShapeDtypeStruct(hist_t.shape, F32)),
        grid=(n // bc,),
        in_specs=in_specs,
        out_specs=(rows2(D_SSM), rows3(GQA, LANES), rows2(D_POOL), rows2(D_STATE), rows2(D_STATE),
                   win_spec, win_spec, hist_spec),
        input_output_aliases=aliases,
        compiler_params=_params(("arbitrary",)),
        name="sample_mix",
    )(u, q3, kn3, vn3, knt, vnt, pin, h0r, h0i, ck, cv, hist_t, *consts, lp['attn_sinks'], *consts2, *prev_windows)


def _smerge_kernel(x_ref, g_ref, pg_ref, yssm_ref, yatt_ref, ypool_ref, npost_ref,
                   wus_ref, wua_ref, wup_ref, wout_ref, o_ref):
    merged = _sigmoid(pg_ref[:, 0:D_MODEL]) * _dot(yssm_ref[...].astype(BF16), wus_ref[...])
    merged += _sigmoid(pg_ref[:, D_MODEL:2 * D_MODEL]) * _dot(yatt_ref[...].astype(BF16), wua_ref[...])
    merged += _sigmoid(pg_ref[:, 2 * D_MODEL:3 * D_MODEL]) * _dot(ypool_ref[...].astype(BF16), wup_ref[...])
    out = _dot(merged.astype(BF16), wout_ref[...])
    o_ref[...] = x_ref[...] + g_ref[...] * _rms(out, npost_ref[...])


def _smerge_call(xs, mods2, pg, yssm, yatt, ypool, lp):
    n = xs.shape[0]
    ops = [xs, mods2, pg, yssm, yatt, ypool, lp['norm_mix_post'],
           lp['w_up_ssm'], lp['w_up_attn'], lp['w_up_pool'], lp['w_out']]
    in_specs = [_const_spec(a.shape) for a in ops]
    in_specs[1] = pl.BlockSpec((n, D_MODEL), lambda i: (0, 2))
    return pl.pallas_call(
        _smerge_kernel,
        out_shape=jax.ShapeDtypeStruct(xs.shape, F32),
        grid=(1,),
        in_specs=in_specs,
        out_specs=_const_spec(xs.shape),
        compiler_params=_params(("arbitrary",)),
        name="sample_merge",
    )(*ops)


def _block_diag(w):
    g, a, b = w.shape
    eye = jnp.eye(g, dtype=w.dtype)
    return jnp.einsum('gab,gh->gahb', w, eye).reshape(g * a, g * b)


def _pair_perm():
    idx = []
    for j in range(GQA):
        idx += list(range(j * HEAD_DIM, (j + 1) * HEAD_DIM))
        idx += list(range((GQA + j) * HEAD_DIM, (GQA + j + 1) * HEAD_DIM))
    return jnp.asarray(idx, dtype=jnp.int32)


def _layer_params(l, p, prep):
    ab_re, ab_im, bb_re, bb_im, pw_re, pw_im = prep
    perm = _pair_perm()
    row = lambda a: a.reshape(1, -1)
    w_in = p['w_in'][l]
    q_cols = w_in[:, D_SSM:D_SSM + D_ATTN][:, perm]
    w_in_a = jnp.concatenate([w_in[:, 0:D_SSM], q_cols, w_in[:, D_SSM + D_ATTN:D_SMALL]], axis=1)
    lp = {
        'norm_mix_pre': row(p['norm_mix_pre'][l]), 'norm_mix_post': row(p['norm_mix_post'][l]),
        'norm_ffn_pre': row(p['norm_ffn_pre'][l]), 'norm_ffn_post': row(p['norm_ffn_post'][l]),
        'w_in_a': w_in_a.astype(BF16),
        'w_in_g': jnp.swapaxes(w_in[:, D_SMALL:].reshape(D_MODEL, D_GATES // GATE_CHUNK, GATE_CHUNK), 0, 1).astype(BF16),
        'w_bu': jnp.concatenate([_block_diag(bb_re[l]), _block_diag(bb_im[l])], axis=1).astype(BF16),
        'ab_re': ab_re[l].reshape(1, D_STATE), 'ab_im': ab_im[l].reshape(1, D_STATE),
        'pw_re': pw_re[l].reshape(SCAN_LEN, D_STATE), 'pw_im': pw_im[l].reshape(SCAN_LEN, D_STATE),
        'c_mat': jnp.concatenate([_block_diag(jnp.swapaxes(p['ssm_c_re'][l], 1, 2)),
                                  _block_diag(jnp.swapaxes(-p['ssm_c_im'][l], 1, 2))], axis=0).astype(BF16),
        'ssm_d': row(p['ssm_d'][l]), 'ssm_w_glu': p['ssm_w_glu'][l].astype(BF16), 'ssm_b_glu': row(p['ssm_b_glu'][l]),
        'attn_sinks': p['attn_sinks'][l],
        'pool_wbd': _block_diag(p['pool_w'][l]).astype(BF16), 'pool_scale': row(p['pool_scale'][l]),
        'w_up_ssm': p['w_up_ssm'][l].astype(BF16), 'w_up_attn': p['w_up_attn'][l][perm, :].astype(BF16),
        'w_up_pool': p['w_up_pool'][l].astype(BF16), 'w_out': p['w_out'][l].astype(BF16),
    }
    j = l // 2
    if l % 2 == 1:
        lp.update({'moe_w_router': p['moe_w_router'][j].astype(BF16), 'moe_b_router': row(p['moe_b_router'][j]),
                   'moe_w_gate': p['moe_w_gate'][j:j + 1].astype(BF16), 'moe_w_up': p['moe_w_up'][j:j + 1].astype(BF16),
                   'moe_w_down': p['moe_w_down'][j:j + 1].astype(BF16)})
    else:
        lp.update({'ffn_w_gate': p['ffn_w_gate'][j].astype(BF16), 'ffn_w_up': p['ffn_w_up'][j].astype(BF16),
                   'ffn_w_down': p['ffn_w_down'][j].astype(BF16)})
    return lp


def kernel(x_prompt, x_sample, state_ssm_re, state_ssm_im, cache_win_k, cache_win_v, state_pool,
           c_prompt, c_sample, w_ada, b_ada, norm_mix_pre, norm_mix_post, norm_ffn_pre, norm_ffn_post,
           w_in, ssm_a_re, ssm_a_im, ssm_log_dt, ssm_b_re, ssm_b_im, ssm_c_re, ssm_c_im, ssm_d,
           ssm_w_glu, ssm_b_glu, attn_sinks, pool_w, pool_scale, w_up_ssm, w_up_attn, w_up_pool, w_out,
           ffn_w_gate, ffn_w_up, ffn_w_down, moe_w_router, moe_b_router, moe_w_gate, moe_w_up, moe_w_down):
    p = dict(norm_mix_pre=norm_mix_pre, norm_mix_post=norm_mix_post, norm_ffn_pre=norm_ffn_pre,
             norm_ffn_post=norm_ffn_post, w_in=w_in, ssm_c_re=ssm_c_re, ssm_c_im=ssm_c_im, ssm_d=ssm_d,
             ssm_w_glu=ssm_w_glu, ssm_b_glu=ssm_b_glu, attn_sinks=attn_sinks, pool_w=pool_w, pool_scale=pool_scale,
             w_up_ssm=w_up_ssm, w_up_attn=w_up_attn, w_up_pool=w_up_pool, w_out=w_out,
             ffn_w_gate=ffn_w_gate, ffn_w_up=ffn_w_up, ffn_w_down=ffn_w_down, moe_w_router=moe_w_router,
             moe_b_router=moe_b_router, moe_w_gate=moe_w_gate, moe_w_up=moe_w_up, moe_w_down=moe_w_down)
    depth = w_in.shape[0]
    b, t_len, d = x_prompt.shape
    ns = x_sample.shape[0]

    mods = _ada_call(jnp.concatenate([c_sample, c_prompt], axis=0), w_ada, b_ada)
    prep = _ssm_prep_call(ssm_a_re, ssm_a_im, ssm_log_dt, ssm_b_re, ssm_b_im)

    xp = x_prompt
    xs = x_sample.reshape(ns, d)
    p_out = ([], [], [], [], [])
    s_out = ([], [], [], [], [])
    win_minor = lambda c: jnp.transpose(c, (0, 1, 3, 4, 2)).reshape(depth, ns, D_KV, -1)
    win_major = lambda c: jnp.transpose(c.reshape(depth, ns, N_KV_HEADS, HEAD_DIM, -1), (0, 1, 4, 2, 3))
    old_windows = [win_minor(cache_win_k), win_minor(cache_win_v)]
    new_windows = []
    for l in range(depth):
        lp = _layer_params(l, p, prep)
        mods2 = mods[l]
        mods3 = mods2.reshape(ns + b, 1, N_MOD * d)

        xp, hre, him, kwin, vwin, plast = _mixer_call(xp, mods3, lp)
        x2 = xp.reshape(b * t_len, d)
        if l % 2 == 1:
            x2 = _moe_routed_call(x2, mods3, lp, t_len)
        else:
            x2 = _ffn_call(x2, mods3, mods2, lp, t_len, False)
        xp = x2.reshape(b, t_len, d)
        p_out[0].append(hre.reshape(b, SSM_GROUPS, SSM_STATE))
        p_out[1].append(him.reshape(b, SSM_GROUPS, SSM_STATE))
        p_out[2].append(kwin.reshape(b, WINDOW, N_KV_HEADS, HEAD_DIM))
        p_out[3].append(vwin.reshape(b, WINDOW, N_KV_HEADS, HEAD_DIM))
        p_out[4].append(plast[:, POOL_HIST - POOL_BUF:, :])

        pa, pg = _sproj_call(xs, mods2, lp)
        u = pa[:, 0:D_SSM]
        q3 = pa[:, D_SSM:D_SSM + D_ATTN].reshape(ns, GQA, LANES)
        k_new = pa[:, D_SSM + D_ATTN:D_SSM + D_ATTN + D_KV]
        v_new = pa[:, D_SSM + D_ATTN + D_KV:D_SSM + D_ATTN + 2 * D_KV]
        pin = pa[:, D_SMALL - D_POOL:]
        yssm, yatt3, ypool, hr, hi, *new_windows, hist_new = _smix_call(
            l, u, q3, k_new.reshape(ns, 1, D_KV), v_new.reshape(ns, 1, D_KV), k_new.T, v_new.T, pin,
            state_ssm_re[l].reshape(ns, D_STATE), state_ssm_im[l].reshape(ns, D_STATE),
            *old_windows, new_windows, jnp.swapaxes(state_pool[l], 0, 1), lp)
        xs = _smerge_call(xs, mods2, pg, yssm, yatt3.reshape(ns, D_ATTN), ypool, lp)
        xs = (_moe_call if l % 2 == 1 else _ffn_call)(xs, mods3, mods2, lp, 1, True)
        s_out[0].append(hr.reshape(ns, SSM_GROUPS, SSM_STATE))
        s_out[1].append(hi.reshape(ns, SSM_GROUPS, SSM_STATE))
        s_out[4].append(jnp.swapaxes(hist_new, 0, 1))

    stack = lambda xs_: [jnp.stack(a, axis=0) for a in xs_]
    p_ssm_re, p_ssm_im, p_win_k, p_win_v, p_pool = stack(p_out)
    s_ssm_re, s_ssm_im, s_pool = stack([s_out[0], s_out[1], s_out[4]])
    s_win_k, s_win_v = [win_major(w) for w in new_windows]
    return (xp, xs.reshape(x_sample.shape), p_ssm_re, p_ssm_im, p_win_k, p_win_v, p_pool,
            s_ssm_re, s_ssm_im, s_win_k, s_win_v, s_pool)
```

```python
import functools
import math
import types

import jax
import jax.numpy as jnp
from jax import lax
from jax.experimental import pallas as pl
from jax.experimental.pallas import tpu as pltpu

F32 = jnp.float32
BF16 = jnp.bfloat16

D_MODEL = 1024
D_SSM = 256
SSM_GROUP = 16
SSM_GROUPS = 16
SSM_STATE = 64
D_STATE = SSM_GROUPS * SSM_STATE
HEAD_DIM = 64
N_HEADS = 8
N_KV_HEADS = 2
GQA = N_HEADS // N_KV_HEADS
D_ATTN = N_HEADS * HEAD_DIM
D_KV = N_KV_HEADS * HEAD_DIM
WINDOW = 128
ATTN_SCALE = HEAD_DIM ** -0.5
NEG_INF = -1e30
LOG2E = math.log2(math.e)
D_POOL = 256
POOL_WINDOWS = (2, 4, 8, 16)
POOL_BUF = 15
POOL_HIST = 16
POOL_PAD = 8
N_MOD = 6
D_FF = 2816
N_EXPERTS = 8
D_EXPERT = 1024
RMS_EPS = 1e-6
PAST_LEN = 16384
D_SMALL = D_SSM + D_ATTN + 2 * D_KV + D_POOL
D_GATES = 3 * D_MODEL
SLOPES = tuple(2.0 ** (-8.0 * (h + 1) / N_HEADS) for h in range(N_HEADS))

SUBLANES = 8
LANES = 128
VMEM_LIMIT = 56 * 1024 * 1024

TT = 512
SEQ_PER_STEP = 1
SCAN_LEN = TT // SUBLANES
Q_BLK = 128
GATE_CHUNK = 256
BU_CHUNK = 512
MIXER_LAG = 0
GATE_LOOP_A = 8
GATE_LOOP_B = D_GATES // GATE_CHUNK - GATE_LOOP_A
FFN_TM = 512
MOE_TM = 1024
MOE_SP_TM = 896
MOE_CAP = 256
MOE_TAIL = 128
SAMPLE_BC = 32
ADA_TN = 1536
FF_CHUNKS = ((0, 1024), (1024, 2048), (2048, D_FF))


def _rms(x, gain):
    return x * lax.rsqrt(jnp.mean(x * x, axis=-1, keepdims=True) + RMS_EPS) * gain


def _sigmoid(x):
    return 1.0 / (1.0 + jnp.exp(-x))


def _silu(x):
    return x * _sigmoid(x)


def _dot(a, b):
    return jnp.dot(a, b, preferred_element_type=F32)


def _dot_t(a, b):
    return lax.dot_general(a, b, (((1,), (1,)), ((), ())), preferred_element_type=F32)


def _halves(dot, a, b):
    half = a.shape[0] // 2
    return jnp.concatenate([dot(a[:half], b), dot(a[half:], b)], axis=0)


def _const_spec(shape):
    nd = len(shape)
    return pl.BlockSpec(shape, lambda *_: (0,) * nd)


def _params(sem):
    return pltpu.CompilerParams(dimension_semantics=sem, vmem_limit_bytes=VMEM_LIMIT)


def _ada_kernel(c_ref, w_ref, b_ref, o_ref):
    c = c_ref[...]
    s = _silu(c).astype(BF16)
    o_ref[0] = _dot(s, w_ref[0].astype(BF16)) + b_ref[0]


def _ada_call(c_all, w_ada, b_ada):
    depth = w_ada.shape[0]
    rows = c_all.shape[0]
    n = N_MOD * D_MODEL
    return pl.pallas_call(
        _ada_kernel,
        out_shape=jax.ShapeDtypeStruct((depth, rows, n), F32),
        grid=(depth, n // ADA_TN),
        in_specs=[
            pl.BlockSpec((rows, D_MODEL), lambda l, j: (0, 0)),
            pl.BlockSpec((1, D_MODEL, ADA_TN), lambda l, j: (l, 0, j)),
            pl.BlockSpec((1, 1, ADA_TN), lambda l, j: (l, 0, j)),
        ],
        out_specs=pl.BlockSpec((1, rows, ADA_TN), lambda l, j: (l, 0, j)),
        compiler_params=_params(("arbitrary", "arbitrary")),
        name="ada_mod",
    )(c_all, w_ada, b_ada.reshape(depth, 1, n))


def _ssm_prep_kernel(are_ref, aim_ref, ldt_ref, bre_ref, bim_ref,
                     abre_o, abim_o, bbre_o, bbim_o, pwre_o, pwim_o):
    ar = are_ref[0]
    ai = aim_ref[0]
    dt = jnp.exp(ldt_ref[0])
    mag = jnp.exp(ar * dt)
    ab_re = mag * jnp.cos(ai * dt)
    ab_im = mag * jnp.sin(ai * dt)
    den = ar * ar + ai * ai
    n_re = ab_re - 1.0
    f_re = (n_re * ar + ab_im * ai) / den
    f_im = (ab_im * ar - n_re * ai) / den
    br = bre_ref[0]
    bi = bim_ref[0]
    bbre_o[0] = f_re * br - f_im * bi
    bbim_o[0] = f_re * bi + f_im * br
    abre_o[0] = ab_re
    abim_o[0] = ab_im
    cr, ci = ab_re, ab_im
    for t in range(SCAN_LEN):
        pwre_o[0, t] = cr
        pwim_o[0, t] = ci
        cr, ci = cr * ab_re - ci * ab_im, cr * ab_im + ci * ab_re


def _ssm_prep_call(a_re, a_im, log_dt, b_re, b_im):
    depth = a_re.shape[0]
    g, p, c = SSM_GROUPS, SSM_STATE, SSM_GROUP
    a4 = lambda a: a.reshape(depth, g, 1, p)
    spec_a = pl.BlockSpec((1, g, 1, p), lambda l: (l, 0, 0, 0))
    spec_b = pl.BlockSpec((1, g, c, p), lambda l: (l, 0, 0, 0))
    spec_pw = pl.BlockSpec((1, SCAN_LEN, g, 1, p), lambda l: (l, 0, 0, 0, 0))
    return pl.pallas_call(
        _ssm_prep_kernel,
        out_shape=(jax.ShapeDtypeStruct((depth, g, 1, p), F32),) * 2
        + (jax.ShapeDtypeStruct((depth, g, c, p), F32),) * 2
        + (jax.ShapeDtypeStruct((depth, SCAN_LEN, g, 1, p), F32),) * 2,
        grid=(depth,),
        in_specs=[spec_a, spec_a, pl.BlockSpec((1, g, 1, 1), lambda l: (l, 0, 0, 0)), spec_b, spec_b],
        out_specs=(spec_a, spec_a, spec_b, spec_b, spec_pw, spec_pw),
        compiler_params=_params(("arbitrary",)),
        name="ssm_prep",
    )(a4(a_re), a4(a_im), log_dt.reshape(depth, g, 1, 1),
      jnp.swapaxes(b_re, 2, 3), jnp.swapaxes(b_im, 2, 3))


def _ssm_out(y, u, d_skip, w_glu, b_glu):
    z = jax.nn.gelu(y + d_skip * u)
    return z * _sigmoid(_halves(_dot, z.astype(BF16), w_glu) + b_glu)


def _lane_lo(shape):
    return (lax.broadcasted_iota(jnp.int32, shape, len(shape) - 1) % LANES) < HEAD_DIM


N_MIXER_SEQ_IN = 4
N_MIXER_IN = 24
N_MIXER_OUT = 6


def _mixer_kernel(*refs):
    ins = refs[:N_MIXER_IN]
    outs = refs[N_MIXER_IN:N_MIXER_IN + N_MIXER_OUT]
    scratch = refs[N_MIXER_IN + N_MIXER_OUT:]
    bias_s = scratch[-1]
    t = pl.program_id(1)

    @pl.when((pl.program_id(0) == 0) & (t == 0))
    def _():
        r_i = lax.broadcasted_iota(jnp.int32, (Q_BLK, 2 * Q_BLK), 0)
        c_i = lax.broadcasted_iota(jnp.int32, (Q_BLK, 2 * Q_BLK), 1)
        dist = r_i - c_i + Q_BLK
        valid = (dist >= 0) & (dist <= WINDOW)
        for hh in range(N_HEADS):
            bias_s[hh] = jnp.where(valid, (-SLOPES[hh] * LOG2E) * dist.astype(F32), NEG_INF)

    lvl_s, kbuf, vbuf, pbuf, hc_s = scratch[-6:-1]

    @pl.when(t == 0)
    def _():
        hc_s[...] = jnp.zeros_like(hc_s)
        kbuf[:, 0:WINDOW, :] = jnp.zeros((SEQ_PER_STEP, WINDOW, D_KV), BF16)
        vbuf[:, 0:WINDOW, :] = jnp.zeros((SEQ_PER_STEP, WINDOW, D_KV), BF16)
        pbuf[:, 0:POOL_PAD + POOL_HIST, :] = jnp.zeros((SEQ_PER_STEP, POOL_PAD + POOL_HIST, D_POOL), F32)
        lvl_s[:, :, 0:POOL_PAD, :] = jnp.zeros((SEQ_PER_STEP, 2, POOL_PAD, LANES), F32)

    chains = []
    for s in range(SEQ_PER_STEP):
        one = functools.partial(lambda s, r: r.at[s:s + 1], s)
        chains.append(_mixer_body(t, *[one(r) for r in ins[:N_MIXER_SEQ_IN]], *ins[N_MIXER_SEQ_IN:],
                                  *[one(r) for r in outs], *[r.at[s] for r in scratch[:-1]], bias_s))
    for _ in range(MIXER_LAG):
        next(chains[0])
    while chains:
        for chain in list(chains):
            if next(chain, StopIteration) is StopIteration:
                chains.remove(chain)


MIXER_REF_NAMES = (
    'x_ref sh_ref sc_ref g_ref npre_ref npost_ref wa_ref wg_ref '
    'wbu_ref abre_ref abim_ref pwre_ref pwim_ref cm_ref dsk_ref wglu_ref bglu_ref '
    'sinks_ref wpool_ref pscale_ref wus_ref wua_ref wup_ref wout_ref '
    'xo_ref hre_o him_o kwin_o vwin_o plast_o '
    'hb_s proj_s bu_s hst_s sg_s unat_s uperm_s operm_s yssm_s yatt_s lvl_s kbuf vbuf pbuf hc_s bias_s').split()


def _mixer_body(t, *refs):
    r = types.SimpleNamespace(**dict(zip(MIXER_REF_NAMES, refs, strict=True)))
    x = r.x_ref[0]
    h = _rms(x, r.npre_ref[...]) * (1.0 + r.sc_ref[0]) + r.sh_ref[0]
    r.hb_s[...] = h.astype(BF16)
    yield
    for c in range(D_SMALL // GATE_CHUNK):
        cols = slice(c * GATE_CHUNK, (c + 1) * GATE_CHUNK)
        r.proj_s[:, cols] = _dot(r.hb_s[...], r.wa_ref[:, cols])
        yield
    res = {}
    yield from _ssm_chain(r)
    yield from _attn_chain(t, r)
    yield from _pool_chain(t, r, res)
    yield from _merge_tail(r, res)


def _ssm_chain(r):
    (proj_s, hb_s, bu_s, hst_s, sg_s, unat_s, uperm_s, operm_s, yssm_s, hc_s, wg_ref, wbu_ref, abre_ref, abim_ref,
     pwre_ref, pwim_ref, cm_ref, dsk_ref, wglu_ref, bglu_ref, hre_o, him_o) = (
        r.proj_s, r.hb_s, r.bu_s, r.hst_s, r.sg_s, r.unat_s, r.uperm_s, r.operm_s, r.yssm_s, r.hc_s, r.wg_ref,
        r.wbu_ref, r.abre_ref, r.abim_ref, r.pwre_ref, r.pwim_ref, r.cm_ref, r.dsk_ref, r.wglu_ref, r.bglu_ref,
        r.hre_o, r.him_o)
    for c in range(D_SSM // LANES):
        unat_s[c] = proj_s[:, c * LANES:(c + 1) * LANES]
        for i in range(SCAN_LEN):
            uperm_s[i * SUBLANES:(i + 1) * SUBLANES, c * LANES:(c + 1) * LANES] = (
                unat_s.at[c][pl.ds(i, SUBLANES, stride=SCAN_LEN), :])
        yield
    for c in range(2 * D_STATE // BU_CHUNK):
        cols = slice(c * BU_CHUNK, (c + 1) * BU_CHUNK)
        bu_s[:, cols] = _dot(uperm_s[...].astype(BF16), wbu_ref[:, cols])
        yield

    abr = jnp.broadcast_to(abre_ref[...], (SUBLANES, D_STATE))
    abi = jnp.broadcast_to(abim_ref[...], (SUBLANES, D_STATE))

    def gate_chunk(c):
        sg_s[c] = _sigmoid(_dot(hb_s[...], wg_ref[c]))

    zero = jnp.zeros((SUBLANES, D_STATE), F32)
    steps_a = SCAN_LEN // GATE_LOOP_A

    er, ei = zero, zero
    for i in range(GATE_LOOP_A):
        gate_chunk(i)
        yield
        for s in range(steps_a):
            rows = slice((i * steps_a + s) * SUBLANES, (i * steps_a + s + 1) * SUBLANES)
            er, ei = (abr * er - abi * ei + bu_s[rows, 0:D_STATE],
                      abr * ei + abi * er + bu_s[rows, D_STATE:2 * D_STATE])
            bu_s[rows, 0:D_STATE] = er
            bu_s[rows, D_STATE:2 * D_STATE] = ei
        yield

    alr = pwre_ref[SCAN_LEN - 1:SCAN_LEN, :]
    ali = pwim_ref[SCAN_LEN - 1:SCAN_LEN, :]
    cr = hc_s[0:1, :]
    ci = hc_s[1:2, :]
    row = lax.broadcasted_iota(jnp.int32, (SUBLANES, D_STATE), 0)
    hin_r = zero
    hin_i = zero
    for j in range(SUBLANES):
        hin_r = jnp.where(row == j, cr, hin_r)
        hin_i = jnp.where(row == j, ci, hin_i)
        cr, ci = (alr * cr - ali * ci + er[j:j + 1, :], alr * ci + ali * cr + ei[j:j + 1, :])
    hc_s[0:1, :] = cr
    hc_s[1:2, :] = ci
    hre_o[0] = cr
    him_o[0] = ci
    yield

    pack = 2 * SUBLANES
    packs_b = TT // pack // GATE_LOOP_B

    for i in range(GATE_LOOP_B):
        gate_chunk(GATE_LOOP_A + i)
        yield
        for s in range(packs_b):
            k = i * packs_b + s
            fixed_r, fixed_i = [], []
            for slab in (2 * k, 2 * k + 1):
                rows = slice(slab * SUBLANES, (slab + 1) * SUBLANES)
                pr = pwre_ref[slab:slab + 1, :]
                pi = pwim_ref[slab:slab + 1, :]
                fixed_r.append(bu_s[rows, 0:D_STATE] + (pr * hin_r - pi * hin_i))
                fixed_i.append(bu_s[rows, D_STATE:2 * D_STATE] + (pr * hin_i + pi * hin_r))
            prow = slice(k * pack, (k + 1) * pack)
            hst_s[prow, 0:D_STATE] = jnp.concatenate(fixed_r, axis=0).astype(BF16)
            hst_s[prow, D_STATE:2 * D_STATE] = jnp.concatenate(fixed_i, axis=0).astype(BF16)
        yield

    y = _halves(_dot, hst_s[...], cm_ref[...])
    yield
    operm_s[...] = _ssm_out(y, uperm_s[...], dsk_ref[...], wglu_ref[...], bglu_ref[...])
    yield
    for c in range(D_SSM // LANES):
        for i in range(SCAN_LEN):
            yssm_s.at[c][pl.ds(i, SUBLANES, stride=SCAN_LEN), :] = (
                operm_s[i * SUBLANES:(i + 1) * SUBLANES, c * LANES:(c + 1) * LANES])
        yield

def _attn_chain(t, r):
    proj_s, kbuf, vbuf, yatt_s, bias_s, sinks_ref, kwin_o, vwin_o = (
        r.proj_s, r.kbuf, r.vbuf, r.yatt_s, r.bias_s, r.sinks_ref, r.kwin_o, r.vwin_o)
    kbuf[WINDOW:WINDOW + TT, :] = proj_s[:, D_SSM + D_ATTN:D_SSM + D_ATTN + D_KV].astype(BF16)
    vbuf[WINDOW:WINDOW + TT, :] = proj_s[:, D_SSM + D_ATTN + D_KV:D_SSM + D_ATTN + 2 * D_KV].astype(BF16)
    lo = _lane_lo((Q_BLK, LANES))
    r_i = lax.broadcasted_iota(jnp.int32, (Q_BLK, 2 * Q_BLK), 0)
    c_i = lax.broadcasted_iota(jnp.int32, (Q_BLK, 2 * Q_BLK), 1)
    first_key = jnp.where(t == 0, Q_BLK, 0)
    def scores(blk):
        r0 = blk * Q_BLK
        pieces = []
        for hh in range(N_HEADS):
            j = hh % GQA
            qp = proj_s[r0:r0 + Q_BLK, D_SSM + j * LANES:D_SSM + (j + 1) * LANES] * (ATTN_SCALE * LOG2E)
            keep = lo if hh < GQA else jnp.logical_not(lo)
            pieces.append(jnp.where(keep, qp, 0.0).astype(BF16))
        return [_dot_t(jnp.concatenate(pieces[g * GQA:(g + 1) * GQA], axis=0), kbuf[r0:r0 + 2 * Q_BLK, :])
                for g in range(N_KV_HEADS)]

    n_blk = TT // Q_BLK
    s_next = scores(0)
    yield
    for blk in range(n_blk):
        r0 = blk * Q_BLK
        s_all = s_next
        if blk + 1 < n_blk:
            s_next = scores(blk + 1)
            yield
        probs, dens = [], []
        for hh in range(N_HEADS):
            s = s_all[hh // GQA][(hh % GQA) * Q_BLK:(hh % GQA + 1) * Q_BLK, :] + bias_s[hh]
            if blk == 0:
                s = jnp.where(c_i >= first_key, s, NEG_INF)
            snk = sinks_ref[hh] * LOG2E
            m = jnp.maximum(jnp.max(s, axis=-1, keepdims=True), snk)
            p = jnp.exp2(s - m)
            dens.append(jnp.sum(p, axis=-1, keepdims=True) + jnp.exp2(snk - m))
            probs.append(p.astype(BF16))
            yield
        o_all = [_dot(jnp.concatenate(probs[g * GQA:(g + 1) * GQA], axis=0), vbuf[r0:r0 + 2 * Q_BLK, :])
                 for g in range(N_KV_HEADS)]
        out = lambda hh: o_all[hh // GQA][(hh % GQA) * Q_BLK:(hh % GQA + 1) * Q_BLK, :] / dens[hh]
        for j in range(GQA):
            yatt_s[r0:r0 + Q_BLK, j * LANES:(j + 1) * LANES] = jnp.where(lo, out(j), out(GQA + j)).astype(BF16)
        yield
    kwin_o[0] = proj_s[TT - WINDOW:TT, D_SSM + D_ATTN:D_SSM + D_ATTN + D_KV]
    vwin_o[0] = proj_s[TT - WINDOW:TT, D_SSM + D_ATTN + D_KV:D_SSM + D_ATTN + 2 * D_KV]
    kbuf[0:WINDOW, :] = kbuf[TT:TT + WINDOW, :]
    vbuf[0:WINDOW, :] = vbuf[TT:TT + WINDOW, :]
    yield

def _pool_chain(t, r, res):
    proj_s, pbuf, lvl_s, plast_o, wpool_ref, pscale_ref = (
        r.proj_s, r.pbuf, r.lvl_s, r.plast_o, r.wpool_ref, r.pscale_ref)
    p0 = D_SMALL - D_POOL
    cur = POOL_PAD + POOL_HIST
    end = cur + TT
    pbuf[cur:end, :] = proj_s[:, p0:D_SMALL]
    plast_o[0] = proj_s[TT - POOL_HIST:TT, p0:D_SMALL]
    pos1 = (t * TT + 1 + lax.broadcasted_iota(jnp.int32, (TT, LANES), 0)).astype(F32)
    lo_t = _lane_lo((TT, LANES))
    a_s, b_s = lvl_s.at[0], lvl_s.at[1]
    mixed = []
    for col, (w_lo, w_hi) in enumerate(((POOL_WINDOWS[0], POOL_WINDOWS[1]), (POOL_WINDOWS[2], POOL_WINDOWS[3]))):
        cs = slice(col * LANES, (col + 1) * LANES)
        pf = pbuf[cur:end, cs]
        a_s[POOL_PAD:end, :] = pbuf[POOL_PAD:end, cs] + pbuf[POOL_PAD - 1:end - 1, cs]
        have = 2
        while have < w_lo:
            b_s[POOL_PAD:end, :] = a_s[POOL_PAD:end, :] + a_s[POOL_PAD - have:end - have, :]
            a_s, b_s, have = b_s, a_s, 2 * have
        acc_lo = a_s[cur:end, :]
        while 2 * have < w_hi:
            b_s[POOL_PAD:end, :] = a_s[POOL_PAD:end, :] + a_s[POOL_PAD - have:end - have, :]
            a_s, b_s, have = b_s, a_s, 2 * have
        acc_hi = a_s[cur:end, :] + a_s[cur - have:end - have, :]
        cnt = jnp.where(lo_t, jnp.minimum(float(w_lo), pos1), jnp.minimum(float(w_hi), pos1))
        mixed.append(jnp.where(lo_t, acc_lo, acc_hi) / cnt - pf)
        yield
    ypool = _halves(_dot, jnp.concatenate(mixed, axis=1).astype(BF16), wpool_ref[...]) * pscale_ref[...]
    pbuf[POOL_PAD:cur, :] = pbuf[end - POOL_HIST:end, :]
    res['ypool'] = ypool
    yield


def _merge_tail(r, res):
    sg_s, yssm_s, yatt_s, wus_ref, wua_ref, wup_ref, wout_ref, x_ref, g_ref, npost_ref, xo_ref = (
        r.sg_s, r.yssm_s, r.yatt_s, r.wus_ref, r.wua_ref, r.wup_ref, r.wout_ref, r.x_ref, r.g_ref, r.npost_ref,
        r.xo_ref)
    ypool = res['ypool']
    per_branch = D_MODEL // GATE_CHUNK
    gate = lambda b: jnp.concatenate([sg_s[b * per_branch + c] for c in range(per_branch)], axis=1)
    yssm = jnp.concatenate([yssm_s[c] for c in range(D_SSM // LANES)], axis=1)
    merged = gate(0) * _dot(yssm.astype(BF16), wus_ref[...])
    yield
    merged += gate(1) * _dot(yatt_s[...], wua_ref[...])
    yield
    merged += gate(2) * _dot(ypool.astype(BF16), wup_ref[...])
    yield
    out = _dot(merged.astype(BF16), wout_ref[...])
    yield
    xo_ref[0] = x_ref[0] + g_ref[0] * _rms(out, npost_ref[...])


def _mixer_call(x, mods3, lp):
    b, t_len, d = x.shape
    nt = t_len // TT
    nsamp = mods3.shape[0] - b
    sps = SEQ_PER_STEP
    mod_spec = lambda col: pl.BlockSpec((sps, 1, D_MODEL), lambda i, j: (nsamp // sps + i, 0, col))
    x_spec = pl.BlockSpec((sps, TT, d), lambda i, j: (i, j, 0))
    consts = [lp['norm_mix_pre'], lp['norm_mix_post'], lp['w_in_a'], lp['w_in_g'],
              lp['w_bu'], lp['ab_re'], lp['ab_im'], lp['pw_re'], lp['pw_im'], lp['c_mat'],
              lp['ssm_d'], lp['ssm_w_glu'], lp['ssm_b_glu']]
    consts2 = [lp['pool_wbd'], lp['pool_scale'], lp['w_up_ssm'], lp['w_up_attn'], lp['w_up_pool'], lp['w_out']]
    in_specs = ([x_spec, mod_spec(0), mod_spec(1), mod_spec(2)]
                + [_const_spec(a.shape) for a in consts]
                + [pl.BlockSpec(memory_space=pltpu.SMEM)]
                + [_const_spec(a.shape) for a in consts2])
    per_b = lambda r, c: pl.BlockSpec((sps, r, c), lambda i, j: (i, 0, 0))
    out_shape = (jax.ShapeDtypeStruct((b, t_len, d), F32),
                 jax.ShapeDtypeStruct((b, 1, D_STATE), F32), jax.ShapeDtypeStruct((b, 1, D_STATE), F32),
                 jax.ShapeDtypeStruct((b, WINDOW, D_KV), F32), jax.ShapeDtypeStruct((b, WINDOW, D_KV), F32),
                 jax.ShapeDtypeStruct((b, POOL_HIST, D_POOL), F32))
    out_specs = (x_spec, per_b(1, D_STATE), per_b(1, D_STATE), per_b(WINDOW, D_KV), per_b(WINDOW, D_KV),
                 per_b(POOL_HIST, D_POOL))
    per_seq = lambda shape, dtype: pltpu.VMEM((sps,) + shape, dtype)
    scratch = [per_seq((TT, d), BF16), per_seq((TT, D_SMALL), F32), per_seq((TT, 2 * D_STATE), F32),
               per_seq((TT, 2 * D_STATE), BF16), per_seq((D_GATES // GATE_CHUNK, TT, GATE_CHUNK), F32),
               per_seq((D_SSM // LANES, TT, LANES), F32), per_seq((TT, D_SSM), F32),
               per_seq((TT, D_SSM), F32), per_seq((D_SSM // LANES, TT, LANES), F32),
               per_seq((TT, D_ATTN), BF16), per_seq((2, POOL_PAD + POOL_HIST + TT, LANES), F32),
               per_seq((WINDOW + TT, D_KV), BF16), per_seq((WINDOW + TT, D_KV), BF16),
               per_seq((POOL_PAD + POOL_HIST + TT, D_POOL), F32), per_seq((SUBLANES, D_STATE), F32),
               pltpu.VMEM((N_HEADS, Q_BLK, 2 * Q_BLK), F32)]
    return pl.pallas_call(
        _mixer_kernel,
        out_shape=out_shape,
        grid=(b // sps, nt),
        in_specs=in_specs,
        out_specs=out_specs,
        scratch_shapes=scratch,
        compiler_params=_params(("arbitrary", "arbitrary")),
        name="prompt_mixer",
    )(x, mods3, mods3, mods3, *consts, lp['attn_sinks'], *consts2)


def _mod_rows(ref, per_row):
    return ref[...] if per_row else ref[0]


def _ffn_kernel(per_row, x_ref, sh_ref, sc_ref, g_ref, npre_ref, npost_ref, wg_ref, wu_ref, wd_ref, o_ref):
    x = x_ref[...]
    h = (_rms(x, npre_ref[...]) * (1.0 + _mod_rows(sc_ref, per_row)) + _mod_rows(sh_ref, per_row)).astype(BF16)
    f = None
    for c0, c1 in FF_CHUNKS:
        act = _silu(_dot(h, wg_ref[:, c0:c1])) * _dot(h, wu_ref[:, c0:c1])
        part = _dot(act.astype(BF16), wd_ref[c0:c1, :])
        f = part if f is None else f + part
    o_ref[...] = x + _mod_rows(g_ref, per_row) * _rms(f, npost_ref[...])


def _mod_specs(mods3, mods2, n_rows, tm, rows_per_seq, per_row):
    nsamp = mods2.shape[0] - (n_rows // rows_per_seq if not per_row else 0)
    if per_row:
        specs = [pl.BlockSpec((tm, D_MODEL), functools.partial(lambda col, i, *_: (i, col), col)) for col in (3, 4, 5)]
        return specs, [mods2] * 3
    tiles_per_seq = rows_per_seq // tm
    specs = [pl.BlockSpec((1, 1, D_MODEL),
                          functools.partial(lambda col, i, *_: (nsamp + i // tiles_per_seq, 0, col), col))
             for col in (3, 4, 5)]
    return specs, [mods3] * 3


def _ffn_call(x2, mods3, mods2, lp, rows_per_seq, per_row):
    n = x2.shape[0]
    tm = n if per_row else FFN_TM
    mspecs, mops = _mod_specs(mods3, mods2, n, tm, rows_per_seq, per_row)
    consts = [lp['norm_ffn_pre'], lp['norm_ffn_post'], lp['ffn_w_gate'], lp['ffn_w_up'], lp['ffn_w_down']]
    x_spec = pl.BlockSpec((tm, D_MODEL), lambda i: (i, 0))
    return pl.pallas_call(
        functools.partial(_ffn_kernel, per_row),
        out_shape=jax.ShapeDtypeStruct(x2.shape, F32),
        grid=(n // tm,),
        in_specs=[x_spec] + mspecs + [_const_spec(a.shape) for a in consts],
        out_specs=x_spec,
        compiler_params=_params(("arbitrary",)),
        name="ffn_rows" if per_row else "ffn_seq",
    )(x2, *mops, *consts)


def _moe_kernel(per_row, x_ref, sh_ref, sc_ref, g_ref, npre_ref, npost_ref, wr_ref, br_ref,
                wg_ref, wu_ref, wd_ref, o_ref, h_s, gates_s, acc_s):
    e = pl.program_id(1)
    lane_e = lax.broadcasted_iota(jnp.int32, gates_s.shape, 1)

    @pl.when(e == 0)
    def _():
        x = x_ref[...]
        h = _rms(x, npre_ref[...]) * (1.0 + _mod_rows(sc_ref, per_row)) + _mod_rows(sh_ref, per_row)
        hb = h.astype(BF16)
        h_s[...] = hb
        logits = _dot(hb, wr_ref[...]) + br_ref[...]
        gates_s[...], _ = _top2_gates(logits, lane_e.astype(F32))
        acc_s[...] = jnp.zeros_like(acc_s)

    h = h_s[...]
    act = _silu(_dot(h, wg_ref[0, 0])) * _dot(h, wu_ref[0, 0])
    y = _dot(act.astype(BF16), wd_ref[0, 0])
    ge = jnp.sum(jnp.where(lane_e == e, gates_s[...], 0.0), axis=-1, keepdims=True)
    acc_s[...] += ge * y

    @pl.when(e == N_EXPERTS - 1)
    def _():
        o_ref[...] = x_ref[...] + _mod_rows(g_ref, per_row) * _rms(acc_s[...], npost_ref[...])


def _moe_call(x2, mods3, mods2, lp, rows_per_seq, per_row):
    n = x2.shape[0]
    tm = n if per_row else MOE_TM
    mspecs, mops = _mod_specs(mods3, mods2, n, tm, rows_per_seq, per_row)
    consts = [lp['norm_ffn_pre'], lp['norm_ffn_post'], lp['moe_w_router'], lp['moe_b_router']]
    x_spec = pl.BlockSpec((tm, D_MODEL), lambda i, e: (i, 0))
    w_spec = lambda a: pl.BlockSpec((1, 1) + a.shape[2:], lambda i, e: (0, e, 0, 0))
    experts = [lp['moe_w_gate'], lp['moe_w_up'], lp['moe_w_down']]
    return pl.pallas_call(
        functools.partial(_moe_kernel, per_row),
        out_shape=jax.ShapeDtypeStruct(x2.shape, F32),
        grid=(n // tm, N_EXPERTS),
        in_specs=[x_spec] + mspecs + [_const_spec(a.shape) for a in consts] + [w_spec(a) for a in experts],
        out_specs=x_spec,
        scratch_shapes=[pltpu.VMEM((tm, D_MODEL), BF16), pltpu.VMEM((tm, N_EXPERTS), F32),
                        pltpu.VMEM((tm, D_MODEL), F32)],
        compiler_params=_params(("arbitrary", "arbitrary")),
        name="moe_rows" if per_row else "moe_seq",
    )(x2, *mops, *consts, *experts)


def _top2_gates(logits, lane_f):
    n_lanes = float(logits.shape[-1])
    v1 = jnp.max(logits, axis=-1, keepdims=True)
    i1 = jnp.min(jnp.where(logits == v1, lane_f, n_lanes), axis=-1, keepdims=True)
    rest = jnp.where(lane_f == i1, -jnp.inf, logits)
    v2 = jnp.max(rest, axis=-1, keepdims=True)
    i2 = jnp.min(jnp.where(rest == v2, lane_f, n_lanes), axis=-1, keepdims=True)
    e2 = jnp.exp(v2 - v1)
    den = 1.0 + e2
    gates = jnp.where(lane_f == i1, 1.0 / den, 0.0) + jnp.where(lane_f == i2, e2 / den, 0.0)
    return gates, (lane_f == i1) | (lane_f == i2)


def _moe_routed_kernel(seq_len, n_rows, x_ref, sh0_ref, sh1_ref, sc0_ref, sc1_ref, g0_ref, g1_ref,
                       npre_ref, npost_ref, wr_ref, br_ref, wg_ref, wu_ref, wd_ref, o_ref,
                       h_s, before_s, dest_s, gate_s, acc_s, cnt_s):
    i = pl.program_id(0)
    e = pl.program_id(1)
    tm = x_ref.shape[0]
    row = lax.broadcasted_iota(jnp.int32, (tm, 1), 0) + i * tm
    valid = row < n_rows
    second = row >= ((i * tm) // seq_len + 1) * seq_len
    pick = lambda a_ref, b_ref: jnp.where(second, b_ref[0], a_ref[0])

    @pl.when((i == 0) & (e == 0))
    def _():
        r_i = lax.broadcasted_iota(jnp.int32, (tm, tm), 0)
        c_i = lax.broadcasted_iota(jnp.int32, (tm, tm), 1)
        before_s[...] = jnp.where(c_i < r_i, 1.0, 0.0).astype(BF16)

    @pl.when(e == 0)
    def _():
        x = jnp.where(valid, x_ref[...], 0.0)
        h = _rms(x, npre_ref[...]) * (1.0 + pick(sc0_ref, sc1_ref)) + pick(sh0_ref, sh1_ref)
        hb = h.astype(BF16)
        h_s[...] = hb
        lane_f = lax.broadcasted_iota(jnp.int32, (tm, LANES), 1).astype(F32)
        logits = jnp.where(lane_f < N_EXPERTS, _halves(_dot, hb, wr_ref[...]) + br_ref[...], -jnp.inf)
        gates, sel = _top2_gates(logits, lane_f)
        self = jnp.where(sel & valid, 1.0, 0.0)
        dest = jnp.where(self > 0.0, _halves(_dot, before_s[...], self.astype(BF16)), -1.0)
        dest_s[...] = dest.T[0:SUBLANES, :]
        gate_s[...] = gates.T[0:SUBLANES, :]
        counts = jnp.sum(self, axis=0, keepdims=True)
        for j in range(N_EXPERTS):
            cnt_s[j] = counts[0, j].astype(jnp.int32)
        acc_s[...] = jnp.zeros_like(acc_s)

    dest_row = dest_s[pl.ds(e, 1), :]
    gate_row = gate_s[pl.ds(e, 1), :]
    def chunk(base, size):
        slot = lax.broadcasted_iota(jnp.int32, (size, tm), 0).astype(F32) + base.astype(F32)
        onehot = jnp.where(slot == dest_row, 1.0, 0.0)
        gate_packed = jnp.sum(onehot * gate_row, axis=-1, keepdims=True)
        pack = onehot.astype(BF16)
        xg = _dot(pack, h_s[...]).astype(BF16)
        act = _silu(_dot(xg, wg_ref[0, 0])) * _dot(xg, wu_ref[0, 0])
        y = _dot(act.astype(BF16), wd_ref[0, 0]) * gate_packed
        acc_s[...] += lax.dot_general(pack, y.astype(BF16), (((0,), (0,)), ((), ())), preferred_element_type=F32)

    shift = MOE_CAP.bit_length() - 1
    cnt = cnt_s[e]
    n_full = lax.shift_right_logical(cnt, shift)
    rem = cnt - lax.shift_left(n_full, shift)
    long_last = (n_full >= 1) & (rem >= 1) & (rem <= MOE_TAIL)
    n_plain = jnp.where(long_last, n_full - 1, n_full + jnp.where(rem >= 1, 1, 0))

    def plain_chunk(c, _):
        chunk(c * MOE_CAP, MOE_CAP)
        return 0

    lax.fori_loop(0, n_plain, plain_chunk, 0)

    @pl.when(long_last)
    def _():
        chunk(n_plain * MOE_CAP, MOE_CAP + MOE_TAIL)

    @pl.when(e == N_EXPERTS - 1)
    def _():
        x = jnp.where(valid, x_ref[...], 0.0)
        o_ref[...] = x + pick(g0_ref, g1_ref) * _rms(acc_s[...], npost_ref[...])


def _moe_routed_call(x2, mods3, lp, seq_len):
    n = x2.shape[0]
    tm = MOE_SP_TM
    n_seq = n // seq_len
    nsamp = mods3.shape[0] - n_seq

    def mod_spec(col, nxt):
        return pl.BlockSpec((1, 1, D_MODEL),
                            lambda i, e: (nsamp + jnp.minimum((i * tm) // seq_len + nxt, n_seq - 1), 0, col))

    mspecs = [mod_spec(col, nxt) for col in (3, 4, 5) for nxt in (0, 1)]
    pad = LANES - N_EXPERTS
    consts = [lp['norm_ffn_pre'], lp['norm_ffn_post'],
              jnp.pad(lp['moe_w_router'], ((0, 0), (0, pad))), jnp.pad(lp['moe_b_router'], ((0, 0), (0, pad)))]
    x_spec = pl.BlockSpec((tm, D_MODEL), lambda i, e: (i, 0))
    w_spec = lambda a: pl.BlockSpec((1, 1) + a.shape[2:], lambda i, e: (0, e, 0, 0))
    experts = [lp['moe_w_gate'], lp['moe_w_up'], lp['moe_w_down']]
    return pl.pallas_call(
        functools.partial(_moe_routed_kernel, seq_len, n),
        out_shape=jax.ShapeDtypeStruct(x2.shape, F32),
        grid=(pl.cdiv(n, tm), N_EXPERTS),
        in_specs=[x_spec] + mspecs + [_const_spec(a.shape) for a in consts] + [w_spec(a) for a in experts],
        out_specs=x_spec,
        scratch_shapes=[pltpu.VMEM((tm, D_MODEL), BF16), pltpu.VMEM((tm, tm), BF16),
                        pltpu.VMEM((SUBLANES, tm), F32), pltpu.VMEM((SUBLANES, tm), F32),
                        pltpu.VMEM((tm, D_MODEL), F32), pltpu.SMEM((N_EXPERTS,), jnp.int32)],
        compiler_params=_params(("arbitrary", "arbitrary")),
        name="moe_routed",
    )(x2, *([mods3] * 6), *consts, *experts)


def _sproj_kernel(x_ref, sh_ref, sc_ref, npre_ref, wa_ref, wg_ref, pa_o, pg_o):
    h = (_rms(x_ref[...], npre_ref[...]) * (1.0 + sc_ref[...]) + sh_ref[...]).astype(BF16)
    pa_o[...] = _dot(h, wa_ref[...])
    for c in range(D_GATES // GATE_CHUNK):
        pg_o[:, c * GATE_CHUNK:(c + 1) * GATE_CHUNK] = _dot(h, wg_ref[c])


def _sproj_call(xs, mods2, lp):
    n = xs.shape[0]
    mod_spec = lambda col: pl.BlockSpec((n, D_MODEL), lambda i: (0, col))
    consts = [lp['norm_mix_pre'], lp['w_in_a'], lp['w_in_g']]
    return pl.pallas_call(
        _sproj_kernel,
        out_shape=(jax.ShapeDtypeStruct((n, D_SMALL), F32), jax.ShapeDtypeStruct((n, D_GATES), F32)),
        grid=(1,),
        in_specs=[_const_spec(xs.shape), mod_spec(0), mod_spec(1)] + [_const_spec(a.shape) for a in consts],
        out_specs=(_const_spec((n, D_SMALL)), _const_spec((n, D_GATES))),
        compiler_params=_params(("arbitrary",)),
        name="sample_proj",
    )(xs, mods2, mods2, *consts)


N_SMIX_IN = 22


def _smix_kernel(*refs):
    (u_ref, q_ref, kn_ref, vn_ref, knt_ref, vnt_ref, pin_ref, h0r_ref, h0i_ref, ck_ref, cv_ref, hist_ref,
     wbu_ref, abre_ref, abim_ref, cm_ref, dsk_ref, wglu_ref, bglu_ref, sinks_ref,
     wpool_ref, pscale_ref) = refs[:N_SMIX_IN]
    yssm_o, yatt_o, ypool_o, hr_o, hi_o, ck_o, cv_o, hist_o = refs[-8:]
    u = u_ref[...]
    bu = _dot(u.astype(BF16), wbu_ref[...])
    abr, abi = abre_ref[...], abim_ref[...]
    h0r, h0i = h0r_ref[...], h0i_ref[...]
    hr = bu[:, 0:D_STATE] + (abr * h0r - abi * h0i)
    hi = bu[:, D_STATE:2 * D_STATE] + (abr * h0i + abi * h0r)
    hr_o[...] = hr
    hi_o[...] = hi
    y = _dot(jnp.concatenate([hr, hi], axis=1).astype(BF16), cm_ref[...])
    yssm_o[...] = _ssm_out(y, u, dsk_ref[...], wglu_ref[...], bglu_ref[...])

    q = q_ref[...] * ATTN_SCALE
    kn, vn = kn_ref[...], vn_ref[...]
    kc = ck_ref[...].astype(BF16)
    vc = cv_ref[...].astype(BF16)
    lo = _lane_lo(q.shape)
    j_i = lax.broadcasted_iota(jnp.int32, (1, GQA, 1), 1)
    dist_c = (WINDOW - lax.broadcasted_iota(jnp.int32, (1, 1, WINDOW), 2)).astype(F32)
    halves = []
    for half in range(N_KV_HEADS):
        keep = lo if half == 0 else jnp.logical_not(lo)
        qm = jnp.where(keep, q, 0.0)
        slope = jnp.zeros((1, GQA, 1), F32)
        snk = jnp.zeros((1, GQA, 1), F32)
        for j in range(GQA):
            slope = jnp.where(j_i == j, SLOPES[half * GQA + j], slope)
            snk = jnp.where(j_i == j, sinks_ref[half * GQA + j], snk)
        s = jnp.einsum('bhd,bds->bhs', qm.astype(BF16), kc, preferred_element_type=F32) - slope * dist_c
        s_new = jnp.sum(qm * kn, axis=-1, keepdims=True)
        m = jnp.maximum(jnp.maximum(jnp.max(s, axis=-1, keepdims=True), s_new), snk)
        p = jnp.exp(s - m)
        p_new = jnp.exp(s_new - m)
        den = jnp.sum(p, axis=-1, keepdims=True) + p_new + jnp.exp(snk - m)
        o = jnp.einsum('bhs,bds->bhd', p.astype(BF16), vc, preferred_element_type=F32) + p_new * vn
        halves.append(o / den)
    yatt_o[...] = jnp.where(lo, halves[0], halves[1])

    bc = ck_ref.shape[0]
    lane_w = lax.broadcasted_iota(jnp.int32, (D_KV, WINDOW), 1)
    lane_b = lax.broadcasted_iota(jnp.int32, knt_ref.shape, 1)
    for src, new_t, dst in ((ck_ref, knt_ref, ck_o), (cv_ref, vnt_ref, cv_o)):
        for b in range(bc):
            col = jnp.sum(jnp.where(lane_b == pl.program_id(0) * bc + b, new_t[...], 0.0), axis=-1, keepdims=True)
            dst[b] = jnp.where(lane_w == WINDOW - 1, col, pltpu.roll(src[b], WINDOW - 1, axis=1))

    pin = pin_ref[...]
    lo2 = _lane_lo((pin.shape[0], LANES))
    mixed = []
    for col, (w_lo, w_hi) in enumerate(((POOL_WINDOWS[0], POOL_WINDOWS[1]), (POOL_WINDOWS[2], POOL_WINDOWS[3]))):
        cs = slice(col * LANES, (col + 1) * LANES)
        pf = pin[:, cs]
        acc = pf
        for k in range(1, w_lo):
            acc = acc + hist_ref[POOL_BUF - k, :, cs]
        acc_lo = acc
        for k in range(w_lo, w_hi):
            acc = acc + hist_ref[POOL_BUF - k, :, cs]
        cnt_lo = float(min(w_lo, PAST_LEN + 1))
        cnt_hi = float(min(w_hi, PAST_LEN + 1))
        mixed.append(jnp.where(lo2, acc_lo / cnt_lo, acc / cnt_hi) - pf)
    ypool_o[...] = _dot(jnp.concatenate(mixed, axis=1).astype(BF16), wpool_ref[...]) * pscale_ref[...]
    hist_o[0:POOL_BUF - 1] = hist_ref[1:POOL_BUF]
    hist_o[POOL_BUF - 1] = pin


def _smix_call(layer, u, q3, kn3, vn3, knt, vnt, pin, h0r, h0i, ck, cv, prev_windows, hist_t, lp):
    n = u.shape[0]
    bc = SAMPLE_BC
    rows2 = lambda c: pl.BlockSpec((bc, c), lambda i: (i, 0))
    rows3 = lambda r, c: pl.BlockSpec((bc, r, c), lambda i: (i, 0, 0))
    win_spec = pl.BlockSpec((None, bc, D_KV, WINDOW), lambda i: (layer, i, 0, 0))
    consts = [lp['w_bu'], lp['ab_re'], lp['ab_im'], lp['c_mat'], lp['ssm_d'], lp['ssm_w_glu'], lp['ssm_b_glu']]
    consts2 = [lp['pool_wbd'], lp['pool_scale']]
    hist_spec = pl.BlockSpec((POOL_BUF, bc, D_POOL), lambda i: (0, i, 0))
    in_specs = ([rows2(D_SSM), rows3(GQA, LANES), rows3(1, LANES), rows3(1, LANES),
                 _const_spec(knt.shape), _const_spec(vnt.shape), rows2(D_POOL),
                 rows2(D_STATE), rows2(D_STATE), win_spec, win_spec, hist_spec]
                + [_const_spec(a.shape) for a in consts]
                + [pl.BlockSpec(memory_space=pltpu.SMEM)]
                + [_const_spec(a.shape) for a in consts2]
                + [pl.BlockSpec(memory_space=pl.ANY) for _ in prev_windows])
    assert len(in_specs) == N_SMIX_IN + len(prev_windows)
    aliases = {N_SMIX_IN + k: 5 + k for k in range(len(prev_windows))}
    return pl.pallas_call(
        _smix_kernel,
        out_shape=(jax.ShapeDtypeStruct((n, D_SSM), F32), jax.ShapeDtypeStruct((n, GQA, LANES), F32),
                   jax.ShapeDtypeStruct((n, D_POOL), F32),
                   jax.ShapeDtypeStruct((n, D_STATE), F32), jax.ShapeDtypeStruct((n, D_STATE), F32),
                   jax.ShapeDtypeStruct(ck.shape, F32), jax.ShapeDtypeStruct(cv.shape, F32),
                   jax.ShapeDtypeStruct(hist_t.shape, F32)),
        grid=(n // bc,),
        in_specs=in_specs,
        out_specs=(rows2(D_SSM), rows3(GQA, LANES), rows2(D_POOL), rows2(D_STATE), rows2(D_STATE),
                   win_spec, win_spec, hist_spec),
        input_output_aliases=aliases,
        compiler_params=_params(("arbitrary",)),
        name="sample_mix",
    )(u, q3, kn3, vn3, knt, vnt, pin, h0r, h0i, ck, cv, hist_t, *consts, lp['attn_sinks'], *consts2, *prev_windows)


def _smerge_kernel(x_ref, g_ref, pg_ref, yssm_ref, yatt_ref, ypool_ref, npost_ref,
                   wus_ref, wua_ref, wup_ref, wout_ref, o_ref):
    merged = _sigmoid(pg_ref[:, 0:D_MODEL]) * _dot(yssm_ref[...].astype(BF16), wus_ref[...])
    merged += _sigmoid(pg_ref[:, D_MODEL:2 * D_MODEL]) * _dot(yatt_ref[...].astype(BF16), wua_ref[...])
    merged += _sigmoid(pg_ref[:, 2 * D_MODEL:3 * D_MODEL]) * _dot(ypool_ref[...].astype(BF16), wup_ref[...])
    out = _dot(merged.astype(BF16), wout_ref[...])
    o_ref[...] = x_ref[...] + g_ref[...] * _rms(out, npost_ref[...])


def _smerge_call(xs, mods2, pg, yssm, yatt, ypool, lp):
    n = xs.shape[0]
    ops = [xs, mods2, pg, yssm, yatt, ypool, lp['norm_mix_post'],
           lp['w_up_ssm'], lp['w_up_attn'], lp['w_up_pool'], lp['w_out']]
    in_specs = [_const_spec(a.shape) for a in ops]
    in_specs[1] = pl.BlockSpec((n, D_MODEL), lambda i: (0, 2))
    return pl.pallas_call(
        _smerge_kernel,
        out_shape=jax.ShapeDtypeStruct(xs.shape, F32),
        grid=(1,),
        in_specs=in_specs,
        out_specs=_const_spec(xs.shape),
        compiler_params=_params(("arbitrary",)),
        name="sample_merge",
    )(*ops)


def _block_diag(w):
    g, a, b = w.shape
    eye = jnp.eye(g, dtype=w.dtype)
    return jnp.einsum('gab,gh->gahb', w, eye).reshape(g * a, g * b)


def _pair_perm():
    idx = []
    for j in range(GQA):
        idx += list(range(j * HEAD_DIM, (j + 1) * HEAD_DIM))
        idx += list(range((GQA + j) * HEAD_DIM, (GQA + j + 1) * HEAD_DIM))
    return jnp.asarray(idx, dtype=jnp.int32)


def _layer_params(l, p, prep):
    ab_re, ab_im, bb_re, bb_im, pw_re, pw_im = prep
    perm = _pair_perm()
    row = lambda a: a.reshape(1, -1)
    w_in = p['w_in'][l]
    q_cols = w_in[:, D_SSM:D_SSM + D_ATTN][:, perm]
    w_in_a = jnp.concatenate([w_in[:, 0:D_SSM], q_cols, w_in[:, D_SSM + D_ATTN:D_SMALL]], axis=1)
    lp = {
        'norm_mix_pre': row(p['norm_mix_pre'][l]), 'norm_mix_post': row(p['norm_mix_post'][l]),
        'norm_ffn_pre': row(p['norm_ffn_pre'][l]), 'norm_ffn_post': row(p['norm_ffn_post'][l]),
        'w_in_a': w_in_a.astype(BF16),
        'w_in_g': jnp.swapaxes(w_in[:, D_SMALL:].reshape(D_MODEL, D_GATES // GATE_CHUNK, GATE_CHUNK), 0, 1).astype(BF16),
        'w_bu': jnp.concatenate([_block_diag(bb_re[l]), _block_diag(bb_im[l])], axis=1).astype(BF16),
        'ab_re': ab_re[l].reshape(1, D_STATE), 'ab_im': ab_im[l].reshape(1, D_STATE),
        'pw_re': pw_re[l].reshape(SCAN_LEN, D_STATE), 'pw_im': pw_im[l].reshape(SCAN_LEN, D_STATE),
        'c_mat': jnp.concatenate([_block_diag(jnp.swapaxes(p['ssm_c_re'][l], 1, 2)),
                                  _block_diag(jnp.swapaxes(-p['ssm_c_im'][l], 1, 2))], axis=0).astype(BF16),
        'ssm_d': row(p['ssm_d'][l]), 'ssm_w_glu': p['ssm_w_glu'][l].astype(BF16), 'ssm_b_glu': row(p['ssm_b_glu'][l]),
        'attn_sinks': p['attn_sinks'][l],
        'pool_wbd': _block_diag(p['pool_w'][l]).astype(BF16), 'pool_scale': row(p['pool_scale'][l]),
        'w_up_ssm': p['w_up_ssm'][l].astype(BF16), 'w_up_attn': p['w_up_attn'][l][perm, :].astype(BF16),
        'w_up_pool': p['w_up_pool'][l].astype(BF16), 'w_out': p['w_out'][l].astype(BF16),
    }
    j = l // 2
    if l % 2 == 1:
        lp.update({'moe_w_router': p['moe_w_router'][j].astype(BF16), 'moe_b_router': row(p['moe_b_router'][j]),
                   'moe_w_gate': p['moe_w_gate'][j:j + 1].astype(BF16), 'moe_w_up': p['moe_w_up'][j:j + 1].astype(BF16),
                   'moe_w_down': p['moe_w_down'][j:j + 1].astype(BF16)})
    else:
        lp.update({'ffn_w_gate': p['ffn_w_gate'][j].astype(BF16), 'ffn_w_up': p['ffn_w_up'][j].astype(BF16),
                   'ffn_w_down': p['ffn_w_down'][j].astype(BF16)})
    return lp


def kernel(x_prompt, x_sample, state_ssm_re, state_ssm_im, cache_win_k, cache_win_v, state_pool,
           c_prompt, c_sample, w_ada, b_ada, norm_mix_pre, norm_mix_post, norm_ffn_pre, norm_ffn_post,
           w_in, ssm_a_re, ssm_a_im, ssm_log_dt, ssm_b_re, ssm_b_im, ssm_c_re, ssm_c_im, ssm_d,
           ssm_w_glu, ssm_b_glu, attn_sinks, pool_w, pool_scale, w_up_ssm, w_up_attn, w_up_pool, w_out,
           ffn_w_gate, ffn_w_up, ffn_w_down, moe_w_router, moe_b_router, moe_w_gate, moe_w_up, moe_w_down):
    p = dict(norm_mix_pre=norm_mix_pre, norm_mix_post=norm_mix_post, norm_ffn_pre=norm_ffn_pre,
             norm_ffn_post=norm_ffn_post, w_in=w_in, ssm_c_re=ssm_c_re, ssm_c_im=ssm_c_im, ssm_d=ssm_d,
             ssm_w_glu=ssm_w_glu, ssm_b_glu=ssm_b_glu, attn_sinks=attn_sinks, pool_w=pool_w, pool_scale=pool_scale,
             w_up_ssm=w_up_ssm, w_up_attn=w_up_attn, w_up_pool=w_up_pool, w_out=w_out,
             ffn_w_gate=ffn_w_gate, ffn_w_up=ffn_w_up, ffn_w_down=ffn_w_down, moe_w_router=moe_w_router,
             moe_b_router=moe_b_router, moe_w_gate=moe_w_gate, moe_w_up=moe_w_up, moe_w_down=moe_w_down)
    depth = w_in.shape[0]
    b, t_len, d = x_prompt.shape
    ns = x_sample.shape[0]

    mods = _ada_call(jnp.concatenate([c_sample, c_prompt], axis=0), w_ada, b_ada)
    prep = _ssm_prep_call(ssm_a_re, ssm_a_im, ssm_log_dt, ssm_b_re, ssm_b_im)

    xp = x_prompt
    xs = x_sample.reshape(ns, d)
    p_out = ([], [], [], [], [])
    s_out = ([], [], [], [], [])
    win_minor = lambda c: jnp.transpose(c, (0, 1, 3, 4, 2)).reshape(depth, ns, D_KV, -1)
    win_major = lambda c: jnp.transpose(c.reshape(depth, ns, N_KV_HEADS, HEAD_DIM, -1), (0, 1, 4, 2, 3))
    old_windows = [win_minor(cache_win_k), win_minor(cache_win_v)]
    new_windows = []
    for l in range(depth):
        lp = _layer_params(l, p, prep)
        mods2 = mods[l]
        mods3 = mods2.reshape(ns + b, 1, N_MOD * d)

        xp, hre, him, kwin, vwin, plast = _mixer_call(xp, mods3, lp)
        x2 = xp.reshape(b * t_len, d)
        if l % 2 == 1:
            x2 = _moe_routed_call(x2, mods3, lp, t_len)
        else:
            x2 = _ffn_call(x2, mods3, mods2, lp, t_len, False)
        xp = x2.reshape(b, t_len, d)
        p_out[0].append(hre.reshape(b, SSM_GROUPS, SSM_STATE))
        p_out[1].append(him.reshape(b, SSM_GROUPS, SSM_STATE))
        p_out[2].append(kwin.reshape(b, WINDOW, N_KV_HEADS, HEAD_DIM))
        p_out[3].append(vwin.reshape(b, WINDOW, N_KV_HEADS, HEAD_DIM))
        p_out[4].append(plast[:, POOL_HIST - POOL_BUF:, :])

        pa, pg = _sproj_call(xs, mods2, lp)
        u = pa[:, 0:D_SSM]
        q3 = pa[:, D_SSM:D_SSM + D_ATTN].reshape(ns, GQA, LANES)
        k_new = pa[:, D_SSM + D_ATTN:D_SSM + D_ATTN + D_KV]
        v_new = pa[:, D_SSM + D_ATTN + D_KV:D_SSM + D_ATTN + 2 * D_KV]
        pin = pa[:, D_SMALL - D_POOL:]
        yssm, yatt3, ypool, hr, hi, *new_windows, hist_new = _smix_call(
            l, u, q3, k_new.reshape(ns, 1, D_KV), v_new.reshape(ns, 1, D_KV), k_new.T, v_new.T, pin,
            state_ssm_re[l].reshape(ns, D_STATE), state_ssm_im[l].reshape(ns, D_STATE),
            *old_windows, new_windows, jnp.swapaxes(state_pool[l], 0, 1), lp)
        xs = _smerge_call(xs, mods2, pg, yssm, yatt3.reshape(ns, D_ATTN), ypool, lp)
        xs = (_moe_call if l % 2 == 1 else _ffn_call)(xs, mods3, mods2, lp, 1, True)
        s_out[0].append(hr.reshape(ns, SSM_GROUPS, SSM_STATE))
        s_out[1].append(hi.reshape(ns, SSM_GROUPS, SSM_STATE))
        s_out[4].append(jnp.swapaxes(hist_new, 0, 1))

    stack = lambda xs_: [jnp.stack(a, axis=0) for a in xs_]
    p_ssm_re, p_ssm_im, p_win_k, p_win_v, p_pool = stack(p_out)
    s_ssm_re, s_ssm_im, s_pool = stack([s_out[0], s_out[1], s_out[4]])
    s_win_k, s_win_v = [win_major(w) for w in new_windows]
    return (xp, xs.reshape(x_sample.shape), p_ssm_re, p_ssm_im, p_win_k, p_win_v, p_pool,
            s_ssm_re, s_ssm_im, s_win_k, s_win_v, s_pool)
```

```python
import functools
import math
import types

import jax
import jax.numpy as jnp
from jax import lax
from jax.experimental import pallas as pl
from jax.experimental.pallas import tpu as pltpu

F32 = jnp.float32
BF16 = jnp.bfloat16

D_MODEL = 1024
D_SSM = 256
SSM_GROUP = 16
SSM_GROUPS = 16
SSM_STATE = 64
D_STATE = SSM_GROUPS * SSM_STATE
HEAD_DIM = 64
N_HEADS = 8
N_KV_HEADS = 2
GQA = N_HEADS // N_KV_HEADS
D_ATTN = N_HEADS * HEAD_DIM
D_KV = N_KV_HEADS * HEAD_DIM
WINDOW = 128
ATTN_SCALE = HEAD_DIM ** -0.5
NEG_INF = -1e30
LOG2E = math.log2(math.e)
D_POOL = 256
POOL_WINDOWS = (2, 4, 8, 16)
POOL_BUF = 15
POOL_HIST = 16
POOL_PAD = 8
N_MOD = 6
D_FF = 2816
N_EXPERTS = 8
D_EXPERT = 1024
RMS_EPS = 1e-6
PAST_LEN = 16384
D_SMALL = D_SSM + D_ATTN + 2 * D_KV + D_POOL
D_GATES = 3 * D_MODEL
SLOPES = tuple(2.0 ** (-8.0 * (h + 1) / N_HEADS) for h in range(N_HEADS))

SUBLANES = 8
LANES = 128
VMEM_LIMIT = 56 * 1024 * 1024

TT = 512
SEQ_PER_STEP = 1
SCAN_LEN = TT // SUBLANES
Q_BLK = 128
GATE_CHUNK = 256
BU_CHUNK = 512
MIXER_LAG = 0
GATE_LOOP_A = 8
GATE_LOOP_B = D_GATES // GATE_CHUNK - GATE_LOOP_A
FFN_TM = 512
MOE_TM = 1024
MOE_SP_TM = 896
MOE_CAP = 256
MOE_TAIL = 128
SAMPLE_BC = 32
ADA_TN = 1536
FF_CHUNKS = ((0, 1024), (1024, 2048), (2048, D_FF))


def _rms(x, gain):
    return x * lax.rsqrt(jnp.mean(x * x, axis=-1, keepdims=True) + RMS_EPS) * gain


def _sigmoid(x):
    return 1.0 / (1.0 + jnp.exp(-x))


def _silu(x):
    return x * _sigmoid(x)


def _dot(a, b):
    return jnp.dot(a, b, preferred_element_type=F32)


def _dot_t(a, b):
    return lax.dot_general(a, b, (((1,), (1,)), ((), ())), preferred_element_type=F32)


def _halves(dot, a, b):
    half = a.shape[0] // 2
    return jnp.concatenate([dot(a[:half], b), dot(a[half:], b)], axis=0)


def _const_spec(shape):
    nd = len(shape)
    return pl.BlockSpec(shape, lambda *_: (0,) * nd)


def _params(sem):
    return pltpu.CompilerParams(dimension_semantics=sem, vmem_limit_bytes=VMEM_LIMIT)


def _ada_kernel(c_ref, w_ref, b_ref, o_ref):
    c = c_ref[...]
    s = _silu(c).astype(BF16)
    o_ref[0] = _dot(s, w_ref[0].astype(BF16)) + b_ref[0]


def _ada_call(c_all, w_ada, b_ada):
    depth = w_ada.shape[0]
    rows = c_all.shape[0]
    n = N_MOD * D_MODEL
    return pl.pallas_call(
        _ada_kernel,
        out_shape=jax.ShapeDtypeStruct((depth, rows, n), F32),
        grid=(depth, n // ADA_TN),
        in_specs=[
            pl.BlockSpec((rows, D_MODEL), lambda l, j: (0, 0)),
            pl.BlockSpec((1, D_MODEL, ADA_TN), lambda l, j: (l, 0, j)),
            pl.BlockSpec((1, 1, ADA_TN), lambda l, j: (l, 0, j)),
        ],
        out_specs=pl.BlockSpec((1, rows, ADA_TN), lambda l, j: (l, 0, j)),
        compiler_params=_params(("arbitrary", "arbitrary")),
        name="ada_mod",
    )(c_all, w_ada, b_ada.reshape(depth, 1, n))


def _ssm_prep_kernel(are_ref, aim_ref, ldt_ref, bre_ref, bim_ref,
                     abre_o, abim_o, bbre_o, bbim_o, pwre_o, pwim_o):
    ar = are_ref[0]
    ai = aim_ref[0]
    dt = jnp.exp(ldt_ref[0])
    mag = jnp.exp(ar * dt)
    ab_re = mag * jnp.cos(ai * dt)
    ab_im = mag * jnp.sin(ai * dt)
    den = ar * ar + ai * ai
    n_re = ab_re - 1.0
    f_re = (n_re * ar + ab_im * ai) / den
    f_im = (ab_im * ar - n_re * ai) / den
    br = bre_ref[0]
    bi = bim_ref[0]
    bbre_o[0] = f_re * br - f_im * bi
    bbim_o[0] = f_re * bi + f_im * br
    abre_o[0] = ab_re
    abim_o[0] = ab_im
    cr, ci = ab_re, ab_im
    for t in range(SCAN_LEN):
        pwre_o[0, t] = cr
        pwim_o[0, t] = ci
        cr, ci = cr * ab_re - ci * ab_im, cr * ab_im + ci * ab_re


def _ssm_prep_call(a_re, a_im, log_dt, b_re, b_im):
    depth = a_re.shape[0]
    g, p, c = SSM_GROUPS, SSM_STATE, SSM_GROUP
    a4 = lambda a: a.reshape(depth, g, 1, p)
    spec_a = pl.BlockSpec((1, g, 1, p), lambda l: (l, 0, 0, 0))
    spec_b = pl.BlockSpec((1, g, c, p), lambda l: (l, 0, 0, 0))
    spec_pw = pl.BlockSpec((1, SCAN_LEN, g, 1, p), lambda l: (l, 0, 0, 0, 0))
    return pl.pallas_call(
        _ssm_prep_kernel,
        out_shape=(jax.ShapeDtypeStruct((depth, g, 1, p), F32),) * 2
        + (jax.ShapeDtypeStruct((depth, g, c, p), F32),) * 2
        + (jax.ShapeDtypeStruct((depth, SCAN_LEN, g, 1, p), F32),) * 2,
        grid=(depth,),
        in_specs=[spec_a, spec_a, pl.BlockSpec((1, g, 1, 1), lambda l: (l, 0, 0, 0)), spec_b, spec_b],
        out_specs=(spec_a, spec_a, spec_b, spec_b, spec_pw, spec_pw),
        compiler_params=_params(("arbitrary",)),
        name="ssm_prep",
    )(a4(a_re), a4(a_im), log_dt.reshape(depth, g, 1, 1),
      jnp.swapaxes(b_re, 2, 3), jnp.swapaxes(b_im, 2, 3))


def _ssm_out(y, u, d_skip, w_glu, b_glu):
    z = jax.nn.gelu(y + d_skip * u)
    return z * _sigmoid(_halves(_dot, z.astype(BF16), w_glu) + b_glu)


def _lane_lo(shape):
    return (lax.broadcasted_iota(jnp.int32, shape, len(shape) - 1) % LANES) < HEAD_DIM


N_MIXER_SEQ_IN = 4
N_MIXER_IN = 24
N_MIXER_OUT = 6


def _mixer_kernel(*refs):
    ins = refs[:N_MIXER_IN]
    outs = refs[N_MIXER_IN:N_MIXER_IN + N_MIXER_OUT]
    scratch = refs[N_MIXER_IN + N_MIXER_OUT:]
    bias_s = scratch[-1]
    t = pl.program_id(1)

    @pl.when((pl.program_id(0) == 0) & (t == 0))
    def _():
        r_i = lax.broadcasted_iota(jnp.int32, (Q_BLK, 2 * Q_BLK), 0)
        c_i = lax.broadcasted_iota(jnp.int32, (Q_BLK, 2 * Q_BLK), 1)
        dist = r_i - c_i + Q_BLK
        valid = (dist >= 0) & (dist <= WINDOW)
        for hh in range(N_HEADS):
            bias_s[hh] = jnp.where(valid, (-SLOPES[hh] * LOG2E) * dist.astype(F32), NEG_INF)

    lvl_s, kbuf, vbuf, pbuf, hc_s = scratch[-6:-1]

    @pl.when(t == 0)
    def _():
        hc_s[...] = jnp.zeros_like(hc_s)
        kbuf[:, 0:WINDOW, :] = jnp.zeros((SEQ_PER_STEP, WINDOW, D_KV), BF16)
        vbuf[:, 0:WINDOW, :] = jnp.zeros((SEQ_PER_STEP, WINDOW, D_KV), BF16)
        pbuf[:, 0:POOL_PAD + POOL_HIST, :] = jnp.zeros((SEQ_PER_STEP, POOL_PAD + POOL_HIST, D_POOL), F32)
        lvl_s[:, :, 0:POOL_PAD, :] = jnp.zeros((SEQ_PER_STEP, 2, POOL_PAD, LANES), F32)

    chains = []
    for s in range(SEQ_PER_STEP):
        one = functools.partial(lambda s, r: r.at[s:s + 1], s)
        chains.append(_mixer_body(t, *[one(r) for r in ins[:N_MIXER_SEQ_IN]], *ins[N_MIXER_SEQ_IN:],
                                  *[one(r) for r in outs], *[r.at[s] for r in scratch[:-1]], bias_s))
    for _ in range(MIXER_LAG):
        next(chains[0])
    while chains:
        for chain in list(chains):
            if next(chain, StopIteration) is StopIteration:
                chains.remove(chain)


MIXER_REF_NAMES = (
    'x_ref sh_ref sc_ref g_ref npre_ref npost_ref wa_ref wg_ref '
    'wbu_ref abre_ref abim_ref pwre_ref pwim_ref cm_ref dsk_ref wglu_ref bglu_ref '
    'sinks_ref wpool_ref pscale_ref wus_ref wua_ref wup_ref wout_ref '
    'xo_ref hre_o him_o kwin_o vwin_o plast_o '
    'hb_s proj_s bu_s hst_s sg_s unat_s uperm_s operm_s yssm_s yatt_s lvl_s kbuf vbuf pbuf hc_s bias_s').split()


def _mixer_body(t, *refs):
    r = types.SimpleNamespace(**dict(zip(MIXER_REF_NAMES, refs, strict=True)))
    x = r.x_ref[0]
    h = _rms(x, r.npre_ref[...]) * (1.0 + r.sc_ref[0]) + r.sh_ref[0]
    r.hb_s[...] = h.astype(BF16)
    yield
    for c in range(D_SMALL // GATE_CHUNK):
        cols = slice(c * GATE_CHUNK, (c + 1) * GATE_CHUNK)
        r.proj_s[:, cols] = _dot(r.hb_s[...], r.wa_ref[:, cols])
        yield
    res = {}
    yield from _ssm_chain(r)
    yield from _attn_chain(t, r)
    yield from _pool_chain(t, r, res)
    yield from _merge_tail(r, res)


def _ssm_chain(r):
    (proj_s, hb_s, bu_s, hst_s, sg_s, unat_s, uperm_s, operm_s, yssm_s, hc_s, wg_ref, wbu_ref, abre_ref, abim_ref,
     pwre_ref, pwim_ref, cm_ref, dsk_ref, wglu_ref, bglu_ref, hre_o, him_o) = (
        r.proj_s, r.hb_s, r.bu_s, r.hst_s, r.sg_s, r.unat_s, r.uperm_s, r.operm_s, r.yssm_s, r.hc_s, r.wg_ref,
        r.wbu_ref, r.abre_ref, r.abim_ref, r.pwre_ref, r.pwim_ref, r.cm_ref, r.dsk_ref, r.wglu_ref, r.bglu_ref,
        r.hre_o, r.him_o)
    for c in range(D_SSM // LANES):
        unat_s[c] = proj_s[:, c * LANES:(c + 1) * LANES]
        for i in range(SCAN_LEN):
            uperm_s[i * SUBLANES:(i + 1) * SUBLANES, c * LANES:(c + 1) * LANES] = (
                unat_s.at[c][pl.ds(i, SUBLANES, stride=SCAN_LEN), :])
        yield
    for c in range(2 * D_STATE // BU_CHUNK):
        cols = slice(c * BU_CHUNK, (c + 1) * BU_CHUNK)
        bu_s[:, cols] = _dot(uperm_s[...].astype(BF16), wbu_ref[:, cols])
        yield

    abr = jnp.broadcast_to(abre_ref[...], (SUBLANES, D_STATE))
    abi = jnp.broadcast_to(abim_ref[...], (SUBLANES, D_STATE))

    def gate_chunk(c):
        sg_s[c] = _sigmoid(_dot(hb_s[...], wg_ref[:, c * GATE_CHUNK:(c + 1) * GATE_CHUNK]))

    zero = jnp.zeros((SUBLANES, D_STATE), F32)
    steps_a = SCAN_LEN // GATE_LOOP_A

    er, ei = zero, zero
    for i in range(GATE_LOOP_A):
        gate_chunk(i)
        yield
        for s in range(steps_a):
            rows = slice((i * steps_a + s) * SUBLANES, (i * steps_a + s + 1) * SUBLANES)
            er, ei = (abr * er - abi * ei + bu_s[rows, 0:D_STATE],
                      abr * ei + abi * er + bu_s[rows, D_STATE:2 * D_STATE])
            bu_s[rows, 0:D_STATE] = er
            bu_s[rows, D_STATE:2 * D_STATE] = ei
        yield

    alr = pwre_ref[SCAN_LEN - 1:SCAN_LEN, :]
    ali = pwim_ref[SCAN_LEN - 1:SCAN_LEN, :]
    cr = hc_s[0:1, :]
    ci = hc_s[1:2, :]
    row = lax.broadcasted_iota(jnp.int32, (SUBLANES, D_STATE), 0)
    hin_r = zero
    hin_i = zero
    for j in range(SUBLANES):
        hin_r = jnp.where(row == j, cr, hin_r)
        hin_i = jnp.where(row == j, ci, hin_i)
        cr, ci = (alr * cr - ali * ci + er[j:j + 1, :], alr * ci + ali * cr + ei[j:j + 1, :])
    hc_s[0:1, :] = cr
    hc_s[1:2, :] = ci
    hre_o[0] = cr
    him_o[0] = ci
    yield

    pack = 2 * SUBLANES
    packs_b = TT // pack // GATE_LOOP_B

    for i in range(GATE_LOOP_B):
        gate_chunk(GATE_LOOP_A + i)
        yield
        for s in range(packs_b):
            k = i * packs_b + s
            fixed_r, fixed_i = [], []
            for slab in (2 * k, 2 * k + 1):
                rows = slice(slab * SUBLANES, (slab + 1) * SUBLANES)
                pr = pwre_ref[slab:slab + 1, :]
                pi = pwim_ref[slab:slab + 1, :]
                fixed_r.append(bu_s[rows, 0:D_STATE] + (pr * hin_r - pi * hin_i))
                fixed_i.append(bu_s[rows, D_STATE:2 * D_STATE] + (pr * hin_i + pi * hin_r))
            prow = slice(k * pack, (k + 1) * pack)
            hst_s[prow, 0:D_STATE] = jnp.concatenate(fixed_r, axis=0).astype(BF16)
            hst_s[prow, D_STATE:2 * D_STATE] = jnp.concatenate(fixed_i, axis=0).astype(BF16)
        yield

    y = _halves(_dot, hst_s[...], cm_ref[...])
    yield
    operm_s[...] = _ssm_out(y, uperm_s[...], dsk_ref[...], wglu_ref[...], bglu_ref[...])
    yield
    for c in range(D_SSM // LANES):
        for i in range(SCAN_LEN):
            yssm_s.at[c][pl.ds(i, SUBLANES, stride=SCAN_LEN), :] = (
                operm_s[i * SUBLANES:(i + 1) * SUBLANES, c * LANES:(c + 1) * LANES])
        yield

def _attn_chain(t, r):
    proj_s, kbuf, vbuf, yatt_s, bias_s, sinks_ref, kwin_o, vwin_o = (
        r.proj_s, r.kbuf, r.vbuf, r.yatt_s, r.bias_s, r.sinks_ref, r.kwin_o, r.vwin_o)
    kbuf[WINDOW:WINDOW + TT, :] = proj_s[:, D_SSM + D_ATTN:D_SSM + D_ATTN + D_KV].astype(BF16)
    vbuf[WINDOW:WINDOW + TT, :] = proj_s[:, D_SSM + D_ATTN + D_KV:D_SSM + D_ATTN + 2 * D_KV].astype(BF16)
    lo = _lane_lo((Q_BLK, LANES))
    r_i = lax.broadcasted_iota(jnp.int32, (Q_BLK, 2 * Q_BLK), 0)
    c_i = lax.broadcasted_iota(jnp.int32, (Q_BLK, 2 * Q_BLK), 1)
    first_key = jnp.where(t == 0, Q_BLK, 0)
    def scores(blk):
        r0 = blk * Q_BLK
        pieces = []
        for hh in range(N_HEADS):
            j = hh % GQA
            qp = proj_s[r0:r0 + Q_BLK, D_SSM + j * LANES:D_SSM + (j + 1) * LANES] * (ATTN_SCALE * LOG2E)
            keep = lo if hh < GQA else jnp.logical_not(lo)
            pieces.append(jnp.where(keep, qp, 0.0).astype(BF16))
        return [_dot_t(jnp.concatenate(pieces[g * GQA:(g + 1) * GQA], axis=0), kbuf[r0:r0 + 2 * Q_BLK, :])
                for g in range(N_KV_HEADS)]

    n_blk = TT // Q_BLK
    s_next = scores(0)
    yield
    for blk in range(n_blk):
        r0 = blk * Q_BLK
        s_all = s_next
        if blk + 1 < n_blk:
            s_next = scores(blk + 1)
            yield
        probs, dens = [], []
        for hh in range(N_HEADS):
            s = s_all[hh // GQA][(hh % GQA) * Q_BLK:(hh % GQA + 1) * Q_BLK, :] + bias_s[hh]
            if blk == 0:
                s = jnp.where(c_i >= first_key, s, NEG_INF)
            snk = sinks_ref[hh] * LOG2E
            m = jnp.maximum(jnp.max(s, axis=-1, keepdims=True), snk)
            p = jnp.exp2(s - m)
            dens.append(jnp.sum(p, axis=-1, keepdims=True) + jnp.exp2(snk - m))
            probs.append(p.astype(BF16))
            yield
        o_all = [_dot(jnp.concatenate(probs[g * GQA:(g + 1) * GQA], axis=0), vbuf[r0:r0 + 2 * Q_BLK, :])
                 for g in range(N_KV_HEADS)]
        out = lambda hh: o_all[hh // GQA][(hh % GQA) * Q_BLK:(hh % GQA + 1) * Q_BLK, :] / dens[hh]
        for j in range(GQA):
            yatt_s[r0:r0 + Q_BLK, j * LANES:(j + 1) * LANES] = jnp.where(lo, out(j), out(GQA + j)).astype(BF16)
        yield
    kwin_o[0] = proj_s[TT - WINDOW:TT, D_SSM + D_ATTN:D_SSM + D_ATTN + D_KV]
    vwin_o[0] = proj_s[TT - WINDOW:TT, D_SSM + D_ATTN + D_KV:D_SSM + D_ATTN + 2 * D_KV]
    kbuf[0:WINDOW, :] = kbuf[TT:TT + WINDOW, :]
    vbuf[0:WINDOW, :] = vbuf[TT:TT + WINDOW, :]
    yield

def _pool_chain(t, r, res):
    proj_s, pbuf, lvl_s, plast_o, wpool_ref, pscale_ref = (
        r.proj_s, r.pbuf, r.lvl_s, r.plast_o, r.wpool_ref, r.pscale_ref)
    p0 = D_SMALL - D_POOL
    cur = POOL_PAD + POOL_HIST
    end = cur + TT
    pbuf[cur:end, :] = proj_s[:, p0:D_SMALL]
    plast_o[0] = proj_s[TT - POOL_HIST:TT, p0:D_SMALL]
    pos1 = (t * TT + 1 + lax.broadcasted_iota(jnp.int32, (TT, LANES), 0)).astype(F32)
    lo_t = _lane_lo((TT, LANES))
    a_s, b_s = lvl_s.at[0], lvl_s.at[1]
    mixed = []
    for col, (w_lo, w_hi) in enumerate(((POOL_WINDOWS[0], POOL_WINDOWS[1]), (POOL_WINDOWS[2], POOL_WINDOWS[3]))):
        cs = slice(col * LANES, (col + 1) * LANES)
        pf = pbuf[cur:end, cs]
        a_s[POOL_PAD:end, :] = pbuf[POOL_PAD:end, cs] + pbuf[POOL_PAD - 1:end - 1, cs]
        have = 2
        while have < w_lo:
            b_s[POOL_PAD:end, :] = a_s[POOL_PAD:end, :] + a_s[POOL_PAD - have:end - have, :]
            a_s, b_s, have = b_s, a_s, 2 * have
        acc_lo = a_s[cur:end, :]
        while 2 * have < w_hi:
            b_s[POOL_PAD:end, :] = a_s[POOL_PAD:end, :] + a_s[POOL_PAD - have:end - have, :]
            a_s, b_s, have = b_s, a_s, 2 * have
        acc_hi = a_s[cur:end, :] + a_s[cur - have:end - have, :]
        cnt = jnp.where(lo_t, jnp.minimum(float(w_lo), pos1), jnp.minimum(float(w_hi), pos1))
        mixed.append(jnp.where(lo_t, acc_lo, acc_hi) / cnt - pf)
        yield
    ypool = _halves(_dot, jnp.concatenate(mixed, axis=1).astype(BF16), wpool_ref[...]) * pscale_ref[...]
    pbuf[POOL_PAD:cur, :] = pbuf[end - POOL_HIST:end, :]
    res['ypool'] = ypool
    yield


def _merge_tail(r, res):
    sg_s, yssm_s, yatt_s, wus_ref, wua_ref, wup_ref, wout_ref, x_ref, g_ref, npost_ref, xo_ref = (
        r.sg_s, r.yssm_s, r.yatt_s, r.wus_ref, r.wua_ref, r.wup_ref, r.wout_ref, r.x_ref, r.g_ref, r.npost_ref,
        r.xo_ref)
    ypool = res['ypool']
    per_branch = D_MODEL // GATE_CHUNK
    gate = lambda b: jnp.concatenate([sg_s[b * per_branch + c] for c in range(per_branch)], axis=1)
    yssm = jnp.concatenate([yssm_s[c] for c in range(D_SSM // LANES)], axis=1)
    merged = gate(0) * _dot(yssm.astype(BF16), wus_ref[...])
    yield
    merged += gate(1) * _dot(yatt_s[...], wua_ref[...])
    yield
    merged += gate(2) * _dot(ypool.astype(BF16), wup_ref[...])
    yield
    out = _dot(merged.astype(BF16), wout_ref[...])
    yield
    xo_ref[0] = x_ref[0] + g_ref[0] * _rms(out, npost_ref[...])


def _mixer_call(x, mods3, lp):
    b, t_len, d = x.shape
    nt = t_len // TT
    nsamp = mods3.shape[0] - b
    sps = SEQ_PER_STEP
    mod_spec = lambda col: pl.BlockSpec((sps, 1, D_MODEL), lambda i, j: (nsamp // sps + i, 0, col))
    x_spec = pl.BlockSpec((sps, TT, d), lambda i, j: (i, j, 0))
    consts = [lp['norm_mix_pre'], lp['norm_mix_post'], lp['w_in_a'], lp['w_in_g'],
              lp['w_bu'], lp['ab_re'], lp['ab_im'], lp['pw_re'], lp['pw_im'], lp['c_mat'],
              lp['ssm_d'], lp['ssm_w_glu'], lp['ssm_b_glu']]
    consts2 = [lp['pool_wbd'], lp['pool_scale'], lp['w_up_ssm'], lp['w_up_attn'], lp['w_up_pool'], lp['w_out']]
    in_specs = ([x_spec, mod_spec(0), mod_spec(1), mod_spec(2)]
                + [_const_spec(a.shape) for a in consts]
                + [pl.BlockSpec(memory_space=pltpu.SMEM)]
                + [_const_spec(a.shape) for a in consts2])
    per_b = lambda r, c: pl.BlockSpec((sps, r, c), lambda i, j: (i, 0, 0))
    out_shape = (jax.ShapeDtypeStruct((b, t_len, d), F32),
                 jax.ShapeDtypeStruct((b, 1, D_STATE), F32), jax.ShapeDtypeStruct((b, 1, D_STATE), F32),
                 jax.ShapeDtypeStruct((b, WINDOW, D_KV), F32), jax.ShapeDtypeStruct((b, WINDOW, D_KV), F32),
                 jax.ShapeDtypeStruct((b, POOL_HIST, D_POOL), F32))
    out_specs = (x_spec, per_b(1, D_STATE), per_b(1, D_STATE), per_b(WINDOW, D_KV), per_b(WINDOW, D_KV),
                 per_b(POOL_HIST, D_POOL))
    per_seq = lambda shape, dtype: pltpu.VMEM((sps,) + shape, dtype)
    scratch = [per_seq((TT, d), BF16), per_seq((TT, D_SMALL), F32), per_seq((TT, 2 * D_STATE), F32),
               per_seq((TT, 2 * D_STATE), BF16), per_seq((D_GATES // GATE_CHUNK, TT, GATE_CHUNK), F32),
               per_seq((D_SSM // LANES, TT, LANES), F32), per_seq((TT, D_SSM), F32),
               per_seq((TT, D_SSM), F32), per_seq((D_SSM // LANES, TT, LANES), F32),
               per_seq((TT, D_ATTN), BF16), per_seq((2, POOL_PAD + POOL_HIST + TT, LANES), F32),
               per_seq((WINDOW + TT, D_KV), BF16), per_seq((WINDOW + TT, D_KV), BF16),
               per_seq((POOL_PAD + POOL_HIST + TT, D_POOL), F32), per_seq((SUBLANES, D_STATE), F32),
               pltpu.VMEM((N_HEADS, Q_BLK, 2 * Q_BLK), F32)]
    return pl.pallas_call(
        _mixer_kernel,
        out_shape=out_shape,
        grid=(b // sps, nt),
        in_specs=in_specs,
        out_specs=out_specs,
        scratch_shapes=scratch,
        compiler_params=_params(("arbitrary", "arbitrary")),
        name="prompt_mixer",
    )(x, mods3, mods3, mods3, *consts, lp['attn_sinks'], *consts2)


def _mod_rows(ref, per_row):
    return ref[...] if per_row else ref[0]


def _ffn_kernel(per_row, x_ref, sh_ref, sc_ref, g_ref, npre_ref, npost_ref, wg_ref, wu_ref, wd_ref, o_ref):
    x = x_ref[...]
    h = (_rms(x, npre_ref[...]) * (1.0 + _mod_rows(sc_ref, per_row)) + _mod_rows(sh_ref, per_row)).astype(BF16)
    f = None
    for c0, c1 in FF_CHUNKS:
        act = _silu(_dot(h, wg_ref[:, c0:c1])) * _dot(h, wu_ref[:, c0:c1])
        part = _dot(act.astype(BF16), wd_ref[c0:c1, :])
        f = part if f is None else f + part
    o_ref[...] = x + _mod_rows(g_ref, per_row) * _rms(f, npost_ref[...])


def _mod_specs(mods3, mods2, n_rows, tm, rows_per_seq, per_row):
    nsamp = mods2.shape[0] - (n_rows // rows_per_seq if not per_row else 0)
    if per_row:
        specs = [pl.BlockSpec((tm, D_MODEL), functools.partial(lambda col, i, *_: (i, col), col)) for col in (3, 4, 5)]
        return specs, [mods2] * 3
    tiles_per_seq = rows_per_seq // tm
    specs = [pl.BlockSpec((1, 1, D_MODEL),
                          functools.partial(lambda col, i, *_: (nsamp + i // tiles_per_seq, 0, col), col))
             for col in (3, 4, 5)]
    return specs, [mods3] * 3


def _ffn_call(x2, mods3, mods2, lp, rows_per_seq, per_row):
    n = x2.shape[0]
    tm = n if per_row else FFN_TM
    mspecs, mops = _mod_specs(mods3, mods2, n, tm, rows_per_seq, per_row)
    consts = [lp['norm_ffn_pre'], lp['norm_ffn_post'], lp['ffn_w_gate'], lp['ffn_w_up'], lp['ffn_w_down']]
    x_spec = pl.BlockSpec((tm, D_MODEL), lambda i: (i, 0))
    return pl.pallas_call(
        functools.partial(_ffn_kernel, per_row),
        out_shape=jax.ShapeDtypeStruct(x2.shape, F32),
        grid=(n // tm,),
        in_specs=[x_spec] + mspecs + [_const_spec(a.shape) for a in consts],
        out_specs=x_spec,
        compiler_params=_params(("arbitrary",)),
        name="ffn_rows" if per_row else "ffn_seq",
    )(x2, *mops, *consts)


def _moe_kernel(per_row, x_ref, sh_ref, sc_ref, g_ref, npre_ref, npost_ref, wr_ref, br_ref,
                wg_ref, wu_ref, wd_ref, o_ref, h_s, gates_s, acc_s):
    e = pl.program_id(1)
    lane_e = lax.broadcasted_iota(jnp.int32, gates_s.shape, 1)

    @pl.when(e == 0)
    def _():
        x = x_ref[...]
        h = _rms(x, npre_ref[...]) * (1.0 + _mod_rows(sc_ref, per_row)) + _mod_rows(sh_ref, per_row)
        hb = h.astype(BF16)
        h_s[...] = hb
        logits = _dot(hb, wr_ref[...]) + br_ref[...]
        gates_s[...], _ = _top2_gates(logits, lane_e.astype(F32))
        acc_s[...] = jnp.zeros_like(acc_s)

    h = h_s[...]
    act = _silu(_dot(h, wg_ref[0, 0])) * _dot(h, wu_ref[0, 0])
    y = _dot(act.astype(BF16), wd_ref[0, 0])
    ge = jnp.sum(jnp.where(lane_e == e, gates_s[...], 0.0), axis=-1, keepdims=True)
    acc_s[...] += ge * y

    @pl.when(e == N_EXPERTS - 1)
    def _():
        o_ref[...] = x_ref[...] + _mod_rows(g_ref, per_row) * _rms(acc_s[...], npost_ref[...])


def _moe_call(x2, mods3, mods2, lp, rows_per_seq, per_row):
    n = x2.shape[0]
    tm = n if per_row else MOE_TM
    mspecs, mops = _mod_specs(mods3, mods2, n, tm, rows_per_seq, per_row)
    consts = [lp['norm_ffn_pre'], lp['norm_ffn_post'], lp['moe_w_router'], lp['moe_b_router']]
    x_spec = pl.BlockSpec((tm, D_MODEL), lambda i, e: (i, 0))
    w_spec = lambda a: pl.BlockSpec((1, 1) + a.shape[2:], lambda i, e: (0, e, 0, 0))
    experts = [lp['moe_w_gate'], lp['moe_w_up'], lp['moe_w_down']]
    return pl.pallas_call(
        functools.partial(_moe_kernel, per_row),
        out_shape=jax.ShapeDtypeStruct(x2.shape, F32),
        grid=(n // tm, N_EXPERTS),
        in_specs=[x_spec] + mspecs + [_const_spec(a.shape) for a in consts] + [w_spec(a) for a in experts],
        out_specs=x_spec,
        scratch_shapes=[pltpu.VMEM((tm, D_MODEL), BF16), pltpu.VMEM((tm, N_EXPERTS), F32),
                        pltpu.VMEM((tm, D_MODEL), F32)],
        compiler_params=_params(("arbitrary", "arbitrary")),
        name="moe_rows" if per_row else "moe_seq",
    )(x2, *mops, *consts, *experts)


def _top2_gates(logits, lane_f):
    n_lanes = float(logits.shape[-1])
    v1 = jnp.max(logits, axis=-1, keepdims=True)
    i1 = jnp.min(jnp.where(logits == v1, lane_f, n_lanes), axis=-1, keepdims=True)
    rest = jnp.where(lane_f == i1, -jnp.inf, logits)
    v2 = jnp.max(rest, axis=-1, keepdims=True)
    i2 = jnp.min(jnp.where(rest == v2, lane_f, n_lanes), axis=-1, keepdims=True)
    e2 = jnp.exp(v2 - v1)
    den = 1.0 + e2
    gates = jnp.where(lane_f == i1, 1.0 / den, 0.0) + jnp.where(lane_f == i2, e2 / den, 0.0)
    return gates, (lane_f == i1) | (lane_f == i2)


def _moe_routed_kernel(seq_len, n_rows, x_ref, sh0_ref, sh1_ref, sc0_ref, sc1_ref, g0_ref, g1_ref,
                       npre_ref, npost_ref, wr_ref, br_ref, wg_ref, wu_ref, wd_ref, o_ref,
                       h_s, before_s, dest_s, gate_s, acc_s, cnt_s):
    i = pl.program_id(0)
    e = pl.program_id(1)
    tm = x_ref.shape[0]
    row = lax.broadcasted_iota(jnp.int32, (tm, 1), 0) + i * tm
    valid = row < n_rows
    second = row >= ((i * tm) // seq_len + 1) * seq_len
    pick = lambda a_ref, b_ref: jnp.where(second, b_ref[0], a_ref[0])

    @pl.when((i == 0) & (e == 0))
    def _():
        r_i = lax.broadcasted_iota(jnp.int32, (tm, tm), 0)
        c_i = lax.broadcasted_iota(jnp.int32, (tm, tm), 1)
        before_s[...] = jnp.where(c_i < r_i, 1.0, 0.0).astype(BF16)

    @pl.when(e == 0)
    def _():
        x = jnp.where(valid, x_ref[...], 0.0)
        h = _rms(x, npre_ref[...]) * (1.0 + pick(sc0_ref, sc1_ref)) + pick(sh0_ref, sh1_ref)
        hb = h.astype(BF16)
        h_s[...] = hb
        lane_f = lax.broadcasted_iota(jnp.int32, (tm, LANES), 1).astype(F32)
        logits = jnp.where(lane_f < N_EXPERTS, _halves(_dot, hb, wr_ref[...]) + br_ref[...], -jnp.inf)
        gates, sel = _top2_gates(logits, lane_f)
        self = jnp.where(sel & valid, 1.0, 0.0)
        dest = jnp.where(self > 0.0, _halves(_dot, before_s[...], self.astype(BF16)), -1.0)
        dest_s[...] = dest.T[0:SUBLANES, :]
        gate_s[...] = gates.T[0:SUBLANES, :]
        counts = jnp.sum(self, axis=0, keepdims=True)
        for j in range(N_EXPERTS):
            cnt_s[j] = counts[0, j].astype(jnp.int32)
        acc_s[...] = jnp.zeros_like(acc_s)

    dest_row = dest_s[pl.ds(e, 1), :]
    gate_row = gate_s[pl.ds(e, 1), :]
    def chunk(base, size):
        slot = lax.broadcasted_iota(jnp.int32, (size, tm), 0).astype(F32) + base.astype(F32)
        onehot = jnp.where(slot == dest_row, 1.0, 0.0)
        gate_packed = jnp.sum(onehot * gate_row, axis=-1, keepdims=True)
        pack = onehot.astype(BF16)
        xg = _dot(pack, h_s[...]).astype(BF16)
        act = _silu(_dot(xg, wg_ref[0, 0])) * _dot(xg, wu_ref[0, 0])
        y = _dot(act.astype(BF16), wd_ref[0, 0]) * gate_packed
        acc_s[...] += lax.dot_general(pack, y.astype(BF16), (((0,), (0,)), ((), ())), preferred_element_type=F32)

    shift = MOE_CAP.bit_length() - 1
    cnt = cnt_s[e]
    n_full = lax.shift_right_logical(cnt, shift)
    rem = cnt - lax.shift_left(n_full, shift)
    long_last = (n_full >= 1) & (rem >= 1) & (rem <= MOE_TAIL)
    n_plain = jnp.where(long_last, n_full - 1, n_full + jnp.where(rem >= 1, 1, 0))

    def plain_chunk(c, _):
        chunk(c * MOE_CAP, MOE_CAP)
        return 0

    lax.fori_loop(0, n_plain, plain_chunk, 0)

    @pl.when(long_last)
    def _():
        chunk(n_plain * MOE_CAP, MOE_CAP + MOE_TAIL)

    @pl.when(e == N_EXPERTS - 1)
    def _():
        x = jnp.where(valid, x_ref[...], 0.0)
        o_ref[...] = x + pick(g0_ref, g1_ref) * _rms(acc_s[...], npost_ref[...])


def _moe_routed_call(x2, mods3, lp, seq_len):
    n = x2.shape[0]
    tm = MOE_SP_TM
    n_seq = n // seq_len
    nsamp = mods3.shape[0] - n_seq

    def mod_spec(col, nxt):
        return pl.BlockSpec((1, 1, D_MODEL),
                            lambda i, e: (nsamp + jnp.minimum((i * tm) // seq_len + nxt, n_seq - 1), 0, col))

    mspecs = [mod_spec(col, nxt) for col in (3, 4, 5) for nxt in (0, 1)]
    pad = LANES - N_EXPERTS
    consts = [lp['norm_ffn_pre'], lp['norm_ffn_post'],
              jnp.pad(lp['moe_w_router'], ((0, 0), (0, pad))), jnp.pad(lp['moe_b_router'], ((0, 0), (0, pad)))]
    x_spec = pl.BlockSpec((tm, D_MODEL), lambda i, e: (i, 0))
    w_spec = lambda a: pl.BlockSpec((1, 1) + a.shape[2:], lambda i, e: (0, e, 0, 0))
    experts = [lp['moe_w_gate'], lp['moe_w_up'], lp['moe_w_down']]
    return pl.pallas_call(
        functools.partial(_moe_routed_kernel, seq_len, n),
        out_shape=jax.ShapeDtypeStruct(x2.shape, F32),
        grid=(pl.cdiv(n, tm), N_EXPERTS),
        in_specs=[x_spec] + mspecs + [_const_spec(a.shape) for a in consts] + [w_spec(a) for a in experts],
        out_specs=x_spec,
        scratch_shapes=[pltpu.VMEM((tm, D_MODEL), BF16), pltpu.VMEM((tm, tm), BF16),
                        pltpu.VMEM((SUBLANES, tm), F32), pltpu.VMEM((SUBLANES, tm), F32),
                        pltpu.VMEM((tm, D_MODEL), F32), pltpu.SMEM((N_EXPERTS,), jnp.int32)],
        compiler_params=_params(("arbitrary", "arbitrary")),
        name="moe_routed",
    )(x2, *([mods3] * 6), *consts, *experts)


def _sproj_kernel(x_ref, sh_ref, sc_ref, npre_ref, wa_ref, wg_ref, pa_o, pg_o):
    h = (_rms(x_ref[...], npre_ref[...]) * (1.0 + sc_ref[...]) + sh_ref[...]).astype(BF16)
    pa_o[...] = _dot(h, wa_ref[...])
    for c in range(D_GATES // GATE_CHUNK):
        cols = slice(c * GATE_CHUNK, (c + 1) * GATE_CHUNK)
        pg_o[:, cols] = _dot(h, wg_ref[:, cols])


def _sproj_call(xs, mods2, lp):
    n = xs.shape[0]
    mod_spec = lambda col: pl.BlockSpec((n, D_MODEL), lambda i: (0, col))
    consts = [lp['norm_mix_pre'], lp['w_in_a'], lp['w_in_g']]
    return pl.pallas_call(
        _sproj_kernel,
        out_shape=(jax.ShapeDtypeStruct((n, D_SMALL), F32), jax.ShapeDtypeStruct((n, D_GATES), F32)),
        grid=(1,),
        in_specs=[_const_spec(xs.shape), mod_spec(0), mod_spec(1)] + [_const_spec(a.shape) for a in consts],
        out_specs=(_const_spec((n, D_SMALL)), _const_spec((n, D_GATES))),
        compiler_params=_params(("arbitrary",)),
        name="sample_proj",
    )(xs, mods2, mods2, *consts)


N_SMIX_IN = 22


def _smix_kernel(*refs):
    (u_ref, q_ref, kn_ref, vn_ref, knt_ref, vnt_ref, pin_ref, h0r_ref, h0i_ref, ck_ref, cv_ref, hist_ref,
     wbu_ref, abre_ref, abim_ref, cm_ref, dsk_ref, wglu_ref, bglu_ref, sinks_ref,
     wpool_ref, pscale_ref) = refs[:N_SMIX_IN]
    yssm_o, yatt_o, ypool_o, hr_o, hi_o, ck_o, cv_o, hist_o = refs[-8:]
    u = u_ref[...]
    bu = _dot(u.astype(BF16), wbu_ref[...])
    abr, abi = abre_ref[...], abim_ref[...]
    h0r, h0i = h0r_ref[...], h0i_ref[...]
    hr = bu[:, 0:D_STATE] + (abr * h0r - abi * h0i)
    hi = bu[:, D_STATE:2 * D_STATE] + (abr * h0i + abi * h0r)
    hr_o[...] = hr
    hi_o[...] = hi
    y = _dot(jnp.concatenate([hr, hi], axis=1).astype(BF16), cm_ref[...])
    yssm_o[...] = _ssm_out(y, u, dsk_ref[...], wglu_ref[...], bglu_ref[...])

    q = q_ref[...] * ATTN_SCALE
    kn, vn = kn_ref[...], vn_ref[...]
    kc = ck_ref[...].astype(BF16)
    vc = cv_ref[...].astype(BF16)
    lo = _lane_lo(q.shape)
    j_i = lax.broadcasted_iota(jnp.int32, (1, GQA, 1), 1)
    dist_c = (WINDOW - lax.broadcasted_iota(jnp.int32, (1, 1, WINDOW), 2)).astype(F32)
    halves = []
    for half in range(N_KV_HEADS):
        keep = lo if half == 0 else jnp.logical_not(lo)
        qm = jnp.where(keep, q, 0.0)
        slope = jnp.zeros((1, GQA, 1), F32)
        snk = jnp.zeros((1, GQA, 1), F32)
        for j in range(GQA):
            slope = jnp.where(j_i == j, SLOPES[half * GQA + j], slope)
            snk = jnp.where(j_i == j, sinks_ref[half * GQA + j], snk)
        s = jnp.einsum('bhd,bds->bhs', qm.astype(BF16), kc, preferred_element_type=F32) - slope * dist_c
        s_new = jnp.sum(qm * kn, axis=-1, keepdims=True)
        m = jnp.maximum(jnp.maximum(jnp.max(s, axis=-1, keepdims=True), s_new), snk)
        p = jnp.exp(s - m)
        p_new = jnp.exp(s_new - m)
        den = jnp.sum(p, axis=-1, keepdims=True) + p_new + jnp.exp(snk - m)
        o = jnp.einsum('bhs,bds->bhd', p.astype(BF16), vc, preferred_element_type=F32) + p_new * vn
        halves.append(o / den)
    yatt_o[...] = jnp.where(lo, halves[0], halves[1])

    bc = ck_ref.shape[0]
    lane_w = lax.broadcasted_iota(jnp.int32, (D_KV, WINDOW), 1)
    lane_b = lax.broadcasted_iota(jnp.int32, knt_ref.shape, 1)
    for src, new_t, dst in ((ck_ref, knt_ref, ck_o), (cv_ref, vnt_ref, cv_o)):
        for b in range(bc):
            col = jnp.sum(jnp.where(lane_b == pl.program_id(0) * bc + b, new_t[...], 0.0), axis=-1, keepdims=True)
            dst[b] = jnp.where(lane_w == WINDOW - 1, col, pltpu.roll(src[b], WINDOW - 1, axis=1))

    pin = pin_ref[...]
    lo2 = _lane_lo((pin.shape[0], LANES))
    mixed = []
    for col, (w_lo, w_hi) in enumerate(((POOL_WINDOWS[0], POOL_WINDOWS[1]), (POOL_WINDOWS[2], POOL_WINDOWS[3]))):
        cs = slice(col * LANES, (col + 1) * LANES)
        pf = pin[:, cs]
        acc = pf
        for k in range(1, w_lo):
            acc = acc + hist_ref[POOL_BUF - k, :, cs]
        acc_lo = acc
        for k in range(w_lo, w_hi):
            acc = acc + hist_ref[POOL_BUF - k, :, cs]
        cnt_lo = float(min(w_lo, PAST_LEN + 1))
        cnt_hi = float(min(w_hi, PAST_LEN + 1))
        mixed.append(jnp.where(lo2, acc_lo / cnt_lo, acc / cnt_hi) - pf)
    ypool_o[...] = _dot(jnp.concatenate(mixed, axis=1).astype(BF16), wpool_ref[...]) * pscale_ref[...]
    hist_o[0:POOL_BUF - 1] = hist_ref[1:POOL_BUF]
    hist_o[POOL_BUF - 1] = pin


def _smix_call(layer, u, q3, kn3, vn3, knt, vnt, pin, h0r, h0i, ck, cv, prev_windows, hist_t, lp):
    n = u.shape[0]
    bc = SAMPLE_BC
    rows2 = lambda c: pl.BlockSpec((bc, c), lambda i: (i, 0))
    rows3 = lambda r, c: pl.BlockSpec((bc, r, c), lambda i: (i, 0, 0))
    win_spec = pl.BlockSpec((None, bc, D_KV, WINDOW), lambda i: (layer, i, 0, 0))
    consts = [lp['w_bu'], lp['ab_re'], lp['ab_im'], lp['c_mat'], lp['ssm_d'], lp['ssm_w_glu'], lp['ssm_b_glu']]
    consts2 = [lp['pool_wbd'], lp['pool_scale']]
    hist_spec = pl.BlockSpec((POOL_BUF, bc, D_POOL), lambda i: (0, i, 0))
    in_specs = ([rows2(D_SSM), rows3(GQA, LANES), rows3(1, LANES), rows3(1, LANES),
                 _const_spec(knt.shape), _const_spec(vnt.shape), rows2(D_POOL),
                 rows2(D_STATE), rows2(D_STATE), win_spec, win_spec, hist_spec]
                + [_const_spec(a.shape) for a in consts]
                + [pl.BlockSpec(memory_space=pltpu.SMEM)]
                + [_const_spec(a.shape) for a in consts2]
                + [pl.BlockSpec(memory_space=pl.ANY) for _ in prev_windows])
    assert len(in_specs) == N_SMIX_IN + len(prev_windows)
    aliases = {N_SMIX_IN + k: 5 + k for k in range(len(prev_windows))}
    return pl.pallas_call(
        _smix_kernel,
        out_shape=(jax.ShapeDtypeStruct((n, D_SSM), F32), jax.ShapeDtypeStruct((n, GQA, LANES), F32),
                   jax.ShapeDtypeStruct((n, D_POOL), F32),
                   jax.ShapeDtypeStruct((n, D_STATE), F32), jax.ShapeDtypeStruct((n, D_STATE), F32),
                   jax.ShapeDtypeStruct(ck.shape, F32), jax.ShapeDtypeStruct(cv.shape, F32),
                   jax.ShapeDtypeStruct(hist_t.shape, F32)),
        grid=(n // bc,),
        in_specs=in_specs,
        out_specs=(rows2(D_SSM), rows3(GQA, LANES), rows2(D_POOL), rows2(D_STATE), rows2(D_STATE),
                   win_spec, win_spec, hist_spec),
        input_output_aliases=aliases,
        compiler_params=_params(("arbitrary",)),
        name="sample_mix",
    )(u, q3, kn3, vn3, knt, vnt, pin, h0r, h0i, ck, cv, hist_t, *consts, lp['attn_sinks'], *consts2, *prev_windows)


def _smerge_kernel(x_ref, g_ref, pg_ref, yssm_ref, yatt_ref, ypool_ref, npost_ref,
                   wus_ref, wua_ref, wup_ref, wout_ref, o_ref):
    merged = _sigmoid(pg_ref[:, 0:D_MODEL]) * _dot(yssm_ref[...].astype(BF16), wus_ref[...])
    merged += _sigmoid(pg_ref[:, D_MODEL:2 * D_MODEL]) * _dot(yatt_ref[...].astype(BF16), wua_ref[...])
    merged += _sigmoid(pg_ref[:, 2 * D_MODEL:3 * D_MODEL]) * _dot(ypool_ref[...].astype(BF16), wup_ref[...])
    out = _dot(merged.astype(BF16), wout_ref[...])
    o_ref[...] = x_ref[...] + g_ref[...] * _rms(out, npost_ref[...])


def _smerge_call(xs, mods2, pg, yssm, yatt, ypool, lp):
    n = xs.shape[0]
    ops = [xs, mods2, pg, yssm, yatt, ypool, lp['norm_mix_post'],
           lp['w_up_ssm'], lp['w_up_attn'], lp['w_up_pool'], lp['w_out']]
    in_specs = [_const_spec(a.shape) for a in ops]
    in_specs[1] = pl.BlockSpec((n, D_MODEL), lambda i: (0, 2))
    return pl.pallas_call(
        _smerge_kernel,
        out_shape=jax.ShapeDtypeStruct(xs.shape, F32),
        grid=(1,),
        in_specs=in_specs,
        out_specs=_const_spec(xs.shape),
        compiler_params=_params(("arbitrary",)),
        name="sample_merge",
    )(*ops)


def _block_diag(w):
    g, a, b = w.shape
    eye = jnp.eye(g, dtype=w.dtype)
    return jnp.einsum('gab,gh->gahb', w, eye).reshape(g * a, g * b)


def _pair_heads(w, axis):
    shape = w.shape
    w = w.reshape(shape[:axis] + (N_KV_HEADS, GQA, HEAD_DIM) + shape[axis + 1:])
    return jnp.swapaxes(w, axis, axis + 1).reshape(shape)


def _layer_params(l, p, prep):
    ab_re, ab_im, bb_re, bb_im, pw_re, pw_im = prep
    row = lambda a: a.reshape(1, -1)
    w_in = p['w_in'][l]
    q_cols = _pair_heads(w_in[:, D_SSM:D_SSM + D_ATTN], 1)
    w_in_a = jnp.concatenate([w_in[:, 0:D_SSM], q_cols, w_in[:, D_SSM + D_ATTN:D_SMALL]], axis=1)
    lp = {
        'norm_mix_pre': row(p['norm_mix_pre'][l]), 'norm_mix_post': row(p['norm_mix_post'][l]),
        'norm_ffn_pre': row(p['norm_ffn_pre'][l]), 'norm_ffn_post': row(p['norm_ffn_post'][l]),
        'w_in_a': w_in_a.astype(BF16),
        'w_in_g': w_in[:, D_SMALL:].astype(BF16),
        'w_bu': jnp.concatenate([_block_diag(bb_re[l]), _block_diag(bb_im[l])], axis=1).astype(BF16),
        'ab_re': ab_re[l].reshape(1, D_STATE), 'ab_im': ab_im[l].reshape(1, D_STATE),
        'pw_re': pw_re[l].reshape(SCAN_LEN, D_STATE), 'pw_im': pw_im[l].reshape(SCAN_LEN, D_STATE),
        'c_mat': jnp.concatenate([_block_diag(jnp.swapaxes(p['ssm_c_re'][l], 1, 2)),
                                  _block_diag(jnp.swapaxes(-p['ssm_c_im'][l], 1, 2))], axis=0).astype(BF16),
        'ssm_d': row(p['ssm_d'][l]), 'ssm_w_glu': p['ssm_w_glu'][l].astype(BF16), 'ssm_b_glu': row(p['ssm_b_glu'][l]),
        'attn_sinks': p['attn_sinks'][l],
        'pool_wbd': _block_diag(p['pool_w'][l]).astype(BF16), 'pool_scale': row(p['pool_scale'][l]),
        'w_up_ssm': p['w_up_ssm'][l].astype(BF16), 'w_up_attn': _pair_heads(p['w_up_attn'][l], 0).astype(BF16),
        'w_up_pool': p['w_up_pool'][l].astype(BF16), 'w_out': p['w_out'][l].astype(BF16),
    }
    j = l // 2
    if l % 2 == 1:
        lp.update({'moe_w_router': p['moe_w_router'][j].astype(BF16), 'moe_b_router': row(p['moe_b_router'][j]),
                   'moe_w_gate': p['moe_w_gate'][j:j + 1].astype(BF16), 'moe_w_up': p['moe_w_up'][j:j + 1].astype(BF16),
                   'moe_w_down': p['moe_w_down'][j:j + 1].astype(BF16)})
    else:
        lp.update({'ffn_w_gate': p['ffn_w_gate'][j].astype(BF16), 'ffn_w_up': p['ffn_w_up'][j].astype(BF16),
                   'ffn_w_down': p['ffn_w_down'][j].astype(BF16)})
    return lp


def kernel(x_prompt, x_sample, state_ssm_re, state_ssm_im, cache_win_k, cache_win_v, state_pool,
           c_prompt, c_sample, w_ada, b_ada, norm_mix_pre, norm_mix_post, norm_ffn_pre, norm_ffn_post,
           w_in, ssm_a_re, ssm_a_im, ssm_log_dt, ssm_b_re, ssm_b_im, ssm_c_re, ssm_c_im, ssm_d,
           ssm_w_glu, ssm_b_glu, attn_sinks, pool_w, pool_scale, w_up_ssm, w_up_attn, w_up_pool, w_out,
           ffn_w_gate, ffn_w_up, ffn_w_down, moe_w_router, moe_b_router, moe_w_gate, moe_w_up, moe_w_down):
    p = dict(norm_mix_pre=norm_mix_pre, norm_mix_post=norm_mix_post, norm_ffn_pre=norm_ffn_pre,
             norm_ffn_post=norm_ffn_post, w_in=w_in, ssm_c_re=ssm_c_re, ssm_c_im=ssm_c_im, ssm_d=ssm_d,
             ssm_w_glu=ssm_w_glu, ssm_b_glu=ssm_b_glu, attn_sinks=attn_sinks, pool_w=pool_w, pool_scale=pool_scale,
             w_up_ssm=w_up_ssm, w_up_attn=w_up_attn, w_up_pool=w_up_pool, w_out=w_out,
             ffn_w_gate=ffn_w_gate, ffn_w_up=ffn_w_up, ffn_w_down=ffn_w_down, moe_w_router=moe_w_router,
             moe_b_router=moe_b_router, moe_w_gate=moe_w_gate, moe_w_up=moe_w_up, moe_w_down=moe_w_down)
    depth = w_in.shape[0]
    b, t_len, d = x_prompt.shape
    ns = x_sample.shape[0]

    mods = _ada_call(jnp.concatenate([c_sample, c_prompt], axis=0), w_ada, b_ada)
    prep = _ssm_prep_call(ssm_a_re, ssm_a_im, ssm_log_dt, ssm_b_re, ssm_b_im)

    xp = x_prompt
    xs = x_sample.reshape(ns, d)
    p_out = ([], [], [], [], [])
    s_out = ([], [], [], [], [])
    win_minor = lambda c: jnp.transpose(c, (0, 1, 3, 4, 2)).reshape(depth, ns, D_KV, -1)
    win_major = lambda c: jnp.transpose(c.reshape(depth, ns, N_KV_HEADS, HEAD_DIM, -1), (0, 1, 4, 2, 3))
    old_windows = [win_minor(cache_win_k), win_minor(cache_win_v)]
    new_windows = []
    for l in range(depth):
        lp = _layer_params(l, p, prep)
        mods2 = mods[l]
        mods3 = mods2.reshape(ns + b, 1, N_MOD * d)

        xp, hre, him, kwin, vwin, plast = _mixer_call(xp, mods3, lp)
        x2 = xp.reshape(b * t_len, d)
        if l % 2 == 1:
            x2 = _moe_routed_call(x2, mods3, lp, t_len)
        else:
            x2 = _ffn_call(x2, mods3, mods2, lp, t_len, False)
        xp = x2.reshape(b, t_len, d)
        p_out[0].append(hre.reshape(b, SSM_GROUPS, SSM_STATE))
        p_out[1].append(him.reshape(b, SSM_GROUPS, SSM_STATE))
        p_out[2].append(kwin.reshape(b, WINDOW, N_KV_HEADS, HEAD_DIM))
        p_out[3].append(vwin.reshape(b, WINDOW, N_KV_HEADS, HEAD_DIM))
        p_out[4].append(plast[:, POOL_HIST - POOL_BUF:, :])

        pa, pg = _sproj_call(xs, mods2, lp)
        u = pa[:, 0:D_SSM]
        q3 = pa[:, D_SSM:D_SSM + D_ATTN].reshape(ns, GQA, LANES)
        k_new = pa[:, D_SSM + D_ATTN:D_SSM + D_ATTN + D_KV]
        v_new = pa[:, D_SSM + D_ATTN + D_KV:D_SSM + D_ATTN + 2 * D_KV]
        pin = pa[:, D_SMALL - D_POOL:]
        yssm, yatt3, ypool, hr, hi, *new_windows, hist_new = _smix_call(
            l, u, q3, k_new.reshape(ns, 1, D_KV), v_new.reshape(ns, 1, D_KV), k_new.T, v_new.T, pin,
            state_ssm_re[l].reshape(ns, D_STATE), state_ssm_im[l].reshape(ns, D_STATE),
            *old_windows, new_windows, jnp.swapaxes(state_pool[l], 0, 1), lp)
        xs = _smerge_call(xs, mods2, pg, yssm, yatt3.reshape(ns, D_ATTN), ypool, lp)
        xs = (_moe_call if l % 2 == 1 else _ffn_call)(xs, mods3, mods2, lp, 1, True)
        s_out[0].append(hr.reshape(ns, SSM_GROUPS, SSM_STATE))
        s_out[1].append(hi.reshape(ns, SSM_GROUPS, SSM_STATE))
        s_out[4].append(jnp.swapaxes(hist_new, 0, 1))

    stack = lambda xs_: [jnp.stack(a, axis=0) for a in xs_]
    p_ssm_re, p_ssm_im, p_win_k, p_win_v, p_pool = stack(p_out)
    s_ssm_re, s_ssm_im, s_pool = stack([s_out[0], s_out[1], s_out[4]])
    s_win_k, s_win_v = [win_major(w) for w in new_windows]
    return (xp, xs.reshape(x_sample.shape), p_ssm_re, p_ssm_im, p_win_k, p_win_v, p_pool,
            s_ssm_re, s_ssm_im, s_win_k, s_win_v, s_pool)
```

```python
import functools
import math
import types

import jax
import jax.numpy as jnp
from jax import lax
from jax.experimental import pallas as pl
from jax.experimental.pallas import tpu as pltpu

F32 = jnp.float32
BF16 = jnp.bfloat16

D_MODEL = 1024
D_SSM = 256
SSM_GROUP = 16
SSM_GROUPS = 16
SSM_STATE = 64
D_STATE = SSM_GROUPS * SSM_STATE
HEAD_DIM = 64
N_HEADS = 8
N_KV_HEADS = 2
GQA = N_HEADS // N_KV_HEADS
D_ATTN = N_HEADS * HEAD_DIM
D_KV = N_KV_HEADS * HEAD_DIM
WINDOW = 128
ATTN_SCALE = HEAD_DIM ** -0.5
NEG_INF = -1e30
LOG2E = math.log2(math.e)
D_POOL = 256
POOL_WINDOWS = (2, 4, 8, 16)
POOL_BUF = 15
POOL_HIST = 16
POOL_PAD = 8
N_MOD = 6
D_FF = 2816
N_EXPERTS = 8
D_EXPERT = 1024
RMS_EPS = 1e-6
PAST_LEN = 16384
D_SMALL = D_SSM + D_ATTN + 2 * D_KV + D_POOL
D_GATES = 3 * D_MODEL
SLOPES = tuple(2.0 ** (-8.0 * (h + 1) / N_HEADS) for h in range(N_HEADS))

SUBLANES = 8
LANES = 128
VMEM_LIMIT = 56 * 1024 * 1024

TT = 512
SEQ_PER_STEP = 1
SCAN_LEN = TT // SUBLANES
Q_BLK = 128
GATE_CHUNK = 256
BU_CHUNK = 512
MIXER_LAG = 0
GATE_LOOP_A = 8
GATE_LOOP_B = D_GATES // GATE_CHUNK - GATE_LOOP_A
FFN_TM = 512
MOE_TM = 1024
MOE_SP_TM = 896
MOE_CAP = 256
MOE_TAIL = 128
SAMPLE_BC = 32
ADA_TN = 1536
FF_CHUNKS = ((0, 1024), (1024, 2048), (2048, D_FF))


def _rms(x, gain):
    return x * lax.rsqrt(jnp.mean(x * x, axis=-1, keepdims=True) + RMS_EPS) * gain


def _sigmoid(x):
    return 1.0 / (1.0 + jnp.exp(-x))


def _silu(x):
    return x * _sigmoid(x)


def _dot(a, b):
    return jnp.dot(a, b, preferred_element_type=F32)


def _dot_t(a, b):
    return lax.dot_general(a, b, (((1,), (1,)), ((), ())), preferred_element_type=F32)


def _halves(dot, a, b):
    half = a.shape[0] // 2
    return jnp.concatenate([dot(a[:half], b), dot(a[half:], b)], axis=0)


def _const_spec(shape):
    nd = len(shape)
    return pl.BlockSpec(shape, lambda *_: (0,) * nd)


def _params(sem):
    return pltpu.CompilerParams(dimension_semantics=sem, vmem_limit_bytes=VMEM_LIMIT)


def _ada_kernel(c_ref, w_ref, b_ref, o_ref, oseq_ref):
    c = c_ref[...]
    s = _silu(c).astype(BF16)
    mod = _dot(s, w_ref[0].astype(BF16)) + b_ref[0]
    o_ref[0] = mod
    n_seq = oseq_ref.shape[1]
    first = mod.shape[0] - n_seq
    for s_i in range(n_seq):
        oseq_ref[0, s_i] = mod[first + s_i:first + s_i + 1, :]


def _ada_call(c_all, n_seq, w_ada, b_ada):
    depth = w_ada.shape[0]
    rows = c_all.shape[0]
    n = N_MOD * D_MODEL
    return pl.pallas_call(
        _ada_kernel,
        out_shape=(jax.ShapeDtypeStruct((depth, rows, n), F32), jax.ShapeDtypeStruct((depth, n_seq, 1, n), F32)),
        grid=(depth, n // ADA_TN),
        in_specs=[
            pl.BlockSpec((rows, D_MODEL), lambda l, j: (0, 0)),
            pl.BlockSpec((1, D_MODEL, ADA_TN), lambda l, j: (l, 0, j)),
            pl.BlockSpec((1, 1, ADA_TN), lambda l, j: (l, 0, j)),
        ],
        out_specs=(pl.BlockSpec((1, rows, ADA_TN), lambda l, j: (l, 0, j)),
                   pl.BlockSpec((1, n_seq, 1, ADA_TN), lambda l, j: (l, 0, 0, j))),
        compiler_params=_params(("arbitrary", "arbitrary")),
        name="ada_mod",
    )(c_all, w_ada, b_ada.reshape(depth, 1, n))


def _ssm_prep_kernel(are_ref, aim_ref, ldt_ref, bre_ref, bim_ref,
                     abre_o, abim_o, bbre_o, bbim_o, pwre_o, pwim_o):
    ar = are_ref[0]
    ai = aim_ref[0]
    dt = jnp.exp(ldt_ref[0])
    mag = jnp.exp(ar * dt)
    ab_re = mag * jnp.cos(ai * dt)
    ab_im = mag * jnp.sin(ai * dt)
    den = ar * ar + ai * ai
    n_re = ab_re - 1.0
    f_re = (n_re * ar + ab_im * ai) / den
    f_im = (ab_im * ar - n_re * ai) / den
    br = bre_ref[0]
    bi = bim_ref[0]
    bbre_o[0] = f_re * br - f_im * bi
    bbim_o[0] = f_re * bi + f_im * br
    abre_o[0] = ab_re
    abim_o[0] = ab_im
    cr, ci = ab_re, ab_im
    for t in range(SCAN_LEN):
        pwre_o[0, t] = cr
        pwim_o[0, t] = ci
        cr, ci = cr * ab_re - ci * ab_im, cr * ab_im + ci * ab_re


def _ssm_prep_call(a_re, a_im, log_dt, b_re, b_im):
    depth = a_re.shape[0]
    g, p, c = SSM_GROUPS, SSM_STATE, SSM_GROUP
    a4 = lambda a: a.reshape(depth, g, 1, p)
    spec_a = pl.BlockSpec((1, g, 1, p), lambda l: (l, 0, 0, 0))
    spec_b = pl.BlockSpec((1, g, c, p), lambda l: (l, 0, 0, 0))
    spec_pw = pl.BlockSpec((1, SCAN_LEN, g, 1, p), lambda l: (l, 0, 0, 0, 0))
    return pl.pallas_call(
        _ssm_prep_kernel,
        out_shape=(jax.ShapeDtypeStruct((depth, g, 1, p), F32),) * 2
        + (jax.ShapeDtypeStruct((depth, g, c, p), F32),) * 2
        + (jax.ShapeDtypeStruct((depth, SCAN_LEN, g, 1, p), F32),) * 2,
        grid=(depth,),
        in_specs=[spec_a, spec_a, pl.BlockSpec((1, g, 1, 1), lambda l: (l, 0, 0, 0)), spec_b, spec_b],
        out_specs=(spec_a, spec_a, spec_b, spec_b, spec_pw, spec_pw),
        compiler_params=_params(("arbitrary",)),
        name="ssm_prep",
    )(a4(a_re), a4(a_im), log_dt.reshape(depth, g, 1, 1),
      jnp.swapaxes(b_re, 2, 3), jnp.swapaxes(b_im, 2, 3))


def _ssm_out(y, u, d_skip, w_glu, b_glu):
    z = jax.nn.gelu(y + d_skip * u)
    return z * _sigmoid(_halves(_dot, z.astype(BF16), w_glu) + b_glu)


def _lane_lo(shape):
    return (lax.broadcasted_iota(jnp.int32, shape, len(shape) - 1) % LANES) < HEAD_DIM


N_MIXER_SEQ_IN = 4
N_MIXER_IN = 24
N_MIXER_OUT = 6


def _mixer_kernel(*refs):
    ins = refs[:N_MIXER_IN]
    outs = refs[N_MIXER_IN:N_MIXER_IN + N_MIXER_OUT]
    scratch = refs[N_MIXER_IN + N_MIXER_OUT:]
    bias_s = scratch[-1]
    t = pl.program_id(1)

    @pl.when((pl.program_id(0) == 0) & (t == 0))
    def _():
        r_i = lax.broadcasted_iota(jnp.int32, (Q_BLK, 2 * Q_BLK), 0)
        c_i = lax.broadcasted_iota(jnp.int32, (Q_BLK, 2 * Q_BLK), 1)
        dist = r_i - c_i + Q_BLK
        valid = (dist >= 0) & (dist <= WINDOW)
        for hh in range(N_HEADS):
            bias_s[hh] = jnp.where(valid, (-SLOPES[hh] * LOG2E) * dist.astype(F32), NEG_INF)

    lvl_s, kbuf, vbuf, pbuf, hc_s = scratch[-6:-1]

    @pl.when(t == 0)
    def _():
        hc_s[...] = jnp.zeros_like(hc_s)
        kbuf[:, 0:WINDOW, :] = jnp.zeros((SEQ_PER_STEP, WINDOW, D_KV), BF16)
        vbuf[:, 0:WINDOW, :] = jnp.zeros((SEQ_PER_STEP, WINDOW, D_KV), BF16)
        pbuf[:, 0:POOL_PAD + POOL_HIST, :] = jnp.zeros((SEQ_PER_STEP, POOL_PAD + POOL_HIST, D_POOL), F32)
        lvl_s[:, :, 0:POOL_PAD, :] = jnp.zeros((SEQ_PER_STEP, 2, POOL_PAD, LANES), F32)

    chains = []
    for s in range(SEQ_PER_STEP):
        one = functools.partial(lambda s, r: r.at[s:s + 1], s)
        chains.append(_mixer_body(t, *[one(r) for r in ins[:N_MIXER_SEQ_IN]], *ins[N_MIXER_SEQ_IN:],
                                  *[one(r) for r in outs], *[r.at[s] for r in scratch[:-1]], bias_s))
    for _ in range(MIXER_LAG):
        next(chains[0])
    while chains:
        for chain in list(chains):
            if next(chain, StopIteration) is StopIteration:
                chains.remove(chain)


MIXER_REF_NAMES = (
    'x_ref sh_ref sc_ref g_ref npre_ref npost_ref wa_ref wg_ref '
    'wbu_ref abre_ref abim_ref pwre_ref pwim_ref cm_ref dsk_ref wglu_ref bglu_ref '
    'sinks_ref wpool_ref pscale_ref wus_ref wua_ref wup_ref wout_ref '
    'xo_ref hre_o him_o kwin_o vwin_o plast_o '
    'hb_s proj_s bu_s hst_s sg_s unat_s uperm_s operm_s yssm_s yatt_s lvl_s kbuf vbuf pbuf hc_s bias_s').split()


def _mixer_body(t, *refs):
    r = types.SimpleNamespace(**dict(zip(MIXER_REF_NAMES, refs, strict=True)))
    x = r.x_ref[0]
    h = _rms(x, r.npre_ref[...]) * (1.0 + r.sc_ref[0]) + r.sh_ref[0]
    r.hb_s[...] = h.astype(BF16)
    yield
    for c in range(D_SMALL // GATE_CHUNK):
        cols = slice(c * GATE_CHUNK, (c + 1) * GATE_CHUNK)
        r.proj_s[:, cols] = _dot(r.hb_s[...], r.wa_ref[c])
        yield
    res = {}
    yield from _ssm_chain(r)
    yield from _attn_chain(t, r)
    yield from _pool_chain(t, r, res)
    yield from _merge_tail(r, res)


def _ssm_chain(r):
    (proj_s, hb_s, bu_s, hst_s, sg_s, unat_s, uperm_s, operm_s, yssm_s, hc_s, wg_ref, wbu_ref, abre_ref, abim_ref,
     pwre_ref, pwim_ref, cm_ref, dsk_ref, wglu_ref, bglu_ref, hre_o, him_o) = (
        r.proj_s, r.hb_s, r.bu_s, r.hst_s, r.sg_s, r.unat_s, r.uperm_s, r.operm_s, r.yssm_s, r.hc_s, r.wg_ref,
        r.wbu_ref, r.abre_ref, r.abim_ref, r.pwre_ref, r.pwim_ref, r.cm_ref, r.dsk_ref, r.wglu_ref, r.bglu_ref,
        r.hre_o, r.him_o)
    for c in range(D_SSM // LANES):
        unat_s[c] = proj_s[:, c * LANES:(c + 1) * LANES]
        for i in range(SCAN_LEN):
            uperm_s[i * SUBLANES:(i + 1) * SUBLANES, c * LANES:(c + 1) * LANES] = (
                unat_s.at[c][pl.ds(i, SUBLANES, stride=SCAN_LEN), :])
        yield
    for c in range(2 * D_STATE // BU_CHUNK):
        cols = slice(c * BU_CHUNK, (c + 1) * BU_CHUNK)
        bu_s[:, cols] = _dot(uperm_s[...].astype(BF16), wbu_ref[c])
        yield

    abr = jnp.broadcast_to(abre_ref[...], (SUBLANES, D_STATE))
    abi = jnp.broadcast_to(abim_ref[...], (SUBLANES, D_STATE))

    def gate_chunk(c):
        sg_s[c] = _sigmoid(_dot(hb_s[...], wg_ref[c]))

    zero = jnp.zeros((SUBLANES, D_STATE), F32)
    steps_a = SCAN_LEN // GATE_LOOP_A

    er, ei = zero, zero
    for i in range(GATE_LOOP_A):
        gate_chunk(i)
        yield
        for s in range(steps_a):
            rows = slice((i * steps_a + s) * SUBLANES, (i * steps_a + s + 1) * SUBLANES)
            er, ei = (abr * er - abi * ei + bu_s[rows, 0:D_STATE],
                      abr * ei + abi * er + bu_s[rows, D_STATE:2 * D_STATE])
            bu_s[rows, 0:D_STATE] = er
            bu_s[rows, D_STATE:2 * D_STATE] = ei
        yield

    alr = pwre_ref[SCAN_LEN - 1:SCAN_LEN, :]
    ali = pwim_ref[SCAN_LEN - 1:SCAN_LEN, :]
    cr = hc_s[0:1, :]
    ci = hc_s[1:2, :]
    row = lax.broadcasted_iota(jnp.int32, (SUBLANES, D_STATE), 0)
    hin_r = zero
    hin_i = zero
    for j in range(SUBLANES):
        hin_r = jnp.where(row == j, cr, hin_r)
        hin_i = jnp.where(row == j, ci, hin_i)
        cr, ci = (alr * cr - ali * ci + er[j:j + 1, :], alr * ci + ali * cr + ei[j:j + 1, :])
    hc_s[0:1, :] = cr
    hc_s[1:2, :] = ci
    hre_o[0] = cr
    him_o[0] = ci
    yield

    pack = 2 * SUBLANES
    packs_b = TT // pack // GATE_LOOP_B

    for i in range(GATE_LOOP_B):
        gate_chunk(GATE_LOOP_A + i)
        yield
        for s in range(packs_b):
            k = i * packs_b + s
            fixed_r, fixed_i = [], []
            for slab in (2 * k, 2 * k + 1):
                rows = slice(slab * SUBLANES, (slab + 1) * SUBLANES)
                pr = pwre_ref[slab:slab + 1, :]
                pi = pwim_ref[slab:slab + 1, :]
                fixed_r.append(bu_s[rows, 0:D_STATE] + (pr * hin_r - pi * hin_i))
                fixed_i.append(bu_s[rows, D_STATE:2 * D_STATE] + (pr * hin_i + pi * hin_r))
            prow = slice(k * pack, (k + 1) * pack)
            hst_s[prow, 0:D_STATE] = jnp.concatenate(fixed_r, axis=0).astype(BF16)
            hst_s[prow, D_STATE:2 * D_STATE] = jnp.concatenate(fixed_i, axis=0).astype(BF16)
        yield

    y = _halves(_dot, hst_s[...], cm_ref[...])
    yield
    operm_s[...] = _ssm_out(y, uperm_s[...], dsk_ref[...], wglu_ref[...], bglu_ref[...])
    yield
    for c in range(D_SSM // LANES):
        for i in range(SCAN_LEN):
            yssm_s.at[c][pl.ds(i, SUBLANES, stride=SCAN_LEN), :] = (
                operm_s[i * SUBLANES:(i + 1) * SUBLANES, c * LANES:(c + 1) * LANES])
        yield

def _attn_chain(t, r):
    proj_s, kbuf, vbuf, yatt_s, bias_s, sinks_ref, kwin_o, vwin_o = (
        r.proj_s, r.kbuf, r.vbuf, r.yatt_s, r.bias_s, r.sinks_ref, r.kwin_o, r.vwin_o)
    kbuf[WINDOW:WINDOW + TT, :] = proj_s[:, D_SSM + D_ATTN:D_SSM + D_ATTN + D_KV].astype(BF16)
    vbuf[WINDOW:WINDOW + TT, :] = proj_s[:, D_SSM + D_ATTN + D_KV:D_SSM + D_ATTN + 2 * D_KV].astype(BF16)
    lo = _lane_lo((Q_BLK, LANES))
    r_i = lax.broadcasted_iota(jnp.int32, (Q_BLK, 2 * Q_BLK), 0)
    c_i = lax.broadcasted_iota(jnp.int32, (Q_BLK, 2 * Q_BLK), 1)
    first_key = jnp.where(t == 0, Q_BLK, 0)
    def scores(blk):
        r0 = blk * Q_BLK
        pieces = []
        for hh in range(N_HEADS):
            j = hh % GQA
            qp = proj_s[r0:r0 + Q_BLK, D_SSM + j * LANES:D_SSM + (j + 1) * LANES] * (ATTN_SCALE * LOG2E)
            keep = lo if hh < GQA else jnp.logical_not(lo)
            pieces.append(jnp.where(keep, qp, 0.0).astype(BF16))
        return [_dot_t(jnp.concatenate(pieces[g * GQA:(g + 1) * GQA], axis=0), kbuf[r0:r0 + 2 * Q_BLK, :])
                for g in range(N_KV_HEADS)]

    n_blk = TT // Q_BLK
    s_next = scores(0)
    yield
    for blk in range(n_blk):
        r0 = blk * Q_BLK
        s_all = s_next
        if blk + 1 < n_blk:
            s_next = scores(blk + 1)
            yield
        probs, dens = [], []
        for hh in range(N_HEADS):
            s = s_all[hh // GQA][(hh % GQA) * Q_BLK:(hh % GQA + 1) * Q_BLK, :] + bias_s[hh]
            if blk == 0:
                s = jnp.where(c_i >= first_key, s, NEG_INF)
            snk = sinks_ref[hh] * LOG2E
            m = jnp.maximum(jnp.max(s, axis=-1, keepdims=True), snk)
            p = jnp.exp2(s - m)
            dens.append(jnp.sum(p, axis=-1, keepdims=True) + jnp.exp2(snk - m))
            probs.append(p.astype(BF16))
            yield
        o_all = [_dot(jnp.concatenate(probs[g * GQA:(g + 1) * GQA], axis=0), vbuf[r0:r0 + 2 * Q_BLK, :])
                 for g in range(N_KV_HEADS)]
        out = lambda hh: o_all[hh // GQA][(hh % GQA) * Q_BLK:(hh % GQA + 1) * Q_BLK, :] / dens[hh]
        for j in range(GQA):
            yatt_s[r0:r0 + Q_BLK, j * LANES:(j + 1) * LANES] = jnp.where(lo, out(j), out(GQA + j)).astype(BF16)
        yield
    kwin_o[0] = proj_s[TT - WINDOW:TT, D_SSM + D_ATTN:D_SSM + D_ATTN + D_KV]
    vwin_o[0] = proj_s[TT - WINDOW:TT, D_SSM + D_ATTN + D_KV:D_SSM + D_ATTN + 2 * D_KV]
    kbuf[0:WINDOW, :] = kbuf[TT:TT + WINDOW, :]
    vbuf[0:WINDOW, :] = vbuf[TT:TT + WINDOW, :]
    yield

def _pool_chain(t, r, res):
    proj_s, pbuf, lvl_s, plast_o, wpool_ref, pscale_ref = (
        r.proj_s, r.pbuf, r.lvl_s, r.plast_o, r.wpool_ref, r.pscale_ref)
    p0 = D_SMALL - D_POOL
    cur = POOL_PAD + POOL_HIST
    end = cur + TT
    pbuf[cur:end, :] = proj_s[:, p0:D_SMALL]
    plast_o[0] = proj_s[TT - POOL_HIST:TT, p0:D_SMALL]
    pos1 = (t * TT + 1 + lax.broadcasted_iota(jnp.int32, (TT, LANES), 0)).astype(F32)
    lo_t = _lane_lo((TT, LANES))
    a_s, b_s = lvl_s.at[0], lvl_s.at[1]
    mixed = []
    for col, (w_lo, w_hi) in enumerate(((POOL_WINDOWS[0], POOL_WINDOWS[1]), (POOL_WINDOWS[2], POOL_WINDOWS[3]))):
        cs = slice(col * LANES, (col + 1) * LANES)
        pf = pbuf[cur:end, cs]
        a_s[POOL_PAD:end, :] = pbuf[POOL_PAD:end, cs] + pbuf[POOL_PAD - 1:end - 1, cs]
        have = 2
        while have < w_lo:
            b_s[POOL_PAD:end, :] = a_s[POOL_PAD:end, :] + a_s[POOL_PAD - have:end - have, :]
            a_s, b_s, have = b_s, a_s, 2 * have
        acc_lo = a_s[cur:end, :]
        while 2 * have < w_hi:
            b_s[POOL_PAD:end, :] = a_s[POOL_PAD:end, :] + a_s[POOL_PAD - have:end - have, :]
            a_s, b_s, have = b_s, a_s, 2 * have
        acc_hi = a_s[cur:end, :] + a_s[cur - have:end - have, :]
        cnt = jnp.where(lo_t, jnp.minimum(float(w_lo), pos1), jnp.minimum(float(w_hi), pos1))
        mixed.append(jnp.where(lo_t, acc_lo, acc_hi) / cnt - pf)
        yield
    ypool = _halves(_dot, jnp.concatenate(mixed, axis=1).astype(BF16), wpool_ref[...]) * pscale_ref[...]
    pbuf[POOL_PAD:cur, :] = pbuf[end - POOL_HIST:end, :]
    res['ypool'] = ypool
    yield


def _merge_tail(r, res):
    sg_s, yssm_s, yatt_s, wus_ref, wua_ref, wup_ref, wout_ref, x_ref, g_ref, npost_ref, xo_ref = (
        r.sg_s, r.yssm_s, r.yatt_s, r.wus_ref, r.wua_ref, r.wup_ref, r.wout_ref, r.x_ref, r.g_ref, r.npost_ref,
        r.xo_ref)
    ypool = res['ypool']
    per_branch = D_MODEL // GATE_CHUNK
    gate = lambda b: jnp.concatenate([sg_s[b * per_branch + c] for c in range(per_branch)], axis=1)
    yssm = jnp.concatenate([yssm_s[c] for c in range(D_SSM // LANES)], axis=1)
    merged = gate(0) * _dot(yssm.astype(BF16), wus_ref[...])
    yield
    merged += gate(1) * _dot(yatt_s[...], wua_ref[...])
    yield
    merged += gate(2) * _dot(ypool.astype(BF16), wup_ref[...])
    yield
    out = _dot(merged.astype(BF16), wout_ref[...])
    yield
    xo_ref[0] = x_ref[0] + g_ref[0] * _rms(out, npost_ref[...])


def _mixer_call(x, mods3, lp):
    b, t_len, d = x.shape
    nt = t_len // TT
    nsamp = mods3.shape[0] - b
    sps = SEQ_PER_STEP
    mod_spec = lambda col: pl.BlockSpec((sps, 1, D_MODEL), lambda i, j: (nsamp // sps + i, 0, col))
    x_spec = pl.BlockSpec((sps, TT, d), lambda i, j: (i, j, 0))
    consts = [lp['norm_mix_pre'], lp['norm_mix_post'], lp['w_in_a'], lp['w_in_g'],
              lp['w_bu'], lp['ab_re'], lp['ab_im'], lp['pw_re'], lp['pw_im'], lp['c_mat'],
              lp['ssm_d'], lp['ssm_w_glu'], lp['ssm_b_glu']]
    consts2 = [lp['pool_wbd'], lp['pool_scale'], lp['w_up_ssm'], lp['w_up_attn'], lp['w_up_pool'], lp['w_out']]
    in_specs = ([x_spec, mod_spec(0), mod_spec(1), mod_spec(2)]
                + [_const_spec(a.shape) for a in consts]
                + [pl.BlockSpec(memory_space=pltpu.SMEM)]
                + [_const_spec(a.shape) for a in consts2])
    per_b = lambda r, c: pl.BlockSpec((sps, r, c), lambda i, j: (i, 0, 0))
    out_shape = (jax.ShapeDtypeStruct((b, t_len, d), F32),
                 jax.ShapeDtypeStruct((b, 1, D_STATE), F32), jax.ShapeDtypeStruct((b, 1, D_STATE), F32),
                 jax.ShapeDtypeStruct((b, WINDOW, D_KV), F32), jax.ShapeDtypeStruct((b, WINDOW, D_KV), F32),
                 jax.ShapeDtypeStruct((b, POOL_HIST, D_POOL), F32))
    out_specs = (x_spec, per_b(1, D_STATE), per_b(1, D_STATE), per_b(WINDOW, D_KV), per_b(WINDOW, D_KV),
                 per_b(POOL_HIST, D_POOL))
    per_seq = lambda shape, dtype: pltpu.VMEM((sps,) + shape, dtype)
    scratch = [per_seq((TT, d), BF16), per_seq((TT, D_SMALL), F32), per_seq((TT, 2 * D_STATE), F32),
               per_seq((TT, 2 * D_STATE), BF16), per_seq((D_GATES // GATE_CHUNK, TT, GATE_CHUNK), F32),
               per_seq((D_SSM // LANES, TT, LANES), F32), per_seq((TT, D_SSM), F32),
               per_seq((TT, D_SSM), F32), per_seq((D_SSM // LANES, TT, LANES), F32),
               per_seq((TT, D_ATTN), BF16), per_seq((2, POOL_PAD + POOL_HIST + TT, LANES), F32),
               per_seq((WINDOW + TT, D_KV), BF16), per_seq((WINDOW + TT, D_KV), BF16),
               per_seq((POOL_PAD + POOL_HIST + TT, D_POOL), F32), per_seq((SUBLANES, D_STATE), F32),
               pltpu.VMEM((N_HEADS, Q_BLK, 2 * Q_BLK), F32)]
    return pl.pallas_call(
        _mixer_kernel,
        out_shape=out_shape,
        grid=(b // sps, nt),
        in_specs=in_specs,
        out_specs=out_specs,
        scratch_shapes=scratch,
        compiler_params=_params(("arbitrary", "arbitrary")),
        name="prompt_mixer",
    )(x, mods3, mods3, mods3, *consts, lp['attn_sinks'], *consts2)


def _mod_rows(ref, per_row):
    return ref[...] if per_row else ref[0]


def _ffn_kernel(per_row, x_ref, sh_ref, sc_ref, g_ref, npre_ref, npost_ref, *w_and_out):
    n_c = len(FF_CHUNKS)
    wg_refs, wu_refs = w_and_out[:n_c], w_and_out[n_c:2 * n_c]
    wd_ref, o_ref = w_and_out[2 * n_c:]
    x = x_ref[...]
    h = (_rms(x, npre_ref[...]) * (1.0 + _mod_rows(sc_ref, per_row)) + _mod_rows(sh_ref, per_row)).astype(BF16)
    f = None
    for (c0, c1), wg_ref, wu_ref in zip(FF_CHUNKS, wg_refs, wu_refs):
        act = _silu(_dot(h, wg_ref[...])) * _dot(h, wu_ref[...])
        part = _dot(act.astype(BF16), wd_ref[c0:c1, :])
        f = part if f is None else f + part
    o_ref[...] = x + _mod_rows(g_ref, per_row) * _rms(f, npost_ref[...])


def _mod_specs(mods3, mods2, n_rows, tm, rows_per_seq, per_row):
    nsamp = mods3.shape[0] - (n_rows // rows_per_seq if not per_row else 0)
    if per_row:
        specs = [pl.BlockSpec((tm, D_MODEL), functools.partial(lambda col, i, *_: (i, col), col)) for col in (3, 4, 5)]
        return specs, [mods2] * 3
    tiles_per_seq = rows_per_seq // tm
    specs = [pl.BlockSpec((1, 1, D_MODEL),
                          functools.partial(lambda col, i, *_: (nsamp + i // tiles_per_seq, 0, col), col))
             for col in (3, 4, 5)]
    return specs, [mods3] * 3


def _ffn_call(x2, mods3, mods2, lp, rows_per_seq, per_row):
    n = x2.shape[0]
    tm = n if per_row else FFN_TM
    mspecs, mops = _mod_specs(mods3, mods2, n, tm, rows_per_seq, per_row)
    consts = [lp['norm_ffn_pre'], lp['norm_ffn_post'], *lp['ffn_w_gate'], *lp['ffn_w_up'], lp['ffn_w_down']]
    x_spec = pl.BlockSpec((tm, D_MODEL), lambda i: (i, 0))
    return pl.pallas_call(
        functools.partial(_ffn_kernel, per_row),
        out_shape=jax.ShapeDtypeStruct(x2.shape, F32),
        grid=(n // tm,),
        in_specs=[x_spec] + mspecs + [_const_spec(a.shape) for a in consts],
        out_specs=x_spec,
        compiler_params=_params(("arbitrary",)),
        name="ffn_rows" if per_row else "ffn_seq",
    )(x2, *mops, *consts)


def _moe_kernel(per_row, x_ref, sh_ref, sc_ref, g_ref, npre_ref, npost_ref, wr_ref, br_ref,
                wg_ref, wu_ref, wd_ref, o_ref, h_s, gates_s, acc_s):
    e = pl.program_id(1)
    lane_e = lax.broadcasted_iota(jnp.int32, gates_s.shape, 1)

    @pl.when(e == 0)
    def _():
        x = x_ref[...]
        h = _rms(x, npre_ref[...]) * (1.0 + _mod_rows(sc_ref, per_row)) + _mod_rows(sh_ref, per_row)
        hb = h.astype(BF16)
        h_s[...] = hb
        logits = _dot(hb, wr_ref[...]) + br_ref[...]
        gates_s[...], _ = _top2_gates(logits, lane_e.astype(F32))
        acc_s[...] = jnp.zeros_like(acc_s)

    h = h_s[...]
    act = _silu(_dot(h, wg_ref[0, 0])) * _dot(h, wu_ref[0, 0])
    y = _dot(act.astype(BF16), wd_ref[0, 0])
    ge = jnp.sum(jnp.where(lane_e == e, gates_s[...], 0.0), axis=-1, keepdims=True)
    acc_s[...] += ge * y

    @pl.when(e == N_EXPERTS - 1)
    def _():
        o_ref[...] = x_ref[...] + _mod_rows(g_ref, per_row) * _rms(acc_s[...], npost_ref[...])


def _moe_call(x2, mods3, mods2, lp, rows_per_seq, per_row):
    n = x2.shape[0]
    tm = n if per_row else MOE_TM
    mspecs, mops = _mod_specs(mods3, mods2, n, tm, rows_per_seq, per_row)
    consts = [lp['norm_ffn_pre'], lp['norm_ffn_post'], lp['moe_w_router'], lp['moe_b_router']]
    x_spec = pl.BlockSpec((tm, D_MODEL), lambda i, e: (i, 0))
    w_spec = lambda a: pl.BlockSpec((1, 1) + a.shape[2:], lambda i, e: (0, e, 0, 0))
    experts = [lp['moe_w_gate'], lp['moe_w_up'], lp['moe_w_down']]
    return pl.pallas_call(
        functools.partial(_moe_kernel, per_row),
        out_shape=jax.ShapeDtypeStruct(x2.shape, F32),
        grid=(n // tm, N_EXPERTS),
        in_specs=[x_spec] + mspecs + [_const_spec(a.shape) for a in consts] + [w_spec(a) for a in experts],
        out_specs=x_spec,
        scratch_shapes=[pltpu.VMEM((tm, D_MODEL), BF16), pltpu.VMEM((tm, N_EXPERTS), F32),
                        pltpu.VMEM((tm, D_MODEL), F32)],
        compiler_params=_params(("arbitrary", "arbitrary")),
        name="moe_rows" if per_row else "moe_seq",
    )(x2, *mops, *consts, *experts)


def _top2_gates(logits, lane_f):
    n_lanes = float(logits.shape[-1])
    v1 = jnp.max(logits, axis=-1, keepdims=True)
    i1 = jnp.min(jnp.where(logits == v1, lane_f, n_lanes), axis=-1, keepdims=True)
    rest = jnp.where(lane_f == i1, -jnp.inf, logits)
    v2 = jnp.max(rest, axis=-1, keepdims=True)
    i2 = jnp.min(jnp.where(rest == v2, lane_f, n_lanes), axis=-1, keepdims=True)
    e2 = jnp.exp(v2 - v1)
    den = 1.0 + e2
    gates = jnp.where(lane_f == i1, 1.0 / den, 0.0) + jnp.where(lane_f == i2, e2 / den, 0.0)
    return gates, (lane_f == i1) | (lane_f == i2)


def _moe_routed_kernel(seq_len, n_rows, x_ref, sh0_ref, sh1_ref, sc0_ref, sc1_ref, g0_ref, g1_ref,
                       npre_ref, npost_ref, wr_ref, br_ref, wg_ref, wu_ref, wd_ref, o_ref,
                       h_s, before_s, dest_s, gate_s, acc_s, cnt_s):
    i = pl.program_id(0)
    e = pl.program_id(1)
    tm = x_ref.shape[0]
    row = lax.broadcasted_iota(jnp.int32, (tm, 1), 0) + i * tm
    valid = row < n_rows
    second = row >= ((i * tm) // seq_len + 1) * seq_len
    pick = lambda a_ref, b_ref: jnp.where(second, b_ref[0], a_ref[0])

    @pl.when((i == 0) & (e == 0))
    def _():
        r_i = lax.broadcasted_iota(jnp.int32, (tm, tm), 0)
        c_i = lax.broadcasted_iota(jnp.int32, (tm, tm), 1)
        before_s[...] = jnp.where(c_i < r_i, 1.0, 0.0).astype(BF16)

    @pl.when(e == 0)
    def _():
        x = jnp.where(valid, x_ref[...], 0.0)
        h = _rms(x, npre_ref[...]) * (1.0 + pick(sc0_ref, sc1_ref)) + pick(sh0_ref, sh1_ref)
        hb = h.astype(BF16)
        h_s[...] = hb
        lane_f = lax.broadcasted_iota(jnp.int32, (tm, LANES), 1).astype(F32)
        logits = jnp.where(lane_f < N_EXPERTS, _halves(_dot, hb, wr_ref[...]) + br_ref[...], -jnp.inf)
        gates, sel = _top2_gates(logits, lane_f)
        self = jnp.where(sel & valid, 1.0, 0.0)
        dest = jnp.where(self > 0.0, _halves(_dot, before_s[...], self.astype(BF16)), -1.0)
        dest_s[...] = dest.T[0:SUBLANES, :]
        gate_s[...] = gates.T[0:SUBLANES, :]
        counts = jnp.sum(self, axis=0, keepdims=True)
        for j in range(N_EXPERTS):
            cnt_s[j] = counts[0, j].astype(jnp.int32)
        acc_s[...] = jnp.zeros_like(acc_s)

    dest_row = dest_s[pl.ds(e, 1), :]
    gate_row = gate_s[pl.ds(e, 1), :]
    def chunk(base, size):
        slot = lax.broadcasted_iota(jnp.int32, (size, tm), 0).astype(F32) + base.astype(F32)
        onehot = jnp.where(slot == dest_row, 1.0, 0.0)
        gate_packed = jnp.sum(onehot * gate_row, axis=-1, keepdims=True)
        pack = onehot.astype(BF16)
        xg = _dot(pack, h_s[...]).astype(BF16)
        act = _silu(_dot(xg, wg_ref[0, 0])) * _dot(xg, wu_ref[0, 0])
        y = _dot(act.astype(BF16), wd_ref[0, 0]) * gate_packed
        acc_s[...] += lax.dot_general(pack, y.astype(BF16), (((0,), (0,)), ((), ())), preferred_element_type=F32)

    shift = MOE_CAP.bit_length() - 1
    cnt = cnt_s[e]
    n_full = lax.shift_right_logical(cnt, shift)
    rem = cnt - lax.shift_left(n_full, shift)
    long_last = (n_full >= 1) & (rem >= 1) & (rem <= MOE_TAIL)
    n_plain = jnp.where(long_last, n_full - 1, n_full + jnp.where(rem >= 1, 1, 0))

    def plain_chunk(c, _):
        chunk(c * MOE_CAP, MOE_CAP)
        return 0

    lax.fori_loop(0, n_plain, plain_chunk, 0)

    @pl.when(long_last)
    def _():
        chunk(n_plain * MOE_CAP, MOE_CAP + MOE_TAIL)

    @pl.when(e == N_EXPERTS - 1)
    def _():
        x = jnp.where(valid, x_ref[...], 0.0)
        o_ref[...] = x + pick(g0_ref, g1_ref) * _rms(acc_s[...], npost_ref[...])


def _moe_routed_call(x2, mods3, lp, seq_len):
    n = x2.shape[0]
    tm = MOE_SP_TM
    n_seq = n // seq_len
    nsamp = mods3.shape[0] - n_seq

    def mod_spec(col, nxt):
        return pl.BlockSpec((1, 1, D_MODEL),
                            lambda i, e: (nsamp + jnp.minimum((i * tm) // seq_len + nxt, n_seq - 1), 0, col))

    mspecs = [mod_spec(col, nxt) for col in (3, 4, 5) for nxt in (0, 1)]
    pad = LANES - N_EXPERTS
    consts = [lp['norm_ffn_pre'], lp['norm_ffn_post'],
              jnp.pad(lp['moe_w_router'], ((0, 0), (0, pad))), jnp.pad(lp['moe_b_router'], ((0, 0), (0, pad)))]
    x_spec = pl.BlockSpec((tm, D_MODEL), lambda i, e: (i, 0))
    w_spec = lambda a: pl.BlockSpec((1, 1) + a.shape[2:], lambda i, e: (0, e, 0, 0))
    experts = [lp['moe_w_gate'], lp['moe_w_up'], lp['moe_w_down']]
    return pl.pallas_call(
        functools.partial(_moe_routed_kernel, seq_len, n),
        out_shape=jax.ShapeDtypeStruct(x2.shape, F32),
        grid=(pl.cdiv(n, tm), N_EXPERTS),
        in_specs=[x_spec] + mspecs + [_const_spec(a.shape) for a in consts] + [w_spec(a) for a in experts],
        out_specs=x_spec,
        scratch_shapes=[pltpu.VMEM((tm, D_MODEL), BF16), pltpu.VMEM((tm, tm), BF16),
                        pltpu.VMEM((SUBLANES, tm), F32), pltpu.VMEM((SUBLANES, tm), F32),
                        pltpu.VMEM((tm, D_MODEL), F32), pltpu.SMEM((N_EXPERTS,), jnp.int32)],
        compiler_params=_params(("arbitrary", "arbitrary")),
        name="moe_routed",
    )(x2, *([mods3] * 6), *consts, *experts)


def _sproj_kernel(x_ref, sh_ref, sc_ref, npre_ref, wa_ref, wg_ref, pa_o, pg_o):
    h = (_rms(x_ref[...], npre_ref[...]) * (1.0 + sc_ref[...]) + sh_ref[...]).astype(BF16)
    for c in range(D_SMALL // GATE_CHUNK):
        pa_o[:, c * GATE_CHUNK:(c + 1) * GATE_CHUNK] = _dot(h, wa_ref[c])
    for c in range(D_GATES // GATE_CHUNK):
        pg_o[:, c * GATE_CHUNK:(c + 1) * GATE_CHUNK] = _dot(h, wg_ref[c])


def _sproj_call(xs, mods2, lp):
    n = xs.shape[0]
    mod_spec = lambda col: pl.BlockSpec((n, D_MODEL), lambda i: (0, col))
    consts = [lp['norm_mix_pre'], lp['w_in_a'], lp['w_in_g']]
    return pl.pallas_call(
        _sproj_kernel,
        out_shape=(jax.ShapeDtypeStruct((n, D_SMALL), F32), jax.ShapeDtypeStruct((n, D_GATES), F32)),
        grid=(1,),
        in_specs=[_const_spec(xs.shape), mod_spec(0), mod_spec(1)] + [_const_spec(a.shape) for a in consts],
        out_specs=(_const_spec((n, D_SMALL)), _const_spec((n, D_GATES))),
        compiler_params=_params(("arbitrary",)),
        name="sample_proj",
    )(xs, mods2, mods2, *consts)


N_SMIX_IN = 22


def _smix_kernel(*refs):
    (u_ref, q_ref, kn_ref, vn_ref, knt_ref, vnt_ref, pin_ref, h0r_ref, h0i_ref, ck_ref, cv_ref, hist_ref,
     wbu_ref, abre_ref, abim_ref, cm_ref, dsk_ref, wglu_ref, bglu_ref, sinks_ref,
     wpool_ref, pscale_ref) = refs[:N_SMIX_IN]
    yssm_o, yatt_o, ypool_o, hr_o, hi_o, ck_o, cv_o, hist_o = refs[-8:]
    u = u_ref[...]
    ub = u.astype(BF16)
    bu = jnp.concatenate([_dot(ub, wbu_ref[c]) for c in range(2 * D_STATE // BU_CHUNK)], axis=1)
    abr, abi = abre_ref[...], abim_ref[...]
    h0r, h0i = h0r_ref[...], h0i_ref[...]
    hr = bu[:, 0:D_STATE] + (abr * h0r - abi * h0i)
    hi = bu[:, D_STATE:2 * D_STATE] + (abr * h0i + abi * h0r)
    hr_o[...] = hr
    hi_o[...] = hi
    y = _dot(jnp.concatenate([hr, hi], axis=1).astype(BF16), cm_ref[...])
    yssm_o[...] = _ssm_out(y, u, dsk_ref[...], wglu_ref[...], bglu_ref[...])

    q = q_ref[...] * ATTN_SCALE
    kn, vn = kn_ref[...], vn_ref[...]
    kc = ck_ref[...].astype(BF16)
    vc = cv_ref[...].astype(BF16)
    lo = _lane_lo(q.shape)
    j_i = lax.broadcasted_iota(jnp.int32, (1, GQA, 1), 1)
    dist_c = (WINDOW - lax.broadcasted_iota(jnp.int32, (1, 1, WINDOW), 2)).astype(F32)
    halves = []
    for half in range(N_KV_HEADS):
        keep = lo if half == 0 else jnp.logical_not(lo)
        qm = jnp.where(keep, q, 0.0)
        slope = jnp.zeros((1, GQA, 1), F32)
        snk = jnp.zeros((1, GQA, 1), F32)
        for j in range(GQA):
            slope = jnp.where(j_i == j, SLOPES[half * GQA + j], slope)
            snk = jnp.where(j_i == j, sinks_ref[half * GQA + j], snk)
        s = jnp.einsum('bhd,bds->bhs', qm.astype(BF16), kc, preferred_element_type=F32) - slope * dist_c
        s_new = jnp.sum(qm * kn, axis=-1, keepdims=True)
        m = jnp.maximum(jnp.maximum(jnp.max(s, axis=-1, keepdims=True), s_new), snk)
        p = jnp.exp(s - m)
        p_new = jnp.exp(s_new - m)
        den = jnp.sum(p, axis=-1, keepdims=True) + p_new + jnp.exp(snk - m)
        o = jnp.einsum('bhs,bds->bhd', p.astype(BF16), vc, preferred_element_type=F32) + p_new * vn
        halves.append(o / den)
    yatt_o[...] = jnp.where(lo, halves[0], halves[1])

    bc = ck_ref.shape[0]
    lane_w = lax.broadcasted_iota(jnp.int32, (D_KV, WINDOW), 1)
    lane_b = lax.broadcasted_iota(jnp.int32, knt_ref.shape, 1)
    for src, new_t, dst in ((ck_ref, knt_ref, ck_o), (cv_ref, vnt_ref, cv_o)):
        for b in range(bc):
            col = jnp.sum(jnp.where(lane_b == pl.program_id(0) * bc + b, new_t[...], 0.0), axis=-1, keepdims=True)
            dst[b] = jnp.where(lane_w == WINDOW - 1, col, pltpu.roll(src[b], WINDOW - 1, axis=1))

    pin = pin_ref[...]
    lo2 = _lane_lo((pin.shape[0], LANES))
    mixed = []
    for col, (w_lo, w_hi) in enumerate(((POOL_WINDOWS[0], POOL_WINDOWS[1]), (POOL_WINDOWS[2], POOL_WINDOWS[3]))):
        cs = slice(col * LANES, (col + 1) * LANES)
        pf = pin[:, cs]
        acc = pf
        for k in range(1, w_lo):
            acc = acc + hist_ref[POOL_BUF - k, :, cs]
        acc_lo = acc
        for k in range(w_lo, w_hi):
            acc = acc + hist_ref[POOL_BUF - k, :, cs]
        cnt_lo = float(min(w_lo, PAST_LEN + 1))
        cnt_hi = float(min(w_hi, PAST_LEN + 1))
        mixed.append(jnp.where(lo2, acc_lo / cnt_lo, acc / cnt_hi) - pf)
    ypool_o[...] = _dot(jnp.concatenate(mixed, axis=1).astype(BF16), wpool_ref[...]) * pscale_ref[...]
    hist_o[0:POOL_BUF - 1] = hist_ref[1:POOL_BUF]
    hist_o[POOL_BUF - 1] = pin


def _smix_call(layer, u, q3, kn3, vn3, knt, vnt, pin, h0r, h0i, ck, cv, prev_windows, hist_t, lp):
    n = u.shape[0]
    bc = SAMPLE_BC
    rows2 = lambda c: pl.BlockSpec((bc, c), lambda i: (i, 0))
    rows3 = lambda r, c: pl.BlockSpec((bc, r, c), lambda i: (i, 0, 0))
    win_spec = pl.BlockSpec((None, bc, D_KV, WINDOW), lambda i: (layer, i, 0, 0))
    consts = [lp['w_bu'], lp['ab_re'], lp['ab_im'], lp['c_mat'], lp['ssm_d'], lp['ssm_w_glu'], lp['ssm_b_glu']]
    consts2 = [lp['pool_wbd'], lp['pool_scale']]
    hist_spec = pl.BlockSpec((POOL_BUF, bc, D_POOL), lambda i: (0, i, 0))
    in_specs = ([rows2(D_SSM), rows3(GQA, LANES), rows3(1, LANES), rows3(1, LANES),
                 _const_spec(knt.shape), _const_spec(vnt.shape), rows2(D_POOL),
                 rows2(D_STATE), rows2(D_STATE), win_spec, win_spec, hist_spec]
                + [_const_spec(a.shape) for a in consts]
                + [pl.BlockSpec(memory_space=pltpu.SMEM)]
                + [_const_spec(a.shape) for a in consts2]
                + [pl.BlockSpec(memory_space=pl.ANY) for _ in prev_windows])
    assert len(in_specs) == N_SMIX_IN + len(prev_windows)
    aliases = {N_SMIX_IN + k: 5 + k for k in range(len(prev_windows))}
    return pl.pallas_call(
        _smix_kernel,
        out_shape=(jax.ShapeDtypeStruct((n, D_SSM), F32), jax.ShapeDtypeStruct((n, GQA, LANES), F32),
                   jax.ShapeDtypeStruct((n, D_POOL), F32),
                   jax.ShapeDtypeStruct((n, D_STATE), F32), jax.ShapeDtypeStruct((n, D_STATE), F32),
                   jax.ShapeDtypeStruct(ck.shape, F32), jax.ShapeDtypeStruct(cv.shape, F32),
                   jax.ShapeDtypeStruct(hist_t.shape, F32)),
        grid=(n // bc,),
        in_specs=in_specs,
        out_specs=(rows2(D_SSM), rows3(GQA, LANES), rows2(D_POOL), rows2(D_STATE), rows2(D_STATE),
                   win_spec, win_spec, hist_spec),
        input_output_aliases=aliases,
        compiler_params=_params(("arbitrary",)),
        name="sample_mix",
    )(u, q3, kn3, vn3, knt, vnt, pin, h0r, h0i, ck, cv, hist_t, *consts, lp['attn_sinks'], *consts2, *prev_windows)


def _smerge_kernel(x_ref, g_ref, pg_ref, yssm_ref, yatt_ref, ypool_ref, npost_ref,
                   wus_ref, wua_ref, wup_ref, wout_ref, o_ref):
    merged = _sigmoid(pg_ref[:, 0:D_MODEL]) * _dot(yssm_ref[...].astype(BF16), wus_ref[...])
    merged += _sigmoid(pg_ref[:, D_MODEL:2 * D_MODEL]) * _dot(yatt_ref[...].astype(BF16), wua_ref[...])
    merged += _sigmoid(pg_ref[:, 2 * D_MODEL:3 * D_MODEL]) * _dot(ypool_ref[...].astype(BF16), wup_ref[...])
    out = _dot(merged.astype(BF16), wout_ref[...])
    o_ref[...] = x_ref[...] + g_ref[...] * _rms(out, npost_ref[...])


def _smerge_call(xs, mods2, pg, yssm, yatt, ypool, lp):
    n = xs.shape[0]
    ops = [xs, mods2, pg, yssm, yatt, ypool, lp['norm_mix_post'],
           lp['w_up_ssm'], lp['w_up_attn'], lp['w_up_pool'], lp['w_out']]
    in_specs = [_const_spec(a.shape) for a in ops]
    in_specs[1] = pl.BlockSpec((n, D_MODEL), lambda i: (0, 2))
    return pl.pallas_call(
        _smerge_kernel,
        out_shape=jax.ShapeDtypeStruct(xs.shape, F32),
        grid=(1,),
        in_specs=in_specs,
        out_specs=_const_spec(xs.shape),
        compiler_params=_params(("arbitrary",)),
        name="sample_merge",
    )(*ops)


def _block_diag(w):
    g, a, b = w.shape
    eye = jnp.eye(g, dtype=w.dtype)
    return jnp.einsum('gab,gh->gahb', w, eye).reshape(g * a, g * b)


def _chunk_major(w, width):
    k, n = w.shape
    return jnp.swapaxes(w.reshape(k, n // width, width), 0, 1)


def _pair_heads(w, axis):
    shape = w.shape
    w = w.reshape(shape[:axis] + (N_KV_HEADS, GQA, HEAD_DIM) + shape[axis + 1:])
    return jnp.swapaxes(w, axis, axis + 1).reshape(shape)


def _layer_params(l, p, prep):
    ab_re, ab_im, bb_re, bb_im, pw_re, pw_im = prep
    row = lambda a: a.reshape(1, -1)
    w_in = p['w_in'][l]
    q_cols = _pair_heads(w_in[:, D_SSM:D_SSM + D_ATTN], 1)
    w_in_a = jnp.concatenate([w_in[:, 0:D_SSM], q_cols, w_in[:, D_SSM + D_ATTN:D_SMALL]], axis=1)
    lp = {
        'norm_mix_pre': row(p['norm_mix_pre'][l]), 'norm_mix_post': row(p['norm_mix_post'][l]),
        'norm_ffn_pre': row(p['norm_ffn_pre'][l]), 'norm_ffn_post': row(p['norm_ffn_post'][l]),
        'w_in_a': _chunk_major(w_in_a, GATE_CHUNK).astype(BF16),
        'w_in_g': _chunk_major(w_in[:, D_SMALL:], GATE_CHUNK).astype(BF16),
        'w_bu': _chunk_major(jnp.concatenate([_block_diag(bb_re[l]), _block_diag(bb_im[l])], axis=1),
                             BU_CHUNK).astype(BF16),
        'ab_re': ab_re[l].reshape(1, D_STATE), 'ab_im': ab_im[l].reshape(1, D_STATE),
        'pw_re': pw_re[l].reshape(SCAN_LEN, D_STATE), 'pw_im': pw_im[l].reshape(SCAN_LEN, D_STATE),
        'c_mat': jnp.concatenate([_block_diag(jnp.swapaxes(p['ssm_c_re'][l], 1, 2)),
                                  _block_diag(jnp.swapaxes(-p['ssm_c_im'][l], 1, 2))], axis=0).astype(BF16),
        'ssm_d': row(p['ssm_d'][l]), 'ssm_w_glu': p['ssm_w_glu'][l].astype(BF16), 'ssm_b_glu': row(p['ssm_b_glu'][l]),
        'attn_sinks': p['attn_sinks'][l],
        'pool_wbd': _block_diag(p['pool_w'][l]).astype(BF16), 'pool_scale': row(p['pool_scale'][l]),
        'w_up_ssm': p['w_up_ssm'][l].astype(BF16), 'w_up_attn': _pair_heads(p['w_up_attn'][l], 0).astype(BF16),
        'w_up_pool': p['w_up_pool'][l].astype(BF16), 'w_out': p['w_out'][l].astype(BF16),
    }
    j = l // 2
    if l % 2 == 1:
        lp.update({'moe_w_router': p['moe_w_router'][j].astype(BF16), 'moe_b_router': row(p['moe_b_router'][j]),
                   'moe_w_gate': p['moe_w_gate'][j:j + 1].astype(BF16), 'moe_w_up': p['moe_w_up'][j:j + 1].astype(BF16),
                   'moe_w_down': p['moe_w_down'][j:j + 1].astype(BF16)})
    else:
        cols = lambda w: [w[j][:, c0:c1].astype(BF16) for c0, c1 in FF_CHUNKS]
        lp.update({'ffn_w_gate': cols(p['ffn_w_gate']), 'ffn_w_up': cols(p['ffn_w_up']),
                   'ffn_w_down': p['ffn_w_down'][j].astype(BF16)})
    return lp


def kernel(x_prompt, x_sample, state_ssm_re, state_ssm_im, cache_win_k, cache_win_v, state_pool,
           c_prompt, c_sample, w_ada, b_ada, norm_mix_pre, norm_mix_post, norm_ffn_pre, norm_ffn_post,
           w_in, ssm_a_re, ssm_a_im, ssm_log_dt, ssm_b_re, ssm_b_im, ssm_c_re, ssm_c_im, ssm_d,
           ssm_w_glu, ssm_b_glu, attn_sinks, pool_w, pool_scale, w_up_ssm, w_up_attn, w_up_pool, w_out,
           ffn_w_gate, ffn_w_up, ffn_w_down, moe_w_router, moe_b_router, moe_w_gate, moe_w_up, moe_w_down):
    p = dict(norm_mix_pre=norm_mix_pre, norm_mix_post=norm_mix_post, norm_ffn_pre=norm_ffn_pre,
             norm_ffn_post=norm_ffn_post, w_in=w_in, ssm_c_re=ssm_c_re, ssm_c_im=ssm_c_im, ssm_d=ssm_d,
             ssm_w_glu=ssm_w_glu, ssm_b_glu=ssm_b_glu, attn_sinks=attn_sinks, pool_w=pool_w, pool_scale=pool_scale,
             w_up_ssm=w_up_ssm, w_up_attn=w_up_attn, w_up_pool=w_up_pool, w_out=w_out,
             ffn_w_gate=ffn_w_gate, ffn_w_up=ffn_w_up, ffn_w_down=ffn_w_down, moe_w_router=moe_w_router,
             moe_b_router=moe_b_router, moe_w_gate=moe_w_gate, moe_w_up=moe_w_up, moe_w_down=moe_w_down)
    depth = w_in.shape[0]
    b, t_len, d = x_prompt.shape
    ns = x_sample.shape[0]

    mods, mods_seq = _ada_call(jnp.concatenate([c_sample, c_prompt], axis=0), b, w_ada, b_ada)
    prep = _ssm_prep_call(ssm_a_re, ssm_a_im, ssm_log_dt, ssm_b_re, ssm_b_im)

    xp = x_prompt
    xs = x_sample.reshape(ns, d)
    p_out = ([], [], [], [], [])
    s_out = ([], [], [], [], [])
    win_minor = lambda c: jnp.transpose(c, (0, 1, 3, 4, 2)).reshape(depth, ns, D_KV, -1)
    win_major = lambda c: jnp.transpose(c.reshape(depth, ns, N_KV_HEADS, HEAD_DIM, -1), (0, 1, 4, 2, 3))
    old_windows = [win_minor(cache_win_k), win_minor(cache_win_v)]
    new_windows = []
    for l in range(depth):
        lp = _layer_params(l, p, prep)
        mods2 = mods[l]
        mods3 = mods_seq[l]

        xp, hre, him, kwin, vwin, plast = _mixer_call(xp, mods3, lp)
        x2 = xp.reshape(b * t_len, d)
        if l % 2 == 1:
            x2 = _moe_routed_call(x2, mods3, lp, t_len)
        else:
            x2 = _ffn_call(x2, mods3, mods2, lp, t_len, False)
        xp = x2.reshape(b, t_len, d)
        p_out[0].append(hre.reshape(b, SSM_GROUPS, SSM_STATE))
        p_out[1].append(him.reshape(b, SSM_GROUPS, SSM_STATE))
        p_out[2].append(kwin.reshape(b, WINDOW, N_KV_HEADS, HEAD_DIM))
        p_out[3].append(vwin.reshape(b, WINDOW, N_KV_HEADS, HEAD_DIM))
        p_out[4].append(plast[:, POOL_HIST - POOL_BUF:, :])

        pa, pg = _sproj_call(xs, mods2, lp)
        u = pa[:, 0:D_SSM]
        q3 = pa[:, D_SSM:D_SSM + D_ATTN].reshape(ns, GQA, LANES)
        k_new = pa[:, D_SSM + D_ATTN:D_SSM + D_ATTN + D_KV]
        v_new = pa[:, D_SSM + D_ATTN + D_KV:D_SSM + D_ATTN + 2 * D_KV]
        pin = pa[:, D_SMALL - D_POOL:]
        yssm, yatt3, ypool, hr, hi, *new_windows, hist_new = _smix_call(
            l, u, q3, k_new.reshape(ns, 1, D_KV), v_new.reshape(ns, 1, D_KV), k_new.T, v_new.T, pin,
            state_ssm_re[l].reshape(ns, D_STATE), state_ssm_im[l].reshape(ns, D_STATE),
            *old_windows, new_windows, jnp.swapaxes(state_pool[l], 0, 1), lp)
        xs = _smerge_call(xs, mods2, pg, yssm, yatt3.reshape(ns, D_ATTN), ypool, lp)
        xs = (_moe_call if l % 2 == 1 else _ffn_call)(xs, mods3, mods2, lp, 1, True)
        s_out[0].append(hr.reshape(ns, SSM_GROUPS, SSM_STATE))
        s_out[1].append(hi.reshape(ns, SSM_GROUPS, SSM_STATE))
        s_out[4].append(jnp.swapaxes(hist_new, 0, 1))

    stack = lambda xs_: [jnp.stack(a, axis=0) for a in xs_]
    p_ssm_re, p_ssm_im, p_win_k, p_win_v, p_pool = stack(p_out)
    s_ssm_re, s_ssm_im, s_pool = stack([s_out[0], s_out[1], s_out[4]])
    s_win_k, s_win_v = [win_major(w) for w in new_windows]
    return (xp, xs.reshape(x_sample.shape), p_ssm_re, p_ssm_im, p_win_k, p_win_v, p_pool,
            s_ssm_re, s_ssm_im, s_win_k, s_win_v, s_pool)
```

```python
import functools
import math
import types

import jax
import jax.numpy as jnp
from jax import lax
from jax.experimental import pallas as pl
from jax.experimental.pallas import tpu as pltpu

F32 = jnp.float32
BF16 = jnp.bfloat16

D_MODEL = 1024
D_SSM = 256
SSM_GROUP = 16
SSM_GROUPS = 16
SSM_STATE = 64
D_STATE = SSM_GROUPS * SSM_STATE
HEAD_DIM = 64
N_HEADS = 8
N_KV_HEADS = 2
GQA = N_HEADS // N_KV_HEADS
D_ATTN = N_HEADS * HEAD_DIM
D_KV = N_KV_HEADS * HEAD_DIM
WINDOW = 128
ATTN_SCALE = HEAD_DIM ** -0.5
NEG_INF = -1e30
LOG2E = math.log2(math.e)
D_POOL = 256
POOL_WINDOWS = (2, 4, 8, 16)
POOL_BUF = 15
POOL_HIST = 16
POOL_PAD = 8
N_MOD = 6
D_FF = 2816
N_EXPERTS = 8
D_EXPERT = 1024
RMS_EPS = 1e-6
PAST_LEN = 16384
D_SMALL = D_SSM + D_ATTN + 2 * D_KV + D_POOL
D_GATES = 3 * D_MODEL
SLOPES = tuple(2.0 ** (-8.0 * (h + 1) / N_HEADS) for h in range(N_HEADS))

SUBLANES = 8
LANES = 128
VMEM_LIMIT = 56 * 1024 * 1024

TT = 512
SEQ_PER_STEP = 1
SCAN_LEN = TT // SUBLANES
Q_BLK = 128
GATE_CHUNK = 256
BU_CHUNK = 512
MIXER_LAG = 0
GATE_LOOP_A = 8
GATE_LOOP_B = D_GATES // GATE_CHUNK - GATE_LOOP_A
FFN_TM = 512
MOE_TM = 1024
MOE_SP_TM = 896
MOE_CAP = 256
MOE_TAIL = 128
SAMPLE_BC = 32
ADA_TN = 1536
FF_CHUNKS = ((0, 1024), (1024, 2048), (2048, D_FF))


def _rms(x, gain):
    return x * lax.rsqrt(jnp.mean(x * x, axis=-1, keepdims=True) + RMS_EPS) * gain


def _sigmoid(x):
    return 1.0 / (1.0 + jnp.exp(-x))


def _silu(x):
    return x * _sigmoid(x)


def _dot(a, b):
    return jnp.dot(a, b, preferred_element_type=F32)


def _dot_t(a, b):
    return lax.dot_general(a, b, (((1,), (1,)), ((), ())), preferred_element_type=F32)


def _halves(dot, a, b):
    half = a.shape[0] // 2
    return jnp.concatenate([dot(a[:half], b), dot(a[half:], b)], axis=0)


def _const_spec(shape):
    nd = len(shape)
    return pl.BlockSpec(shape, lambda *_: (0,) * nd)


def _params(sem):
    return pltpu.CompilerParams(dimension_semantics=sem, vmem_limit_bytes=VMEM_LIMIT)


def _ada_kernel(c_ref, w_ref, b_ref, o_ref, oseq_ref):
    c = c_ref[...]
    s = _silu(c).astype(BF16)
    mod = _dot(s, w_ref[0].astype(BF16)) + b_ref[0]
    o_ref[0] = mod
    n_seq = oseq_ref.shape[1]
    first = mod.shape[0] - n_seq
    for s_i in range(n_seq):
        oseq_ref[0, s_i] = mod[first + s_i:first + s_i + 1, :]


def _ada_call(c_all, n_seq, w_ada, b_ada):
    depth = w_ada.shape[0]
    rows = c_all.shape[0]
    n = N_MOD * D_MODEL
    return pl.pallas_call(
        _ada_kernel,
        out_shape=(jax.ShapeDtypeStruct((depth, rows, n), F32), jax.ShapeDtypeStruct((depth, n_seq, 1, n), F32)),
        grid=(depth, n // ADA_TN),
        in_specs=[
            pl.BlockSpec((rows, D_MODEL), lambda l, j: (0, 0)),
            pl.BlockSpec((1, D_MODEL, ADA_TN), lambda l, j: (l, 0, j)),
            pl.BlockSpec((1, 1, ADA_TN), lambda l, j: (l, 0, j)),
        ],
        out_specs=(pl.BlockSpec((1, rows, ADA_TN), lambda l, j: (l, 0, j)),
                   pl.BlockSpec((1, n_seq, 1, ADA_TN), lambda l, j: (l, 0, 0, j))),
        compiler_params=_params(("arbitrary", "arbitrary")),
        name="ada_mod",
    )(c_all, w_ada, b_ada.reshape(depth, 1, n))


def _ssm_prep_kernel(are_ref, aim_ref, ldt_ref, bre_ref, bim_ref,
                     abre_o, abim_o, bbre_o, bbim_o, pwre_o, pwim_o):
    ar = are_ref[0]
    ai = aim_ref[0]
    dt = jnp.exp(ldt_ref[0])
    mag = jnp.exp(ar * dt)
    ab_re = mag * jnp.cos(ai * dt)
    ab_im = mag * jnp.sin(ai * dt)
    den = ar * ar + ai * ai
    n_re = ab_re - 1.0
    f_re = (n_re * ar + ab_im * ai) / den
    f_im = (ab_im * ar - n_re * ai) / den
    br = bre_ref[0]
    bi = bim_ref[0]
    bbre_o[0] = f_re * br - f_im * bi
    bbim_o[0] = f_re * bi + f_im * br
    abre_o[0] = ab_re
    abim_o[0] = ab_im
    cr, ci = ab_re, ab_im
    for t in range(SCAN_LEN):
        pwre_o[0, t] = cr
        pwim_o[0, t] = ci
        cr, ci = cr * ab_re - ci * ab_im, cr * ab_im + ci * ab_re


def _ssm_prep_call(a_re, a_im, log_dt, b_re, b_im):
    depth = a_re.shape[0]
    g, p, c = SSM_GROUPS, SSM_STATE, SSM_GROUP
    a4 = lambda a: a.reshape(depth, g, 1, p)
    spec_a = pl.BlockSpec((1, g, 1, p), lambda l: (l, 0, 0, 0))
    spec_b = pl.BlockSpec((1, g, c, p), lambda l: (l, 0, 0, 0))
    spec_pw = pl.BlockSpec((1, SCAN_LEN, g, 1, p), lambda l: (l, 0, 0, 0, 0))
    return pl.pallas_call(
        _ssm_prep_kernel,
        out_shape=(jax.ShapeDtypeStruct((depth, g, 1, p), F32),) * 2
        + (jax.ShapeDtypeStruct((depth, g, c, p), F32),) * 2
        + (jax.ShapeDtypeStruct((depth, SCAN_LEN, g, 1, p), F32),) * 2,
        grid=(depth,),
        in_specs=[spec_a, spec_a, pl.BlockSpec((1, g, 1, 1), lambda l: (l, 0, 0, 0)), spec_b, spec_b],
        out_specs=(spec_a, spec_a, spec_b, spec_b, spec_pw, spec_pw),
        compiler_params=_params(("arbitrary",)),
        name="ssm_prep",
    )(a4(a_re), a4(a_im), log_dt.reshape(depth, g, 1, 1),
      jnp.swapaxes(b_re, 2, 3), jnp.swapaxes(b_im, 2, 3))


def _ssm_out(y, u, d_skip, w_glu, b_glu):
    z = jax.nn.gelu(y + d_skip * u)
    return z * _sigmoid(_halves(_dot, z.astype(BF16), w_glu) + b_glu)


def _lane_lo(shape):
    return (lax.broadcasted_iota(jnp.int32, shape, len(shape) - 1) % LANES) < HEAD_DIM


N_MIXER_SEQ_IN = 4
N_MIXER_IN = 24
N_MIXER_OUT = 6


def _mixer_kernel(*refs):
    ins = refs[:N_MIXER_IN]
    outs = refs[N_MIXER_IN:N_MIXER_IN + N_MIXER_OUT]
    scratch = refs[N_MIXER_IN + N_MIXER_OUT:]
    bias_s, wg3_s = scratch[-2:]
    wg_ref = ins[MIXER_REF_NAMES.index('wg_ref')]
    t = pl.program_id(1)

    @pl.when((pl.program_id(0) == 0) & (t == 0))
    def _():
        for c in range(D_GATES // GATE_CHUNK):
            wg3_s[c] = wg_ref[:, c * GATE_CHUNK:(c + 1) * GATE_CHUNK]
        r_i = lax.broadcasted_iota(jnp.int32, (Q_BLK, 2 * Q_BLK), 0)
        c_i = lax.broadcasted_iota(jnp.int32, (Q_BLK, 2 * Q_BLK), 1)
        dist = r_i - c_i + Q_BLK
        valid = (dist >= 0) & (dist <= WINDOW)
        for hh in range(N_HEADS):
            bias_s[hh] = jnp.where(valid, (-SLOPES[hh] * LOG2E) * dist.astype(F32), NEG_INF)

    lvl_s, kbuf, vbuf, pbuf, hc_s = scratch[-7:-2]

    @pl.when(t == 0)
    def _():
        hc_s[...] = jnp.zeros_like(hc_s)
        kbuf[:, 0:WINDOW, :] = jnp.zeros((SEQ_PER_STEP, WINDOW, D_KV), BF16)
        vbuf[:, 0:WINDOW, :] = jnp.zeros((SEQ_PER_STEP, WINDOW, D_KV), BF16)
        pbuf[:, 0:POOL_PAD + POOL_HIST, :] = jnp.zeros((SEQ_PER_STEP, POOL_PAD + POOL_HIST, D_POOL), F32)
        lvl_s[:, :, 0:POOL_PAD, :] = jnp.zeros((SEQ_PER_STEP, 2, POOL_PAD, LANES), F32)

    chains = []
    for s in range(SEQ_PER_STEP):
        one = functools.partial(lambda s, r: r.at[s:s + 1], s)
        chains.append(_mixer_body(t, *[one(r) for r in ins[:N_MIXER_SEQ_IN]], *ins[N_MIXER_SEQ_IN:],
                                  *[one(r) for r in outs], *[r.at[s] for r in scratch[:-2]], bias_s, wg3_s))
    for _ in range(MIXER_LAG):
        next(chains[0])
    while chains:
        for chain in list(chains):
            if next(chain, StopIteration) is StopIteration:
                chains.remove(chain)


MIXER_REF_NAMES = (
    'x_ref sh_ref sc_ref g_ref npre_ref npost_ref wa_ref wg_ref '
    'wbu_ref abre_ref abim_ref pwre_ref pwim_ref cm_ref dsk_ref wglu_ref bglu_ref '
    'sinks_ref wpool_ref pscale_ref wus_ref wua_ref wup_ref wout_ref '
    'xo_ref hre_o him_o kwin_o vwin_o plast_o '
    'hb_s proj_s bu_s hst_s sg_s unat_s uperm_s operm_s yssm_s yatt_s lvl_s kbuf vbuf pbuf hc_s '
    'bias_s wg3_s').split()


def _mixer_body(t, *refs):
    r = types.SimpleNamespace(**dict(zip(MIXER_REF_NAMES, refs, strict=True)))
    x = r.x_ref[0]
    h = _rms(x, r.npre_ref[...]) * (1.0 + r.sc_ref[0]) + r.sh_ref[0]
    r.hb_s[...] = h.astype(BF16)
    yield
    for c in range(D_SMALL // GATE_CHUNK):
        cols = slice(c * GATE_CHUNK, (c + 1) * GATE_CHUNK)
        r.proj_s[:, cols] = _dot(r.hb_s[...], r.wa_ref[:, cols])
        yield
    res = {}
    yield from _ssm_chain(r)
    yield from _attn_chain(t, r)
    yield from _pool_chain(t, r, res)
    yield from _merge_tail(r, res)


def _ssm_chain(r):
    (proj_s, hb_s, bu_s, hst_s, sg_s, unat_s, uperm_s, operm_s, yssm_s, hc_s, wg_ref, wbu_ref, abre_ref, abim_ref,
     pwre_ref, pwim_ref, cm_ref, dsk_ref, wglu_ref, bglu_ref, hre_o, him_o) = (
        r.proj_s, r.hb_s, r.bu_s, r.hst_s, r.sg_s, r.unat_s, r.uperm_s, r.operm_s, r.yssm_s, r.hc_s, r.wg3_s,
        r.wbu_ref, r.abre_ref, r.abim_ref, r.pwre_ref, r.pwim_ref, r.cm_ref, r.dsk_ref, r.wglu_ref, r.bglu_ref,
        r.hre_o, r.him_o)
    for c in range(D_SSM // LANES):
        unat_s[c] = proj_s[:, c * LANES:(c + 1) * LANES]
        for i in range(SCAN_LEN):
            uperm_s[i * SUBLANES:(i + 1) * SUBLANES, c * LANES:(c + 1) * LANES] = (
                unat_s.at[c][pl.ds(i, SUBLANES, stride=SCAN_LEN), :])
        yield
    for c in range(2 * D_STATE // BU_CHUNK):
        cols = slice(c * BU_CHUNK, (c + 1) * BU_CHUNK)
        bu_s[:, cols] = _dot(uperm_s[...].astype(BF16), wbu_ref[:, cols])
        yield

    abr = jnp.broadcast_to(abre_ref[...], (SUBLANES, D_STATE))
    abi = jnp.broadcast_to(abim_ref[...], (SUBLANES, D_STATE))

    def gate_chunk(c):
        sg_s[c] = _sigmoid(_dot(hb_s[...], wg_ref[c]))

    zero = jnp.zeros((SUBLANES, D_STATE), F32)
    steps_a = SCAN_LEN // GATE_LOOP_A

    er, ei = zero, zero
    for i in range(GATE_LOOP_A):
        gate_chunk(i)
        yield
        for s in range(steps_a):
            rows = slice((i * steps_a + s) * SUBLANES, (i * steps_a + s + 1) * SUBLANES)
            er, ei = (abr * er - abi * ei + bu_s[rows, 0:D_STATE],
                      abr * ei + abi * er + bu_s[rows, D_STATE:2 * D_STATE])
            bu_s[rows, 0:D_STATE] = er
            bu_s[rows, D_STATE:2 * D_STATE] = ei
        yield

    alr = pwre_ref[SCAN_LEN - 1:SCAN_LEN, :]
    ali = pwim_ref[SCAN_LEN - 1:SCAN_LEN, :]
    cr = hc_s[0:1, :]
    ci = hc_s[1:2, :]
    row = lax.broadcasted_iota(jnp.int32, (SUBLANES, D_STATE), 0)
    hin_r = zero
    hin_i = zero
    for j in range(SUBLANES):
        hin_r = jnp.where(row == j, cr, hin_r)
        hin_i = jnp.where(row == j, ci, hin_i)
        cr, ci = (alr * cr - ali * ci + er[j:j + 1, :], alr * ci + ali * cr + ei[j:j + 1, :])
    hc_s[0:1, :] = cr
    hc_s[1:2, :] = ci
    hre_o[0] = cr
    him_o[0] = ci
    yield

    pack = 2 * SUBLANES
    packs_b = TT // pack // GATE_LOOP_B

    for i in range(GATE_LOOP_B):
        gate_chunk(GATE_LOOP_A + i)
        yield
        for s in range(packs_b):
            k = i * packs_b + s
            fixed_r, fixed_i = [], []
            for slab in (2 * k, 2 * k + 1):
                rows = slice(slab * SUBLANES, (slab + 1) * SUBLANES)
                pr = pwre_ref[slab:slab + 1, :]
                pi = pwim_ref[slab:slab + 1, :]
                fixed_r.append(bu_s[rows, 0:D_STATE] + (pr * hin_r - pi * hin_i))
                fixed_i.append(bu_s[rows, D_STATE:2 * D_STATE] + (pr * hin_i + pi * hin_r))
            prow = slice(k * pack, (k + 1) * pack)
            hst_s[prow, 0:D_STATE] = jnp.concatenate(fixed_r, axis=0).astype(BF16)
            hst_s[prow, D_STATE:2 * D_STATE] = jnp.concatenate(fixed_i, axis=0).astype(BF16)
        yield

    y = _halves(_dot, hst_s[...], cm_ref[...])
    yield
    operm_s[...] = _ssm_out(y, uperm_s[...], dsk_ref[...], wglu_ref[...], bglu_ref[...])
    yield
    for c in range(D_SSM // LANES):
        for i in range(SCAN_LEN):
            yssm_s.at[c][pl.ds(i, SUBLANES, stride=SCAN_LEN), :] = (
                operm_s[i * SUBLANES:(i + 1) * SUBLANES, c * LANES:(c + 1) * LANES])
        yield

def _attn_chain(t, r):
    proj_s, kbuf, vbuf, yatt_s, bias_s, sinks_ref, kwin_o, vwin_o = (
        r.proj_s, r.kbuf, r.vbuf, r.yatt_s, r.bias_s, r.sinks_ref, r.kwin_o, r.vwin_o)
    kbuf[WINDOW:WINDOW + TT, :] = proj_s[:, D_SSM + D_ATTN:D_SSM + D_ATTN + D_KV].astype(BF16)
    vbuf[WINDOW:WINDOW + TT, :] = proj_s[:, D_SSM + D_ATTN + D_KV:D_SSM + D_ATTN + 2 * D_KV].astype(BF16)
    lo = _lane_lo((Q_BLK, LANES))
    r_i = lax.broadcasted_iota(jnp.int32, (Q_BLK, 2 * Q_BLK), 0)
    c_i = lax.broadcasted_iota(jnp.int32, (Q_BLK, 2 * Q_BLK), 1)
    first_key = jnp.where(t == 0, Q_BLK, 0)
    def scores(blk):
        r0 = blk * Q_BLK
        pieces = []
        for hh in range(N_HEADS):
            j = hh % GQA
            qp = proj_s[r0:r0 + Q_BLK, D_SSM + j * LANES:D_SSM + (j + 1) * LANES] * (ATTN_SCALE * LOG2E)
            keep = lo if hh < GQA else jnp.logical_not(lo)
            pieces.append(jnp.where(keep, qp, 0.0).astype(BF16))
        return [_dot_t(jnp.concatenate(pieces[g * GQA:(g + 1) * GQA], axis=0), kbuf[r0:r0 + 2 * Q_BLK, :])
                for g in range(N_KV_HEADS)]

    n_blk = TT // Q_BLK
    s_next = scores(0)
    yield
    for blk in range(n_blk):
        r0 = blk * Q_BLK
        s_all = s_next
        if blk + 1 < n_blk:
            s_next = scores(blk + 1)
            yield
        probs, dens = [], []
        for hh in range(N_HEADS):
            s = s_all[hh // GQA][(hh % GQA) * Q_BLK:(hh % GQA + 1) * Q_BLK, :] + bias_s[hh]
            if blk == 0:
                s = jnp.where(c_i >= first_key, s, NEG_INF)
            snk = sinks_ref[hh] * LOG2E
            m = jnp.maximum(jnp.max(s, axis=-1, keepdims=True), snk)
            p = jnp.exp2(s - m)
            dens.append(jnp.sum(p, axis=-1, keepdims=True) + jnp.exp2(snk - m))
            probs.append(p.astype(BF16))
            yield
        o_all = [_dot(jnp.concatenate(probs[g * GQA:(g + 1) * GQA], axis=0), vbuf[r0:r0 + 2 * Q_BLK, :])
                 for g in range(N_KV_HEADS)]
        out = lambda hh: o_all[hh // GQA][(hh % GQA) * Q_BLK:(hh % GQA + 1) * Q_BLK, :] / dens[hh]
        for j in range(GQA):
            yatt_s[r0:r0 + Q_BLK, j * LANES:(j + 1) * LANES] = jnp.where(lo, out(j), out(GQA + j)).astype(BF16)
        yield
    kwin_o[0] = proj_s[TT - WINDOW:TT, D_SSM + D_ATTN:D_SSM + D_ATTN + D_KV]
    vwin_o[0] = proj_s[TT - WINDOW:TT, D_SSM + D_ATTN + D_KV:D_SSM + D_ATTN + 2 * D_KV]
    kbuf[0:WINDOW, :] = kbuf[TT:TT + WINDOW, :]
    vbuf[0:WINDOW, :] = vbuf[TT:TT + WINDOW, :]
    yield

def _pool_chain(t, r, res):
    proj_s, pbuf, lvl_s, plast_o, wpool_ref, pscale_ref = (
        r.proj_s, r.pbuf, r.lvl_s, r.plast_o, r.wpool_ref, r.pscale_ref)
    p0 = D_SMALL - D_POOL
    cur = POOL_PAD + POOL_HIST
    end = cur + TT
    pbuf[cur:end, :] = proj_s[:, p0:D_SMALL]
    plast_o[0] = proj_s[TT - POOL_HIST:TT, p0:D_SMALL]
    pos1 = (t * TT + 1 + lax.broadcasted_iota(jnp.int32, (TT, LANES), 0)).astype(F32)
    lo_t = _lane_lo((TT, LANES))
    a_s, b_s = lvl_s.at[0], lvl_s.at[1]
    mixed = []
    for col, (w_lo, w_hi) in enumerate(((POOL_WINDOWS[0], POOL_WINDOWS[1]), (POOL_WINDOWS[2], POOL_WINDOWS[3]))):
        cs = slice(col * LANES, (col + 1) * LANES)
        pf = pbuf[cur:end, cs]
        a_s[POOL_PAD:end, :] = pbuf[POOL_PAD:end, cs] + pbuf[POOL_PAD - 1:end - 1, cs]
        have = 2
        while have < w_lo:
            b_s[POOL_PAD:end, :] = a_s[POOL_PAD:end, :] + a_s[POOL_PAD - have:end - have, :]
            a_s, b_s, have = b_s, a_s, 2 * have
        acc_lo = a_s[cur:end, :]
        while 2 * have < w_hi:
            b_s[POOL_PAD:end, :] = a_s[POOL_PAD:end, :] + a_s[POOL_PAD - have:end - have, :]
            a_s, b_s, have = b_s, a_s, 2 * have
        acc_hi = a_s[cur:end, :] + a_s[cur - have:end - have, :]
        cnt = jnp.where(lo_t, jnp.minimum(float(w_lo), pos1), jnp.minimum(float(w_hi), pos1))
        mixed.append(jnp.where(lo_t, acc_lo, acc_hi) / cnt - pf)
        yield
    ypool = _halves(_dot, jnp.concatenate(mixed, axis=1).astype(BF16), wpool_ref[...]) * pscale_ref[...]
    pbuf[POOL_PAD:cur, :] = pbuf[end - POOL_HIST:end, :]
    res['ypool'] = ypool
    yield


def _merge_tail(r, res):
    sg_s, yssm_s, yatt_s, wus_ref, wua_ref, wup_ref, wout_ref, x_ref, g_ref, npost_ref, xo_ref = (
        r.sg_s, r.yssm_s, r.yatt_s, r.wus_ref, r.wua_ref, r.wup_ref, r.wout_ref, r.x_ref, r.g_ref, r.npost_ref,
        r.xo_ref)
    ypool = res['ypool']
    per_branch = D_MODEL // GATE_CHUNK
    gate = lambda b: jnp.concatenate([sg_s[b * per_branch + c] for c in range(per_branch)], axis=1)
    yssm = jnp.concatenate([yssm_s[c] for c in range(D_SSM // LANES)], axis=1)
    merged = gate(0) * _dot(yssm.astype(BF16), wus_ref[...])
    yield
    merged += gate(1) * _dot(yatt_s[...], wua_ref[...])
    yield
    merged += gate(2) * _dot(ypool.astype(BF16), wup_ref[...])
    yield
    out = _dot(merged.astype(BF16), wout_ref[...])
    yield
    xo_ref[0] = x_ref[0] + g_ref[0] * _rms(out, npost_ref[...])


def _mixer_call(x, mods3, lp):
    b, t_len, d = x.shape
    nt = t_len // TT
    nsamp = mods3.shape[0] - b
    sps = SEQ_PER_STEP
    mod_spec = lambda col: pl.BlockSpec((sps, 1, D_MODEL), lambda i, j: (nsamp // sps + i, 0, col))
    x_spec = pl.BlockSpec((sps, TT, d), lambda i, j: (i, j, 0))
    consts = [lp['norm_mix_pre'], lp['norm_mix_post'], lp['w_in_a'], lp['w_in_g'],
              lp['w_bu'], lp['ab_re'], lp['ab_im'], lp['pw_re'], lp['pw_im'], lp['c_mat'],
              lp['ssm_d'], lp['ssm_w_glu'], lp['ssm_b_glu']]
    consts2 = [lp['pool_wbd'], lp['pool_scale'], lp['w_up_ssm'], lp['w_up_attn'], lp['w_up_pool'], lp['w_out']]
    in_specs = ([x_spec, mod_spec(0), mod_spec(1), mod_spec(2)]
                + [_const_spec(a.shape) for a in consts]
                + [pl.BlockSpec(memory_space=pltpu.SMEM)]
                + [_const_spec(a.shape) for a in consts2])
    per_b = lambda r, c: pl.BlockSpec((sps, r, c), lambda i, j: (i, 0, 0))
    out_shape = (jax.ShapeDtypeStruct((b, t_len, d), F32),
                 jax.ShapeDtypeStruct((b, 1, D_STATE), F32), jax.ShapeDtypeStruct((b, 1, D_STATE), F32),
                 jax.ShapeDtypeStruct((b, WINDOW, D_KV), F32), jax.ShapeDtypeStruct((b, WINDOW, D_KV), F32),
                 jax.ShapeDtypeStruct((b, POOL_HIST, D_POOL), F32))
    out_specs = (x_spec, per_b(1, D_STATE), per_b(1, D_STATE), per_b(WINDOW, D_KV), per_b(WINDOW, D_KV),
                 per_b(POOL_HIST, D_POOL))
    per_seq = lambda shape, dtype: pltpu.VMEM((sps,) + shape, dtype)
    scratch = [per_seq((TT, d), BF16), per_seq((TT, D_SMALL), F32), per_seq((TT, 2 * D_STATE), F32),
               per_seq((TT, 2 * D_STATE), BF16), per_seq((D_GATES // GATE_CHUNK, TT, GATE_CHUNK), F32),
               per_seq((D_SSM // LANES, TT, LANES), F32), per_seq((TT, D_SSM), F32),
               per_seq((TT, D_SSM), F32), per_seq((D_SSM // LANES, TT, LANES), F32),
               per_seq((TT, D_ATTN), BF16), per_seq((2, POOL_PAD + POOL_HIST + TT, LANES), F32),
               per_seq((WINDOW + TT, D_KV), BF16), per_seq((WINDOW + TT, D_KV), BF16),
               per_seq((POOL_PAD + POOL_HIST + TT, D_POOL), F32), per_seq((SUBLANES, D_STATE), F32),
               pltpu.VMEM((N_HEADS, Q_BLK, 2 * Q_BLK), F32),
               pltpu.VMEM((D_GATES // GATE_CHUNK, d, GATE_CHUNK), BF16)]
    return pl.pallas_call(
        _mixer_kernel,
        out_shape=out_shape,
        grid=(b // sps, nt),
        in_specs=in_specs,
        out_specs=out_specs,
        scratch_shapes=scratch,
        compiler_params=_params(("arbitrary", "arbitrary")),
        name="prompt_mixer",
    )(x, mods3, mods3, mods3, *consts, lp['attn_sinks'], *consts2)


def _mod_rows(ref, per_row):
    return ref[...] if per_row else ref[0]


def _ffn_kernel(per_row, x_ref, sh_ref, sc_ref, g_ref, npre_ref, npost_ref, wg_ref, wu_ref, wd_ref, o_ref):
    x = x_ref[...]
    h = (_rms(x, npre_ref[...]) * (1.0 + _mod_rows(sc_ref, per_row)) + _mod_rows(sh_ref, per_row)).astype(BF16)
    f = None
    for c0, c1 in FF_CHUNKS:
        act = _silu(_dot(h, wg_ref[:, c0:c1])) * _dot(h, wu_ref[:, c0:c1])
        part = _dot(act.astype(BF16), wd_ref[c0:c1, :])
        f = part if f is None else f + part
    o_ref[...] = x + _mod_rows(g_ref, per_row) * _rms(f, npost_ref[...])


def _mod_specs(mods3, mods2, n_rows, tm, rows_per_seq, per_row):
    nsamp = mods3.shape[0] - (n_rows // rows_per_seq if not per_row else 0)
    if per_row:
        specs = [pl.BlockSpec((tm, D_MODEL), functools.partial(lambda col, i, *_: (i, col), col)) for col in (3, 4, 5)]
        return specs, [mods2] * 3
    tiles_per_seq = rows_per_seq // tm
    specs = [pl.BlockSpec((1, 1, D_MODEL),
                          functools.partial(lambda col, i, *_: (nsamp + i // tiles_per_seq, 0, col), col))
             for col in (3, 4, 5)]
    return specs, [mods3] * 3


def _ffn_call(x2, mods3, mods2, lp, rows_per_seq, per_row):
    n = x2.shape[0]
    tm = n if per_row else FFN_TM
    mspecs, mops = _mod_specs(mods3, mods2, n, tm, rows_per_seq, per_row)
    consts = [lp['norm_ffn_pre'], lp['norm_ffn_post'], lp['ffn_w_gate'], lp['ffn_w_up'], lp['ffn_w_down']]
    x_spec = pl.BlockSpec((tm, D_MODEL), lambda i: (i, 0))
    return pl.pallas_call(
        functools.partial(_ffn_kernel, per_row),
        out_shape=jax.ShapeDtypeStruct(x2.shape, F32),
        grid=(n // tm,),
        in_specs=[x_spec] + mspecs + [_const_spec(a.shape) for a in consts],
        out_specs=x_spec,
        compiler_params=_params(("arbitrary",)),
        name="ffn_rows" if per_row else "ffn_seq",
    )(x2, *mops, *consts)


def _moe_kernel(per_row, x_ref, sh_ref, sc_ref, g_ref, npre_ref, npost_ref, wr_ref, br_ref,
                wg_ref, wu_ref, wd_ref, o_ref, h_s, gates_s, acc_s):
    e = pl.program_id(1)
    lane_e = lax.broadcasted_iota(jnp.int32, gates_s.shape, 1)

    @pl.when(e == 0)
    def _():
        x = x_ref[...]
        h = _rms(x, npre_ref[...]) * (1.0 + _mod_rows(sc_ref, per_row)) + _mod_rows(sh_ref, per_row)
        hb = h.astype(BF16)
        h_s[...] = hb
        logits = _dot(hb, wr_ref[...]) + br_ref[...]
        gates_s[...], _ = _top2_gates(logits, lane_e.astype(F32))
        acc_s[...] = jnp.zeros_like(acc_s)

    h = h_s[...]
    act = _silu(_dot(h, wg_ref[0, 0])) * _dot(h, wu_ref[0, 0])
    y = _dot(act.astype(BF16), wd_ref[0, 0])
    ge = jnp.sum(jnp.where(lane_e == e, gates_s[...], 0.0), axis=-1, keepdims=True)
    acc_s[...] += ge * y

    @pl.when(e == N_EXPERTS - 1)
    def _():
        o_ref[...] = x_ref[...] + _mod_rows(g_ref, per_row) * _rms(acc_s[...], npost_ref[...])


def _moe_call(x2, mods3, mods2, lp, rows_per_seq, per_row):
    n = x2.shape[0]
    tm = n if per_row else MOE_TM
    mspecs, mops = _mod_specs(mods3, mods2, n, tm, rows_per_seq, per_row)
    consts = [lp['norm_ffn_pre'], lp['norm_ffn_post'], lp['moe_w_router'], lp['moe_b_router']]
    x_spec = pl.BlockSpec((tm, D_MODEL), lambda i, e: (i, 0))
    w_spec = lambda a: pl.BlockSpec((1, 1) + a.shape[2:], lambda i, e: (0, e, 0, 0))
    experts = [lp['moe_w_gate'], lp['moe_w_up'], lp['moe_w_down']]
    return pl.pallas_call(
        functools.partial(_moe_kernel, per_row),
        out_shape=jax.ShapeDtypeStruct(x2.shape, F32),
        grid=(n // tm, N_EXPERTS),
        in_specs=[x_spec] + mspecs + [_const_spec(a.shape) for a in consts] + [w_spec(a) for a in experts],
        out_specs=x_spec,
        scratch_shapes=[pltpu.VMEM((tm, D_MODEL), BF16), pltpu.VMEM((tm, N_EXPERTS), F32),
                        pltpu.VMEM((tm, D_MODEL), F32)],
        compiler_params=_params(("arbitrary", "arbitrary")),
        name="moe_rows" if per_row else "moe_seq",
    )(x2, *mops, *consts, *experts)


def _top2_gates(logits, lane_f):
    n_lanes = float(logits.shape[-1])
    v1 = jnp.max(logits, axis=-1, keepdims=True)
    i1 = jnp.min(jnp.where(logits == v1, lane_f, n_lanes), axis=-1, keepdims=True)
    rest = jnp.where(lane_f == i1, -jnp.inf, logits)
    v2 = jnp.max(rest, axis=-1, keepdims=True)
    i2 = jnp.min(jnp.where(rest == v2, lane_f, n_lanes), axis=-1, keepdims=True)
    e2 = jnp.exp(v2 - v1)
    den = 1.0 + e2
    gates = jnp.where(lane_f == i1, 1.0 / den, 0.0) + jnp.where(lane_f == i2, e2 / den, 0.0)
    return gates, (lane_f == i1) | (lane_f == i2)


def _moe_routed_kernel(seq_len, n_rows, x_ref, sh0_ref, sh1_ref, sc0_ref, sc1_ref, g0_ref, g1_ref,
                       npre_ref, npost_ref, wr_ref, br_ref, wg_ref, wu_ref, wd_ref, o_ref,
                       h_s, before_s, dest_s, gate_s, acc_s, cnt_s):
    i = pl.program_id(0)
    e = pl.program_id(1)
    tm = x_ref.shape[0]
    row = lax.broadcasted_iota(jnp.int32, (tm, 1), 0) + i * tm
    valid = row < n_rows
    second = row >= ((i * tm) // seq_len + 1) * seq_len
    pick = lambda a_ref, b_ref: jnp.where(second, b_ref[0], a_ref[0])

    @pl.when((i == 0) & (e == 0))
    def _():
        r_i = lax.broadcasted_iota(jnp.int32, (tm, tm), 0)
        c_i = lax.broadcasted_iota(jnp.int32, (tm, tm), 1)
        before_s[...] = jnp.where(c_i < r_i, 1.0, 0.0).astype(BF16)

    @pl.when(e == 0)
    def _():
        x = jnp.where(valid, x_ref[...], 0.0)
        h = _rms(x, npre_ref[...]) * (1.0 + pick(sc0_ref, sc1_ref)) + pick(sh0_ref, sh1_ref)
        hb = h.astype(BF16)
        h_s[...] = hb
        lane_f = lax.broadcasted_iota(jnp.int32, (tm, LANES), 1).astype(F32)
        logits = jnp.where(lane_f < N_EXPERTS, _halves(_dot, hb, wr_ref[...]) + br_ref[...], -jnp.inf)
        gates, sel = _top2_gates(logits, lane_f)
        self = jnp.where(sel & valid, 1.0, 0.0)
        dest = jnp.where(self > 0.0, _halves(_dot, before_s[...], self.astype(BF16)), -1.0)
        dest_s[...] = dest.T[0:SUBLANES, :]
        gate_s[...] = gates.T[0:SUBLANES, :]
        counts = jnp.sum(self, axis=0, keepdims=True)
        for j in range(N_EXPERTS):
            cnt_s[j] = counts[0, j].astype(jnp.int32)
        acc_s[...] = jnp.zeros_like(acc_s)

    dest_row = dest_s[pl.ds(e, 1), :]
    gate_row = gate_s[pl.ds(e, 1), :]
    def chunk(base, size):
        slot = lax.broadcasted_iota(jnp.int32, (size, tm), 0).astype(F32) + base.astype(F32)
        onehot = jnp.where(slot == dest_row, 1.0, 0.0)
        gate_packed = jnp.sum(onehot * gate_row, axis=-1, keepdims=True)
        pack = onehot.astype(BF16)
        xg = _dot(pack, h_s[...]).astype(BF16)
        act = _silu(_dot(xg, wg_ref[0, 0])) * _dot(xg, wu_ref[0, 0])
        y = _dot(act.astype(BF16), wd_ref[0, 0]) * gate_packed
        acc_s[...] += lax.dot_general(pack, y.astype(BF16), (((0,), (0,)), ((), ())), preferred_element_type=F32)

    shift = MOE_CAP.bit_length() - 1
    cnt = cnt_s[e]
    n_full = lax.shift_right_logical(cnt, shift)
    rem = cnt - lax.shift_left(n_full, shift)
    long_last = (n_full >= 1) & (rem >= 1) & (rem <= MOE_TAIL)
    n_plain = jnp.where(long_last, n_full - 1, n_full + jnp.where(rem >= 1, 1, 0))

    def plain_chunk(c, _):
        chunk(c * MOE_CAP, MOE_CAP)
        return 0

    lax.fori_loop(0, n_plain, plain_chunk, 0)

    @pl.when(long_last)
    def _():
        chunk(n_plain * MOE_CAP, MOE_CAP + MOE_TAIL)

    @pl.when(e == N_EXPERTS - 1)
    def _():
        x = jnp.where(valid, x_ref[...], 0.0)
        o_ref[...] = x + pick(g0_ref, g1_ref) * _rms(acc_s[...], npost_ref[...])


def _moe_routed_call(x2, mods3, lp, seq_len):
    n = x2.shape[0]
    tm = MOE_SP_TM
    n_seq = n // seq_len
    nsamp = mods3.shape[0] - n_seq

    def mod_spec(col, nxt):
        return pl.BlockSpec((1, 1, D_MODEL),
                            lambda i, e: (nsamp + jnp.minimum((i * tm) // seq_len + nxt, n_seq - 1), 0, col))

    mspecs = [mod_spec(col, nxt) for col in (3, 4, 5) for nxt in (0, 1)]
    pad = LANES - N_EXPERTS
    consts = [lp['norm_ffn_pre'], lp['norm_ffn_post'],
              jnp.pad(lp['moe_w_router'], ((0, 0), (0, pad))), jnp.pad(lp['moe_b_router'], ((0, 0), (0, pad)))]
    x_spec = pl.BlockSpec((tm, D_MODEL), lambda i, e: (i, 0))
    w_spec = lambda a: pl.BlockSpec((1, 1) + a.shape[2:], lambda i, e: (0, e, 0, 0))
    experts = [lp['moe_w_gate'], lp['moe_w_up'], lp['moe_w_down']]
    return pl.pallas_call(
        functools.partial(_moe_routed_kernel, seq_len, n),
        out_shape=jax.ShapeDtypeStruct(x2.shape, F32),
        grid=(pl.cdiv(n, tm), N_EXPERTS),
        in_specs=[x_spec] + mspecs + [_const_spec(a.shape) for a in consts] + [w_spec(a) for a in experts],
        out_specs=x_spec,
        scratch_shapes=[pltpu.VMEM((tm, D_MODEL), BF16), pltpu.VMEM((tm, tm), BF16),
                        pltpu.VMEM((SUBLANES, tm), F32), pltpu.VMEM((SUBLANES, tm), F32),
                        pltpu.VMEM((tm, D_MODEL), F32), pltpu.SMEM((N_EXPERTS,), jnp.int32)],
        compiler_params=_params(("arbitrary", "arbitrary")),
        name="moe_routed",
    )(x2, *([mods3] * 6), *consts, *experts)


def _sproj_kernel(x_ref, sh_ref, sc_ref, npre_ref, wa_ref, wg_ref, pa_o, pg_o):
    h = (_rms(x_ref[...], npre_ref[...]) * (1.0 + sc_ref[...]) + sh_ref[...]).astype(BF16)
    pa_o[...] = _dot(h, wa_ref[...])
    pg_o[...] = _dot(h, wg_ref[...])


def _sproj_call(xs, mods2, lp):
    n = xs.shape[0]
    mod_spec = lambda col: pl.BlockSpec((n, D_MODEL), lambda i: (0, col))
    consts = [lp['norm_mix_pre'], lp['w_in_a'], lp['w_in_g']]
    return pl.pallas_call(
        _sproj_kernel,
        out_shape=(jax.ShapeDtypeStruct((n, D_SMALL), F32), jax.ShapeDtypeStruct((n, D_GATES), F32)),
        grid=(1,),
        in_specs=[_const_spec(xs.shape), mod_spec(0), mod_spec(1)] + [_const_spec(a.shape) for a in consts],
        out_specs=(_const_spec((n, D_SMALL)), _const_spec((n, D_GATES))),
        compiler_params=_params(("arbitrary",)),
        name="sample_proj",
    )(xs, mods2, mods2, *consts)


N_SMIX_IN = 22


def _smix_kernel(*refs):
    (u_ref, q_ref, kn_ref, vn_ref, knt_ref, vnt_ref, pin_ref, h0r_ref, h0i_ref, ck_ref, cv_ref, hist_ref,
     wbu_ref, abre_ref, abim_ref, cm_ref, dsk_ref, wglu_ref, bglu_ref, sinks_ref,
     wpool_ref, pscale_ref) = refs[:N_SMIX_IN]
    yssm_o, yatt_o, ypool_o, hr_o, hi_o, ck_o, cv_o, hist_o = refs[-8:]
    u = u_ref[...]
    bu = _dot(u.astype(BF16), wbu_ref[...])
    abr, abi = abre_ref[...], abim_ref[...]
    h0r, h0i = h0r_ref[...], h0i_ref[...]
    hr = bu[:, 0:D_STATE] + (abr * h0r - abi * h0i)
    hi = bu[:, D_STATE:2 * D_STATE] + (abr * h0i + abi * h0r)
    hr_o[...] = hr
    hi_o[...] = hi
    y = _dot(jnp.concatenate([hr, hi], axis=1).astype(BF16), cm_ref[...])
    yssm_o[...] = _ssm_out(y, u, dsk_ref[...], wglu_ref[...], bglu_ref[...])

    q = q_ref[...] * ATTN_SCALE
    kn, vn = kn_ref[...], vn_ref[...]
    kc = ck_ref[...].astype(BF16)
    vc = cv_ref[...].astype(BF16)
    lo = _lane_lo(q.shape)
    j_i = lax.broadcasted_iota(jnp.int32, (1, GQA, 1), 1)
    dist_c = (WINDOW - lax.broadcasted_iota(jnp.int32, (1, 1, WINDOW), 2)).astype(F32)
    halves = []
    for half in range(N_KV_HEADS):
        keep = lo if half == 0 else jnp.logical_not(lo)
        qm = jnp.where(keep, q, 0.0)
        slope = jnp.zeros((1, GQA, 1), F32)
        snk = jnp.zeros((1, GQA, 1), F32)
        for j in range(GQA):
            slope = jnp.where(j_i == j, SLOPES[half * GQA + j], slope)
            snk = jnp.where(j_i == j, sinks_ref[half * GQA + j], snk)
        s = jnp.einsum('bhd,bds->bhs', qm.astype(BF16), kc, preferred_element_type=F32) - slope * dist_c
        s_new = jnp.sum(qm * kn, axis=-1, keepdims=True)
        m = jnp.maximum(jnp.maximum(jnp.max(s, axis=-1, keepdims=True), s_new), snk)
        p = jnp.exp(s - m)
        p_new = jnp.exp(s_new - m)
        den = jnp.sum(p, axis=-1, keepdims=True) + p_new + jnp.exp(snk - m)
        o = jnp.einsum('bhs,bds->bhd', p.astype(BF16), vc, preferred_element_type=F32) + p_new * vn
        halves.append(o / den)
    yatt_o[...] = jnp.where(lo, halves[0], halves[1])

    bc = ck_ref.shape[0]
    lane_w = lax.broadcasted_iota(jnp.int32, (D_KV, WINDOW), 1)
    lane_b = lax.broadcasted_iota(jnp.int32, knt_ref.shape, 1)
    for src, new_t, dst in ((ck_ref, knt_ref, ck_o), (cv_ref, vnt_ref, cv_o)):
        for b in range(bc):
            col = jnp.sum(jnp.where(lane_b == pl.program_id(0) * bc + b, new_t[...], 0.0), axis=-1, keepdims=True)
            dst[b] = jnp.where(lane_w == WINDOW - 1, col, pltpu.roll(src[b], WINDOW - 1, axis=1))

    pin = pin_ref[...]
    lo2 = _lane_lo((pin.shape[0], LANES))
    mixed = []
    for col, (w_lo, w_hi) in enumerate(((POOL_WINDOWS[0], POOL_WINDOWS[1]), (POOL_WINDOWS[2], POOL_WINDOWS[3]))):
        cs = slice(col * LANES, (col + 1) * LANES)
        pf = pin[:, cs]
        acc = pf
        for k in range(1, w_lo):
            acc = acc + hist_ref[POOL_BUF - k, :, cs]
        acc_lo = acc
        for k in range(w_lo, w_hi):
            acc = acc + hist_ref[POOL_BUF - k, :, cs]
        cnt_lo = float(min(w_lo, PAST_LEN + 1))
        cnt_hi = float(min(w_hi, PAST_LEN + 1))
        mixed.append(jnp.where(lo2, acc_lo / cnt_lo, acc / cnt_hi) - pf)
    ypool_o[...] = _dot(jnp.concatenate(mixed, axis=1).astype(BF16), wpool_ref[...]) * pscale_ref[...]
    hist_o[0:POOL_BUF - 1] = hist_ref[1:POOL_BUF]
    hist_o[POOL_BUF - 1] = pin


def _smix_call(layer, u, q3, kn3, vn3, knt, vnt, pin, h0r, h0i, ck, cv, prev_windows, hist_t, lp):
    n = u.shape[0]
    bc = SAMPLE_BC
    rows2 = lambda c: pl.BlockSpec((bc, c), lambda i: (i, 0))
    rows3 = lambda r, c: pl.BlockSpec((bc, r, c), lambda i: (i, 0, 0))
    win_spec = pl.BlockSpec((None, bc, D_KV, WINDOW), lambda i: (layer, i, 0, 0))
    consts = [lp['w_bu'], lp['ab_re'], lp['ab_im'], lp['c_mat'], lp['ssm_d'], lp['ssm_w_glu'], lp['ssm_b_glu']]
    consts2 = [lp['pool_wbd'], lp['pool_scale']]
    hist_spec = pl.BlockSpec((POOL_BUF, bc, D_POOL), lambda i: (0, i, 0))
    in_specs = ([rows2(D_SSM), rows3(GQA, LANES), rows3(1, LANES), rows3(1, LANES),
                 _const_spec(knt.shape), _const_spec(vnt.shape), rows2(D_POOL),
                 rows2(D_STATE), rows2(D_STATE), win_spec, win_spec, hist_spec]
                + [_const_spec(a.shape) for a in consts]
                + [pl.BlockSpec(memory_space=pltpu.SMEM)]
                + [_const_spec(a.shape) for a in consts2]
                + [pl.BlockSpec(memory_space=pl.ANY) for _ in prev_windows])
    assert len(in_specs) == N_SMIX_IN + len(prev_windows)
    aliases = {N_SMIX_IN + k: 5 + k for k in range(len(prev_windows))}
    return pl.pallas_call(
        _smix_kernel,
        out_shape=(jax.ShapeDtypeStruct((n, D_SSM), F32), jax.ShapeDtypeStruct((n, GQA, LANES), F32),
                   jax.ShapeDtypeStruct((n, D_POOL), F32),
                   jax.ShapeDtypeStruct((n, D_STATE), F32), jax.ShapeDtypeStruct((n, D_STATE), F32),
                   jax.ShapeDtypeStruct(ck.shape, F32), jax.ShapeDtypeStruct(cv.shape, F32),
                   jax.ShapeDtypeStruct(hist_t.shape, F32)),
        grid=(n // bc,),
        in_specs=in_specs,
        out_specs=(rows2(D_SSM), rows3(GQA, LANES), rows2(D_POOL), rows2(D_STATE), rows2(D_STATE),
                   win_spec, win_spec, hist_spec),
        input_output_aliases=aliases,
        compiler_params=_params(("arbitrary",)),
        name="sample_mix",
    )(u, q3, kn3, vn3, knt, vnt, pin, h0r, h0i, ck, cv, hist_t, *consts, lp['attn_sinks'], *consts2, *prev_windows)


def _smerge_kernel(x_ref, g_ref, pg_ref, yssm_ref, yatt_ref, ypool_ref, npost_ref,
                   wus_ref, wua_ref, wup_ref, wout_ref, o_ref):
    merged = _sigmoid(pg_ref[:, 0:D_MODEL]) * _dot(yssm_ref[...].astype(BF16), wus_ref[...])
    merged += _sigmoid(pg_ref[:, D_MODEL:2 * D_MODEL]) * _dot(yatt_ref[...].astype(BF16), wua_ref[...])
    merged += _sigmoid(pg_ref[:, 2 * D_MODEL:3 * D_MODEL]) * _dot(ypool_ref[...].astype(BF16), wup_ref[...])
    out = _dot(merged.astype(BF16), wout_ref[...])
    o_ref[...] = x_ref[...] + g_ref[...] * _rms(out, npost_ref[...])


def _smerge_call(xs, mods2, pg, yssm, yatt, ypool, lp):
    n = xs.shape[0]
    ops = [xs, mods2, pg, yssm, yatt, ypool, lp['norm_mix_post'],
           lp['w_up_ssm'], lp['w_up_attn'], lp['w_up_pool'], lp['w_out']]
    in_specs = [_const_spec(a.shape) for a in ops]
    in_specs[1] = pl.BlockSpec((n, D_MODEL), lambda i: (0, 2))
    return pl.pallas_call(
        _smerge_kernel,
        out_shape=jax.ShapeDtypeStruct(xs.shape, F32),
        grid=(1,),
        in_specs=in_specs,
        out_specs=_const_spec(xs.shape),
        compiler_params=_params(("arbitrary",)),
        name="sample_merge",
    )(*ops)


def _block_diag(w):
    g, a, b = w.shape
    eye = jnp.eye(g, dtype=w.dtype)
    return jnp.einsum('gab,gh->gahb', w, eye).reshape(g * a, g * b)


def _pair_heads(w, axis):
    shape = w.shape
    w = w.reshape(shape[:axis] + (N_KV_HEADS, GQA, HEAD_DIM) + shape[axis + 1:])
    return jnp.swapaxes(w, axis, axis + 1).reshape(shape)


def _layer_params(l, p, prep):
    ab_re, ab_im, bb_re, bb_im, pw_re, pw_im = prep
    row = lambda a: a.reshape(1, -1)
    w_in = p['w_in'][l]
    q_cols = _pair_heads(w_in[:, D_SSM:D_SSM + D_ATTN], 1)
    w_in_a = jnp.concatenate([w_in[:, 0:D_SSM], q_cols, w_in[:, D_SSM + D_ATTN:D_SMALL]], axis=1)
    lp = {
        'norm_mix_pre': row(p['norm_mix_pre'][l]), 'norm_mix_post': row(p['norm_mix_post'][l]),
        'norm_ffn_pre': row(p['norm_ffn_pre'][l]), 'norm_ffn_post': row(p['norm_ffn_post'][l]),
        'w_in_a': w_in_a.astype(BF16),
        'w_in_g': w_in[:, D_SMALL:].astype(BF16),
        'w_bu': jnp.concatenate([_block_diag(bb_re[l]), _block_diag(bb_im[l])], axis=1).astype(BF16),
        'ab_re': ab_re[l].reshape(1, D_STATE), 'ab_im': ab_im[l].reshape(1, D_STATE),
        'pw_re': pw_re[l].reshape(SCAN_LEN, D_STATE), 'pw_im': pw_im[l].reshape(SCAN_LEN, D_STATE),
        'c_mat': jnp.concatenate([_block_diag(jnp.swapaxes(p['ssm_c_re'][l], 1, 2)),
                                  _block_diag(jnp.swapaxes(-p['ssm_c_im'][l], 1, 2))], axis=0).astype(BF16),
        'ssm_d': row(p['ssm_d'][l]), 'ssm_w_glu': p['ssm_w_glu'][l].astype(BF16), 'ssm_b_glu': row(p['ssm_b_glu'][l]),
        'attn_sinks': p['attn_sinks'][l],
        'pool_wbd': _block_diag(p['pool_w'][l]).astype(BF16), 'pool_scale': row(p['pool_scale'][l]),
        'w_up_ssm': p['w_up_ssm'][l].astype(BF16), 'w_up_attn': _pair_heads(p['w_up_attn'][l], 0).astype(BF16),
        'w_up_pool': p['w_up_pool'][l].astype(BF16), 'w_out': p['w_out'][l].astype(BF16),
    }
    j = l // 2
    if l % 2 == 1:
        lp.update({'moe_w_router': p['moe_w_router'][j].astype(BF16), 'moe_b_router': row(p['moe_b_router'][j]),
                   'moe_w_gate': p['moe_w_gate'][j:j + 1].astype(BF16), 'moe_w_up': p['moe_w_up'][j:j + 1].astype(BF16),
                   'moe_w_down': p['moe_w_down'][j:j + 1].astype(BF16)})
    else:
        lp.update({'ffn_w_gate': p['ffn_w_gate'][j].astype(BF16), 'ffn_w_up': p['ffn_w_up'][j].astype(BF16),
                   'ffn_w_down': p['ffn_w_down'][j].astype(BF16)})
    return lp


def kernel(x_prompt, x_sample, state_ssm_re, state_ssm_im, cache_win_k, cache_win_v, state_pool,
           c_prompt, c_sample, w_ada, b_ada, norm_mix_pre, norm_mix_post, norm_ffn_pre, norm_ffn_post,
           w_in, ssm_a_re, ssm_a_im, ssm_log_dt, ssm_b_re, ssm_b_im, ssm_c_re, ssm_c_im, ssm_d,
           ssm_w_glu, ssm_b_glu, attn_sinks, pool_w, pool_scale, w_up_ssm, w_up_attn, w_up_pool, w_out,
           ffn_w_gate, ffn_w_up, ffn_w_down, moe_w_router, moe_b_router, moe_w_gate, moe_w_up, moe_w_down):
    p = dict(norm_mix_pre=norm_mix_pre, norm_mix_post=norm_mix_post, norm_ffn_pre=norm_ffn_pre,
             norm_ffn_post=norm_ffn_post, w_in=w_in, ssm_c_re=ssm_c_re, ssm_c_im=ssm_c_im, ssm_d=ssm_d,
             ssm_w_glu=ssm_w_glu, ssm_b_glu=ssm_b_glu, attn_sinks=attn_sinks, pool_w=pool_w, pool_scale=pool_scale,
             w_up_ssm=w_up_ssm, w_up_attn=w_up_attn, w_up_pool=w_up_pool, w_out=w_out,
             ffn_w_gate=ffn_w_gate, ffn_w_up=ffn_w_up, ffn_w_down=ffn_w_down, moe_w_router=moe_w_router,
             moe_b_router=moe_b_router, moe_w_gate=moe_w_gate, moe_w_up=moe_w_up, moe_w_down=moe_w_down)
    depth = w_in.shape[0]
    b, t_len, d = x_prompt.shape
    ns = x_sample.shape[0]

    mods, mods_seq = _ada_call(jnp.concatenate([c_sample, c_prompt], axis=0), b, w_ada, b_ada)
    prep = _ssm_prep_call(ssm_a_re, ssm_a_im, ssm_log_dt, ssm_b_re, ssm_b_im)

    xp = x_prompt
    xs = x_sample.reshape(ns, d)
    p_out = ([], [], [], [], [])
    s_out = ([], [], [], [], [])
    win_minor = lambda c: jnp.transpose(c, (0, 1, 3, 4, 2)).reshape(depth, ns, D_KV, -1)
    win_major = lambda c: jnp.transpose(c.reshape(depth, ns, N_KV_HEADS, HEAD_DIM, -1), (0, 1, 4, 2, 3))
    old_windows = [win_minor(cache_win_k), win_minor(cache_win_v)]
    new_windows = []
    for l in range(depth):
        lp = _layer_params(l, p, prep)
        mods2 = mods[l]
        mods3 = mods_seq[l]

        xp, hre, him, kwin, vwin, plast = _mixer_call(xp, mods3, lp)
        x2 = xp.reshape(b * t_len, d)
        if l % 2 == 1:
            x2 = _moe_routed_call(x2, mods3, lp, t_len)
        else:
            x2 = _ffn_call(x2, mods3, mods2, lp, t_len, False)
        xp = x2.reshape(b, t_len, d)
        p_out[0].append(hre.reshape(b, SSM_GROUPS, SSM_STATE))
        p_out[1].append(him.reshape(b, SSM_GROUPS, SSM_STATE))
        p_out[2].append(kwin.reshape(b, WINDOW, N_KV_HEADS, HEAD_DIM))
        p_out[3].append(vwin.reshape(b, WINDOW, N_KV_HEADS, HEAD_DIM))
        p_out[4].append(plast[:, POOL_HIST - POOL_BUF:, :])

        pa, pg = _sproj_call(xs, mods2, lp)
        u = pa[:, 0:D_SSM]
        q3 = pa[:, D_SSM:D_SSM + D_ATTN].reshape(ns, GQA, LANES)
        k_new = pa[:, D_SSM + D_ATTN:D_SSM + D_ATTN + D_KV]
        v_new = pa[:, D_SSM + D_ATTN + D_KV:D_SSM + D_ATTN + 2 * D_KV]
        pin = pa[:, D_SMALL - D_POOL:]
        yssm, yatt3, ypool, hr, hi, *new_windows, hist_new = _smix_call(
            l, u, q3, k_new.reshape(ns, 1, D_KV), v_new.reshape(ns, 1, D_KV), k_new.T, v_new.T, pin,
            state_ssm_re[l].reshape(ns, D_STATE), state_ssm_im[l].reshape(ns, D_STATE),
            *old_windows, new_windows, jnp.swapaxes(state_pool[l], 0, 1), lp)
        xs = _smerge_call(xs, mods2, pg, yssm, yatt3.reshape(ns, D_ATTN), ypool, lp)
        xs = (_moe_call if l % 2 == 1 else _ffn_call)(xs, mods3, mods2, lp, 1, True)
        s_out[0].append(hr.reshape(ns, SSM_GROUPS, SSM_STATE))
        s_out[1].append(hi.reshape(ns, SSM_GROUPS, SSM_STATE))
        s_out[4].append(jnp.swapaxes(hist_new, 0, 1))

    stack = lambda xs_: [jnp.stack(a, axis=0) for a in xs_]
    p_ssm_re, p_ssm_im, p_win_k, p_win_v, p_pool = stack(p_out)
    s_ssm_re, s_ssm_im, s_pool = stack([s_out[0], s_out[1], s_out[4]])
    s_win_k, s_win_v = [win_major(w) for w in new_windows]
    return (xp, xs.reshape(x_sample.shape), p_ssm_re, p_ssm_im, p_win_k, p_win_v, p_pool,
            s_ssm_re, s_ssm_im, s_win_k, s_win_v, s_pool)
```

```python
import functools
import math
import types

import jax
import jax.numpy as jnp
from jax import lax
from jax.experimental import pallas as pl
from jax.experimental.pallas import tpu as pltpu

F32 = jnp.float32
BF16 = jnp.bfloat16

D_MODEL = 1024
D_SSM = 256
SSM_GROUP = 16
SSM_GROUPS = 16
SSM_STATE = 64
D_STATE = SSM_GROUPS * SSM_STATE
HEAD_DIM = 64
N_HEADS = 8
N_KV_HEADS = 2
GQA = N_HEADS // N_KV_HEADS
D_ATTN = N_HEADS * HEAD_DIM
D_KV = N_KV_HEADS * HEAD_DIM
WINDOW = 128
ATTN_SCALE = HEAD_DIM ** -0.5
NEG_INF = -1e30
LOG2E = math.log2(math.e)
D_POOL = 256
POOL_WINDOWS = (2, 4, 8, 16)
POOL_BUF = 15
POOL_HIST = 16
POOL_PAD = 8
N_MOD = 6
D_FF = 2816
N_EXPERTS = 8
D_EXPERT = 1024
RMS_EPS = 1e-6
PAST_LEN = 16384
D_SMALL = D_SSM + D_ATTN + 2 * D_KV + D_POOL
D_GATES = 3 * D_MODEL
SLOPES = tuple(2.0 ** (-8.0 * (h + 1) / N_HEADS) for h in range(N_HEADS))

SUBLANES = 8
LANES = 128
VMEM_LIMIT = 56 * 1024 * 1024

TT = 512
SEQ_PER_STEP = 1
SCAN_LEN = TT // SUBLANES
Q_BLK = 128
GATE_CHUNK = 512
PROJ_CHUNK = 256
BU_CHUNK = 512
MIXER_LAG = 0
GATE_LOOP_A = 4
GATE_LOOP_B = D_GATES // GATE_CHUNK - GATE_LOOP_A
FFN_TM = 512
MOE_TM = 1024
MOE_SP_TM = 1024
MOE_CAP = 256
MOE_TAIL = 128
SAMPLE_BC = 32
ADA_TN = 1536
FF_CHUNKS = ((0, 1024), (1024, 2048), (2048, D_FF))


def _rms(x, gain):
    return x * lax.rsqrt(jnp.mean(x * x, axis=-1, keepdims=True) + RMS_EPS) * gain


def _sigmoid(x):
    return 1.0 / (1.0 + jnp.exp(-x))


def _silu(x):
    return x * _sigmoid(x)


def _dot(a, b):
    return jnp.dot(a, b, preferred_element_type=F32)


def _dot_t(a, b):
    return lax.dot_general(a, b, (((1,), (1,)), ((), ())), preferred_element_type=F32)


def _halves(dot, a, b):
    half = a.shape[0] // 2
    return jnp.concatenate([dot(a[:half], b), dot(a[half:], b)], axis=0)


def _const_spec(shape):
    nd = len(shape)
    return pl.BlockSpec(shape, lambda *_: (0,) * nd)


def _params(sem):
    return pltpu.CompilerParams(dimension_semantics=sem, vmem_limit_bytes=VMEM_LIMIT)


def _ada_kernel(c_ref, w_ref, b_ref, o_ref, oseq_ref):
    c = c_ref[...]
    s = _silu(c).astype(BF16)
    mod = _dot(s, w_ref[0].astype(BF16)) + b_ref[0]
    o_ref[0] = mod
    n_seq = oseq_ref.shape[1]
    first = mod.shape[0] - n_seq
    for s_i in range(n_seq):
        oseq_ref[0, s_i] = mod[first + s_i:first + s_i + 1, :]


def _ada_call(c_all, n_seq, w_ada, b_ada):
    depth = w_ada.shape[0]
    rows = c_all.shape[0]
    n = N_MOD * D_MODEL
    return pl.pallas_call(
        _ada_kernel,
        out_shape=(jax.ShapeDtypeStruct((depth, rows, n), F32), jax.ShapeDtypeStruct((depth, n_seq, 1, n), F32)),
        grid=(depth, n // ADA_TN),
        in_specs=[
            pl.BlockSpec((rows, D_MODEL), lambda l, j: (0, 0)),
            pl.BlockSpec((1, D_MODEL, ADA_TN), lambda l, j: (l, 0, j)),
            pl.BlockSpec((1, 1, ADA_TN), lambda l, j: (l, 0, j)),
        ],
        out_specs=(pl.BlockSpec((1, rows, ADA_TN), lambda l, j: (l, 0, j)),
                   pl.BlockSpec((1, n_seq, 1, ADA_TN), lambda l, j: (l, 0, 0, j))),
        compiler_params=_params(("arbitrary", "arbitrary")),
        name="ada_mod",
    )(c_all, w_ada, b_ada.reshape(depth, 1, n))


def _ssm_prep_kernel(are_ref, aim_ref, ldt_ref, bre_ref, bim_ref,
                     abre_o, abim_o, bbre_o, bbim_o, pwre_o, pwim_o):
    ar = are_ref[0]
    ai = aim_ref[0]
    dt = jnp.exp(ldt_ref[0])
    mag = jnp.exp(ar * dt)
    ab_re = mag * jnp.cos(ai * dt)
    ab_im = mag * jnp.sin(ai * dt)
    den = ar * ar + ai * ai
    n_re = ab_re - 1.0
    f_re = (n_re * ar + ab_im * ai) / den
    f_im = (ab_im * ar - n_re * ai) / den
    br = bre_ref[0]
    bi = bim_ref[0]
    bbre_o[0] = f_re * br - f_im * bi
    bbim_o[0] = f_re * bi + f_im * br
    abre_o[0] = ab_re
    abim_o[0] = ab_im
    cr, ci = ab_re, ab_im
    for t in range(SCAN_LEN):
        pwre_o[0, t] = cr
        pwim_o[0, t] = ci
        cr, ci = cr * ab_re - ci * ab_im, cr * ab_im + ci * ab_re


def _ssm_prep_call(a_re, a_im, log_dt, b_re, b_im):
    depth = a_re.shape[0]
    g, p, c = SSM_GROUPS, SSM_STATE, SSM_GROUP
    a4 = lambda a: a.reshape(depth, g, 1, p)
    spec_a = pl.BlockSpec((1, g, 1, p), lambda l: (l, 0, 0, 0))
    spec_b = pl.BlockSpec((1, g, c, p), lambda l: (l, 0, 0, 0))
    spec_pw = pl.BlockSpec((1, SCAN_LEN, g, 1, p), lambda l: (l, 0, 0, 0, 0))
    return pl.pallas_call(
        _ssm_prep_kernel,
        out_shape=(jax.ShapeDtypeStruct((depth, g, 1, p), F32),) * 2
        + (jax.ShapeDtypeStruct((depth, g, c, p), F32),) * 2
        + (jax.ShapeDtypeStruct((depth, SCAN_LEN, g, 1, p), F32),) * 2,
        grid=(depth,),
        in_specs=[spec_a, spec_a, pl.BlockSpec((1, g, 1, 1), lambda l: (l, 0, 0, 0)), spec_b, spec_b],
        out_specs=(spec_a, spec_a, spec_b, spec_b, spec_pw, spec_pw),
        compiler_params=_params(("arbitrary",)),
        name="ssm_prep",
    )(a4(a_re), a4(a_im), log_dt.reshape(depth, g, 1, 1),
      jnp.swapaxes(b_re, 2, 3), jnp.swapaxes(b_im, 2, 3))


def _ssm_out(y, u, d_skip, w_glu, b_glu):
    z = jax.nn.gelu(y + d_skip * u)
    return z * _sigmoid(_halves(_dot, z.astype(BF16), w_glu) + b_glu)


def _lane_lo(shape):
    return (lax.broadcasted_iota(jnp.int32, shape, len(shape) - 1) % LANES) < HEAD_DIM


N_MIXER_SEQ_IN = 4
N_MIXER_IN = 24
N_MIXER_OUT = 6


def _mixer_kernel(*refs):
    ins = refs[:N_MIXER_IN]
    outs = refs[N_MIXER_IN:N_MIXER_IN + N_MIXER_OUT]
    scratch = refs[N_MIXER_IN + N_MIXER_OUT:]
    bias_s, wg3_s = scratch[-2:]
    wg_ref = ins[MIXER_REF_NAMES.index('wg_ref')]
    t = pl.program_id(1)

    @pl.when((pl.program_id(0) == 0) & (t == 0))
    def _():
        for c in range(D_GATES // GATE_CHUNK):
            wg3_s[c] = wg_ref[:, c * GATE_CHUNK:(c + 1) * GATE_CHUNK]
        r_i = lax.broadcasted_iota(jnp.int32, (Q_BLK, 2 * Q_BLK), 0)
        c_i = lax.broadcasted_iota(jnp.int32, (Q_BLK, 2 * Q_BLK), 1)
        dist = r_i - c_i + Q_BLK
        valid = (dist >= 0) & (dist <= WINDOW)
        for hh in range(N_HEADS):
            bias_s[hh] = jnp.where(valid, (-SLOPES[hh] * LOG2E) * dist.astype(F32), NEG_INF)

    lvl_s, kbuf, vbuf, pbuf, hc_s = scratch[-7:-2]

    @pl.when(t == 0)
    def _():
        hc_s[...] = jnp.zeros_like(hc_s)
        kbuf[:, 0:WINDOW, :] = jnp.zeros((SEQ_PER_STEP, WINDOW, D_KV), BF16)
        vbuf[:, 0:WINDOW, :] = jnp.zeros((SEQ_PER_STEP, WINDOW, D_KV), BF16)
        pbuf[:, 0:POOL_PAD + POOL_HIST, :] = jnp.zeros((SEQ_PER_STEP, POOL_PAD + POOL_HIST, D_POOL), F32)
        lvl_s[:, :, 0:POOL_PAD, :] = jnp.zeros((SEQ_PER_STEP, 2, POOL_PAD, LANES), F32)

    chains = []
    for s in range(SEQ_PER_STEP):
        one = functools.partial(lambda s, r: r.at[s:s + 1], s)
        chains.append(_mixer_body(t, *[one(r) for r in ins[:N_MIXER_SEQ_IN]], *ins[N_MIXER_SEQ_IN:],
                                  *[one(r) for r in outs], *[r.at[s] for r in scratch[:-2]], bias_s, wg3_s))
    for _ in range(MIXER_LAG):
        next(chains[0])
    while chains:
        for chain in list(chains):
            if next(chain, StopIteration) is StopIteration:
                chains.remove(chain)


MIXER_REF_NAMES = (
    'x_ref sh_ref sc_ref g_ref npre_ref npost_ref wa_ref wg_ref '
    'wbu_ref abre_ref abim_ref pwre_ref pwim_ref cm_ref dsk_ref wglu_ref bglu_ref '
    'sinks_ref wpool_ref pscale_ref wus_ref wua_ref wup_ref wout_ref '
    'xo_ref hre_o him_o kwin_o vwin_o plast_o '
    'hb_s proj_s bu_s hst_s sg_s unat_s uperm_s operm_s yssm_s yatt_s lvl_s kbuf vbuf pbuf hc_s '
    'bias_s wg3_s').split()


def _mixer_body(t, *refs):
    r = types.SimpleNamespace(**dict(zip(MIXER_REF_NAMES, refs, strict=True)))
    x = r.x_ref[0]
    h = _rms(x, r.npre_ref[...]) * (1.0 + r.sc_ref[0]) + r.sh_ref[0]
    r.hb_s[...] = h.astype(BF16)
    yield
    for c in range(D_SMALL // PROJ_CHUNK):
        cols = slice(c * PROJ_CHUNK, (c + 1) * PROJ_CHUNK)
        r.proj_s[:, cols] = _dot(r.hb_s[...], r.wa_ref[:, cols])
        yield
    res = {}
    yield from _ssm_chain(r)
    yield from _attn_chain(t, r)
    yield from _pool_chain(t, r, res)
    yield from _merge_tail(r, res)


def _ssm_chain(r):
    (proj_s, hb_s, bu_s, hst_s, sg_s, unat_s, uperm_s, operm_s, yssm_s, hc_s, wg_ref, wbu_ref, abre_ref, abim_ref,
     pwre_ref, pwim_ref, cm_ref, dsk_ref, wglu_ref, bglu_ref, hre_o, him_o) = (
        r.proj_s, r.hb_s, r.bu_s, r.hst_s, r.sg_s, r.unat_s, r.uperm_s, r.operm_s, r.yssm_s, r.hc_s, r.wg3_s,
        r.wbu_ref, r.abre_ref, r.abim_ref, r.pwre_ref, r.pwim_ref, r.cm_ref, r.dsk_ref, r.wglu_ref, r.bglu_ref,
        r.hre_o, r.him_o)
    for c in range(D_SSM // LANES):
        unat_s[c] = proj_s[:, c * LANES:(c + 1) * LANES]
        for i in range(SCAN_LEN):
            uperm_s[i * SUBLANES:(i + 1) * SUBLANES, c * LANES:(c + 1) * LANES] = (
                unat_s.at[c][pl.ds(i, SUBLANES, stride=SCAN_LEN), :])
        yield
    for c in range(2 * D_STATE // BU_CHUNK):
        cols = slice(c * BU_CHUNK, (c + 1) * BU_CHUNK)
        bu_s[:, cols] = _dot(uperm_s[...].astype(BF16), wbu_ref[:, cols])
        yield

    abr = jnp.broadcast_to(abre_ref[...], (SUBLANES, D_STATE))
    abi = jnp.broadcast_to(abim_ref[...], (SUBLANES, D_STATE))

    def gate_chunk(c):
        sg_s[c] = _sigmoid(_dot(hb_s[...], wg_ref[c]))

    zero = jnp.zeros((SUBLANES, D_STATE), F32)
    steps_a = SCAN_LEN // GATE_LOOP_A

    er, ei = zero, zero
    for i in range(GATE_LOOP_A):
        gate_chunk(i)
        yield
        for s in range(steps_a):
            rows = slice((i * steps_a + s) * SUBLANES, (i * steps_a + s + 1) * SUBLANES)
            er, ei = (abr * er - abi * ei + bu_s[rows, 0:D_STATE],
                      abr * ei + abi * er + bu_s[rows, D_STATE:2 * D_STATE])
            bu_s[rows, 0:D_STATE] = er
            bu_s[rows, D_STATE:2 * D_STATE] = ei
        yield

    alr = pwre_ref[SCAN_LEN - 1:SCAN_LEN, :]
    ali = pwim_ref[SCAN_LEN - 1:SCAN_LEN, :]
    cr = hc_s[0:1, :]
    ci = hc_s[1:2, :]
    row = lax.broadcasted_iota(jnp.int32, (SUBLANES, D_STATE), 0)
    hin_r = zero
    hin_i = zero
    for j in range(SUBLANES):
        hin_r = jnp.where(row == j, cr, hin_r)
        hin_i = jnp.where(row == j, ci, hin_i)
        cr, ci = (alr * cr - ali * ci + er[j:j + 1, :], alr * ci + ali * cr + ei[j:j + 1, :])
    hc_s[0:1, :] = cr
    hc_s[1:2, :] = ci
    hre_o[0] = cr
    him_o[0] = ci
    yield

    pack = 2 * SUBLANES
    packs_b = TT // pack // GATE_LOOP_B

    for i in range(GATE_LOOP_B):
        gate_chunk(GATE_LOOP_A + i)
        yield
        for s in range(packs_b):
            k = i * packs_b + s
            fixed_r, fixed_i = [], []
            for slab in (2 * k, 2 * k + 1):
                rows = slice(slab * SUBLANES, (slab + 1) * SUBLANES)
                pr = pwre_ref[slab:slab + 1, :]
                pi = pwim_ref[slab:slab + 1, :]
                fixed_r.append(bu_s[rows, 0:D_STATE] + (pr * hin_r - pi * hin_i))
                fixed_i.append(bu_s[rows, D_STATE:2 * D_STATE] + (pr * hin_i + pi * hin_r))
            prow = slice(k * pack, (k + 1) * pack)
            hst_s[prow, 0:D_STATE] = jnp.concatenate(fixed_r, axis=0).astype(BF16)
            hst_s[prow, D_STATE:2 * D_STATE] = jnp.concatenate(fixed_i, axis=0).astype(BF16)
        yield

    y = _halves(_dot, hst_s[...], cm_ref[...])
    yield
    operm_s[...] = _ssm_out(y, uperm_s[...], dsk_ref[...], wglu_ref[...], bglu_ref[...])
    yield
    for c in range(D_SSM // LANES):
        for i in range(SCAN_LEN):
            yssm_s.at[c][pl.ds(i, SUBLANES, stride=SCAN_LEN), :] = (
                operm_s[i * SUBLANES:(i + 1) * SUBLANES, c * LANES:(c + 1) * LANES])
        yield

def _attn_chain(t, r):
    proj_s, kbuf, vbuf, yatt_s, bias_s, sinks_ref, kwin_o, vwin_o = (
        r.proj_s, r.kbuf, r.vbuf, r.yatt_s, r.bias_s, r.sinks_ref, r.kwin_o, r.vwin_o)
    kbuf[WINDOW:WINDOW + TT, :] = proj_s[:, D_SSM + D_ATTN:D_SSM + D_ATTN + D_KV].astype(BF16)
    vbuf[WINDOW:WINDOW + TT, :] = proj_s[:, D_SSM + D_ATTN + D_KV:D_SSM + D_ATTN + 2 * D_KV].astype(BF16)
    lo = _lane_lo((Q_BLK, LANES))
    r_i = lax.broadcasted_iota(jnp.int32, (Q_BLK, 2 * Q_BLK), 0)
    c_i = lax.broadcasted_iota(jnp.int32, (Q_BLK, 2 * Q_BLK), 1)
    first_key = jnp.where(t == 0, Q_BLK, 0)
    def scores(blk):
        r0 = blk * Q_BLK
        pieces = []
        for hh in range(N_HEADS):
            j = hh % GQA
            qp = proj_s[r0:r0 + Q_BLK, D_SSM + j * LANES:D_SSM + (j + 1) * LANES] * (ATTN_SCALE * LOG2E)
            keep = lo if hh < GQA else jnp.logical_not(lo)
            pieces.append(jnp.where(keep, qp, 0.0).astype(BF16))
        return [_dot_t(jnp.concatenate(pieces[g * GQA:(g + 1) * GQA], axis=0), kbuf[r0:r0 + 2 * Q_BLK, :])
                for g in range(N_KV_HEADS)]

    n_blk = TT // Q_BLK
    s_next = scores(0)
    yield
    for blk in range(n_blk):
        r0 = blk * Q_BLK
        s_all = s_next
        if blk + 1 < n_blk:
            s_next = scores(blk + 1)
            yield
        probs, dens = [], []
        for hh in range(N_HEADS):
            s = s_all[hh // GQA][(hh % GQA) * Q_BLK:(hh % GQA + 1) * Q_BLK, :] + bias_s[hh]
            if blk == 0:
                s = jnp.where(c_i >= first_key, s, NEG_INF)
            snk = sinks_ref[hh] * LOG2E
            m = jnp.maximum(jnp.max(s, axis=-1, keepdims=True), snk)
            p = jnp.exp2(s - m)
            dens.append(jnp.sum(p, axis=-1, keepdims=True) + jnp.exp2(snk - m))
            probs.append(p.astype(BF16))
            yield
        o_all = [_dot(jnp.concatenate(probs[g * GQA:(g + 1) * GQA], axis=0), vbuf[r0:r0 + 2 * Q_BLK, :])
                 for g in range(N_KV_HEADS)]
        out = lambda hh: o_all[hh // GQA][(hh % GQA) * Q_BLK:(hh % GQA + 1) * Q_BLK, :] / dens[hh]
        for j in range(GQA):
            yatt_s[r0:r0 + Q_BLK, j * LANES:(j + 1) * LANES] = jnp.where(lo, out(j), out(GQA + j)).astype(BF16)
        yield
    kwin_o[0] = proj_s[TT - WINDOW:TT, D_SSM + D_ATTN:D_SSM + D_ATTN + D_KV]
    vwin_o[0] = proj_s[TT - WINDOW:TT, D_SSM + D_ATTN + D_KV:D_SSM + D_ATTN + 2 * D_KV]
    kbuf[0:WINDOW, :] = kbuf[TT:TT + WINDOW, :]
    vbuf[0:WINDOW, :] = vbuf[TT:TT + WINDOW, :]
    yield

def _pool_chain(t, r, res):
    proj_s, pbuf, lvl_s, plast_o, wpool_ref, pscale_ref = (
        r.proj_s, r.pbuf, r.lvl_s, r.plast_o, r.wpool_ref, r.pscale_ref)
    p0 = D_SMALL - D_POOL
    cur = POOL_PAD + POOL_HIST
    end = cur + TT
    pbuf[cur:end, :] = proj_s[:, p0:D_SMALL]
    plast_o[0] = proj_s[TT - POOL_HIST:TT, p0:D_SMALL]
    pos1 = (t * TT + 1 + lax.broadcasted_iota(jnp.int32, (TT, LANES), 0)).astype(F32)
    lo_t = _lane_lo((TT, LANES))
    a_s, b_s = lvl_s.at[0], lvl_s.at[1]
    mixed = []
    for col, (w_lo, w_hi) in enumerate(((POOL_WINDOWS[0], POOL_WINDOWS[1]), (POOL_WINDOWS[2], POOL_WINDOWS[3]))):
        cs = slice(col * LANES, (col + 1) * LANES)
        pf = pbuf[cur:end, cs]
        a_s[POOL_PAD:end, :] = pbuf[POOL_PAD:end, cs] + pbuf[POOL_PAD - 1:end - 1, cs]
        have = 2
        while have < w_lo:
            b_s[POOL_PAD:end, :] = a_s[POOL_PAD:end, :] + a_s[POOL_PAD - have:end - have, :]
            a_s, b_s, have = b_s, a_s, 2 * have
        acc_lo = a_s[cur:end, :]
        while 2 * have < w_hi:
            b_s[POOL_PAD:end, :] = a_s[POOL_PAD:end, :] + a_s[POOL_PAD - have:end - have, :]
            a_s, b_s, have = b_s, a_s, 2 * have
        acc_hi = a_s[cur:end, :] + a_s[cur - have:end - have, :]
        cnt = jnp.where(lo_t, jnp.minimum(float(w_lo), pos1), jnp.minimum(float(w_hi), pos1))
        mixed.append(jnp.where(lo_t, acc_lo, acc_hi) / cnt - pf)
        yield
    ypool = _halves(_dot, jnp.concatenate(mixed, axis=1).astype(BF16), wpool_ref[...]) * pscale_ref[...]
    pbuf[POOL_PAD:cur, :] = pbuf[end - POOL_HIST:end, :]
    res['ypool'] = ypool
    yield


def _merge_tail(r, res):
    sg_s, yssm_s, yatt_s, wus_ref, wua_ref, wup_ref, wout_ref, x_ref, g_ref, npost_ref, xo_ref = (
        r.sg_s, r.yssm_s, r.yatt_s, r.wus_ref, r.wua_ref, r.wup_ref, r.wout_ref, r.x_ref, r.g_ref, r.npost_ref,
        r.xo_ref)
    ypool = res['ypool']
    per_branch = D_MODEL // GATE_CHUNK
    gate = lambda b: jnp.concatenate([sg_s[b * per_branch + c] for c in range(per_branch)], axis=1)
    yssm = jnp.concatenate([yssm_s[c] for c in range(D_SSM // LANES)], axis=1)
    merged = gate(0) * _dot(yssm.astype(BF16), wus_ref[...])
    yield
    merged += gate(1) * _dot(yatt_s[...], wua_ref[...])
    yield
    merged += gate(2) * _dot(ypool.astype(BF16), wup_ref[...])
    yield
    out = _dot(merged.astype(BF16), wout_ref[...])
    yield
    xo_ref[0] = x_ref[0] + g_ref[0] * _rms(out, npost_ref[...])


def _mixer_call(x, mods3, lp):
    b, t_len, d = x.shape
    nt = t_len // TT
    nsamp = mods3.shape[0] - b
    sps = SEQ_PER_STEP
    mod_spec = lambda col: pl.BlockSpec((sps, 1, D_MODEL), lambda i, j: (nsamp // sps + i, 0, col))
    x_spec = pl.BlockSpec((sps, TT, d), lambda i, j: (i, j, 0))
    consts = [lp['norm_mix_pre'], lp['norm_mix_post'], lp['w_in_a'], lp['w_in_g'],
              lp['w_bu'], lp['ab_re'], lp['ab_im'], lp['pw_re'], lp['pw_im'], lp['c_mat'],
              lp['ssm_d'], lp['ssm_w_glu'], lp['ssm_b_glu']]
    consts2 = [lp['pool_wbd'], lp['pool_scale'], lp['w_up_ssm'], lp['w_up_attn'], lp['w_up_pool'], lp['w_out']]
    in_specs = ([x_spec, mod_spec(0), mod_spec(1), mod_spec(2)]
                + [_const_spec(a.shape) for a in consts]
                + [pl.BlockSpec(memory_space=pltpu.SMEM)]
                + [_const_spec(a.shape) for a in consts2])
    per_b = lambda r, c: pl.BlockSpec((sps, r, c), lambda i, j: (i, 0, 0))
    out_shape = (jax.ShapeDtypeStruct((b, t_len, d), F32),
                 jax.ShapeDtypeStruct((b, 1, D_STATE), F32), jax.ShapeDtypeStruct((b, 1, D_STATE), F32),
                 jax.ShapeDtypeStruct((b, WINDOW, D_KV), F32), jax.ShapeDtypeStruct((b, WINDOW, D_KV), F32),
                 jax.ShapeDtypeStruct((b, POOL_HIST, D_POOL), F32))
    out_specs = (x_spec, per_b(1, D_STATE), per_b(1, D_STATE), per_b(WINDOW, D_KV), per_b(WINDOW, D_KV),
                 per_b(POOL_HIST, D_POOL))
    per_seq = lambda shape, dtype: pltpu.VMEM((sps,) + shape, dtype)
    scratch = [per_seq((TT, d), BF16), per_seq((TT, D_SMALL), F32), per_seq((TT, 2 * D_STATE), F32),
               per_seq((TT, 2 * D_STATE), BF16), per_seq((D_GATES // GATE_CHUNK, TT, GATE_CHUNK), F32),
               per_seq((D_SSM // LANES, TT, LANES), F32), per_seq((TT, D_SSM), F32),
               per_seq((TT, D_SSM), F32), per_seq((D_SSM // LANES, TT, LANES), F32),
               per_seq((TT, D_ATTN), BF16), per_seq((2, POOL_PAD + POOL_HIST + TT, LANES), F32),
               per_seq((WINDOW + TT, D_KV), BF16), per_seq((WINDOW + TT, D_KV), BF16),
               per_seq((POOL_PAD + POOL_HIST + TT, D_POOL), F32), per_seq((SUBLANES, D_STATE), F32),
               pltpu.VMEM((N_HEADS, Q_BLK, 2 * Q_BLK), F32),
               pltpu.VMEM((D_GATES // GATE_CHUNK, d, GATE_CHUNK), BF16)]
    return pl.pallas_call(
        _mixer_kernel,
        out_shape=out_shape,
        grid=(b // sps, nt),
        in_specs=in_specs,
        out_specs=out_specs,
        scratch_shapes=scratch,
        compiler_params=_params(("arbitrary", "arbitrary")),
        name="prompt_mixer",
    )(x, mods3, mods3, mods3, *consts, lp['attn_sinks'], *consts2)


def _mod_rows(ref, per_row):
    return ref[...] if per_row else ref[0]


def _ffn_kernel(per_row, x_ref, sh_ref, sc_ref, g_ref, npre_ref, npost_ref, wg_ref, wu_ref, wd_ref, o_ref):
    x = x_ref[...]
    h = (_rms(x, npre_ref[...]) * (1.0 + _mod_rows(sc_ref, per_row)) + _mod_rows(sh_ref, per_row)).astype(BF16)
    f = None
    for c0, c1 in FF_CHUNKS:
        act = _silu(_dot(h, wg_ref[:, c0:c1])) * _dot(h, wu_ref[:, c0:c1])
        part = _dot(act.astype(BF16), wd_ref[c0:c1, :])
        f = part if f is None else f + part
    o_ref[...] = x + _mod_rows(g_ref, per_row) * _rms(f, npost_ref[...])


def _mod_specs(mods3, mods2, n_rows, tm, rows_per_seq, per_row):
    nsamp = mods3.shape[0] - (n_rows // rows_per_seq if not per_row else 0)
    if per_row:
        specs = [pl.BlockSpec((tm, D_MODEL), functools.partial(lambda col, i, *_: (i, col), col)) for col in (3, 4, 5)]
        return specs, [mods2] * 3
    tiles_per_seq = rows_per_seq // tm
    specs = [pl.BlockSpec((1, 1, D_MODEL),
                          functools.partial(lambda col, i, *_: (nsamp + i // tiles_per_seq, 0, col), col))
             for col in (3, 4, 5)]
    return specs, [mods3] * 3


def _ffn_call(x2, mods3, mods2, lp, rows_per_seq, per_row):
    n = x2.shape[0]
    tm = n if per_row else FFN_TM
    mspecs, mops = _mod_specs(mods3, mods2, n, tm, rows_per_seq, per_row)
    consts = [lp['norm_ffn_pre'], lp['norm_ffn_post'], lp['ffn_w_gate'], lp['ffn_w_up'], lp['ffn_w_down']]
    x_spec = pl.BlockSpec((tm, D_MODEL), lambda i: (i, 0))
    return pl.pallas_call(
        functools.partial(_ffn_kernel, per_row),
        out_shape=jax.ShapeDtypeStruct(x2.shape, F32),
        grid=(n // tm,),
        in_specs=[x_spec] + mspecs + [_const_spec(a.shape) for a in consts],
        out_specs=x_spec,
        compiler_params=_params(("arbitrary",)),
        name="ffn_rows" if per_row else "ffn_seq",
    )(x2, *mops, *consts)


def _moe_kernel(per_row, x_ref, sh_ref, sc_ref, g_ref, npre_ref, npost_ref, wr_ref, br_ref,
                wg_ref, wu_ref, wd_ref, o_ref, h_s, gates_s, acc_s):
    e = pl.program_id(1)
    lane_e = lax.broadcasted_iota(jnp.int32, gates_s.shape, 1)

    @pl.when(e == 0)
    def _():
        x = x_ref[...]
        h = _rms(x, npre_ref[...]) * (1.0 + _mod_rows(sc_ref, per_row)) + _mod_rows(sh_ref, per_row)
        hb = h.astype(BF16)
        h_s[...] = hb
        logits = _dot(hb, wr_ref[...]) + br_ref[...]
        gates_s[...], _ = _top2_gates(logits, lane_e.astype(F32))
        acc_s[...] = jnp.zeros_like(acc_s)

    h = h_s[...]
    act = _silu(_dot(h, wg_ref[0, 0])) * _dot(h, wu_ref[0, 0])
    y = _dot(act.astype(BF16), wd_ref[0, 0])
    ge = jnp.sum(jnp.where(lane_e == e, gates_s[...], 0.0), axis=-1, keepdims=True)
    acc_s[...] += ge * y

    @pl.when(e == N_EXPERTS - 1)
    def _():
        o_ref[...] = x_ref[...] + _mod_rows(g_ref, per_row) * _rms(acc_s[...], npost_ref[...])


def _moe_call(x2, mods3, mods2, lp, rows_per_seq, per_row):
    n = x2.shape[0]
    tm = n if per_row else MOE_TM
    mspecs, mops = _mod_specs(mods3, mods2, n, tm, rows_per_seq, per_row)
    consts = [lp['norm_ffn_pre'], lp['norm_ffn_post'], lp['moe_w_router'], lp['moe_b_router']]
    x_spec = pl.BlockSpec((tm, D_MODEL), lambda i, e: (i, 0))
    w_spec = lambda a: pl.BlockSpec((1, 1) + a.shape[2:], lambda i, e: (0, e, 0, 0))
    experts = [lp['moe_w_gate'], lp['moe_w_up'], lp['moe_w_down']]
    return pl.pallas_call(
        functools.partial(_moe_kernel, per_row),
        out_shape=jax.ShapeDtypeStruct(x2.shape, F32),
        grid=(n // tm, N_EXPERTS),
        in_specs=[x_spec] + mspecs + [_const_spec(a.shape) for a in consts] + [w_spec(a) for a in experts],
        out_specs=x_spec,
        scratch_shapes=[pltpu.VMEM((tm, D_MODEL), BF16), pltpu.VMEM((tm, N_EXPERTS), F32),
                        pltpu.VMEM((tm, D_MODEL), F32)],
        compiler_params=_params(("arbitrary", "arbitrary")),
        name="moe_rows" if per_row else "moe_seq",
    )(x2, *mops, *consts, *experts)


def _top2_gates(logits, lane_f):
    n_lanes = float(logits.shape[-1])
    v1 = jnp.max(logits, axis=-1, keepdims=True)
    i1 = jnp.min(jnp.where(logits == v1, lane_f, n_lanes), axis=-1, keepdims=True)
    rest = jnp.where(lane_f == i1, -jnp.inf, logits)
    v2 = jnp.max(rest, axis=-1, keepdims=True)
    i2 = jnp.min(jnp.where(rest == v2, lane_f, n_lanes), axis=-1, keepdims=True)
    e2 = jnp.exp(v2 - v1)
    den = 1.0 + e2
    gates = jnp.where(lane_f == i1, 1.0 / den, 0.0) + jnp.where(lane_f == i2, e2 / den, 0.0)
    return gates, (lane_f == i1) | (lane_f == i2)


def _moe_routed_kernel(seq_len, n_rows, x_ref, sh0_ref, sh1_ref, sc0_ref, sc1_ref, g0_ref, g1_ref,
                       npre_ref, npost_ref, wr_ref, br_ref, wg_ref, wu_ref, wd_ref, o_ref,
                       h_s, before_s, dest_s, gate_s, acc_s, cnt_s):
    i = pl.program_id(0)
    e = pl.program_id(1)
    tm = x_ref.shape[0]
    row = lax.broadcasted_iota(jnp.int32, (tm, 1), 0) + i * tm
    valid = row < n_rows
    second = row >= ((i * tm) // seq_len + 1) * seq_len
    pick = lambda a_ref, b_ref: jnp.where(second, b_ref[0], a_ref[0])

    @pl.when((i == 0) & (e == 0))
    def _():
        r_i = lax.broadcasted_iota(jnp.int32, (tm, tm), 0)
        c_i = lax.broadcasted_iota(jnp.int32, (tm, tm), 1)
        before_s[...] = jnp.where(c_i < r_i, 1.0, 0.0).astype(BF16)

    @pl.when(e == 0)
    def _():
        x = jnp.where(valid, x_ref[...], 0.0)
        h = _rms(x, npre_ref[...]) * (1.0 + pick(sc0_ref, sc1_ref)) + pick(sh0_ref, sh1_ref)
        hb = h.astype(BF16)
        h_s[...] = hb
        lane_f = lax.broadcasted_iota(jnp.int32, (tm, LANES), 1).astype(F32)
        logits = jnp.where(lane_f < N_EXPERTS, _halves(_dot, hb, wr_ref[...]) + br_ref[...], -jnp.inf)
        gates, sel = _top2_gates(logits, lane_f)
        self = jnp.where(sel & valid, 1.0, 0.0)
        dest = jnp.where(self > 0.0, _halves(_dot, before_s[...], self.astype(BF16)), -1.0)
        dest_s[...] = dest.T[0:SUBLANES, :]
        gate_s[...] = gates.T[0:SUBLANES, :]
        counts = jnp.sum(self, axis=0, keepdims=True)
        for j in range(N_EXPERTS):
            cnt_s[j] = counts[0, j].astype(jnp.int32)
        acc_s[...] = jnp.zeros_like(acc_s)

    dest_row = dest_s[pl.ds(e, 1), :]
    gate_row = gate_s[pl.ds(e, 1), :]
    def chunk(base, size):
        slot = lax.broadcasted_iota(jnp.int32, (size, tm), 0).astype(F32) + base.astype(F32)
        onehot = jnp.where(slot == dest_row, 1.0, 0.0)
        gate_packed = jnp.sum(onehot * gate_row, axis=-1, keepdims=True)
        pack = onehot.astype(BF16)
        xg = _dot(pack, h_s[...]).astype(BF16)
        act = _silu(_dot(xg, wg_ref[0, 0])) * _dot(xg, wu_ref[0, 0])
        y = _dot(act.astype(BF16), wd_ref[0, 0]) * gate_packed
        acc_s[...] += lax.dot_general(pack, y.astype(BF16), (((0,), (0,)), ((), ())), preferred_element_type=F32)

    shift = MOE_CAP.bit_length() - 1
    cnt = cnt_s[e]
    n_full = lax.shift_right_logical(cnt, shift)
    rem = cnt - lax.shift_left(n_full, shift)
    long_last = (n_full >= 1) & (rem >= 1) & (rem <= MOE_TAIL)
    n_plain = jnp.where(long_last, n_full - 1, n_full + jnp.where(rem >= 1, 1, 0))

    def plain_chunk(c, _):
        chunk(c * MOE_CAP, MOE_CAP)
        return 0

    lax.fori_loop(0, n_plain, plain_chunk, 0)

    @pl.when(long_last)
    def _():
        chunk(n_plain * MOE_CAP, MOE_CAP + MOE_TAIL)

    @pl.when(e == N_EXPERTS - 1)
    def _():
        x = jnp.where(valid, x_ref[...], 0.0)
        o_ref[...] = x + pick(g0_ref, g1_ref) * _rms(acc_s[...], npost_ref[...])


def _moe_routed_call(x2, mods3, lp, seq_len):
    n = x2.shape[0]
    tm = MOE_SP_TM
    n_seq = n // seq_len
    nsamp = mods3.shape[0] - n_seq

    def mod_spec(col, nxt):
        return pl.BlockSpec((1, 1, D_MODEL),
                            lambda i, e: (nsamp + jnp.minimum((i * tm) // seq_len + nxt, n_seq - 1), 0, col))

    mspecs = [mod_spec(col, nxt) for col in (3, 4, 5) for nxt in (0, 1)]
    pad = LANES - N_EXPERTS
    consts = [lp['norm_ffn_pre'], lp['norm_ffn_post'],
              jnp.pad(lp['moe_w_router'], ((0, 0), (0, pad))), jnp.pad(lp['moe_b_router'], ((0, 0), (0, pad)))]
    x_spec = pl.BlockSpec((tm, D_MODEL), lambda i, e: (i, 0))
    w_spec = lambda a: pl.BlockSpec((1, 1) + a.shape[2:], lambda i, e: (0, e, 0, 0))
    experts = [lp['moe_w_gate'], lp['moe_w_up'], lp['moe_w_down']]
    return pl.pallas_call(
        functools.partial(_moe_routed_kernel, seq_len, n),
        out_shape=jax.ShapeDtypeStruct(x2.shape, F32),
        grid=(pl.cdiv(n, tm), N_EXPERTS),
        in_specs=[x_spec] + mspecs + [_const_spec(a.shape) for a in consts] + [w_spec(a) for a in experts],
        out_specs=x_spec,
        scratch_shapes=[pltpu.VMEM((tm, D_MODEL), BF16), pltpu.VMEM((tm, tm), BF16),
                        pltpu.VMEM((SUBLANES, tm), F32), pltpu.VMEM((SUBLANES, tm), F32),
                        pltpu.VMEM((tm, D_MODEL), F32), pltpu.SMEM((N_EXPERTS,), jnp.int32)],
        compiler_params=_params(("arbitrary", "arbitrary")),
        name="moe_routed",
    )(x2, *([mods3] * 6), *consts, *experts)


def _sproj_kernel(x_ref, sh_ref, sc_ref, npre_ref, wa_ref, wg_ref, pa_o, pg_o):
    h = (_rms(x_ref[...], npre_ref[...]) * (1.0 + sc_ref[...]) + sh_ref[...]).astype(BF16)
    pa_o[...] = _dot(h, wa_ref[...])
    pg_o[...] = _dot(h, wg_ref[...])


def _sproj_call(xs, mods2, lp):
    n = xs.shape[0]
    mod_spec = lambda col: pl.BlockSpec((n, D_MODEL), lambda i: (0, col))
    consts = [lp['norm_mix_pre'], lp['w_in_a'], lp['w_in_g']]
    return pl.pallas_call(
        _sproj_kernel,
        out_shape=(jax.ShapeDtypeStruct((n, D_SMALL), F32), jax.ShapeDtypeStruct((n, D_GATES), F32)),
        grid=(1,),
        in_specs=[_const_spec(xs.shape), mod_spec(0), mod_spec(1)] + [_const_spec(a.shape) for a in consts],
        out_specs=(_const_spec((n, D_SMALL)), _const_spec((n, D_GATES))),
        compiler_params=_params(("arbitrary",)),
        name="sample_proj",
    )(xs, mods2, mods2, *consts)


N_SMIX_IN = 22


def _smix_kernel(*refs):
    (u_ref, q_ref, kn_ref, vn_ref, knt_ref, vnt_ref, pin_ref, h0r_ref, h0i_ref, ck_ref, cv_ref, hist_ref,
     wbu_ref, abre_ref, abim_ref, cm_ref, dsk_ref, wglu_ref, bglu_ref, sinks_ref,
     wpool_ref, pscale_ref) = refs[:N_SMIX_IN]
    yssm_o, yatt_o, ypool_o, hr_o, hi_o, ck_o, cv_o, hist_o = refs[-8:]
    u = u_ref[...]
    bu = _dot(u.astype(BF16), wbu_ref[...])
    abr, abi = abre_ref[...], abim_ref[...]
    h0r, h0i = h0r_ref[...], h0i_ref[...]
    hr = bu[:, 0:D_STATE] + (abr * h0r - abi * h0i)
    hi = bu[:, D_STATE:2 * D_STATE] + (abr * h0i + abi * h0r)
    hr_o[...] = hr
    hi_o[...] = hi
    y = _dot(jnp.concatenate([hr, hi], axis=1).astype(BF16), cm_ref[...])
    yssm_o[...] = _ssm_out(y, u, dsk_ref[...], wglu_ref[...], bglu_ref[...])

    q = q_ref[...] * ATTN_SCALE
    kn, vn = kn_ref[...], vn_ref[...]
    kc = ck_ref[...].astype(BF16)
    vc = cv_ref[...].astype(BF16)
    lo = _lane_lo(q.shape)
    j_i = lax.broadcasted_iota(jnp.int32, (1, GQA, 1), 1)
    dist_c = (WINDOW - lax.broadcasted_iota(jnp.int32, (1, 1, WINDOW), 2)).astype(F32)
    halves = []
    for half in range(N_KV_HEADS):
        keep = lo if half == 0 else jnp.logical_not(lo)
        qm = jnp.where(keep, q, 0.0)
        slope = jnp.zeros((1, GQA, 1), F32)
        snk = jnp.zeros((1, GQA, 1), F32)
        for j in range(GQA):
            slope = jnp.where(j_i == j, SLOPES[half * GQA + j], slope)
            snk = jnp.where(j_i == j, sinks_ref[half * GQA + j], snk)
        s = jnp.einsum('bhd,bds->bhs', qm.astype(BF16), kc, preferred_element_type=F32) - slope * dist_c
        s_new = jnp.sum(qm * kn, axis=-1, keepdims=True)
        m = jnp.maximum(jnp.maximum(jnp.max(s, axis=-1, keepdims=True), s_new), snk)
        p = jnp.exp(s - m)
        p_new = jnp.exp(s_new - m)
        den = jnp.sum(p, axis=-1, keepdims=True) + p_new + jnp.exp(snk - m)
        o = jnp.einsum('bhs,bds->bhd', p.astype(BF16), vc, preferred_element_type=F32) + p_new * vn
        halves.append(o / den)
    yatt_o[...] = jnp.where(lo, halves[0], halves[1])

    bc = ck_ref.shape[0]
    lane_w = lax.broadcasted_iota(jnp.int32, (D_KV, WINDOW), 1)
    lane_b = lax.broadcasted_iota(jnp.int32, knt_ref.shape, 1)
    for src, new_t, dst in ((ck_ref, knt_ref, ck_o), (cv_ref, vnt_ref, cv_o)):
        for b in range(bc):
            col = jnp.sum(jnp.where(lane_b == pl.program_id(0) * bc + b, new_t[...], 0.0), axis=-1, keepdims=True)
            dst[b] = jnp.where(lane_w == WINDOW - 1, col, pltpu.roll(src[b], WINDOW - 1, axis=1))

    pin = pin_ref[...]
    lo2 = _lane_lo((pin.shape[0], LANES))
    mixed = []
    for col, (w_lo, w_hi) in enumerate(((POOL_WINDOWS[0], POOL_WINDOWS[1]), (POOL_WINDOWS[2], POOL_WINDOWS[3]))):
        cs = slice(col * LANES, (col + 1) * LANES)
        pf = pin[:, cs]
        acc = pf
        for k in range(1, w_lo):
            acc = acc + hist_ref[POOL_BUF - k, :, cs]
        acc_lo = acc
        for k in range(w_lo, w_hi):
            acc = acc + hist_ref[POOL_BUF - k, :, cs]
        cnt_lo = float(min(w_lo, PAST_LEN + 1))
        cnt_hi = float(min(w_hi, PAST_LEN + 1))
        mixed.append(jnp.where(lo2, acc_lo / cnt_lo, acc / cnt_hi) - pf)
    ypool_o[...] = _dot(jnp.concatenate(mixed, axis=1).astype(BF16), wpool_ref[...]) * pscale_ref[...]
    hist_o[0:POOL_BUF - 1] = hist_ref[1:POOL_BUF]
    hist_o[POOL_BUF - 1] = pin


def _smix_call(layer, u, q3, kn3, vn3, knt, vnt, pin, h0r, h0i, ck, cv, prev_windows, hist_t, lp):
    n = u.shape[0]
    bc = SAMPLE_BC
    rows2 = lambda c: pl.BlockSpec((bc, c), lambda i: (i, 0))
    rows3 = lambda r, c: pl.BlockSpec((bc, r, c), lambda i: (i, 0, 0))
    win_spec = pl.BlockSpec((None, bc, D_KV, WINDOW), lambda i: (layer, i, 0, 0))
    consts = [lp['w_bu'], lp['ab_re'], lp['ab_im'], lp['c_mat'], lp['ssm_d'], lp['ssm_w_glu'], lp['ssm_b_glu']]
    consts2 = [lp['pool_wbd'], lp['pool_scale']]
    hist_spec = pl.BlockSpec((POOL_BUF, bc, D_POOL), lambda i: (0, i, 0))
    in_specs = ([rows2(D_SSM), rows3(GQA, LANES), rows3(1, LANES), rows3(1, LANES),
                 _const_spec(knt.shape), _const_spec(vnt.shape), rows2(D_POOL),
                 rows2(D_STATE), rows2(D_STATE), win_spec, win_spec, hist_spec]
                + [_const_spec(a.shape) for a in consts]
                + [pl.BlockSpec(memory_space=pltpu.SMEM)]
                + [_const_spec(a.shape) for a in consts2]
                + [pl.BlockSpec(memory_space=pl.ANY) for _ in prev_windows])
    assert len(in_specs) == N_SMIX_IN + len(prev_windows)
    aliases = {N_SMIX_IN + k: 5 + k for k in range(len(prev_windows))}
    return pl.pallas_call(
        _smix_kernel,
        out_shape=(jax.ShapeDtypeStruct((n, D_SSM), F32), jax.ShapeDtypeStruct((n, GQA, LANES), F32),
                   jax.ShapeDtypeStruct((n, D_POOL), F32),
                   jax.ShapeDtypeStruct((n, D_STATE), F32), jax.ShapeDtypeStruct((n, D_STATE), F32),
                   jax.ShapeDtypeStruct(ck.shape, F32), jax.ShapeDtypeStruct(cv.shape, F32),
                   jax.ShapeDtypeStruct(hist_t.shape, F32)),
        grid=(n // bc,),
        in_specs=in_specs,
        out_specs=(rows2(D_SSM), rows3(GQA, LANES), rows2(D_POOL), rows2(D_STATE), rows2(D_STATE),
                   win_spec, win_spec, hist_spec),
        input_output_aliases=aliases,
        compiler_params=_params(("arbitrary",)),
        name="sample_mix",
    )(u, q3, kn3, vn3, knt, vnt, pin, h0r, h0i, ck, cv, hist_t, *consts, lp['attn_sinks'], *consts2, *prev_windows)


def _smerge_kernel(x_ref, g_ref, pg_ref, yssm_ref, yatt_ref, ypool_ref, npost_ref,
                   wus_ref, wua_ref, wup_ref, wout_ref, o_ref):
    merged = _sigmoid(pg_ref[:, 0:D_MODEL]) * _dot(yssm_ref[...].astype(BF16), wus_ref[...])
    merged += _sigmoid(pg_ref[:, D_MODEL:2 * D_MODEL]) * _dot(yatt_ref[...].astype(BF16), wua_ref[...])
    merged += _sigmoid(pg_ref[:, 2 * D_MODEL:3 * D_MODEL]) * _dot(ypool_ref[...].astype(BF16), wup_ref[...])
    out = _dot(merged.astype(BF16), wout_ref[...])
    o_ref[...] = x_ref[...] + g_ref[...] * _rms(out, npost_ref[...])


def _smerge_call(xs, mods2, pg, yssm, yatt, ypool, lp):
    n = xs.shape[0]
    ops = [xs, mods2, pg, yssm, yatt, ypool, lp['norm_mix_post'],
           lp['w_up_ssm'], lp['w_up_attn'], lp['w_up_pool'], lp['w_out']]
    in_specs = [_const_spec(a.shape) for a in ops]
    in_specs[1] = pl.BlockSpec((n, D_MODEL), lambda i: (0, 2))
    return pl.pallas_call(
        _smerge_kernel,
        out_shape=jax.ShapeDtypeStruct(xs.shape, F32),
        grid=(1,),
        in_specs=in_specs,
        out_specs=_const_spec(xs.shape),
        compiler_params=_params(("arbitrary",)),
        name="sample_merge",
    )(*ops)


def _block_diag(w):
    g, a, b = w.shape
    eye = jnp.eye(g, dtype=w.dtype)
    return jnp.einsum('gab,gh->gahb', w, eye).reshape(g * a, g * b)


def _pair_heads(w, axis):
    shape = w.shape
    w = w.reshape(shape[:axis] + (N_KV_HEADS, GQA, HEAD_DIM) + shape[axis + 1:])
    return jnp.swapaxes(w, axis, axis + 1).reshape(shape)


def _layer_params(l, p, prep):
    ab_re, ab_im, bb_re, bb_im, pw_re, pw_im = prep
    row = lambda a: a.reshape(1, -1)
    w_in = p['w_in'][l]
    q_cols = _pair_heads(w_in[:, D_SSM:D_SSM + D_ATTN], 1)
    w_in_a = jnp.concatenate([w_in[:, 0:D_SSM], q_cols, w_in[:, D_SSM + D_ATTN:D_SMALL]], axis=1)
    lp = {
        'norm_mix_pre': row(p['norm_mix_pre'][l]), 'norm_mix_post': row(p['norm_mix_post'][l]),
        'norm_ffn_pre': row(p['norm_ffn_pre'][l]), 'norm_ffn_post': row(p['norm_ffn_post'][l]),
        'w_in_a': w_in_a.astype(BF16),
        'w_in_g': w_in[:, D_SMALL:].astype(BF16),
        'w_bu': jnp.concatenate([_block_diag(bb_re[l]), _block_diag(bb_im[l])], axis=1).astype(BF16),
        'ab_re': ab_re[l].reshape(1, D_STATE), 'ab_im': ab_im[l].reshape(1, D_STATE),
        'pw_re': pw_re[l].reshape(SCAN_LEN, D_STATE), 'pw_im': pw_im[l].reshape(SCAN_LEN, D_STATE),
        'c_mat': jnp.concatenate([_block_diag(jnp.swapaxes(p['ssm_c_re'][l], 1, 2)),
                                  _block_diag(jnp.swapaxes(-p['ssm_c_im'][l], 1, 2))], axis=0).astype(BF16),
        'ssm_d': row(p['ssm_d'][l]), 'ssm_w_glu': p['ssm_w_glu'][l].astype(BF16), 'ssm_b_glu': row(p['ssm_b_glu'][l]),
        'attn_sinks': p['attn_sinks'][l],
        'pool_wbd': _block_diag(p['pool_w'][l]).astype(BF16), 'pool_scale': row(p['pool_scale'][l]),
        'w_up_ssm': p['w_up_ssm'][l].astype(BF16), 'w_up_attn': _pair_heads(p['w_up_attn'][l], 0).astype(BF16),
        'w_up_pool': p['w_up_pool'][l].astype(BF16), 'w_out': p['w_out'][l].astype(BF16),
    }
    j = l // 2
    if l % 2 == 1:
        lp.update({'moe_w_router': p['moe_w_router'][j].astype(BF16), 'moe_b_router': row(p['moe_b_router'][j]),
                   'moe_w_gate': p['moe_w_gate'][j:j + 1].astype(BF16), 'moe_w_up': p['moe_w_up'][j:j + 1].astype(BF16),
                   'moe_w_down': p['moe_w_down'][j:j + 1].astype(BF16)})
    else:
        lp.update({'ffn_w_gate': p['ffn_w_gate'][j].astype(BF16), 'ffn_w_up': p['ffn_w_up'][j].astype(BF16),
                   'ffn_w_down': p['ffn_w_down'][j].astype(BF16)})
    return lp


def kernel(x_prompt, x_sample, state_ssm_re, state_ssm_im, cache_win_k, cache_win_v, state_pool,
           c_prompt, c_sample, w_ada, b_ada, norm_mix_pre, norm_mix_post, norm_ffn_pre, norm_ffn_post,
           w_in, ssm_a_re, ssm_a_im, ssm_log_dt, ssm_b_re, ssm_b_im, ssm_c_re, ssm_c_im, ssm_d,
           ssm_w_glu, ssm_b_glu, attn_sinks, pool_w, pool_scale, w_up_ssm, w_up_attn, w_up_pool, w_out,
           ffn_w_gate, ffn_w_up, ffn_w_down, moe_w_router, moe_b_router, moe_w_gate, moe_w_up, moe_w_down):
    p = dict(norm_mix_pre=norm_mix_pre, norm_mix_post=norm_mix_post, norm_ffn_pre=norm_ffn_pre,
             norm_ffn_post=norm_ffn_post, w_in=w_in, ssm_c_re=ssm_c_re, ssm_c_im=ssm_c_im, ssm_d=ssm_d,
             ssm_w_glu=ssm_w_glu, ssm_b_glu=ssm_b_glu, attn_sinks=attn_sinks, pool_w=pool_w, pool_scale=pool_scale,
             w_up_ssm=w_up_ssm, w_up_attn=w_up_attn, w_up_pool=w_up_pool, w_out=w_out,
             ffn_w_gate=ffn_w_gate, ffn_w_up=ffn_w_up, ffn_w_down=ffn_w_down, moe_w_router=moe_w_router,
             moe_b_router=moe_b_router, moe_w_gate=moe_w_gate, moe_w_up=moe_w_up, moe_w_down=moe_w_down)
    depth = w_in.shape[0]
    b, t_len, d = x_prompt.shape
    ns = x_sample.shape[0]

    mods, mods_seq = _ada_call(jnp.concatenate([c_sample, c_prompt], axis=0), b, w_ada, b_ada)
    prep = _ssm_prep_call(ssm_a_re, ssm_a_im, ssm_log_dt, ssm_b_re, ssm_b_im)

    xp = x_prompt
    xs = x_sample.reshape(ns, d)
    p_out = ([], [], [], [], [])
    s_out = ([], [], [], [], [])
    win_minor = lambda c: jnp.transpose(c, (0, 1, 3, 4, 2)).reshape(depth, ns, D_KV, -1)
    win_major = lambda c: jnp.transpose(c.reshape(depth, ns, N_KV_HEADS, HEAD_DIM, -1), (0, 1, 4, 2, 3))
    old_windows = [win_minor(cache_win_k), win_minor(cache_win_v)]
    new_windows = []
    for l in range(depth):
        lp = _layer_params(l, p, prep)
        mods2 = mods[l]
        mods3 = mods_seq[l]

        xp, hre, him, kwin, vwin, plast = _mixer_call(xp, mods3, lp)
        x2 = xp.reshape(b * t_len, d)
        if l % 2 == 1:
            x2 = _moe_routed_call(x2, mods3, lp, t_len)
        else:
            x2 = _ffn_call(x2, mods3, mods2, lp, t_len, False)
        xp = x2.reshape(b, t_len, d)
        p_out[0].append(hre.reshape(b, SSM_GROUPS, SSM_STATE))
        p_out[1].append(him.reshape(b, SSM_GROUPS, SSM_STATE))
        p_out[2].append(kwin.reshape(b, WINDOW, N_KV_HEADS, HEAD_DIM))
        p_out[3].append(vwin.reshape(b, WINDOW, N_KV_HEADS, HEAD_DIM))
        p_out[4].append(plast[:, POOL_HIST - POOL_BUF:, :])

        pa, pg = _sproj_call(xs, mods2, lp)
        u = pa[:, 0:D_SSM]
        q3 = pa[:, D_SSM:D_SSM + D_ATTN].reshape(ns, GQA, LANES)
        k_new = pa[:, D_SSM + D_ATTN:D_SSM + D_ATTN + D_KV]
        v_new = pa[:, D_SSM + D_ATTN + D_KV:D_SSM + D_ATTN + 2 * D_KV]
        pin = pa[:, D_SMALL - D_POOL:]
        yssm, yatt3, ypool, hr, hi, *new_windows, hist_new = _smix_call(
            l, u, q3, k_new.reshape(ns, 1, D_KV), v_new.reshape(ns, 1, D_KV), k_new.T, v_new.T, pin,
            state_ssm_re[l].reshape(ns, D_STATE), state_ssm_im[l].reshape(ns, D_STATE),
            *old_windows, new_windows, jnp.swapaxes(state_pool[l], 0, 1), lp)
        xs = _smerge_call(xs, mods2, pg, yssm, yatt3.reshape(ns, D_ATTN), ypool, lp)
        xs = (_moe_call if l % 2 == 1 else _ffn_call)(xs, mods3, mods2, lp, 1, True)
        s_out[0].append(hr.reshape(ns, SSM_GROUPS, SSM_STATE))
        s_out[1].append(hi.reshape(ns, SSM_GROUPS, SSM_STATE))
        s_out[4].append(jnp.swapaxes(hist_new, 0, 1))

    stack = lambda xs_: [jnp.stack(a, axis=0) for a in xs_]
    p_ssm_re, p_ssm_im, p_win_k, p_win_v, p_pool = stack(p_out)
    s_ssm_re, s_ssm_im, s_pool = stack([s_out[0], s_out[1], s_out[4]])
    s_win_k, s_win_v = [win_major(w) for w in new_windows]
    return (xp, xs.reshape(x_sample.shape), p_ssm_re, p_ssm_im, p_win_k, p_win_v, p_pool,
            s_ssm_re, s_ssm_im, s_win_k, s_win_v, s_pool)
```

```python
import functools
import math
import types

import jax
import jax.numpy as jnp
from jax import lax
from jax.experimental import pallas as pl
from jax.experimental.pallas import tpu as pltpu

F32 = jnp.float32
BF16 = jnp.bfloat16

D_MODEL = 1024
D_SSM = 256
SSM_GROUP = 16
SSM_GROUPS = 16
SSM_STATE = 64
D_STATE = SSM_GROUPS * SSM_STATE
HEAD_DIM = 64
N_HEADS = 8
N_KV_HEADS = 2
GQA = N_HEADS // N_KV_HEADS
D_ATTN = N_HEADS * HEAD_DIM
D_KV = N_KV_HEADS * HEAD_DIM
WINDOW = 128
ATTN_SCALE = HEAD_DIM ** -0.5
NEG_INF = -1e30
LOG2E = math.log2(math.e)
D_POOL = 256
POOL_WINDOWS = (2, 4, 8, 16)
POOL_BUF = 15
POOL_HIST = 16
POOL_PAD = 8
N_MOD = 6
D_FF = 2816
N_EXPERTS = 8
D_EXPERT = 1024
RMS_EPS = 1e-6
PAST_LEN = 16384
D_SMALL = D_SSM + D_ATTN + 2 * D_KV + D_POOL
D_GATES = 3 * D_MODEL
SLOPES = tuple(2.0 ** (-8.0 * (h + 1) / N_HEADS) for h in range(N_HEADS))

SUBLANES = 8
LANES = 128
VMEM_LIMIT = 56 * 1024 * 1024

TT = 512
SEQ_PER_STEP = 1
SCAN_LEN = TT // SUBLANES
Q_BLK = 128
GATE_CHUNK = 512
PROJ_CHUNK = 256
BU_CHUNK = 512
MIXER_LAG = 0
GATE_LOOP_A = 4
GATE_LOOP_B = D_GATES // GATE_CHUNK - GATE_LOOP_A
FFN_TM = 512
MOE_TM = 1024
MOE_SP_TM = 1024
MOE_CAP = 256
MOE_TAIL = 128
SAMPLE_BC = 32
ADA_TN = 1536
FF_CHUNKS = ((0, 1024), (1024, 2048), (2048, D_FF))


def _rms(x, gain):
    return x * lax.rsqrt(jnp.mean(x * x, axis=-1, keepdims=True) + RMS_EPS) * gain


def _sigmoid(x):
    return 1.0 / (1.0 + jnp.exp(-x))


def _silu(x):
    return x * _sigmoid(x)


def _dot(a, b):
    return jnp.dot(a, b, preferred_element_type=F32)


def _dot_t(a, b):
    return lax.dot_general(a, b, (((1,), (1,)), ((), ())), preferred_element_type=F32)


def _halves(dot, a, b):
    half = a.shape[0] // 2
    return jnp.concatenate([dot(a[:half], b), dot(a[half:], b)], axis=0)


def _const_spec(shape):
    nd = len(shape)
    return pl.BlockSpec(shape, lambda *_: (0,) * nd)


def _params(sem):
    return pltpu.CompilerParams(dimension_semantics=sem, vmem_limit_bytes=VMEM_LIMIT)


def _ada_kernel(c_ref, w_ref, b_ref, o_ref, oseq_ref):
    c = c_ref[...]
    s = _silu(c).astype(BF16)
    mod = _dot(s, w_ref[0].astype(BF16)) + b_ref[0]
    o_ref[0] = mod
    n_seq = oseq_ref.shape[1]
    first = mod.shape[0] - n_seq
    for s_i in range(n_seq):
        oseq_ref[0, s_i] = mod[first + s_i:first + s_i + 1, :]


def _ada_call(c_all, n_seq, w_ada, b_ada):
    depth = w_ada.shape[0]
    rows = c_all.shape[0]
    n = N_MOD * D_MODEL
    return pl.pallas_call(
        _ada_kernel,
        out_shape=(jax.ShapeDtypeStruct((depth, rows, n), F32), jax.ShapeDtypeStruct((depth, n_seq, 1, n), F32)),
        grid=(depth, n // ADA_TN),
        in_specs=[
            pl.BlockSpec((rows, D_MODEL), lambda l, j: (0, 0)),
            pl.BlockSpec((1, D_MODEL, ADA_TN), lambda l, j: (l, 0, j)),
            pl.BlockSpec((1, 1, ADA_TN), lambda l, j: (l, 0, j)),
        ],
        out_specs=(pl.BlockSpec((1, rows, ADA_TN), lambda l, j: (l, 0, j)),
                   pl.BlockSpec((1, n_seq, 1, ADA_TN), lambda l, j: (l, 0, 0, j))),
        compiler_params=_params(("arbitrary", "arbitrary")),
        name="ada_mod",
    )(c_all, w_ada, b_ada.reshape(depth, 1, n))


def _ssm_prep_kernel(are_ref, aim_ref, ldt_ref, bre_ref, bim_ref,
                     abre_o, abim_o, bbre_o, bbim_o, pwre_o, pwim_o):
    ar = are_ref[0]
    ai = aim_ref[0]
    dt = jnp.exp(ldt_ref[0])
    mag = jnp.exp(ar * dt)
    ab_re = mag * jnp.cos(ai * dt)
    ab_im = mag * jnp.sin(ai * dt)
    den = ar * ar + ai * ai
    n_re = ab_re - 1.0
    f_re = (n_re * ar + ab_im * ai) / den
    f_im = (ab_im * ar - n_re * ai) / den
    br = bre_ref[0]
    bi = bim_ref[0]
    bbre_o[0] = f_re * br - f_im * bi
    bbim_o[0] = f_re * bi + f_im * br
    abre_o[0] = ab_re
    abim_o[0] = ab_im
    cr, ci = ab_re, ab_im
    for t in range(SCAN_LEN):
        pwre_o[0, t] = cr
        pwim_o[0, t] = ci
        cr, ci = cr * ab_re - ci * ab_im, cr * ab_im + ci * ab_re


def _ssm_prep_call(a_re, a_im, log_dt, b_re, b_im):
    depth = a_re.shape[0]
    g, p, c = SSM_GROUPS, SSM_STATE, SSM_GROUP
    a4 = lambda a: a.reshape(depth, g, 1, p)
    spec_a = pl.BlockSpec((1, g, 1, p), lambda l: (l, 0, 0, 0))
    spec_b = pl.BlockSpec((1, g, c, p), lambda l: (l, 0, 0, 0))
    spec_pw = pl.BlockSpec((1, SCAN_LEN, g, 1, p), lambda l: (l, 0, 0, 0, 0))
    return pl.pallas_call(
        _ssm_prep_kernel,
        out_shape=(jax.ShapeDtypeStruct((depth, g, 1, p), F32),) * 2
        + (jax.ShapeDtypeStruct((depth, g, c, p), F32),) * 2
        + (jax.ShapeDtypeStruct((depth, SCAN_LEN, g, 1, p), F32),) * 2,
        grid=(depth,),
        in_specs=[spec_a, spec_a, pl.BlockSpec((1, g, 1, 1), lambda l: (l, 0, 0, 0)), spec_b, spec_b],
        out_specs=(spec_a, spec_a, spec_b, spec_b, spec_pw, spec_pw),
        compiler_params=_params(("arbitrary",)),
        name="ssm_prep",
    )(a4(a_re), a4(a_im), log_dt.reshape(depth, g, 1, 1),
      jnp.swapaxes(b_re, 2, 3), jnp.swapaxes(b_im, 2, 3))


def _ssm_out(y, u, d_skip, w_glu, b_glu):
    z = jax.nn.gelu(y + d_skip * u)
    return z * _sigmoid(_halves(_dot, z.astype(BF16), w_glu) + b_glu)


def _lane_lo(shape):
    return (lax.broadcasted_iota(jnp.int32, shape, len(shape) - 1) % LANES) < HEAD_DIM


N_MIXER_SEQ_IN = 4
N_MIXER_IN = 24
N_MIXER_OUT = 6


def _mixer_kernel(*refs):
    ins = refs[:N_MIXER_IN]
    outs = refs[N_MIXER_IN:N_MIXER_IN + N_MIXER_OUT]
    scratch = refs[N_MIXER_IN + N_MIXER_OUT:]
    bias_s, wg3_s = scratch[-2:]
    wg_ref = ins[MIXER_REF_NAMES.index('wg_ref')]
    t = pl.program_id(1)

    @pl.when((pl.program_id(0) == 0) & (t == 0))
    def _():
        for c in range(D_GATES // GATE_CHUNK):
            wg3_s[c] = wg_ref[:, c * GATE_CHUNK:(c + 1) * GATE_CHUNK]
        r_i = lax.broadcasted_iota(jnp.int32, (Q_BLK, 2 * Q_BLK), 0)
        c_i = lax.broadcasted_iota(jnp.int32, (Q_BLK, 2 * Q_BLK), 1)
        dist = r_i - c_i + Q_BLK
        valid = (dist >= 0) & (dist <= WINDOW)
        for hh in range(N_HEADS):
            bias_s[hh] = jnp.where(valid, (-SLOPES[hh] * LOG2E) * dist.astype(F32), NEG_INF)

    lvl_s, kbuf, vbuf, pbuf, hc_s = scratch[-7:-2]

    @pl.when(t == 0)
    def _():
        hc_s[...] = jnp.zeros_like(hc_s)
        kbuf[:, 0:WINDOW, :] = jnp.zeros((SEQ_PER_STEP, WINDOW, D_KV), BF16)
        vbuf[:, 0:WINDOW, :] = jnp.zeros((SEQ_PER_STEP, WINDOW, D_KV), BF16)
        pbuf[:, 0:POOL_PAD + POOL_HIST, :] = jnp.zeros((SEQ_PER_STEP, POOL_PAD + POOL_HIST, D_POOL), F32)
        lvl_s[:, :, 0:POOL_PAD, :] = jnp.zeros((SEQ_PER_STEP, 2, POOL_PAD, LANES), F32)

    chains = []
    for s in range(SEQ_PER_STEP):
        one = functools.partial(lambda s, r: r.at[s:s + 1], s)
        chains.append(_mixer_body(t, *[one(r) for r in ins[:N_MIXER_SEQ_IN]], *ins[N_MIXER_SEQ_IN:],
                                  *[one(r) for r in outs], *[r.at[s] for r in scratch[:-2]], bias_s, wg3_s))
    for _ in range(MIXER_LAG):
        next(chains[0])
    while chains:
        for chain in list(chains):
            if next(chain, StopIteration) is StopIteration:
                chains.remove(chain)


MIXER_REF_NAMES = (
    'x_ref sh_ref sc_ref g_ref npre_ref npost_ref wa_ref wg_ref '
    'wbu_ref abre_ref abim_ref pwre_ref pwim_ref cm_ref dsk_ref wglu_ref bglu_ref '
    'sinks_ref wpool_ref pscale_ref wus_ref wua_ref wup_ref wout_ref '
    'xo_ref hre_o him_o kwin_o vwin_o plast_o '
    'hb_s proj_s bu_s hst_s sg_s unat_s uperm_s operm_s yssm_s yatt_s lvl_s kbuf vbuf pbuf hc_s '
    'bias_s wg3_s').split()


def _mixer_body(t, *refs):
    r = types.SimpleNamespace(**dict(zip(MIXER_REF_NAMES, refs, strict=True)))
    x = r.x_ref[0]
    h = _rms(x, r.npre_ref[...]) * (1.0 + r.sc_ref[0]) + r.sh_ref[0]
    r.hb_s[...] = h.astype(BF16)
    yield
    for c in range(D_SMALL // PROJ_CHUNK):
        cols = slice(c * PROJ_CHUNK, (c + 1) * PROJ_CHUNK)
        r.proj_s[:, cols] = _dot(r.hb_s[...], r.wa_ref[:, cols])
        yield
    res = {}
    yield from _ssm_chain(r)
    yield from _attn_chain(t, r)
    yield from _pool_chain(t, r, res)
    yield from _merge_tail(r, res)


def _ssm_chain(r):
    (proj_s, hb_s, bu_s, hst_s, sg_s, unat_s, uperm_s, operm_s, yssm_s, hc_s, wg_ref, wbu_ref, abre_ref, abim_ref,
     pwre_ref, pwim_ref, cm_ref, dsk_ref, wglu_ref, bglu_ref, hre_o, him_o) = (
        r.proj_s, r.hb_s, r.bu_s, r.hst_s, r.sg_s, r.unat_s, r.uperm_s, r.operm_s, r.yssm_s, r.hc_s, r.wg3_s,
        r.wbu_ref, r.abre_ref, r.abim_ref, r.pwre_ref, r.pwim_ref, r.cm_ref, r.dsk_ref, r.wglu_ref, r.bglu_ref,
        r.hre_o, r.him_o)
    for c in range(D_SSM // LANES):
        unat_s[c] = proj_s[:, c * LANES:(c + 1) * LANES]
        for i in range(SCAN_LEN):
            uperm_s[i * SUBLANES:(i + 1) * SUBLANES, c * LANES:(c + 1) * LANES] = (
                unat_s.at[c][pl.ds(i, SUBLANES, stride=SCAN_LEN), :])
        yield
    for c in range(2 * D_STATE // BU_CHUNK):
        cols = slice(c * BU_CHUNK, (c + 1) * BU_CHUNK)
        bu_s[:, cols] = _dot(uperm_s[...].astype(BF16), wbu_ref[:, cols])
        yield

    abr = jnp.broadcast_to(abre_ref[...], (SUBLANES, D_STATE))
    abi = jnp.broadcast_to(abim_ref[...], (SUBLANES, D_STATE))

    def gate_chunk(c):
        sg_s[c] = _sigmoid(_dot(hb_s[...], wg_ref[c]))

    zero = jnp.zeros((SUBLANES, D_STATE), F32)
    steps_a = SCAN_LEN // GATE_LOOP_A

    er, ei = zero, zero
    for i in range(GATE_LOOP_A):
        gate_chunk(i)
        yield
        for s in range(steps_a):
            rows = slice((i * steps_a + s) * SUBLANES, (i * steps_a + s + 1) * SUBLANES)
            er, ei = (abr * er - abi * ei + bu_s[rows, 0:D_STATE],
                      abr * ei + abi * er + bu_s[rows, D_STATE:2 * D_STATE])
            bu_s[rows, 0:D_STATE] = er
            bu_s[rows, D_STATE:2 * D_STATE] = ei
        yield

    alr = pwre_ref[SCAN_LEN - 1:SCAN_LEN, :]
    ali = pwim_ref[SCAN_LEN - 1:SCAN_LEN, :]
    cr = hc_s[0:1, :]
    ci = hc_s[1:2, :]
    row = lax.broadcasted_iota(jnp.int32, (SUBLANES, D_STATE), 0)
    hin_r = zero
    hin_i = zero
    for j in range(SUBLANES):
        hin_r = jnp.where(row == j, cr, hin_r)
        hin_i = jnp.where(row == j, ci, hin_i)
        cr, ci = (alr * cr - ali * ci + er[j:j + 1, :], alr * ci + ali * cr + ei[j:j + 1, :])
    hc_s[0:1, :] = cr
    hc_s[1:2, :] = ci
    hre_o[0] = cr
    him_o[0] = ci
    yield

    pack = 2 * SUBLANES
    packs_b = TT // pack // GATE_LOOP_B

    for i in range(GATE_LOOP_B):
        gate_chunk(GATE_LOOP_A + i)
        yield
        for s in range(packs_b):
            k = i * packs_b + s
            fixed_r, fixed_i = [], []
            for slab in (2 * k, 2 * k + 1):
                rows = slice(slab * SUBLANES, (slab + 1) * SUBLANES)
                pr = pwre_ref[slab:slab + 1, :]
                pi = pwim_ref[slab:slab + 1, :]
                fixed_r.append(bu_s[rows, 0:D_STATE] + (pr * hin_r - pi * hin_i))
                fixed_i.append(bu_s[rows, D_STATE:2 * D_STATE] + (pr * hin_i + pi * hin_r))
            prow = slice(k * pack, (k + 1) * pack)
            hst_s[prow, 0:D_STATE] = jnp.concatenate(fixed_r, axis=0).astype(BF16)
            hst_s[prow, D_STATE:2 * D_STATE] = jnp.concatenate(fixed_i, axis=0).astype(BF16)
        yield

    y = _halves(_dot, hst_s[...], cm_ref[...])
    yield
    operm_s[...] = _ssm_out(y, uperm_s[...], dsk_ref[...], wglu_ref[...], bglu_ref[...])
    yield
    for c in range(D_SSM // LANES):
        for i in range(SCAN_LEN):
            yssm_s.at[c][pl.ds(i, SUBLANES, stride=SCAN_LEN), :] = (
                operm_s[i * SUBLANES:(i + 1) * SUBLANES, c * LANES:(c + 1) * LANES])
        yield

def _attn_chain(t, r):
    proj_s, kbuf, vbuf, yatt_s, bias_s, sinks_ref, kwin_o, vwin_o = (
        r.proj_s, r.kbuf, r.vbuf, r.yatt_s, r.bias_s, r.sinks_ref, r.kwin_o, r.vwin_o)
    kbuf[WINDOW:WINDOW + TT, :] = proj_s[:, D_SSM + D_ATTN:D_SSM + D_ATTN + D_KV].astype(BF16)
    vbuf[WINDOW:WINDOW + TT, :] = proj_s[:, D_SSM + D_ATTN + D_KV:D_SSM + D_ATTN + 2 * D_KV].astype(BF16)
    lo = _lane_lo((Q_BLK, LANES))
    r_i = lax.broadcasted_iota(jnp.int32, (Q_BLK, 2 * Q_BLK), 0)
    c_i = lax.broadcasted_iota(jnp.int32, (Q_BLK, 2 * Q_BLK), 1)
    first_key = jnp.where(t == 0, Q_BLK, 0)
    def scores(blk):
        r0 = blk * Q_BLK
        pieces = []
        for hh in range(N_HEADS):
            j = hh % GQA
            qp = proj_s[r0:r0 + Q_BLK, D_SSM + j * LANES:D_SSM + (j + 1) * LANES] * (ATTN_SCALE * LOG2E)
            keep = lo if hh < GQA else jnp.logical_not(lo)
            pieces.append(jnp.where(keep, qp, 0.0).astype(BF16))
        return [_dot_t(jnp.concatenate(pieces[g * GQA:(g + 1) * GQA], axis=0), kbuf[r0:r0 + 2 * Q_BLK, :])
                for g in range(N_KV_HEADS)]

    n_blk = TT // Q_BLK
    s_next = scores(0)
    yield
    for blk in range(n_blk):
        r0 = blk * Q_BLK
        s_all = s_next
        if blk + 1 < n_blk:
            s_next = scores(blk + 1)
            yield
        probs, dens = [], []
        for hh in range(N_HEADS):
            s = s_all[hh // GQA][(hh % GQA) * Q_BLK:(hh % GQA + 1) * Q_BLK, :] + bias_s[hh]
            if blk == 0:
                s = jnp.where(c_i >= first_key, s, NEG_INF)
            snk = sinks_ref[hh] * LOG2E
            m = jnp.maximum(jnp.max(s, axis=-1, keepdims=True), snk)
            p = jnp.exp2(s - m)
            dens.append(jnp.sum(p, axis=-1, keepdims=True) + jnp.exp2(snk - m))
            probs.append(p.astype(BF16))
            yield
        o_all = [_dot(jnp.concatenate(probs[g * GQA:(g + 1) * GQA], axis=0), vbuf[r0:r0 + 2 * Q_BLK, :])
                 for g in range(N_KV_HEADS)]
        out = lambda hh: o_all[hh // GQA][(hh % GQA) * Q_BLK:(hh % GQA + 1) * Q_BLK, :] / dens[hh]
        for j in range(GQA):
            yatt_s[r0:r0 + Q_BLK, j * LANES:(j + 1) * LANES] = jnp.where(lo, out(j), out(GQA + j)).astype(BF16)
        yield
    kwin_o[0] = proj_s[TT - WINDOW:TT, D_SSM + D_ATTN:D_SSM + D_ATTN + D_KV]
    vwin_o[0] = proj_s[TT - WINDOW:TT, D_SSM + D_ATTN + D_KV:D_SSM + D_ATTN + 2 * D_KV]
    kbuf[0:WINDOW, :] = kbuf[TT:TT + WINDOW, :]
    vbuf[0:WINDOW, :] = vbuf[TT:TT + WINDOW, :]
    yield

def _pool_chain(t, r, res):
    proj_s, pbuf, lvl_s, plast_o, wpool_ref, pscale_ref = (
        r.proj_s, r.pbuf, r.lvl_s, r.plast_o, r.wpool_ref, r.pscale_ref)
    p0 = D_SMALL - D_POOL
    cur = POOL_PAD + POOL_HIST
    end = cur + TT
    pbuf[cur:end, :] = proj_s[:, p0:D_SMALL]
    plast_o[0] = proj_s[TT - POOL_HIST:TT, p0:D_SMALL]
    pos1 = (t * TT + 1 + lax.broadcasted_iota(jnp.int32, (TT, LANES), 0)).astype(F32)
    lo_t = _lane_lo((TT, LANES))
    a_s, b_s = lvl_s.at[0], lvl_s.at[1]
    mixed = []
    for col, (w_lo, w_hi) in enumerate(((POOL_WINDOWS[0], POOL_WINDOWS[1]), (POOL_WINDOWS[2], POOL_WINDOWS[3]))):
        cs = slice(col * LANES, (col + 1) * LANES)
        pf = pbuf[cur:end, cs]
        a_s[POOL_PAD:end, :] = pbuf[POOL_PAD:end, cs] + pbuf[POOL_PAD - 1:end - 1, cs]
        have = 2
        while have < w_lo:
            b_s[POOL_PAD:end, :] = a_s[POOL_PAD:end, :] + a_s[POOL_PAD - have:end - have, :]
            a_s, b_s, have = b_s, a_s, 2 * have
        acc_lo = a_s[cur:end, :]
        while 2 * have < w_hi:
            b_s[POOL_PAD:end, :] = a_s[POOL_PAD:end, :] + a_s[POOL_PAD - have:end - have, :]
            a_s, b_s, have = b_s, a_s, 2 * have
        acc_hi = a_s[cur:end, :] + a_s[cur - have:end - have, :]
        cnt = jnp.where(lo_t, jnp.minimum(float(w_lo), pos1), jnp.minimum(float(w_hi), pos1))
        mixed.append(jnp.where(lo_t, acc_lo, acc_hi) / cnt - pf)
        yield
    ypool = _halves(_dot, jnp.concatenate(mixed, axis=1).astype(BF16), wpool_ref[...]) * pscale_ref[...]
    pbuf[POOL_PAD:cur, :] = pbuf[end - POOL_HIST:end, :]
    res['ypool'] = ypool
    yield


def _merge_tail(r, res):
    sg_s, yssm_s, yatt_s, wus_ref, wua_ref, wup_ref, wout_ref, x_ref, g_ref, npost_ref, xo_ref = (
        r.sg_s, r.yssm_s, r.yatt_s, r.wus_ref, r.wua_ref, r.wup_ref, r.wout_ref, r.x_ref, r.g_ref, r.npost_ref,
        r.xo_ref)
    ypool = res['ypool']
    per_branch = D_MODEL // GATE_CHUNK
    gate = lambda b: jnp.concatenate([sg_s[b * per_branch + c] for c in range(per_branch)], axis=1)
    yssm = jnp.concatenate([yssm_s[c] for c in range(D_SSM // LANES)], axis=1)
    merged = gate(0) * _dot(yssm.astype(BF16), wus_ref[...])
    yield
    merged += gate(1) * _dot(yatt_s[...], wua_ref[...])
    yield
    merged += gate(2) * _dot(ypool.astype(BF16), wup_ref[...])
    yield
    out = _dot(merged.astype(BF16), wout_ref[...])
    yield
    xo_ref[0] = x_ref[0] + g_ref[0] * _rms(out, npost_ref[...])


def _mixer_call(x, mods3, lp):
    b, t_len, d = x.shape
    nt = t_len // TT
    nsamp = mods3.shape[0] - b
    sps = SEQ_PER_STEP
    mod_spec = lambda col: pl.BlockSpec((sps, 1, D_MODEL), lambda i, j: (nsamp // sps + i, 0, col))
    x_spec = pl.BlockSpec((sps, TT, d), lambda i, j: (i, j, 0))
    consts = [lp['norm_mix_pre'], lp['norm_mix_post'], lp['w_in_a'], lp['w_in_g'],
              lp['w_bu'], lp['ab_re'], lp['ab_im'], lp['pw_re'], lp['pw_im'], lp['c_mat'],
              lp['ssm_d'], lp['ssm_w_glu'], lp['ssm_b_glu']]
    consts2 = [lp['pool_wbd'], lp['pool_scale'], lp['w_up_ssm'], lp['w_up_attn'], lp['w_up_pool'], lp['w_out']]
    in_specs = ([x_spec, mod_spec(0), mod_spec(1), mod_spec(2)]
                + [_const_spec(a.shape) for a in consts]
                + [pl.BlockSpec(memory_space=pltpu.SMEM)]
                + [_const_spec(a.shape) for a in consts2])
    per_b = lambda r, c: pl.BlockSpec((sps, r, c), lambda i, j: (i, 0, 0))
    out_shape = (jax.ShapeDtypeStruct((b, t_len, d), F32),
                 jax.ShapeDtypeStruct((b, 1, D_STATE), F32), jax.ShapeDtypeStruct((b, 1, D_STATE), F32),
                 jax.ShapeDtypeStruct((b, WINDOW, D_KV), F32), jax.ShapeDtypeStruct((b, WINDOW, D_KV), F32),
                 jax.ShapeDtypeStruct((b, POOL_HIST, D_POOL), F32))
    out_specs = (x_spec, per_b(1, D_STATE), per_b(1, D_STATE), per_b(WINDOW, D_KV), per_b(WINDOW, D_KV),
                 per_b(POOL_HIST, D_POOL))
    per_seq = lambda shape, dtype: pltpu.VMEM((sps,) + shape, dtype)
    scratch = [per_seq((TT, d), BF16), per_seq((TT, D_SMALL), F32), per_seq((TT, 2 * D_STATE), F32),
               per_seq((TT, 2 * D_STATE), BF16), per_seq((D_GATES // GATE_CHUNK, TT, GATE_CHUNK), F32),
               per_seq((D_SSM // LANES, TT, LANES), F32), per_seq((TT, D_SSM), F32),
               per_seq((TT, D_SSM), F32), per_seq((D_SSM // LANES, TT, LANES), F32),
               per_seq((TT, D_ATTN), BF16), per_seq((2, POOL_PAD + POOL_HIST + TT, LANES), F32),
               per_seq((WINDOW + TT, D_KV), BF16), per_seq((WINDOW + TT, D_KV), BF16),
               per_seq((POOL_PAD + POOL_HIST + TT, D_POOL), F32), per_seq((SUBLANES, D_STATE), F32),
               pltpu.VMEM((N_HEADS, Q_BLK, 2 * Q_BLK), F32),
               pltpu.VMEM((D_GATES // GATE_CHUNK, d, GATE_CHUNK), BF16)]
    return pl.pallas_call(
        _mixer_kernel,
        out_shape=out_shape,
        grid=(b // sps, nt),
        in_specs=in_specs,
        out_specs=out_specs,
        scratch_shapes=scratch,
        compiler_params=_params(("arbitrary", "arbitrary")),
        name="prompt_mixer",
    )(x, mods3, mods3, mods3, *consts, lp['attn_sinks'], *consts2)


def _mod_rows(ref, per_row):
    return ref[...] if per_row else ref[0]


def _ffn_kernel(per_row, x_ref, sh_ref, sc_ref, g_ref, npre_ref, npost_ref, wg_ref, wu_ref, wd_ref, o_ref):
    x = x_ref[...]
    h = (_rms(x, npre_ref[...]) * (1.0 + _mod_rows(sc_ref, per_row)) + _mod_rows(sh_ref, per_row)).astype(BF16)
    f = None
    for c0, c1 in FF_CHUNKS:
        act = _silu(_dot(h, wg_ref[:, c0:c1])) * _dot(h, wu_ref[:, c0:c1])
        part = _dot(act.astype(BF16), wd_ref[c0:c1, :])
        f = part if f is None else f + part
    o_ref[...] = x + _mod_rows(g_ref, per_row) * _rms(f, npost_ref[...])


def _mod_specs(mods3, mods2, n_rows, tm, rows_per_seq, per_row):
    nsamp = mods3.shape[0] - (n_rows // rows_per_seq if not per_row else 0)
    if per_row:
        specs = [pl.BlockSpec((tm, D_MODEL), functools.partial(lambda col, i, *_: (i, col), col)) for col in (3, 4, 5)]
        return specs, [mods2] * 3
    tiles_per_seq = rows_per_seq // tm
    specs = [pl.BlockSpec((1, 1, D_MODEL),
                          functools.partial(lambda col, i, *_: (nsamp + i // tiles_per_seq, 0, col), col))
             for col in (3, 4, 5)]
    return specs, [mods3] * 3


def _ffn_call(x2, mods3, mods2, lp, rows_per_seq, per_row):
    n = x2.shape[0]
    tm = n if per_row else FFN_TM
    mspecs, mops = _mod_specs(mods3, mods2, n, tm, rows_per_seq, per_row)
    consts = [lp['norm_ffn_pre'], lp['norm_ffn_post'], lp['ffn_w_gate'], lp['ffn_w_up'], lp['ffn_w_down']]
    x_spec = pl.BlockSpec((tm, D_MODEL), lambda i: (i, 0))
    return pl.pallas_call(
        functools.partial(_ffn_kernel, per_row),
        out_shape=jax.ShapeDtypeStruct(x2.shape, F32),
        grid=(n // tm,),
        in_specs=[x_spec] + mspecs + [_const_spec(a.shape) for a in consts],
        out_specs=x_spec,
        compiler_params=_params(("arbitrary",)),
        name="ffn_rows" if per_row else "ffn_seq",
    )(x2, *mops, *consts)


def _moe_kernel(per_row, x_ref, sh_ref, sc_ref, g_ref, npre_ref, npost_ref, wr_ref, br_ref,
                wg_ref, wu_ref, wd_ref, o_ref, h_s, gates_s, acc_s):
    e = pl.program_id(1)
    lane_e = lax.broadcasted_iota(jnp.int32, gates_s.shape, 1)

    @pl.when(e == 0)
    def _():
        x = x_ref[...]
        h = _rms(x, npre_ref[...]) * (1.0 + _mod_rows(sc_ref, per_row)) + _mod_rows(sh_ref, per_row)
        hb = h.astype(BF16)
        h_s[...] = hb
        logits = _dot(hb, wr_ref[...]) + br_ref[...]
        gates_s[...], _ = _top2_gates(logits, lane_e.astype(F32))
        acc_s[...] = jnp.zeros_like(acc_s)

    h = h_s[...]
    act = _silu(_dot(h, wg_ref[0, 0])) * _dot(h, wu_ref[0, 0])
    y = _dot(act.astype(BF16), wd_ref[0, 0])
    ge = jnp.sum(jnp.where(lane_e == e, gates_s[...], 0.0), axis=-1, keepdims=True)
    acc_s[...] += ge * y

    @pl.when(e == N_EXPERTS - 1)
    def _():
        o_ref[...] = x_ref[...] + _mod_rows(g_ref, per_row) * _rms(acc_s[...], npost_ref[...])


def _moe_call(x2, mods3, mods2, lp, rows_per_seq, per_row):
    n = x2.shape[0]
    tm = n if per_row else MOE_TM
    mspecs, mops = _mod_specs(mods3, mods2, n, tm, rows_per_seq, per_row)
    consts = [lp['norm_ffn_pre'], lp['norm_ffn_post'], lp['moe_w_router'], lp['moe_b_router']]
    x_spec = pl.BlockSpec((tm, D_MODEL), lambda i, e: (i, 0))
    w_spec = lambda a: pl.BlockSpec((1, 1) + a.shape[2:], lambda i, e: (0, e, 0, 0))
    experts = [lp['moe_w_gate'], lp['moe_w_up'], lp['moe_w_down']]
    return pl.pallas_call(
        functools.partial(_moe_kernel, per_row),
        out_shape=jax.ShapeDtypeStruct(x2.shape, F32),
        grid=(n // tm, N_EXPERTS),
        in_specs=[x_spec] + mspecs + [_const_spec(a.shape) for a in consts] + [w_spec(a) for a in experts],
        out_specs=x_spec,
        scratch_shapes=[pltpu.VMEM((tm, D_MODEL), BF16), pltpu.VMEM((tm, N_EXPERTS), F32),
                        pltpu.VMEM((tm, D_MODEL), F32)],
        compiler_params=_params(("arbitrary", "arbitrary")),
        name="moe_rows" if per_row else "moe_seq",
    )(x2, *mops, *consts, *experts)


def _top2_gates(logits, lane_f):
    n_lanes = float(logits.shape[-1])
    v1 = jnp.max(logits, axis=-1, keepdims=True)
    i1 = jnp.min(jnp.where(logits == v1, lane_f, n_lanes), axis=-1, keepdims=True)
    rest = jnp.where(lane_f == i1, -jnp.inf, logits)
    v2 = jnp.max(rest, axis=-1, keepdims=True)
    i2 = jnp.min(jnp.where(rest == v2, lane_f, n_lanes), axis=-1, keepdims=True)
    e2 = jnp.exp(v2 - v1)
    den = 1.0 + e2
    gates = jnp.where(lane_f == i1, 1.0 / den, 0.0) + jnp.where(lane_f == i2, e2 / den, 0.0)
    return gates, (lane_f == i1) | (lane_f == i2)


def _moe_routed_kernel(seq_len, n_rows, x_ref, sh0_ref, sh1_ref, sc0_ref, sc1_ref, g0_ref, g1_ref,
                       npre_ref, npost_ref, wr_ref, br_ref, wg_ref, wu_ref, wd_ref, o_ref,
                       h_s, before_s, dest_s, gate_s, acc_s, cnt_s):
    i = pl.program_id(0)
    e = pl.program_id(1)
    tm = x_ref.shape[0]
    row = lax.broadcasted_iota(jnp.int32, (tm, 1), 0) + i * tm
    if n_rows % tm == 0:
        keep = lambda v: v
    else:
        valid = row < n_rows
        keep = lambda v: jnp.where(valid, v, 0.0)
    if seq_len % tm == 0:
        pick = lambda a_ref, b_ref: a_ref[0]
    else:
        second = row >= ((i * tm) // seq_len + 1) * seq_len
        pick = lambda a_ref, b_ref: jnp.where(second, b_ref[0], a_ref[0])

    @pl.when((i == 0) & (e == 0))
    def _():
        r_i = lax.broadcasted_iota(jnp.int32, (tm, tm), 0)
        c_i = lax.broadcasted_iota(jnp.int32, (tm, tm), 1)
        before_s[...] = jnp.where(c_i < r_i, 1.0, 0.0).astype(BF16)

    @pl.when(e == 0)
    def _():
        x = keep(x_ref[...])
        h = _rms(x, npre_ref[...]) * (1.0 + pick(sc0_ref, sc1_ref)) + pick(sh0_ref, sh1_ref)
        hb = h.astype(BF16)
        h_s[...] = hb
        lane_f = lax.broadcasted_iota(jnp.int32, (tm, LANES), 1).astype(F32)
        logits = jnp.where(lane_f < N_EXPERTS, _halves(_dot, hb, wr_ref[...]) + br_ref[...], -jnp.inf)
        gates, sel = _top2_gates(logits, lane_f)
        self = keep(jnp.where(sel, 1.0, 0.0))
        dest = jnp.where(self > 0.0, _halves(_dot, before_s[...], self.astype(BF16)), -1.0)
        dest_s[...] = dest.T[0:SUBLANES, :]
        gate_s[...] = gates.T[0:SUBLANES, :]
        counts = jnp.sum(self, axis=0, keepdims=True)
        for j in range(N_EXPERTS):
            cnt_s[j] = counts[0, j].astype(jnp.int32)
        acc_s[...] = jnp.zeros_like(acc_s)

    dest_row = dest_s[pl.ds(e, 1), :]
    gate_row = gate_s[pl.ds(e, 1), :]
    def chunk(base, size):
        slot = lax.broadcasted_iota(jnp.int32, (size, tm), 0).astype(F32) + base.astype(F32)
        onehot = jnp.where(slot == dest_row, 1.0, 0.0)
        gate_packed = jnp.sum(onehot * gate_row, axis=-1, keepdims=True)
        pack = onehot.astype(BF16)
        xg = _dot(pack, h_s[...]).astype(BF16)
        act = _silu(_dot(xg, wg_ref[0, 0])) * _dot(xg, wu_ref[0, 0])
        y = _dot(act.astype(BF16), wd_ref[0, 0]) * gate_packed
        acc_s[...] += lax.dot_general(pack, y.astype(BF16), (((0,), (0,)), ((), ())), preferred_element_type=F32)

    shift = MOE_CAP.bit_length() - 1
    cnt = cnt_s[e]
    n_full = lax.shift_right_logical(cnt, shift)
    rem = cnt - lax.shift_left(n_full, shift)
    long_last = (n_full >= 1) & (rem >= 1) & (rem <= MOE_TAIL)
    n_plain = jnp.where(long_last, n_full - 1, n_full + jnp.where(rem >= 1, 1, 0))

    def plain_chunk(c, _):
        chunk(c * MOE_CAP, MOE_CAP)
        return 0

    lax.fori_loop(0, n_plain, plain_chunk, 0)

    @pl.when(long_last)
    def _():
        chunk(n_plain * MOE_CAP, MOE_CAP + MOE_TAIL)

    @pl.when(e == N_EXPERTS - 1)
    def _():
        o_ref[...] = keep(x_ref[...]) + pick(g0_ref, g1_ref) * _rms(acc_s[...], npost_ref[...])


def _moe_routed_call(x2, mods3, lp, seq_len):
    n = x2.shape[0]
    tm = MOE_SP_TM
    n_seq = n // seq_len
    nsamp = mods3.shape[0] - n_seq

    def mod_spec(col, nxt):
        return pl.BlockSpec((1, 1, D_MODEL),
                            lambda i, e: (nsamp + jnp.minimum((i * tm) // seq_len + nxt, n_seq - 1), 0, col))

    mspecs = [mod_spec(col, nxt) for col in (3, 4, 5) for nxt in (0, 1)]
    pad = LANES - N_EXPERTS
    consts = [lp['norm_ffn_pre'], lp['norm_ffn_post'],
              jnp.pad(lp['moe_w_router'], ((0, 0), (0, pad))), jnp.pad(lp['moe_b_router'], ((0, 0), (0, pad)))]
    x_spec = pl.BlockSpec((tm, D_MODEL), lambda i, e: (i, 0))
    w_spec = lambda a: pl.BlockSpec((1, 1) + a.shape[2:], lambda i, e: (0, e, 0, 0))
    experts = [lp['moe_w_gate'], lp['moe_w_up'], lp['moe_w_down']]
    return pl.pallas_call(
        functools.partial(_moe_routed_kernel, seq_len, n),
        out_shape=jax.ShapeDtypeStruct(x2.shape, F32),
        grid=(pl.cdiv(n, tm), N_EXPERTS),
        in_specs=[x_spec] + mspecs + [_const_spec(a.shape) for a in consts] + [w_spec(a) for a in experts],
        out_specs=x_spec,
        scratch_shapes=[pltpu.VMEM((tm, D_MODEL), BF16), pltpu.VMEM((tm, tm), BF16),
                        pltpu.VMEM((SUBLANES, tm), F32), pltpu.VMEM((SUBLANES, tm), F32),
                        pltpu.VMEM((tm, D_MODEL), F32), pltpu.SMEM((N_EXPERTS,), jnp.int32)],
        compiler_params=_params(("arbitrary", "arbitrary")),
        name="moe_routed",
    )(x2, *([mods3] * 6), *consts, *experts)


def _sproj_kernel(x_ref, sh_ref, sc_ref, npre_ref, wa_ref, wg_ref, pa_o, pg_o):
    h = (_rms(x_ref[...], npre_ref[...]) * (1.0 + sc_ref[...]) + sh_ref[...]).astype(BF16)
    pa_o[...] = _dot(h, wa_ref[...])
    pg_o[...] = _dot(h, wg_ref[...])


def _sproj_call(xs, mods2, lp):
    n = xs.shape[0]
    mod_spec = lambda col: pl.BlockSpec((n, D_MODEL), lambda i: (0, col))
    consts = [lp['norm_mix_pre'], lp['w_in_a'], lp['w_in_g']]
    return pl.pallas_call(
        _sproj_kernel,
        out_shape=(jax.ShapeDtypeStruct((n, D_SMALL), F32), jax.ShapeDtypeStruct((n, D_GATES), F32)),
        grid=(1,),
        in_specs=[_const_spec(xs.shape), mod_spec(0), mod_spec(1)] + [_const_spec(a.shape) for a in consts],
        out_specs=(_const_spec((n, D_SMALL)), _const_spec((n, D_GATES))),
        compiler_params=_params(("arbitrary",)),
        name="sample_proj",
    )(xs, mods2, mods2, *consts)


N_SMIX_IN = 22


def _smix_kernel(*refs):
    (u_ref, q_ref, kn_ref, vn_ref, knt_ref, vnt_ref, pin_ref, h0r_ref, h0i_ref, ck_ref, cv_ref, hist_ref,
     wbu_ref, abre_ref, abim_ref, cm_ref, dsk_ref, wglu_ref, bglu_ref, sinks_ref,
     wpool_ref, pscale_ref) = refs[:N_SMIX_IN]
    yssm_o, yatt_o, ypool_o, hr_o, hi_o, ck_o, cv_o, hist_o = refs[-8:]
    u = u_ref[...]
    bu = _dot(u.astype(BF16), wbu_ref[...])
    abr, abi = abre_ref[...], abim_ref[...]
    h0r, h0i = h0r_ref[...], h0i_ref[...]
    hr = bu[:, 0:D_STATE] + (abr * h0r - abi * h0i)
    hi = bu[:, D_STATE:2 * D_STATE] + (abr * h0i + abi * h0r)
    hr_o[...] = hr
    hi_o[...] = hi
    y = _dot(jnp.concatenate([hr, hi], axis=1).astype(BF16), cm_ref[...])
    yssm_o[...] = _ssm_out(y, u, dsk_ref[...], wglu_ref[...], bglu_ref[...])

    q = q_ref[...] * ATTN_SCALE
    kn, vn = kn_ref[...], vn_ref[...]
    kc = ck_ref[...].astype(BF16)
    vc = cv_ref[...].astype(BF16)
    lo = _lane_lo(q.shape)
    j_i = lax.broadcasted_iota(jnp.int32, (1, GQA, 1), 1)
    dist_c = (WINDOW - lax.broadcasted_iota(jnp.int32, (1, 1, WINDOW), 2)).astype(F32)
    halves = []
    for half in range(N_KV_HEADS):
        keep = lo if half == 0 else jnp.logical_not(lo)
        qm = jnp.where(keep, q, 0.0)
        slope = jnp.zeros((1, GQA, 1), F32)
        snk = jnp.zeros((1, GQA, 1), F32)
        for j in range(GQA):
            slope = jnp.where(j_i == j, SLOPES[half * GQA + j], slope)
            snk = jnp.where(j_i == j, sinks_ref[half * GQA + j], snk)
        s = jnp.einsum('bhd,bds->bhs', qm.astype(BF16), kc, preferred_element_type=F32) - slope * dist_c
        s_new = jnp.sum(qm * kn, axis=-1, keepdims=True)
        m = jnp.maximum(jnp.maximum(jnp.max(s, axis=-1, keepdims=True), s_new), snk)
        p = jnp.exp(s - m)
        p_new = jnp.exp(s_new - m)
        den = jnp.sum(p, axis=-1, keepdims=True) + p_new + jnp.exp(snk - m)
        o = jnp.einsum('bhs,bds->bhd', p.astype(BF16), vc, preferred_element_type=F32) + p_new * vn
        halves.append(o / den)
    yatt_o[...] = jnp.where(lo, halves[0], halves[1])

    bc = ck_ref.shape[0]
    lane_w = lax.broadcasted_iota(jnp.int32, (D_KV, WINDOW), 1)
    lane_b = lax.broadcasted_iota(jnp.int32, knt_ref.shape, 1)
    for src, new_t, dst in ((ck_ref, knt_ref, ck_o), (cv_ref, vnt_ref, cv_o)):
        for b in range(bc):
            col = jnp.sum(jnp.where(lane_b == pl.program_id(0) * bc + b, new_t[...], 0.0), axis=-1, keepdims=True)
            dst[b] = jnp.where(lane_w == WINDOW - 1, col, pltpu.roll(src[b], WINDOW - 1, axis=1))

    pin = pin_ref[...]
    lo2 = _lane_lo((pin.shape[0], LANES))
    mixed = []
    for col, (w_lo, w_hi) in enumerate(((POOL_WINDOWS[0], POOL_WINDOWS[1]), (POOL_WINDOWS[2], POOL_WINDOWS[3]))):
        cs = slice(col * LANES, (col + 1) * LANES)
        pf = pin[:, cs]
        acc = pf
        for k in range(1, w_lo):
            acc = acc + hist_ref[POOL_BUF - k, :, cs]
        acc_lo = acc
        for k in range(w_lo, w_hi):
            acc = acc + hist_ref[POOL_BUF - k, :, cs]
        cnt_lo = float(min(w_lo, PAST_LEN + 1))
        cnt_hi = float(min(w_hi, PAST_LEN + 1))
        mixed.append(jnp.where(lo2, acc_lo / cnt_lo, acc / cnt_hi) - pf)
    ypool_o[...] = _dot(jnp.concatenate(mixed, axis=1).astype(BF16), wpool_ref[...]) * pscale_ref[...]
    hist_o[0:POOL_BUF - 1] = hist_ref[1:POOL_BUF]
    hist_o[POOL_BUF - 1] = pin


def _smix_call(layer, u, q3, kn3, vn3, knt, vnt, pin, h0r, h0i, ck, cv, prev_windows, hist_t, lp):
    n = u.shape[0]
    bc = SAMPLE_BC
    rows2 = lambda c: pl.BlockSpec((bc, c), lambda i: (i, 0))
    rows3 = lambda r, c: pl.BlockSpec((bc, r, c), lambda i: (i, 0, 0))
    win_spec = pl.BlockSpec((None, bc, D_KV, WINDOW), lambda i: (layer, i, 0, 0))
    consts = [lp['w_bu'], lp['ab_re'], lp['ab_im'], lp['c_mat'], lp['ssm_d'], lp['ssm_w_glu'], lp['ssm_b_glu']]
    consts2 = [lp['pool_wbd'], lp['pool_scale']]
    hist_spec = pl.BlockSpec((POOL_BUF, bc, D_POOL), lambda i: (0, i, 0))
    in_specs = ([rows2(D_SSM), rows3(GQA, LANES), rows3(1, LANES), rows3(1, LANES),
                 _const_spec(knt.shape), _const_spec(vnt.shape), rows2(D_POOL),
                 rows2(D_STATE), rows2(D_STATE), win_spec, win_spec, hist_spec]
                + [_const_spec(a.shape) for a in consts]
                + [pl.BlockSpec(memory_space=pltpu.SMEM)]
                + [_const_spec(a.shape) for a in consts2]
                + [pl.BlockSpec(memory_space=pl.ANY) for _ in prev_windows])
    assert len(in_specs) == N_SMIX_IN + len(prev_windows)
    aliases = {N_SMIX_IN + k: 5 + k for k in range(len(prev_windows))}
    return pl.pallas_call(
        _smix_kernel,
        out_shape=(jax.ShapeDtypeStruct((n, D_SSM), F32), jax.ShapeDtypeStruct((n, GQA, LANES), F32),
                   jax.ShapeDtypeStruct((n, D_POOL), F32),
                   jax.ShapeDtypeStruct((n, D_STATE), F32), jax.ShapeDtypeStruct((n, D_STATE), F32),
                   jax.ShapeDtypeStruct(ck.shape, F32), jax.ShapeDtypeStruct(cv.shape, F32),
                   jax.ShapeDtypeStruct(hist_t.shape, F32)),
        grid=(n // bc,),
        in_specs=in_specs,
        out_specs=(rows2(D_SSM), rows3(GQA, LANES), rows2(D_POOL), rows2(D_STATE), rows2(D_STATE),
                   win_spec, win_spec, hist_spec),
        input_output_aliases=aliases,
        compiler_params=_params(("arbitrary",)),
        name="sample_mix",
    )(u, q3, kn3, vn3, knt, vnt, pin, h0r, h0i, ck, cv, hist_t, *consts, lp['attn_sinks'], *consts2, *prev_windows)


def _smerge_kernel(x_ref, g_ref, pg_ref, yssm_ref, yatt_ref, ypool_ref, npost_ref,
                   wus_ref, wua_ref, wup_ref, wout_ref, o_ref):
    merged = _sigmoid(pg_ref[:, 0:D_MODEL]) * _dot(yssm_ref[...].astype(BF16), wus_ref[...])
    merged += _sigmoid(pg_ref[:, D_MODEL:2 * D_MODEL]) * _dot(yatt_ref[...].astype(BF16), wua_ref[...])
    merged += _sigmoid(pg_ref[:, 2 * D_MODEL:3 * D_MODEL]) * _dot(ypool_ref[...].astype(BF16), wup_ref[...])
    out = _dot(merged.astype(BF16), wout_ref[...])
    o_ref[...] = x_ref[...] + g_ref[...] * _rms(out, npost_ref[...])


def _smerge_call(xs, mods2, pg, yssm, yatt, ypool, lp):
    n = xs.shape[0]
    ops = [xs, mods2, pg, yssm, yatt, ypool, lp['norm_mix_post'],
           lp['w_up_ssm'], lp['w_up_attn'], lp['w_up_pool'], lp['w_out']]
    in_specs = [_const_spec(a.shape) for a in ops]
    in_specs[1] = pl.BlockSpec((n, D_MODEL), lambda i: (0, 2))
    return pl.pallas_call(
        _smerge_kernel,
        out_shape=jax.ShapeDtypeStruct(xs.shape, F32),
        grid=(1,),
        in_specs=in_specs,
        out_specs=_const_spec(xs.shape),
        compiler_params=_params(("arbitrary",)),
        name="sample_merge",
    )(*ops)


def _block_diag(w):
    g, a, b = w.shape
    eye = jnp.eye(g, dtype=w.dtype)
    return jnp.einsum('gab,gh->gahb', w, eye).reshape(g * a, g * b)


def _pair_heads(w, axis):
    shape = w.shape
    w = w.reshape(shape[:axis] + (N_KV_HEADS, GQA, HEAD_DIM) + shape[axis + 1:])
    return jnp.swapaxes(w, axis, axis + 1).reshape(shape)


def _layer_params(l, p, prep):
    ab_re, ab_im, bb_re, bb_im, pw_re, pw_im = prep
    row = lambda a: a.reshape(1, -1)
    w_in = p['w_in'][l]
    q_cols = _pair_heads(w_in[:, D_SSM:D_SSM + D_ATTN], 1)
    w_in_a = jnp.concatenate([w_in[:, 0:D_SSM], q_cols, w_in[:, D_SSM + D_ATTN:D_SMALL]], axis=1)
    lp = {
        'norm_mix_pre': row(p['norm_mix_pre'][l]), 'norm_mix_post': row(p['norm_mix_post'][l]),
        'norm_ffn_pre': row(p['norm_ffn_pre'][l]), 'norm_ffn_post': row(p['norm_ffn_post'][l]),
        'w_in_a': w_in_a.astype(BF16),
        'w_in_g': w_in[:, D_SMALL:].astype(BF16),
        'w_bu': jnp.concatenate([_block_diag(bb_re[l]), _block_diag(bb_im[l])], axis=1).astype(BF16),
        'ab_re': ab_re[l].reshape(1, D_STATE), 'ab_im': ab_im[l].reshape(1, D_STATE),
        'pw_re': pw_re[l].reshape(SCAN_LEN, D_STATE), 'pw_im': pw_im[l].reshape(SCAN_LEN, D_STATE),
        'c_mat': jnp.concatenate([_block_diag(jnp.swapaxes(p['ssm_c_re'][l], 1, 2)),
                                  _block_diag(jnp.swapaxes(-p['ssm_c_im'][l], 1, 2))], axis=0).astype(BF16),
        'ssm_d': row(p['ssm_d'][l]), 'ssm_w_glu': p['ssm_w_glu'][l].astype(BF16), 'ssm_b_glu': row(p['ssm_b_glu'][l]),
        'attn_sinks': p['attn_sinks'][l],
        'pool_wbd': _block_diag(p['pool_w'][l]).astype(BF16), 'pool_scale': row(p['pool_scale'][l]),
        'w_up_ssm': p['w_up_ssm'][l].astype(BF16), 'w_up_attn': _pair_heads(p['w_up_attn'][l], 0).astype(BF16),
        'w_up_pool': p['w_up_pool'][l].astype(BF16), 'w_out': p['w_out'][l].astype(BF16),
    }
    j = l // 2
    if l % 2 == 1:
        lp.update({'moe_w_router': p['moe_w_router'][j].astype(BF16), 'moe_b_router': row(p['moe_b_router'][j]),
                   'moe_w_gate': p['moe_w_gate'][j:j + 1].astype(BF16), 'moe_w_up': p['moe_w_up'][j:j + 1].astype(BF16),
                   'moe_w_down': p['moe_w_down'][j:j + 1].astype(BF16)})
    else:
        lp.update({'ffn_w_gate': p['ffn_w_gate'][j].astype(BF16), 'ffn_w_up': p['ffn_w_up'][j].astype(BF16),
                   'ffn_w_down': p['ffn_w_down'][j].astype(BF16)})
    return lp


def kernel(x_prompt, x_sample, state_ssm_re, state_ssm_im, cache_win_k, cache_win_v, state_pool,
           c_prompt, c_sample, w_ada, b_ada, norm_mix_pre, norm_mix_post, norm_ffn_pre, norm_ffn_post,
           w_in, ssm_a_re, ssm_a_im, ssm_log_dt, ssm_b_re, ssm_b_im, ssm_c_re, ssm_c_im, ssm_d,
           ssm_w_glu, ssm_b_glu, attn_sinks, pool_w, pool_scale, w_up_ssm, w_up_attn, w_up_pool, w_out,
           ffn_w_gate, ffn_w_up, ffn_w_down, moe_w_router, moe_b_router, moe_w_gate, moe_w_up, moe_w_down):
    p = dict(norm_mix_pre=norm_mix_pre, norm_mix_post=norm_mix_post, norm_ffn_pre=norm_ffn_pre,
             norm_ffn_post=norm_ffn_post, w_in=w_in, ssm_c_re=ssm_c_re, ssm_c_im=ssm_c_im, ssm_d=ssm_d,
             ssm_w_glu=ssm_w_glu, ssm_b_glu=ssm_b_glu, attn_sinks=attn_sinks, pool_w=pool_w, pool_scale=pool_scale,
             w_up_ssm=w_up_ssm, w_up_attn=w_up_attn, w_up_pool=w_up_pool, w_out=w_out,
             ffn_w_gate=ffn_w_gate, ffn_w_up=ffn_w_up, ffn_w_down=ffn_w_down, moe_w_router=moe_w_router,
             moe_b_router=moe_b_router, moe_w_gate=moe_w_gate, moe_w_up=moe_w_up, moe_w_down=moe_w_down)
    depth = w_in.shape[0]
    b, t_len, d = x_prompt.shape
    ns = x_sample.shape[0]

    mods, mods_seq = _ada_call(jnp.concatenate([c_sample, c_prompt], axis=0), b, w_ada, b_ada)
    prep = _ssm_prep_call(ssm_a_re, ssm_a_im, ssm_log_dt, ssm_b_re, ssm_b_im)

    xp = x_prompt
    xs = x_sample.reshape(ns, d)
    p_out = ([], [], [], [], [])
    s_out = ([], [], [], [], [])
    win_minor = lambda c: jnp.transpose(c, (0, 1, 3, 4, 2)).reshape(depth, ns, D_KV, -1)
    win_major = lambda c: jnp.transpose(c.reshape(depth, ns, N_KV_HEADS, HEAD_DIM, -1), (0, 1, 4, 2, 3))
    old_windows = [win_minor(cache_win_k), win_minor(cache_win_v)]
    new_windows = []
    for l in range(depth):
        lp = _layer_params(l, p, prep)
        mods2 = mods[l]
        mods3 = mods_seq[l]

        xp, hre, him, kwin, vwin, plast = _mixer_call(xp, mods3, lp)
        x2 = xp.reshape(b * t_len, d)
        if l % 2 == 1:
            x2 = _moe_routed_call(x2, mods3, lp, t_len)
        else:
            x2 = _ffn_call(x2, mods3, mods2, lp, t_len, False)
        xp = x2.reshape(b, t_len, d)
        p_out[0].append(hre.reshape(b, SSM_GROUPS, SSM_STATE))
        p_out[1].append(him.reshape(b, SSM_GROUPS, SSM_STATE))
        p_out[2].append(kwin.reshape(b, WINDOW, N_KV_HEADS, HEAD_DIM))
        p_out[3].append(vwin.reshape(b, WINDOW, N_KV_HEADS, HEAD_DIM))
        p_out[4].append(plast[:, POOL_HIST - POOL_BUF:, :])

        pa, pg = _sproj_call(xs, mods2, lp)
        u = pa[:, 0:D_SSM]
        q3 = pa[:, D_SSM:D_SSM + D_ATTN].reshape(ns, GQA, LANES)
        k_new = pa[:, D_SSM + D_ATTN:D_SSM + D_ATTN + D_KV]
        v_new = pa[:, D_SSM + D_ATTN + D_KV:D_SSM + D_ATTN + 2 * D_KV]
        pin = pa[:, D_SMALL - D_POOL:]
        yssm, yatt3, ypool, hr, hi, *new_windows, hist_new = _smix_call(
            l, u, q3, k_new.reshape(ns, 1, D_KV), v_new.reshape(ns, 1, D_KV), k_new.T, v_new.T, pin,
            state_ssm_re[l].reshape(ns, D_STATE), state_ssm_im[l].reshape(ns, D_STATE),
            *old_windows, new_windows, jnp.swapaxes(state_pool[l], 0, 1), lp)
        xs = _smerge_call(xs, mods2, pg, yssm, yatt3.reshape(ns, D_ATTN), ypool, lp)
        xs = (_moe_call if l % 2 == 1 else _ffn_call)(xs, mods3, mods2, lp, 1, True)
        s_out[0].append(hr.reshape(ns, SSM_GROUPS, SSM_STATE))
        s_out[1].append(hi.reshape(ns, SSM_GROUPS, SSM_STATE))
        s_out[4].append(jnp.swapaxes(hist_new, 0, 1))

    stack = lambda xs_: [jnp.stack(a, axis=0) for a in xs_]
    p_ssm_re, p_ssm_im, p_win_k, p_win_v, p_pool = stack(p_out)
    s_ssm_re, s_ssm_im, s_pool = stack([s_out[0], s_out[1], s_out[4]])
    s_win_k, s_win_v = [win_major(w) for w in new_windows]
    return (xp, xs.reshape(x_sample.shape), p_ssm_re, p_ssm_im, p_win_k, p_win_v, p_pool,
            s_ssm_re, s_ssm_im, s_win_k, s_win_v, s_pool)
```

```python
import functools
import math
import types

import jax
import jax.numpy as jnp
from jax import lax
from jax.experimental import pallas as pl
from jax.experimental.pallas import tpu as pltpu

F32 = jnp.float32
BF16 = jnp.bfloat16

D_MODEL = 1024
D_SSM = 256
SSM_GROUP = 16
SSM_GROUPS = 16
SSM_STATE = 64
D_STATE = SSM_GROUPS * SSM_STATE
HEAD_DIM = 64
N_HEADS = 8
N_KV_HEADS = 2
GQA = N_HEADS // N_KV_HEADS
D_ATTN = N_HEADS * HEAD_DIM
D_KV = N_KV_HEADS * HEAD_DIM
WINDOW = 128
ATTN_SCALE = HEAD_DIM ** -0.5
NEG_INF = -1e30
LOG2E = math.log2(math.e)
D_POOL = 256
POOL_WINDOWS = (2, 4, 8, 16)
POOL_BUF = 15
POOL_HIST = 16
POOL_PAD = 8
N_MOD = 6
D_FF = 2816
N_EXPERTS = 8
D_EXPERT = 1024
RMS_EPS = 1e-6
PAST_LEN = 16384
D_SMALL = D_SSM + D_ATTN + 2 * D_KV + D_POOL
D_GATES = 3 * D_MODEL
SLOPES = tuple(2.0 ** (-8.0 * (h + 1) / N_HEADS) for h in range(N_HEADS))

SUBLANES = 8
LANES = 128
VMEM_LIMIT = 56 * 1024 * 1024

TT = 512
SEQ_PER_STEP = 1
SCAN_LEN = TT // SUBLANES
Q_BLK = 128
GATE_CHUNK = 512
PROJ_CHUNK = 256
BU_CHUNK = 512
MIXER_LAG = 0
GATE_LOOP_A = 4
GATE_LOOP_B = D_GATES // GATE_CHUNK - GATE_LOOP_A
FFN_TM = 512
MOE_TM = 1024
MOE_SP_TM = 1024
MOE_CAP = 256
MOE_TAIL = 128
MOE_RANK_BLOCK = 256
SAMPLE_BC = 32
ADA_TN = 1536
FF_CHUNKS = ((0, 1024), (1024, 2048), (2048, D_FF))


def _rms(x, gain):
    return x * lax.rsqrt(jnp.mean(x * x, axis=-1, keepdims=True) + RMS_EPS) * gain


def _sigmoid(x):
    return 1.0 / (1.0 + jnp.exp(-x))


def _silu(x):
    return x * _sigmoid(x)


def _dot(a, b):
    return jnp.dot(a, b, preferred_element_type=F32)


def _dot_t(a, b):
    return lax.dot_general(a, b, (((1,), (1,)), ((), ())), preferred_element_type=F32)


def _halves(dot, a, b):
    half = a.shape[0] // 2
    return jnp.concatenate([dot(a[:half], b), dot(a[half:], b)], axis=0)


def _const_spec(shape):
    nd = len(shape)
    return pl.BlockSpec(shape, lambda *_: (0,) * nd)


def _params(sem):
    return pltpu.CompilerParams(dimension_semantics=sem, vmem_limit_bytes=VMEM_LIMIT)


def _ada_kernel(c_ref, w_ref, b_ref, o_ref, oseq_ref):
    c = c_ref[...]
    s = _silu(c).astype(BF16)
    mod = _dot(s, w_ref[0].astype(BF16)) + b_ref[0]
    o_ref[0] = mod
    n_seq = oseq_ref.shape[1]
    first = mod.shape[0] - n_seq
    for s_i in range(n_seq):
        oseq_ref[0, s_i] = mod[first + s_i:first + s_i + 1, :]


def _ada_call(c_all, n_seq, w_ada, b_ada):
    depth = w_ada.shape[0]
    rows = c_all.shape[0]
    n = N_MOD * D_MODEL
    return pl.pallas_call(
        _ada_kernel,
        out_shape=(jax.ShapeDtypeStruct((depth, rows, n), F32), jax.ShapeDtypeStruct((depth, n_seq, 1, n), F32)),
        grid=(depth, n // ADA_TN),
        in_specs=[
            pl.BlockSpec((rows, D_MODEL), lambda l, j: (0, 0)),
            pl.BlockSpec((1, D_MODEL, ADA_TN), lambda l, j: (l, 0, j)),
            pl.BlockSpec((1, 1, ADA_TN), lambda l, j: (l, 0, j)),
        ],
        out_specs=(pl.BlockSpec((1, rows, ADA_TN), lambda l, j: (l, 0, j)),
                   pl.BlockSpec((1, n_seq, 1, ADA_TN), lambda l, j: (l, 0, 0, j))),
        compiler_params=_params(("arbitrary", "arbitrary")),
        name="ada_mod",
    )(c_all, w_ada, b_ada.reshape(depth, 1, n))


def _ssm_prep_kernel(are_ref, aim_ref, ldt_ref, bre_ref, bim_ref,
                     abre_o, abim_o, bbre_o, bbim_o, pwre_o, pwim_o):
    ar = are_ref[0]
    ai = aim_ref[0]
    dt = jnp.exp(ldt_ref[0])
    mag = jnp.exp(ar * dt)
    ab_re = mag * jnp.cos(ai * dt)
    ab_im = mag * jnp.sin(ai * dt)
    den = ar * ar + ai * ai
    n_re = ab_re - 1.0
    f_re = (n_re * ar + ab_im * ai) / den
    f_im = (ab_im * ar - n_re * ai) / den
    br = bre_ref[0]
    bi = bim_ref[0]
    bbre_o[0] = f_re * br - f_im * bi
    bbim_o[0] = f_re * bi + f_im * br
    abre_o[0] = ab_re
    abim_o[0] = ab_im
    cr, ci = ab_re, ab_im
    for t in range(SCAN_LEN):
        pwre_o[0, t] = cr
        pwim_o[0, t] = ci
        cr, ci = cr * ab_re - ci * ab_im, cr * ab_im + ci * ab_re


def _ssm_prep_call(a_re, a_im, log_dt, b_re, b_im):
    depth = a_re.shape[0]
    g, p, c = SSM_GROUPS, SSM_STATE, SSM_GROUP
    a4 = lambda a: a.reshape(depth, g, 1, p)
    spec_a = pl.BlockSpec((1, g, 1, p), lambda l: (l, 0, 0, 0))
    spec_b = pl.BlockSpec((1, g, c, p), lambda l: (l, 0, 0, 0))
    spec_pw = pl.BlockSpec((1, SCAN_LEN, g, 1, p), lambda l: (l, 0, 0, 0, 0))
    return pl.pallas_call(
        _ssm_prep_kernel,
        out_shape=(jax.ShapeDtypeStruct((depth, g, 1, p), F32),) * 2
        + (jax.ShapeDtypeStruct((depth, g, c, p), F32),) * 2
        + (jax.ShapeDtypeStruct((depth, SCAN_LEN, g, 1, p), F32),) * 2,
        grid=(depth,),
        in_specs=[spec_a, spec_a, pl.BlockSpec((1, g, 1, 1), lambda l: (l, 0, 0, 0)), spec_b, spec_b],
        out_specs=(spec_a, spec_a, spec_b, spec_b, spec_pw, spec_pw),
        compiler_params=_params(("arbitrary",)),
        name="ssm_prep",
    )(a4(a_re), a4(a_im), log_dt.reshape(depth, g, 1, 1),
      jnp.swapaxes(b_re, 2, 3), jnp.swapaxes(b_im, 2, 3))


def _ssm_out(y, u, d_skip, w_glu, b_glu):
    z = jax.nn.gelu(y + d_skip * u)
    return z * _sigmoid(_halves(_dot, z.astype(BF16), w_glu) + b_glu)


def _lane_lo(shape):
    return (lax.broadcasted_iota(jnp.int32, shape, len(shape) - 1) % LANES) < HEAD_DIM


N_MIXER_SEQ_IN = 4
N_MIXER_IN = 24
N_MIXER_OUT = 6


def _mixer_kernel(*refs):
    ins = refs[:N_MIXER_IN]
    outs = refs[N_MIXER_IN:N_MIXER_IN + N_MIXER_OUT]
    scratch = refs[N_MIXER_IN + N_MIXER_OUT:]
    bias_s, wg3_s = scratch[-2:]
    wg_ref = ins[MIXER_REF_NAMES.index('wg_ref')]
    t = pl.program_id(1)

    @pl.when((pl.program_id(0) == 0) & (t == 0))
    def _():
        for c in range(D_GATES // GATE_CHUNK):
            wg3_s[c] = wg_ref[:, c * GATE_CHUNK:(c + 1) * GATE_CHUNK]
        r_i = lax.broadcasted_iota(jnp.int32, (Q_BLK, 2 * Q_BLK), 0)
        c_i = lax.broadcasted_iota(jnp.int32, (Q_BLK, 2 * Q_BLK), 1)
        dist = r_i - c_i + Q_BLK
        valid = (dist >= 0) & (dist <= WINDOW)
        for hh in range(N_HEADS):
            bias_s[hh] = jnp.where(valid, (-SLOPES[hh] * LOG2E) * dist.astype(F32), NEG_INF)

    lvl_s, kbuf, vbuf, pbuf, hc_s = scratch[-7:-2]

    @pl.when(t == 0)
    def _():
        hc_s[...] = jnp.zeros_like(hc_s)
        kbuf[:, 0:WINDOW, :] = jnp.zeros((SEQ_PER_STEP, WINDOW, D_KV), BF16)
        vbuf[:, 0:WINDOW, :] = jnp.zeros((SEQ_PER_STEP, WINDOW, D_KV), BF16)
        pbuf[:, 0:POOL_PAD + POOL_HIST, :] = jnp.zeros((SEQ_PER_STEP, POOL_PAD + POOL_HIST, D_POOL), F32)
        lvl_s[:, :, 0:POOL_PAD, :] = jnp.zeros((SEQ_PER_STEP, 2, POOL_PAD, LANES), F32)

    chains = []
    for s in range(SEQ_PER_STEP):
        one = functools.partial(lambda s, r: r.at[s:s + 1], s)
        chains.append(_mixer_body(t, *[one(r) for r in ins[:N_MIXER_SEQ_IN]], *ins[N_MIXER_SEQ_IN:],
                                  *[one(r) for r in outs], *[r.at[s] for r in scratch[:-2]], bias_s, wg3_s))
    for _ in range(MIXER_LAG):
        next(chains[0])
    while chains:
        for chain in list(chains):
            if next(chain, StopIteration) is StopIteration:
                chains.remove(chain)


MIXER_REF_NAMES = (
    'x_ref sh_ref sc_ref g_ref npre_ref npost_ref wa_ref wg_ref '
    'wbu_ref abre_ref abim_ref pwre_ref pwim_ref cm_ref dsk_ref wglu_ref bglu_ref '
    'sinks_ref wpool_ref pscale_ref wus_ref wua_ref wup_ref wout_ref '
    'xo_ref hre_o him_o kwin_o vwin_o plast_o '
    'hb_s proj_s bu_s hst_s sg_s unat_s uperm_s operm_s yssm_s yatt_s lvl_s kbuf vbuf pbuf hc_s '
    'bias_s wg3_s').split()


def _mixer_body(t, *refs):
    r = types.SimpleNamespace(**dict(zip(MIXER_REF_NAMES, refs, strict=True)))
    x = r.x_ref[0]
    h = _rms(x, r.npre_ref[...]) * (1.0 + r.sc_ref[0]) + r.sh_ref[0]
    r.hb_s[...] = h.astype(BF16)
    yield
    for c in range(D_SMALL // PROJ_CHUNK):
        cols = slice(c * PROJ_CHUNK, (c + 1) * PROJ_CHUNK)
        r.proj_s[:, cols] = _dot(r.hb_s[...], r.wa_ref[:, cols])
        yield
    res = {}
    yield from _ssm_chain(r)
    yield from _attn_chain(t, r)
    yield from _pool_chain(t, r, res)
    yield from _merge_tail(r, res)


def _ssm_chain(r):
    (proj_s, hb_s, bu_s, hst_s, sg_s, unat_s, uperm_s, operm_s, yssm_s, hc_s, wg_ref, wbu_ref, abre_ref, abim_ref,
     pwre_ref, pwim_ref, cm_ref, dsk_ref, wglu_ref, bglu_ref, hre_o, him_o) = (
        r.proj_s, r.hb_s, r.bu_s, r.hst_s, r.sg_s, r.unat_s, r.uperm_s, r.operm_s, r.yssm_s, r.hc_s, r.wg3_s,
        r.wbu_ref, r.abre_ref, r.abim_ref, r.pwre_ref, r.pwim_ref, r.cm_ref, r.dsk_ref, r.wglu_ref, r.bglu_ref,
        r.hre_o, r.him_o)
    for c in range(D_SSM // LANES):
        unat_s[c] = proj_s[:, c * LANES:(c + 1) * LANES]
        for i in range(SCAN_LEN):
            uperm_s[i * SUBLANES:(i + 1) * SUBLANES, c * LANES:(c + 1) * LANES] = (
                unat_s.at[c][pl.ds(i, SUBLANES, stride=SCAN_LEN), :])
        yield
    for c in range(2 * D_STATE // BU_CHUNK):
        cols = slice(c * BU_CHUNK, (c + 1) * BU_CHUNK)
        bu_s[:, cols] = _dot(uperm_s[...].astype(BF16), wbu_ref[:, cols])
        yield

    abr = jnp.broadcast_to(abre_ref[...], (SUBLANES, D_STATE))
    abi = jnp.broadcast_to(abim_ref[...], (SUBLANES, D_STATE))

    def gate_chunk(c):
        sg_s[c] = _sigmoid(_dot(hb_s[...], wg_ref[c]))

    zero = jnp.zeros((SUBLANES, D_STATE), F32)
    steps_a = SCAN_LEN // GATE_LOOP_A

    er, ei = zero, zero
    for i in range(GATE_LOOP_A):
        gate_chunk(i)
        yield
        for s in range(steps_a):
            rows = slice((i * steps_a + s) * SUBLANES, (i * steps_a + s + 1) * SUBLANES)
            er, ei = (abr * er - abi * ei + bu_s[rows, 0:D_STATE],
                      abr * ei + abi * er + bu_s[rows, D_STATE:2 * D_STATE])
            bu_s[rows, 0:D_STATE] = er
            bu_s[rows, D_STATE:2 * D_STATE] = ei
        yield

    alr = pwre_ref[SCAN_LEN - 1:SCAN_LEN, :]
    ali = pwim_ref[SCAN_LEN - 1:SCAN_LEN, :]
    cr = hc_s[0:1, :]
    ci = hc_s[1:2, :]
    row = lax.broadcasted_iota(jnp.int32, (SUBLANES, D_STATE), 0)
    hin_r = zero
    hin_i = zero
    for j in range(SUBLANES):
        hin_r = jnp.where(row == j, cr, hin_r)
        hin_i = jnp.where(row == j, ci, hin_i)
        cr, ci = (alr * cr - ali * ci + er[j:j + 1, :], alr * ci + ali * cr + ei[j:j + 1, :])
    hc_s[0:1, :] = cr
    hc_s[1:2, :] = ci
    hre_o[0] = cr
    him_o[0] = ci
    yield

    pack = 2 * SUBLANES
    packs_b = TT // pack // GATE_LOOP_B

    for i in range(GATE_LOOP_B):
        gate_chunk(GATE_LOOP_A + i)
        yield
        for s in range(packs_b):
            k = i * packs_b + s
            fixed_r, fixed_i = [], []
            for slab in (2 * k, 2 * k + 1):
                rows = slice(slab * SUBLANES, (slab + 1) * SUBLANES)
                pr = pwre_ref[slab:slab + 1, :]
                pi = pwim_ref[slab:slab + 1, :]
                fixed_r.append(bu_s[rows, 0:D_STATE] + (pr * hin_r - pi * hin_i))
                fixed_i.append(bu_s[rows, D_STATE:2 * D_STATE] + (pr * hin_i + pi * hin_r))
            prow = slice(k * pack, (k + 1) * pack)
            hst_s[prow, 0:D_STATE] = jnp.concatenate(fixed_r, axis=0).astype(BF16)
            hst_s[prow, D_STATE:2 * D_STATE] = jnp.concatenate(fixed_i, axis=0).astype(BF16)
        yield

    y = _halves(_dot, hst_s[...], cm_ref[...])
    yield
    operm_s[...] = _ssm_out(y, uperm_s[...], dsk_ref[...], wglu_ref[...], bglu_ref[...])
    yield
    for c in range(D_SSM // LANES):
        for i in range(SCAN_LEN):
            yssm_s.at[c][pl.ds(i, SUBLANES, stride=SCAN_LEN), :] = (
                operm_s[i * SUBLANES:(i + 1) * SUBLANES, c * LANES:(c + 1) * LANES])
        yield

def _attn_chain(t, r):
    proj_s, kbuf, vbuf, yatt_s, bias_s, sinks_ref, kwin_o, vwin_o = (
        r.proj_s, r.kbuf, r.vbuf, r.yatt_s, r.bias_s, r.sinks_ref, r.kwin_o, r.vwin_o)
    kbuf[WINDOW:WINDOW + TT, :] = proj_s[:, D_SSM + D_ATTN:D_SSM + D_ATTN + D_KV].astype(BF16)
    vbuf[WINDOW:WINDOW + TT, :] = proj_s[:, D_SSM + D_ATTN + D_KV:D_SSM + D_ATTN + 2 * D_KV].astype(BF16)
    lo = _lane_lo((Q_BLK, LANES))
    r_i = lax.broadcasted_iota(jnp.int32, (Q_BLK, 2 * Q_BLK), 0)
    c_i = lax.broadcasted_iota(jnp.int32, (Q_BLK, 2 * Q_BLK), 1)
    first_key = jnp.where(t == 0, Q_BLK, 0)
    def scores(blk):
        r0 = blk * Q_BLK
        pieces = []
        for hh in range(N_HEADS):
            j = hh % GQA
            qp = proj_s[r0:r0 + Q_BLK, D_SSM + j * LANES:D_SSM + (j + 1) * LANES] * (ATTN_SCALE * LOG2E)
            keep = lo if hh < GQA else jnp.logical_not(lo)
            pieces.append(jnp.where(keep, qp, 0.0).astype(BF16))
        return [_dot_t(jnp.concatenate(pieces[g * GQA:(g + 1) * GQA], axis=0), kbuf[r0:r0 + 2 * Q_BLK, :])
                for g in range(N_KV_HEADS)]

    n_blk = TT // Q_BLK
    s_next = scores(0)
    yield
    for blk in range(n_blk):
        r0 = blk * Q_BLK
        s_all = s_next
        if blk + 1 < n_blk:
            s_next = scores(blk + 1)
            yield
        probs, dens = [], []
        for hh in range(N_HEADS):
            s = s_all[hh // GQA][(hh % GQA) * Q_BLK:(hh % GQA + 1) * Q_BLK, :] + bias_s[hh]
            if blk == 0:
                s = jnp.where(c_i >= first_key, s, NEG_INF)
            snk = sinks_ref[hh] * LOG2E
            m = jnp.maximum(jnp.max(s, axis=-1, keepdims=True), snk)
            p = jnp.exp2(s - m)
            dens.append(jnp.sum(p, axis=-1, keepdims=True) + jnp.exp2(snk - m))
            probs.append(p.astype(BF16))
            yield
        o_all = [_dot(jnp.concatenate(probs[g * GQA:(g + 1) * GQA], axis=0), vbuf[r0:r0 + 2 * Q_BLK, :])
                 for g in range(N_KV_HEADS)]
        out = lambda hh: o_all[hh // GQA][(hh % GQA) * Q_BLK:(hh % GQA + 1) * Q_BLK, :] / dens[hh]
        for j in range(GQA):
            yatt_s[r0:r0 + Q_BLK, j * LANES:(j + 1) * LANES] = jnp.where(lo, out(j), out(GQA + j)).astype(BF16)
        yield
    kwin_o[0] = proj_s[TT - WINDOW:TT, D_SSM + D_ATTN:D_SSM + D_ATTN + D_KV]
    vwin_o[0] = proj_s[TT - WINDOW:TT, D_SSM + D_ATTN + D_KV:D_SSM + D_ATTN + 2 * D_KV]
    kbuf[0:WINDOW, :] = kbuf[TT:TT + WINDOW, :]
    vbuf[0:WINDOW, :] = vbuf[TT:TT + WINDOW, :]
    yield

def _pool_chain(t, r, res):
    proj_s, pbuf, lvl_s, plast_o, wpool_ref, pscale_ref = (
        r.proj_s, r.pbuf, r.lvl_s, r.plast_o, r.wpool_ref, r.pscale_ref)
    p0 = D_SMALL - D_POOL
    cur = POOL_PAD + POOL_HIST
    end = cur + TT
    pbuf[cur:end, :] = proj_s[:, p0:D_SMALL]
    plast_o[0] = proj_s[TT - POOL_HIST:TT, p0:D_SMALL]
    pos1 = (t * TT + 1 + lax.broadcasted_iota(jnp.int32, (TT, LANES), 0)).astype(F32)
    lo_t = _lane_lo((TT, LANES))
    a_s, b_s = lvl_s.at[0], lvl_s.at[1]
    mixed = []
    for col, (w_lo, w_hi) in enumerate(((POOL_WINDOWS[0], POOL_WINDOWS[1]), (POOL_WINDOWS[2], POOL_WINDOWS[3]))):
        cs = slice(col * LANES, (col + 1) * LANES)
        pf = pbuf[cur:end, cs]
        a_s[POOL_PAD:end, :] = pbuf[POOL_PAD:end, cs] + pbuf[POOL_PAD - 1:end - 1, cs]
        have = 2
        while have < w_lo:
            b_s[POOL_PAD:end, :] = a_s[POOL_PAD:end, :] + a_s[POOL_PAD - have:end - have, :]
            a_s, b_s, have = b_s, a_s, 2 * have
        acc_lo = a_s[cur:end, :]
        while 2 * have < w_hi:
            b_s[POOL_PAD:end, :] = a_s[POOL_PAD:end, :] + a_s[POOL_PAD - have:end - have, :]
            a_s, b_s, have = b_s, a_s, 2 * have
        acc_hi = a_s[cur:end, :] + a_s[cur - have:end - have, :]
        cnt = jnp.where(lo_t, jnp.minimum(float(w_lo), pos1), jnp.minimum(float(w_hi), pos1))
        mixed.append(jnp.where(lo_t, acc_lo, acc_hi) / cnt - pf)
        yield
    ypool = _halves(_dot, jnp.concatenate(mixed, axis=1).astype(BF16), wpool_ref[...]) * pscale_ref[...]
    pbuf[POOL_PAD:cur, :] = pbuf[end - POOL_HIST:end, :]
    res['ypool'] = ypool
    yield


def _merge_tail(r, res):
    sg_s, yssm_s, yatt_s, wus_ref, wua_ref, wup_ref, wout_ref, x_ref, g_ref, npost_ref, xo_ref = (
        r.sg_s, r.yssm_s, r.yatt_s, r.wus_ref, r.wua_ref, r.wup_ref, r.wout_ref, r.x_ref, r.g_ref, r.npost_ref,
        r.xo_ref)
    ypool = res['ypool']
    per_branch = D_MODEL // GATE_CHUNK
    gate = lambda b: jnp.concatenate([sg_s[b * per_branch + c] for c in range(per_branch)], axis=1)
    yssm = jnp.concatenate([yssm_s[c] for c in range(D_SSM // LANES)], axis=1)
    merged = gate(0) * _dot(yssm.astype(BF16), wus_ref[...])
    yield
    merged += gate(1) * _dot(yatt_s[...], wua_ref[...])
    yield
    merged += gate(2) * _dot(ypool.astype(BF16), wup_ref[...])
    yield
    out = _dot(merged.astype(BF16), wout_ref[...])
    yield
    xo_ref[0] = x_ref[0] + g_ref[0] * _rms(out, npost_ref[...])


def _mixer_call(x, mods3, lp):
    b, t_len, d = x.shape
    nt = t_len // TT
    nsamp = mods3.shape[0] - b
    sps = SEQ_PER_STEP
    mod_spec = lambda col: pl.BlockSpec((sps, 1, D_MODEL), lambda i, j: (nsamp // sps + i, 0, col))
    x_spec = pl.BlockSpec((sps, TT, d), lambda i, j: (i, j, 0))
    consts = [lp['norm_mix_pre'], lp['norm_mix_post'], lp['w_in_a'], lp['w_in_g'],
              lp['w_bu'], lp['ab_re'], lp['ab_im'], lp['pw_re'], lp['pw_im'], lp['c_mat'],
              lp['ssm_d'], lp['ssm_w_glu'], lp['ssm_b_glu']]
    consts2 = [lp['pool_wbd'], lp['pool_scale'], lp['w_up_ssm'], lp['w_up_attn'], lp['w_up_pool'], lp['w_out']]
    in_specs = ([x_spec, mod_spec(0), mod_spec(1), mod_spec(2)]
                + [_const_spec(a.shape) for a in consts]
                + [pl.BlockSpec(memory_space=pltpu.SMEM)]
                + [_const_spec(a.shape) for a in consts2])
    per_b = lambda r, c: pl.BlockSpec((sps, r, c), lambda i, j: (i, 0, 0))
    out_shape = (jax.ShapeDtypeStruct((b, t_len, d), F32),
                 jax.ShapeDtypeStruct((b, 1, D_STATE), F32), jax.ShapeDtypeStruct((b, 1, D_STATE), F32),
                 jax.ShapeDtypeStruct((b, WINDOW, D_KV), F32), jax.ShapeDtypeStruct((b, WINDOW, D_KV), F32),
                 jax.ShapeDtypeStruct((b, POOL_HIST, D_POOL), F32))
    out_specs = (x_spec, per_b(1, D_STATE), per_b(1, D_STATE), per_b(WINDOW, D_KV), per_b(WINDOW, D_KV),
                 per_b(POOL_HIST, D_POOL))
    per_seq = lambda shape, dtype: pltpu.VMEM((sps,) + shape, dtype)
    scratch = [per_seq((TT, d), BF16), per_seq((TT, D_SMALL), F32), per_seq((TT, 2 * D_STATE), F32),
               per_seq((TT, 2 * D_STATE), BF16), per_seq((D_GATES // GATE_CHUNK, TT, GATE_CHUNK), F32),
               per_seq((D_SSM // LANES, TT, LANES), F32), per_seq((TT, D_SSM), F32),
               per_seq((TT, D_SSM), F32), per_seq((D_SSM // LANES, TT, LANES), F32),
               per_seq((TT, D_ATTN), BF16), per_seq((2, POOL_PAD + POOL_HIST + TT, LANES), F32),
               per_seq((WINDOW + TT, D_KV), BF16), per_seq((WINDOW + TT, D_KV), BF16),
               per_seq((POOL_PAD + POOL_HIST + TT, D_POOL), F32), per_seq((SUBLANES, D_STATE), F32),
               pltpu.VMEM((N_HEADS, Q_BLK, 2 * Q_BLK), F32),
               pltpu.VMEM((D_GATES // GATE_CHUNK, d, GATE_CHUNK), BF16)]
    return pl.pallas_call(
        _mixer_kernel,
        out_shape=out_shape,
        grid=(b // sps, nt),
        in_specs=in_specs,
        out_specs=out_specs,
        scratch_shapes=scratch,
        compiler_params=_params(("arbitrary", "arbitrary")),
        name="prompt_mixer",
    )(x, mods3, mods3, mods3, *consts, lp['attn_sinks'], *consts2)


def _mod_rows(ref, per_row):
    return ref[...] if per_row else ref[0]


def _ffn_kernel(per_row, x_ref, sh_ref, sc_ref, g_ref, npre_ref, npost_ref, wg_ref, wu_ref, wd_ref, o_ref):
    x = x_ref[...]
    h = (_rms(x, npre_ref[...]) * (1.0 + _mod_rows(sc_ref, per_row)) + _mod_rows(sh_ref, per_row)).astype(BF16)
    f = None
    for c0, c1 in FF_CHUNKS:
        act = _silu(_dot(h, wg_ref[:, c0:c1])) * _dot(h, wu_ref[:, c0:c1])
        part = _dot(act.astype(BF16), wd_ref[c0:c1, :])
        f = part if f is None else f + part
    o_ref[...] = x + _mod_rows(g_ref, per_row) * _rms(f, npost_ref[...])


def _mod_specs(mods3, mods2, n_rows, tm, rows_per_seq, per_row):
    nsamp = mods3.shape[0] - (n_rows // rows_per_seq if not per_row else 0)
    if per_row:
        specs = [pl.BlockSpec((tm, D_MODEL), functools.partial(lambda col, i, *_: (i, col), col)) for col in (3, 4, 5)]
        return specs, [mods2] * 3
    tiles_per_seq = rows_per_seq // tm
    specs = [pl.BlockSpec((1, 1, D_MODEL),
                          functools.partial(lambda col, i, *_: (nsamp + i // tiles_per_seq, 0, col), col))
             for col in (3, 4, 5)]
    return specs, [mods3] * 3


def _ffn_call(x2, mods3, mods2, lp, rows_per_seq, per_row):
    n = x2.shape[0]
    tm = n if per_row else FFN_TM
    mspecs, mops = _mod_specs(mods3, mods2, n, tm, rows_per_seq, per_row)
    consts = [lp['norm_ffn_pre'], lp['norm_ffn_post'], lp['ffn_w_gate'], lp['ffn_w_up'], lp['ffn_w_down']]
    x_spec = pl.BlockSpec((tm, D_MODEL), lambda i: (i, 0))
    return pl.pallas_call(
        functools.partial(_ffn_kernel, per_row),
        out_shape=jax.ShapeDtypeStruct(x2.shape, F32),
        grid=(n // tm,),
        in_specs=[x_spec] + mspecs + [_const_spec(a.shape) for a in consts],
        out_specs=x_spec,
        compiler_params=_params(("arbitrary",)),
        name="ffn_rows" if per_row else "ffn_seq",
    )(x2, *mops, *consts)


def _moe_kernel(per_row, x_ref, sh_ref, sc_ref, g_ref, npre_ref, npost_ref, wr_ref, br_ref,
                wg_ref, wu_ref, wd_ref, o_ref, h_s, gates_s, acc_s):
    e = pl.program_id(1)
    lane_e = lax.broadcasted_iota(jnp.int32, gates_s.shape, 1)

    @pl.when(e == 0)
    def _():
        x = x_ref[...]
        h = _rms(x, npre_ref[...]) * (1.0 + _mod_rows(sc_ref, per_row)) + _mod_rows(sh_ref, per_row)
        hb = h.astype(BF16)
        h_s[...] = hb
        logits = _dot(hb, wr_ref[...]) + br_ref[...]
        gates_s[...], _ = _top2_gates(logits, lane_e.astype(F32))
        acc_s[...] = jnp.zeros_like(acc_s)

    h = h_s[...]
    act = _silu(_dot(h, wg_ref[0, 0])) * _dot(h, wu_ref[0, 0])
    y = _dot(act.astype(BF16), wd_ref[0, 0])
    ge = jnp.sum(jnp.where(lane_e == e, gates_s[...], 0.0), axis=-1, keepdims=True)
    acc_s[...] += ge * y

    @pl.when(e == N_EXPERTS - 1)
    def _():
        o_ref[...] = x_ref[...] + _mod_rows(g_ref, per_row) * _rms(acc_s[...], npost_ref[...])


def _moe_call(x2, mods3, mods2, lp, rows_per_seq, per_row):
    n = x2.shape[0]
    tm = n if per_row else MOE_TM
    mspecs, mops = _mod_specs(mods3, mods2, n, tm, rows_per_seq, per_row)
    consts = [lp['norm_ffn_pre'], lp['norm_ffn_post'], lp['moe_w_router'], lp['moe_b_router']]
    x_spec = pl.BlockSpec((tm, D_MODEL), lambda i, e: (i, 0))
    w_spec = lambda a: pl.BlockSpec((1, 1) + a.shape[2:], lambda i, e: (0, e, 0, 0))
    experts = [lp['moe_w_gate'], lp['moe_w_up'], lp['moe_w_down']]
    return pl.pallas_call(
        functools.partial(_moe_kernel, per_row),
        out_shape=jax.ShapeDtypeStruct(x2.shape, F32),
        grid=(n // tm, N_EXPERTS),
        in_specs=[x_spec] + mspecs + [_const_spec(a.shape) for a in consts] + [w_spec(a) for a in experts],
        out_specs=x_spec,
        scratch_shapes=[pltpu.VMEM((tm, D_MODEL), BF16), pltpu.VMEM((tm, N_EXPERTS), F32),
                        pltpu.VMEM((tm, D_MODEL), F32)],
        compiler_params=_params(("arbitrary", "arbitrary")),
        name="moe_rows" if per_row else "moe_seq",
    )(x2, *mops, *consts, *experts)


def _top2_gates(logits, lane_f):
    n_lanes = float(logits.shape[-1])
    v1 = jnp.max(logits, axis=-1, keepdims=True)
    i1 = jnp.min(jnp.where(logits == v1, lane_f, n_lanes), axis=-1, keepdims=True)
    rest = jnp.where(lane_f == i1, -jnp.inf, logits)
    v2 = jnp.max(rest, axis=-1, keepdims=True)
    i2 = jnp.min(jnp.where(rest == v2, lane_f, n_lanes), axis=-1, keepdims=True)
    e2 = jnp.exp(v2 - v1)
    den = 1.0 + e2
    gates = jnp.where(lane_f == i1, 1.0 / den, 0.0) + jnp.where(lane_f == i2, e2 / den, 0.0)
    return gates, (lane_f == i1) | (lane_f == i2)


def _moe_routed_kernel(seq_len, n_rows, x_ref, sh0_ref, sh1_ref, sc0_ref, sc1_ref, g0_ref, g1_ref,
                       npre_ref, npost_ref, wr_ref, br_ref, wg_ref, wu_ref, wd_ref, o_ref,
                       h_s, before_s, dest_s, gate_s, acc_s, cnt_s):
    i = pl.program_id(0)
    e = pl.program_id(1)
    tm = x_ref.shape[0]
    row = lax.broadcasted_iota(jnp.int32, (tm, 1), 0) + i * tm
    if n_rows % tm == 0:
        keep = lambda v: v
    else:
        valid = row < n_rows
        keep = lambda v: jnp.where(valid, v, 0.0)
    if seq_len % tm == 0:
        pick = lambda a_ref, b_ref: a_ref[0]
    else:
        second = row >= ((i * tm) // seq_len + 1) * seq_len
        pick = lambda a_ref, b_ref: jnp.where(second, b_ref[0], a_ref[0])

    @pl.when((i == 0) & (e == 0))
    def _():
        r_i = lax.broadcasted_iota(jnp.int32, before_s.shape, 0)
        c_i = lax.broadcasted_iota(jnp.int32, before_s.shape, 1)
        before_s[...] = jnp.where(c_i < r_i, 1.0, 0.0).astype(BF16)

    @pl.when(e == 0)
    def _():
        x = keep(x_ref[...])
        h = _rms(x, npre_ref[...]) * (1.0 + pick(sc0_ref, sc1_ref)) + pick(sh0_ref, sh1_ref)
        hb = h.astype(BF16)
        h_s[...] = hb
        lane_f = lax.broadcasted_iota(jnp.int32, (tm, LANES), 1).astype(F32)
        logits = jnp.where(lane_f < N_EXPERTS, _halves(_dot, hb, wr_ref[...]) + br_ref[...], -jnp.inf)
        gates, sel = _top2_gates(logits, lane_f)
        self = keep(jnp.where(sel, 1.0, 0.0))
        counts = jnp.zeros((1, LANES), F32)
        dest_blocks = []
        for k in range(tm // MOE_RANK_BLOCK):
            self_k = self[k * MOE_RANK_BLOCK:(k + 1) * MOE_RANK_BLOCK, :]
            rank_k = _dot(before_s[...], self_k.astype(BF16)) + counts
            dest_blocks.append(jnp.where(self_k > 0.0, rank_k, -1.0))
            counts = counts + jnp.sum(self_k, axis=0, keepdims=True)
        dest = jnp.concatenate(dest_blocks, axis=0)
        dest_s[...] = dest.T[0:SUBLANES, :]
        gate_s[...] = gates.T[0:SUBLANES, :]
        for j in range(N_EXPERTS):
            cnt_s[j] = counts[0, j].astype(jnp.int32)
        acc_s[...] = jnp.zeros_like(acc_s)

    dest_row = dest_s[pl.ds(e, 1), :]
    gate_row = gate_s[pl.ds(e, 1), :]
    def chunk(base, size):
        slot = lax.broadcasted_iota(jnp.int32, (size, tm), 0).astype(F32) + base.astype(F32)
        onehot = jnp.where(slot == dest_row, 1.0, 0.0)
        gate_packed = jnp.sum(onehot * gate_row, axis=-1, keepdims=True)
        pack = onehot.astype(BF16)
        xg = _dot(pack, h_s[...]).astype(BF16)
        act = _silu(_dot(xg, wg_ref[0, 0])) * _dot(xg, wu_ref[0, 0])
        y = _dot(act.astype(BF16), wd_ref[0, 0]) * gate_packed
        acc_s[...] += lax.dot_general(pack, y.astype(BF16), (((0,), (0,)), ((), ())), preferred_element_type=F32)

    shift = MOE_CAP.bit_length() - 1
    cnt = cnt_s[e]
    n_full = lax.shift_right_logical(cnt, shift)
    rem = cnt - lax.shift_left(n_full, shift)
    long_last = (n_full >= 1) & (rem >= 1) & (rem <= MOE_TAIL)
    n_plain = jnp.where(long_last, n_full - 1, n_full + jnp.where(rem >= 1, 1, 0))

    def plain_chunk(c, _):
        chunk(c * MOE_CAP, MOE_CAP)
        return 0

    lax.fori_loop(0, n_plain, plain_chunk, 0)

    @pl.when(long_last)
    def _():
        chunk(n_plain * MOE_CAP, MOE_CAP + MOE_TAIL)

    @pl.when(e == N_EXPERTS - 1)
    def _():
        o_ref[...] = keep(x_ref[...]) + pick(g0_ref, g1_ref) * _rms(acc_s[...], npost_ref[...])


def _moe_routed_call(x2, mods3, lp, seq_len):
    n = x2.shape[0]
    tm = MOE_SP_TM
    assert tm % MOE_RANK_BLOCK == 0
    n_seq = n // seq_len
    nsamp = mods3.shape[0] - n_seq

    def mod_spec(col, nxt):
        return pl.BlockSpec((1, 1, D_MODEL),
                            lambda i, e: (nsamp + jnp.minimum((i * tm) // seq_len + nxt, n_seq - 1), 0, col))

    mspecs = [mod_spec(col, nxt) for col in (3, 4, 5) for nxt in (0, 1)]
    pad = LANES - N_EXPERTS
    consts = [lp['norm_ffn_pre'], lp['norm_ffn_post'],
              jnp.pad(lp['moe_w_router'], ((0, 0), (0, pad))), jnp.pad(lp['moe_b_router'], ((0, 0), (0, pad)))]
    x_spec = pl.BlockSpec((tm, D_MODEL), lambda i, e: (i, 0))
    w_spec = lambda a: pl.BlockSpec((1, 1) + a.shape[2:], lambda i, e: (0, e, 0, 0))
    experts = [lp['moe_w_gate'], lp['moe_w_up'], lp['moe_w_down']]
    return pl.pallas_call(
        functools.partial(_moe_routed_kernel, seq_len, n),
        out_shape=jax.ShapeDtypeStruct(x2.shape, F32),
        grid=(pl.cdiv(n, tm), N_EXPERTS),
        in_specs=[x_spec] + mspecs + [_const_spec(a.shape) for a in consts] + [w_spec(a) for a in experts],
        out_specs=x_spec,
        scratch_shapes=[pltpu.VMEM((tm, D_MODEL), BF16), pltpu.VMEM((MOE_RANK_BLOCK, MOE_RANK_BLOCK), BF16),
                        pltpu.VMEM((SUBLANES, tm), F32), pltpu.VMEM((SUBLANES, tm), F32),
                        pltpu.VMEM((tm, D_MODEL), F32), pltpu.SMEM((N_EXPERTS,), jnp.int32)],
        compiler_params=_params(("arbitrary", "arbitrary")),
        name="moe_routed",
    )(x2, *([mods3] * 6), *consts, *experts)


def _sproj_kernel(x_ref, sh_ref, sc_ref, npre_ref, wa_ref, wg_ref, pa_o, pg_o):
    h = (_rms(x_ref[...], npre_ref[...]) * (1.0 + sc_ref[...]) + sh_ref[...]).astype(BF16)
    pa_o[...] = _dot(h, wa_ref[...])
    pg_o[...] = _dot(h, wg_ref[...])


def _sproj_call(xs, mods2, lp):
    n = xs.shape[0]
    mod_spec = lambda col: pl.BlockSpec((n, D_MODEL), lambda i: (0, col))
    consts = [lp['norm_mix_pre'], lp['w_in_a'], lp['w_in_g']]
    return pl.pallas_call(
        _sproj_kernel,
        out_shape=(jax.ShapeDtypeStruct((n, D_SMALL), F32), jax.ShapeDtypeStruct((n, D_GATES), F32)),
        grid=(1,),
        in_specs=[_const_spec(xs.shape), mod_spec(0), mod_spec(1)] + [_const_spec(a.shape) for a in consts],
        out_specs=(_const_spec((n, D_SMALL)), _const_spec((n, D_GATES))),
        compiler_params=_params(("arbitrary",)),
        name="sample_proj",
    )(xs, mods2, mods2, *consts)


N_SMIX_IN = 22


def _smix_kernel(*refs):
    (u_ref, q_ref, kn_ref, vn_ref, knt_ref, vnt_ref, pin_ref, h0r_ref, h0i_ref, ck_ref, cv_ref, hist_ref,
     wbu_ref, abre_ref, abim_ref, cm_ref, dsk_ref, wglu_ref, bglu_ref, sinks_ref,
     wpool_ref, pscale_ref) = refs[:N_SMIX_IN]
    yssm_o, yatt_o, ypool_o, hr_o, hi_o, ck_o, cv_o, hist_o = refs[-8:]
    u = u_ref[...]
    bu = _dot(u.astype(BF16), wbu_ref[...])
    abr, abi = abre_ref[...], abim_ref[...]
    h0r, h0i = h0r_ref[...], h0i_ref[...]
    hr = bu[:, 0:D_STATE] + (abr * h0r - abi * h0i)
    hi = bu[:, D_STATE:2 * D_STATE] + (abr * h0i + abi * h0r)
    hr_o[...] = hr
    hi_o[...] = hi
    y = _dot(jnp.concatenate([hr, hi], axis=1).astype(BF16), cm_ref[...])
    yssm_o[...] = _ssm_out(y, u, dsk_ref[...], wglu_ref[...], bglu_ref[...])

    q = q_ref[...] * ATTN_SCALE
    kn, vn = kn_ref[...], vn_ref[...]
    kc = ck_ref[...].astype(BF16)
    vc = cv_ref[...].astype(BF16)
    lo = _lane_lo(q.shape)
    j_i = lax.broadcasted_iota(jnp.int32, (1, GQA, 1), 1)
    dist_c = (WINDOW - lax.broadcasted_iota(jnp.int32, (1, 1, WINDOW), 2)).astype(F32)
    halves = []
    for half in range(N_KV_HEADS):
        keep = lo if half == 0 else jnp.logical_not(lo)
        qm = jnp.where(keep, q, 0.0)
        slope = jnp.zeros((1, GQA, 1), F32)
        snk = jnp.zeros((1, GQA, 1), F32)
        for j in range(GQA):
            slope = jnp.where(j_i == j, SLOPES[half * GQA + j], slope)
            snk = jnp.where(j_i == j, sinks_ref[half * GQA + j], snk)
        s = jnp.einsum('bhd,bds->bhs', qm.astype(BF16), kc, preferred_element_type=F32) - slope * dist_c
        s_new = jnp.sum(qm * kn, axis=-1, keepdims=True)
        m = jnp.maximum(jnp.maximum(jnp.max(s, axis=-1, keepdims=True), s_new), snk)
        p = jnp.exp(s - m)
        p_new = jnp.exp(s_new - m)
        den = jnp.sum(p, axis=-1, keepdims=True) + p_new + jnp.exp(snk - m)
        o = jnp.einsum('bhs,bds->bhd', p.astype(BF16), vc, preferred_element_type=F32) + p_new * vn
        halves.append(o / den)
    yatt_o[...] = jnp.where(lo, halves[0], halves[1])

    bc = ck_ref.shape[0]
    lane_w = lax.broadcasted_iota(jnp.int32, (D_KV, WINDOW), 1)
    lane_b = lax.broadcasted_iota(jnp.int32, knt_ref.shape, 1)
    for src, new_t, dst in ((ck_ref, knt_ref, ck_o), (cv_ref, vnt_ref, cv_o)):
        for b in range(bc):
            col = jnp.sum(jnp.where(lane_b == pl.program_id(0) * bc + b, new_t[...], 0.0), axis=-1, keepdims=True)
            dst[b] = jnp.where(lane_w == WINDOW - 1, col, pltpu.roll(src[b], WINDOW - 1, axis=1))

    pin = pin_ref[...]
    lo2 = _lane_lo((pin.shape[0], LANES))
    mixed = []
    for col, (w_lo, w_hi) in enumerate(((POOL_WINDOWS[0], POOL_WINDOWS[1]), (POOL_WINDOWS[2], POOL_WINDOWS[3]))):
        cs = slice(col * LANES, (col + 1) * LANES)
        pf = pin[:, cs]
        acc = pf
        for k in range(1, w_lo):
            acc = acc + hist_ref[POOL_BUF - k, :, cs]
        acc_lo = acc
        for k in range(w_lo, w_hi):
            acc = acc + hist_ref[POOL_BUF - k, :, cs]
        cnt_lo = float(min(w_lo, PAST_LEN + 1))
        cnt_hi = float(min(w_hi, PAST_LEN + 1))
        mixed.append(jnp.where(lo2, acc_lo / cnt_lo, acc / cnt_hi) - pf)
    ypool_o[...] = _dot(jnp.concatenate(mixed, axis=1).astype(BF16), wpool_ref[...]) * pscale_ref[...]
    hist_o[0:POOL_BUF - 1] = hist_ref[1:POOL_BUF]
    hist_o[POOL_BUF - 1] = pin


def _smix_call(layer, u, q3, kn3, vn3, knt, vnt, pin, h0r, h0i, ck, cv, prev_windows, hist_t, lp):
    n = u.shape[0]
    bc = SAMPLE_BC
    rows2 = lambda c: pl.BlockSpec((bc, c), lambda i: (i, 0))
    rows3 = lambda r, c: pl.BlockSpec((bc, r, c), lambda i: (i, 0, 0))
    win_spec = pl.BlockSpec((None, bc, D_KV, WINDOW), lambda i: (layer, i, 0, 0))
    consts = [lp['w_bu'], lp['ab_re'], lp['ab_im'], lp['c_mat'], lp['ssm_d'], lp['ssm_w_glu'], lp['ssm_b_glu']]
    consts2 = [lp['pool_wbd'], lp['pool_scale']]
    hist_spec = pl.BlockSpec((POOL_BUF, bc, D_POOL), lambda i: (0, i, 0))
    in_specs = ([rows2(D_SSM), rows3(GQA, LANES), rows3(1, LANES), rows3(1, LANES),
                 _const_spec(knt.shape), _const_spec(vnt.shape), rows2(D_POOL),
                 rows2(D_STATE), rows2(D_STATE), win_spec, win_spec, hist_spec]
                + [_const_spec(a.shape) for a in consts]
                + [pl.BlockSpec(memory_space=pltpu.SMEM)]
                + [_const_spec(a.shape) for a in consts2]
                + [pl.BlockSpec(memory_space=pl.ANY) for _ in prev_windows])
    assert len(in_specs) == N_SMIX_IN + len(prev_windows)
    aliases = {N_SMIX_IN + k: 5 + k for k in range(len(prev_windows))}
    return pl.pallas_call(
        _smix_kernel,
        out_shape=(jax.ShapeDtypeStruct((n, D_SSM), F32), jax.ShapeDtypeStruct((n, GQA, LANES), F32),
                   jax.ShapeDtypeStruct((n, D_POOL), F32),
                   jax.ShapeDtypeStruct((n, D_STATE), F32), jax.ShapeDtypeStruct((n, D_STATE), F32),
                   jax.ShapeDtypeStruct(ck.shape, F32), jax.ShapeDtypeStruct(cv.shape, F32),
                   jax.ShapeDtypeStruct(hist_t.shape, F32)),
        grid=(n // bc,),
        in_specs=in_specs,
        out_specs=(rows2(D_SSM), rows3(GQA, LANES), rows2(D_POOL), rows2(D_STATE), rows2(D_STATE),
                   win_spec, win_spec, hist_spec),
        input_output_aliases=aliases,
        compiler_params=_params(("arbitrary",)),
        name="sample_mix",
    )(u, q3, kn3, vn3, knt, vnt, pin, h0r, h0i, ck, cv, hist_t, *consts, lp['attn_sinks'], *consts2, *prev_windows)


def _smerge_kernel(x_ref, g_ref, pg_ref, yssm_ref, yatt_ref, ypool_ref, npost_ref,
                   wus_ref, wua_ref, wup_ref, wout_ref, o_ref):
    merged = _sigmoid(pg_ref[:, 0:D_MODEL]) * _dot(yssm_ref[...].astype(BF16), wus_ref[...])
    merged += _sigmoid(pg_ref[:, D_MODEL:2 * D_MODEL]) * _dot(yatt_ref[...].astype(BF16), wua_ref[...])
    merged += _sigmoid(pg_ref[:, 2 * D_MODEL:3 * D_MODEL]) * _dot(ypool_ref[...].astype(BF16), wup_ref[...])
    out = _dot(merged.astype(BF16), wout_ref[...])
    o_ref[...] = x_ref[...] + g_ref[...] * _rms(out, npost_ref[...])


def _smerge_call(xs, mods2, pg, yssm, yatt, ypool, lp):
    n = xs.shape[0]
    ops = [xs, mods2, pg, yssm, yatt, ypool, lp['norm_mix_post'],
           lp['w_up_ssm'], lp['w_up_attn'], lp['w_up_pool'], lp['w_out']]
    in_specs = [_const_spec(a.shape) for a in ops]
    in_specs[1] = pl.BlockSpec((n, D_MODEL), lambda i: (0, 2))
    return pl.pallas_call(
        _smerge_kernel,
        out_shape=jax.ShapeDtypeStruct(xs.shape, F32),
        grid=(1,),
        in_specs=in_specs,
        out_specs=_const_spec(xs.shape),
        compiler_params=_params(("arbitrary",)),
        name="sample_merge",
    )(*ops)


def _block_diag(w):
    g, a, b = w.shape
    eye = jnp.eye(g, dtype=w.dtype)
    return jnp.einsum('gab,gh->gahb', w, eye).reshape(g * a, g * b)


def _pair_heads(w, axis):
    shape = w.shape
    w = w.reshape(shape[:axis] + (N_KV_HEADS, GQA, HEAD_DIM) + shape[axis + 1:])
    return jnp.swapaxes(w, axis, axis + 1).reshape(shape)


def _layer_params(l, p, prep):
    ab_re, ab_im, bb_re, bb_im, pw_re, pw_im = prep
    row = lambda a: a.reshape(1, -1)
    w_in = p['w_in'][l]
    q_cols = _pair_heads(w_in[:, D_SSM:D_SSM + D_ATTN], 1)
    w_in_a = jnp.concatenate([w_in[:, 0:D_SSM], q_cols, w_in[:, D_SSM + D_ATTN:D_SMALL]], axis=1)
    lp = {
        'norm_mix_pre': row(p['norm_mix_pre'][l]), 'norm_mix_post': row(p['norm_mix_post'][l]),
        'norm_ffn_pre': row(p['norm_ffn_pre'][l]), 'norm_ffn_post': row(p['norm_ffn_post'][l]),
        'w_in_a': w_in_a.astype(BF16),
        'w_in_g': w_in[:, D_SMALL:].astype(BF16),
        'w_bu': jnp.concatenate([_block_diag(bb_re[l]), _block_diag(bb_im[l])], axis=1).astype(BF16),
        'ab_re': ab_re[l].reshape(1, D_STATE), 'ab_im': ab_im[l].reshape(1, D_STATE),
        'pw_re': pw_re[l].reshape(SCAN_LEN, D_STATE), 'pw_im': pw_im[l].reshape(SCAN_LEN, D_STATE),
        'c_mat': jnp.concatenate([_block_diag(jnp.swapaxes(p['ssm_c_re'][l], 1, 2)),
                                  _block_diag(jnp.swapaxes(-p['ssm_c_im'][l], 1, 2))], axis=0).astype(BF16),
        'ssm_d': row(p['ssm_d'][l]), 'ssm_w_glu': p['ssm_w_glu'][l].astype(BF16), 'ssm_b_glu': row(p['ssm_b_glu'][l]),
        'attn_sinks': p['attn_sinks'][l],
        'pool_wbd': _block_diag(p['pool_w'][l]).astype(BF16), 'pool_scale': row(p['pool_scale'][l]),
        'w_up_ssm': p['w_up_ssm'][l].astype(BF16), 'w_up_attn': _pair_heads(p['w_up_attn'][l], 0).astype(BF16),
        'w_up_pool': p['w_up_pool'][l].astype(BF16), 'w_out': p['w_out'][l].astype(BF16),
    }
    j = l // 2
    if l % 2 == 1:
        lp.update({'moe_w_router': p['moe_w_router'][j].astype(BF16), 'moe_b_router': row(p['moe_b_router'][j]),
                   'moe_w_gate': p['moe_w_gate'][j:j + 1].astype(BF16), 'moe_w_up': p['moe_w_up'][j:j + 1].astype(BF16),
                   'moe_w_down': p['moe_w_down'][j:j + 1].astype(BF16)})
    else:
        lp.update({'ffn_w_gate': p['ffn_w_gate'][j].astype(BF16), 'ffn_w_up': p['ffn_w_up'][j].astype(BF16),
                   'ffn_w_down': p['ffn_w_down'][j].astype(BF16)})
    return lp


def kernel(x_prompt, x_sample, state_ssm_re, state_ssm_im, cache_win_k, cache_win_v, state_pool,
           c_prompt, c_sample, w_ada, b_ada, norm_mix_pre, norm_mix_post, norm_ffn_pre, norm_ffn_post,
           w_in, ssm_a_re, ssm_a_im, ssm_log_dt, ssm_b_re, ssm_b_im, ssm_c_re, ssm_c_im, ssm_d,
           ssm_w_glu, ssm_b_glu, attn_sinks, pool_w, pool_scale, w_up_ssm, w_up_attn, w_up_pool, w_out,
           ffn_w_gate, ffn_w_up, ffn_w_down, moe_w_router, moe_b_router, moe_w_gate, moe_w_up, moe_w_down):
    p = dict(norm_mix_pre=norm_mix_pre, norm_mix_post=norm_mix_post, norm_ffn_pre=norm_ffn_pre,
             norm_ffn_post=norm_ffn_post, w_in=w_in, ssm_c_re=ssm_c_re, ssm_c_im=ssm_c_im, ssm_d=ssm_d,
             ssm_w_glu=ssm_w_glu, ssm_b_glu=ssm_b_glu, attn_sinks=attn_sinks, pool_w=pool_w, pool_scale=pool_scale,
             w_up_ssm=w_up_ssm, w_up_attn=w_up_attn, w_up_pool=w_up_pool, w_out=w_out,
             ffn_w_gate=ffn_w_gate, ffn_w_up=ffn_w_up, ffn_w_down=ffn_w_down, moe_w_router=moe_w_router,
             moe_b_router=moe_b_router, moe_w_gate=moe_w_gate, moe_w_up=moe_w_up, moe_w_down=moe_w_down)
    depth = w_in.shape[0]
    b, t_len, d = x_prompt.shape
    ns = x_sample.shape[0]

    mods, mods_seq = _ada_call(jnp.concatenate([c_sample, c_prompt], axis=0), b, w_ada, b_ada)
    prep = _ssm_prep_call(ssm_a_re, ssm_a_im, ssm_log_dt, ssm_b_re, ssm_b_im)

    xp = x_prompt
    xs = x_sample.reshape(ns, d)
    p_out = ([], [], [], [], [])
    s_out = ([], [], [], [], [])
    win_minor = lambda c: jnp.transpose(c, (0, 1, 3, 4, 2)).reshape(depth, ns, D_KV, -1)
    win_major = lambda c: jnp.transpose(c.reshape(depth, ns, N_KV_HEADS, HEAD_DIM, -1), (0, 1, 4, 2, 3))
    old_windows = [win_minor(cache_win_k), win_minor(cache_win_v)]
    new_windows = []
    for l in range(depth):
        lp = _layer_params(l, p, prep)
        mods2 = mods[l]
        mods3 = mods_seq[l]

        xp, hre, him, kwin, vwin, plast = _mixer_call(xp, mods3, lp)
        x2 = xp.reshape(b * t_len, d)
        if l % 2 == 1:
            x2 = _moe_routed_call(x2, mods3, lp, t_len)
        else:
            x2 = _ffn_call(x2, mods3, mods2, lp, t_len, False)
        xp = x2.reshape(b, t_len, d)
        p_out[0].append(hre.reshape(b, SSM_GROUPS, SSM_STATE))
        p_out[1].append(him.reshape(b, SSM_GROUPS, SSM_STATE))
        p_out[2].append(kwin.reshape(b, WINDOW, N_KV_HEADS, HEAD_DIM))
        p_out[3].append(vwin.reshape(b, WINDOW, N_KV_HEADS, HEAD_DIM))
        p_out[4].append(plast[:, POOL_HIST - POOL_BUF:, :])

        pa, pg = _sproj_call(xs, mods2, lp)
        u = pa[:, 0:D_SSM]
        q3 = pa[:, D_SSM:D_SSM + D_ATTN].reshape(ns, GQA, LANES)
        k_new = pa[:, D_SSM + D_ATTN:D_SSM + D_ATTN + D_KV]
        v_new = pa[:, D_SSM + D_ATTN + D_KV:D_SSM + D_ATTN + 2 * D_KV]
        pin = pa[:, D_SMALL - D_POOL:]
        yssm, yatt3, ypool, hr, hi, *new_windows, hist_new = _smix_call(
            l, u, q3, k_new.reshape(ns, 1, D_KV), v_new.reshape(ns, 1, D_KV), k_new.T, v_new.T, pin,
            state_ssm_re[l].reshape(ns, D_STATE), state_ssm_im[l].reshape(ns, D_STATE),
            *old_windows, new_windows, jnp.swapaxes(state_pool[l], 0, 1), lp)
        xs = _smerge_call(xs, mods2, pg, yssm, yatt3.reshape(ns, D_ATTN), ypool, lp)
        xs = (_moe_call if l % 2 == 1 else _ffn_call)(xs, mods3, mods2, lp, 1, True)
        s_out[0].append(hr.reshape(ns, SSM_GROUPS, SSM_STATE))
        s_out[1].append(hi.reshape(ns, SSM_GROUPS, SSM_STATE))
        s_out[4].append(jnp.swapaxes(hist_new, 0, 1))

    stack = lambda xs_: [jnp.stack(a, axis=0) for a in xs_]
    p_ssm_re, p_ssm_im, p_win_k, p_win_v, p_pool = stack(p_out)
    s_ssm_re, s_ssm_im, s_pool = stack([s_out[0], s_out[1], s_out[4]])
    s_win_k, s_win_v = [win_major(w) for w in new_windows]
    return (xp, xs.reshape(x_sample.shape), p_ssm_re, p_ssm_im, p_win_k, p_win_v, p_pool,
            s_ssm_re, s_ssm_im, s_win_k, s_win_v, s_pool)
```

```python
import functools
import math
import types

import jax
import jax.numpy as jnp
from jax import lax
from jax.experimental import pallas as pl
from jax.experimental.pallas import tpu as pltpu

F32 = jnp.float32
BF16 = jnp.bfloat16

D_MODEL = 1024
D_SSM = 256
SSM_GROUP = 16
SSM_GROUPS = 16
SSM_STATE = 64
D_STATE = SSM_GROUPS * SSM_STATE
HEAD_DIM = 64
N_HEADS = 8
N_KV_HEADS = 2
GQA = N_HEADS // N_KV_HEADS
D_ATTN = N_HEADS * HEAD_DIM
D_KV = N_KV_HEADS * HEAD_DIM
WINDOW = 128
ATTN_SCALE = HEAD_DIM ** -0.5
NEG_INF = -1e30
LOG2E = math.log2(math.e)
D_POOL = 256
POOL_WINDOWS = (2, 4, 8, 16)
POOL_BUF = 15
POOL_HIST = 16
POOL_PAD = 8
N_MOD = 6
D_FF = 2816
N_EXPERTS = 8
D_EXPERT = 1024
RMS_EPS = 1e-6
PAST_LEN = 16384
D_SMALL = D_SSM + D_ATTN + 2 * D_KV + D_POOL
D_GATES = 3 * D_MODEL
SLOPES = tuple(2.0 ** (-8.0 * (h + 1) / N_HEADS) for h in range(N_HEADS))

SUBLANES = 8
LANES = 128
VMEM_LIMIT = 56 * 1024 * 1024

TT = 512
SEQ_PER_STEP = 1
SCAN_LEN = TT // SUBLANES
Q_BLK = 128
GATE_CHUNK = 512
PROJ_CHUNK = 256
BU_CHUNK = 512
MIXER_LAG = 0
GATE_LOOP_A = 4
GATE_LOOP_B = D_GATES // GATE_CHUNK - GATE_LOOP_A
FFN_TM = 512
MOE_TM = 1024
MOE_SP_TM = 1024
MOE_CAP = 256
MOE_TAIL = 128
MOE_RANK_BLOCK = 256
SAMPLE_BC = 32
ADA_TN = 1536
FF_CHUNKS = ((0, 1024), (1024, 2048), (2048, D_FF))


def _rms(x, gain):
    return x * lax.rsqrt(jnp.mean(x * x, axis=-1, keepdims=True) + RMS_EPS) * gain


def _sigmoid(x):
    return 1.0 / (1.0 + jnp.exp(-x))


def _silu(x):
    return x * _sigmoid(x)


def _dot(a, b):
    return jnp.dot(a, b, preferred_element_type=F32)


def _dot_t(a, b):
    return lax.dot_general(a, b, (((1,), (1,)), ((), ())), preferred_element_type=F32)


def _halves(dot, a, b):
    half = a.shape[0] // 2
    return jnp.concatenate([dot(a[:half], b), dot(a[half:], b)], axis=0)


def _const_spec(shape):
    nd = len(shape)
    return pl.BlockSpec(shape, lambda *_: (0,) * nd)


def _params(sem):
    return pltpu.CompilerParams(dimension_semantics=sem, vmem_limit_bytes=VMEM_LIMIT)


def _ada_kernel(c_ref, w_ref, b_ref, o_ref, oseq_ref):
    c = c_ref[...]
    s = _silu(c).astype(BF16)
    mod = _dot(s, w_ref[0].astype(BF16)) + b_ref[0]
    o_ref[0] = mod
    n_seq = oseq_ref.shape[1]
    first = mod.shape[0] - n_seq
    for s_i in range(n_seq):
        oseq_ref[0, s_i] = mod[first + s_i:first + s_i + 1, :]


def _ada_call(c_all, n_seq, w_ada, b_ada):
    depth = w_ada.shape[0]
    rows = c_all.shape[0]
    n = N_MOD * D_MODEL
    return pl.pallas_call(
        _ada_kernel,
        out_shape=(jax.ShapeDtypeStruct((depth, rows, n), F32), jax.ShapeDtypeStruct((depth, n_seq, 1, n), F32)),
        grid=(depth, n // ADA_TN),
        in_specs=[
            pl.BlockSpec((rows, D_MODEL), lambda l, j: (0, 0)),
            pl.BlockSpec((1, D_MODEL, ADA_TN), lambda l, j: (l, 0, j)),
            pl.BlockSpec((1, 1, ADA_TN), lambda l, j: (l, 0, j)),
        ],
        out_specs=(pl.BlockSpec((1, rows, ADA_TN), lambda l, j: (l, 0, j)),
                   pl.BlockSpec((1, n_seq, 1, ADA_TN), lambda l, j: (l, 0, 0, j))),
        compiler_params=_params(("arbitrary", "arbitrary")),
        name="ada_mod",
    )(c_all, w_ada, b_ada.reshape(depth, 1, n))


def _ssm_prep_kernel(are_ref, aim_ref, ldt_ref, bre_ref, bim_ref,
                     abre_o, abim_o, bbre_o, bbim_o, pwre_o, pwim_o):
    ar = are_ref[0]
    ai = aim_ref[0]
    dt = jnp.exp(ldt_ref[0])
    mag = jnp.exp(ar * dt)
    ab_re = mag * jnp.cos(ai * dt)
    ab_im = mag * jnp.sin(ai * dt)
    den = ar * ar + ai * ai
    n_re = ab_re - 1.0
    f_re = (n_re * ar + ab_im * ai) / den
    f_im = (ab_im * ar - n_re * ai) / den
    br = bre_ref[0]
    bi = bim_ref[0]
    bbre_o[0] = f_re * br - f_im * bi
    bbim_o[0] = f_re * bi + f_im * br
    abre_o[0] = ab_re
    abim_o[0] = ab_im
    cr, ci = ab_re, ab_im
    for t in range(SCAN_LEN):
        pwre_o[0, t] = cr
        pwim_o[0, t] = ci
        cr, ci = cr * ab_re - ci * ab_im, cr * ab_im + ci * ab_re


def _ssm_prep_call(a_re, a_im, log_dt, b_re, b_im):
    depth = a_re.shape[0]
    g, p, c = SSM_GROUPS, SSM_STATE, SSM_GROUP
    a4 = lambda a: a.reshape(depth, g, 1, p)
    spec_a = pl.BlockSpec((1, g, 1, p), lambda l: (l, 0, 0, 0))
    spec_b = pl.BlockSpec((1, g, c, p), lambda l: (l, 0, 0, 0))
    spec_pw = pl.BlockSpec((1, SCAN_LEN, g, 1, p), lambda l: (l, 0, 0, 0, 0))
    return pl.pallas_call(
        _ssm_prep_kernel,
        out_shape=(jax.ShapeDtypeStruct((depth, g, 1, p), F32),) * 2
        + (jax.ShapeDtypeStruct((depth, g, c, p), F32),) * 2
        + (jax.ShapeDtypeStruct((depth, SCAN_LEN, g, 1, p), F32),) * 2,
        grid=(depth,),
        in_specs=[spec_a, spec_a, pl.BlockSpec((1, g, 1, 1), lambda l: (l, 0, 0, 0)), spec_b, spec_b],
        out_specs=(spec_a, spec_a, spec_b, spec_b, spec_pw, spec_pw),
        compiler_params=_params(("arbitrary",)),
        name="ssm_prep",
    )(a4(a_re), a4(a_im), log_dt.reshape(depth, g, 1, 1),
      jnp.swapaxes(b_re, 2, 3), jnp.swapaxes(b_im, 2, 3))


def _ssm_out(y, u, d_skip, w_glu, b_glu):
    z = jax.nn.gelu(y + d_skip * u)
    return z * _sigmoid(_halves(_dot, z.astype(BF16), w_glu) + b_glu)


def _lane_lo(shape):
    return (lax.broadcasted_iota(jnp.int32, shape, len(shape) - 1) % LANES) < HEAD_DIM


N_MIXER_SEQ_IN = 4
N_MIXER_IN = 24
N_MIXER_OUT = 6


def _mixer_kernel(*refs):
    ins = refs[:N_MIXER_IN]
    outs = refs[N_MIXER_IN:N_MIXER_IN + N_MIXER_OUT]
    scratch = refs[N_MIXER_IN + N_MIXER_OUT:]
    bias_s, wg3_s = scratch[-2:]
    wg_ref = ins[MIXER_REF_NAMES.index('wg_ref')]
    t = pl.program_id(1)

    @pl.when((pl.program_id(0) == 0) & (t == 0))
    def _():
        for c in range(D_GATES // GATE_CHUNK):
            wg3_s[c] = wg_ref[:, c * GATE_CHUNK:(c + 1) * GATE_CHUNK]
        r_i = lax.broadcasted_iota(jnp.int32, (Q_BLK, 2 * Q_BLK), 0)
        c_i = lax.broadcasted_iota(jnp.int32, (Q_BLK, 2 * Q_BLK), 1)
        dist = r_i - c_i + Q_BLK
        valid = (dist >= 0) & (dist <= WINDOW)
        for hh in range(N_HEADS):
            bias_s[hh] = jnp.where(valid, (-SLOPES[hh] * LOG2E) * dist.astype(F32), NEG_INF)

    lvl_s, kbuf, vbuf, pbuf, hc_s = scratch[-7:-2]

    @pl.when(t == 0)
    def _():
        hc_s[...] = jnp.zeros_like(hc_s)
        kbuf[:, 0:WINDOW, :] = jnp.zeros((SEQ_PER_STEP, WINDOW, D_KV), BF16)
        vbuf[:, 0:WINDOW, :] = jnp.zeros((SEQ_PER_STEP, WINDOW, D_KV), BF16)
        pbuf[:, 0:POOL_PAD + POOL_HIST, :] = jnp.zeros((SEQ_PER_STEP, POOL_PAD + POOL_HIST, D_POOL), F32)
        lvl_s[:, :, 0:POOL_PAD, :] = jnp.zeros((SEQ_PER_STEP, 2, POOL_PAD, LANES), F32)

    chains = []
    for s in range(SEQ_PER_STEP):
        one = functools.partial(lambda s, r: r.at[s:s + 1], s)
        chains.append(_mixer_body(t, *[one(r) for r in ins[:N_MIXER_SEQ_IN]], *ins[N_MIXER_SEQ_IN:],
                                  *[one(r) for r in outs], *[r.at[s] for r in scratch[:-2]], bias_s, wg3_s))
    for _ in range(MIXER_LAG):
        next(chains[0])
    while chains:
        for chain in list(chains):
            if next(chain, StopIteration) is StopIteration:
                chains.remove(chain)


MIXER_REF_NAMES = (
    'x_ref sh_ref sc_ref g_ref npre_ref npost_ref wa_ref wg_ref '
    'wbu_ref abre_ref abim_ref pwre_ref pwim_ref cm_ref dsk_ref wglu_ref bglu_ref '
    'sinks_ref wpool_ref pscale_ref wus_ref wua_ref wup_ref wout_ref '
    'xo_ref hre_o him_o kwin_o vwin_o plast_o '
    'hb_s proj_s bu_s hst_s sg_s unat_s uperm_s operm_s yssm_s yatt_s lvl_s kbuf vbuf pbuf hc_s '
    'bias_s wg3_s').split()


def _mixer_body(t, *refs):
    r = types.SimpleNamespace(**dict(zip(MIXER_REF_NAMES, refs, strict=True)))
    x = r.x_ref[0]
    h = _rms(x, r.npre_ref[...] * (1.0 + r.sc_ref[0])) + r.sh_ref[0]
    r.hb_s[...] = h.astype(BF16)
    yield
    for c in range(D_SMALL // PROJ_CHUNK):
        cols = slice(c * PROJ_CHUNK, (c + 1) * PROJ_CHUNK)
        r.proj_s[:, cols] = _dot(r.hb_s[...], r.wa_ref[:, cols])
        yield
    res = {}
    yield from _ssm_chain(r)
    yield from _attn_chain(t, r)
    yield from _pool_chain(t, r, res)
    yield from _merge_tail(r, res)


def _ssm_chain(r):
    (proj_s, hb_s, bu_s, hst_s, sg_s, unat_s, uperm_s, operm_s, yssm_s, hc_s, wg_ref, wbu_ref, abre_ref, abim_ref,
     pwre_ref, pwim_ref, cm_ref, dsk_ref, wglu_ref, bglu_ref, hre_o, him_o) = (
        r.proj_s, r.hb_s, r.bu_s, r.hst_s, r.sg_s, r.unat_s, r.uperm_s, r.operm_s, r.yssm_s, r.hc_s, r.wg3_s,
        r.wbu_ref, r.abre_ref, r.abim_ref, r.pwre_ref, r.pwim_ref, r.cm_ref, r.dsk_ref, r.wglu_ref, r.bglu_ref,
        r.hre_o, r.him_o)
    for c in range(D_SSM // LANES):
        unat_s[c] = proj_s[:, c * LANES:(c + 1) * LANES]
        for i in range(SCAN_LEN):
            uperm_s[i * SUBLANES:(i + 1) * SUBLANES, c * LANES:(c + 1) * LANES] = (
                unat_s.at[c][pl.ds(i, SUBLANES, stride=SCAN_LEN), :])
        yield
    for c in range(2 * D_STATE // BU_CHUNK):
        cols = slice(c * BU_CHUNK, (c + 1) * BU_CHUNK)
        bu_s[:, cols] = _dot(uperm_s[...].astype(BF16), wbu_ref[:, cols])
        yield

    abr = jnp.broadcast_to(abre_ref[...], (SUBLANES, D_STATE))
    abi = jnp.broadcast_to(abim_ref[...], (SUBLANES, D_STATE))

    def gate_chunk(c):
        sg_s[c] = _sigmoid(_dot(hb_s[...], wg_ref[c]))

    zero = jnp.zeros((SUBLANES, D_STATE), F32)
    steps_a = SCAN_LEN // GATE_LOOP_A

    er, ei = zero, zero
    for i in range(GATE_LOOP_A):
        gate_chunk(i)
        yield
        for s in range(steps_a):
            rows = slice((i * steps_a + s) * SUBLANES, (i * steps_a + s + 1) * SUBLANES)
            er, ei = (abr * er - abi * ei + bu_s[rows, 0:D_STATE],
                      abr * ei + abi * er + bu_s[rows, D_STATE:2 * D_STATE])
            bu_s[rows, 0:D_STATE] = er
            bu_s[rows, D_STATE:2 * D_STATE] = ei
        yield

    alr = pwre_ref[SCAN_LEN - 1:SCAN_LEN, :]
    ali = pwim_ref[SCAN_LEN - 1:SCAN_LEN, :]
    cr = hc_s[0:1, :]
    ci = hc_s[1:2, :]
    row = lax.broadcasted_iota(jnp.int32, (SUBLANES, D_STATE), 0)
    hin_r = zero
    hin_i = zero
    for j in range(SUBLANES):
        hin_r = jnp.where(row == j, cr, hin_r)
        hin_i = jnp.where(row == j, ci, hin_i)
        cr, ci = (alr * cr - ali * ci + er[j:j + 1, :], alr * ci + ali * cr + ei[j:j + 1, :])
    hc_s[0:1, :] = cr
    hc_s[1:2, :] = ci
    hre_o[0] = cr
    him_o[0] = ci
    yield

    pack = 2 * SUBLANES
    packs_b = TT // pack // GATE_LOOP_B

    for i in range(GATE_LOOP_B):
        gate_chunk(GATE_LOOP_A + i)
        yield
        for s in range(packs_b):
            k = i * packs_b + s
            fixed_r, fixed_i = [], []
            for slab in (2 * k, 2 * k + 1):
                rows = slice(slab * SUBLANES, (slab + 1) * SUBLANES)
                pr = pwre_ref[slab:slab + 1, :]
                pi = pwim_ref[slab:slab + 1, :]
                fixed_r.append(bu_s[rows, 0:D_STATE] + (pr * hin_r - pi * hin_i))
                fixed_i.append(bu_s[rows, D_STATE:2 * D_STATE] + (pr * hin_i + pi * hin_r))
            prow = slice(k * pack, (k + 1) * pack)
            hst_s[prow, 0:D_STATE] = jnp.concatenate(fixed_r, axis=0).astype(BF16)
            hst_s[prow, D_STATE:2 * D_STATE] = jnp.concatenate(fixed_i, axis=0).astype(BF16)
        yield

    y = _halves(_dot, hst_s[...], cm_ref[...])
    yield
    operm_s[...] = _ssm_out(y, uperm_s[...], dsk_ref[...], wglu_ref[...], bglu_ref[...])
    yield
    for c in range(D_SSM // LANES):
        for i in range(SCAN_LEN):
            yssm_s.at[c][pl.ds(i, SUBLANES, stride=SCAN_LEN), :] = (
                operm_s[i * SUBLANES:(i + 1) * SUBLANES, c * LANES:(c + 1) * LANES])
        yield

def _attn_chain(t, r):
    proj_s, kbuf, vbuf, yatt_s, bias_s, sinks_ref, kwin_o, vwin_o = (
        r.proj_s, r.kbuf, r.vbuf, r.yatt_s, r.bias_s, r.sinks_ref, r.kwin_o, r.vwin_o)
    kbuf[WINDOW:WINDOW + TT, :] = proj_s[:, D_SSM + D_ATTN:D_SSM + D_ATTN + D_KV].astype(BF16)
    vbuf[WINDOW:WINDOW + TT, :] = proj_s[:, D_SSM + D_ATTN + D_KV:D_SSM + D_ATTN + 2 * D_KV].astype(BF16)
    lo = _lane_lo((Q_BLK, LANES))
    r_i = lax.broadcasted_iota(jnp.int32, (Q_BLK, 2 * Q_BLK), 0)
    c_i = lax.broadcasted_iota(jnp.int32, (Q_BLK, 2 * Q_BLK), 1)
    first_key = jnp.where(t == 0, Q_BLK, 0)
    def scores(blk):
        r0 = blk * Q_BLK
        pieces = []
        for hh in range(N_HEADS):
            j = hh % GQA
            qp = proj_s[r0:r0 + Q_BLK, D_SSM + j * LANES:D_SSM + (j + 1) * LANES] * (ATTN_SCALE * LOG2E)
            keep = lo if hh < GQA else jnp.logical_not(lo)
            pieces.append(jnp.where(keep, qp, 0.0).astype(BF16))
        return [_dot_t(jnp.concatenate(pieces[g * GQA:(g + 1) * GQA], axis=0), kbuf[r0:r0 + 2 * Q_BLK, :])
                for g in range(N_KV_HEADS)]

    n_blk = TT // Q_BLK
    s_next = scores(0)
    yield
    for blk in range(n_blk):
        r0 = blk * Q_BLK
        s_all = s_next
        if blk + 1 < n_blk:
            s_next = scores(blk + 1)
            yield
        probs, dens = [], []
        for hh in range(N_HEADS):
            s = s_all[hh // GQA][(hh % GQA) * Q_BLK:(hh % GQA + 1) * Q_BLK, :] + bias_s[hh]
            if blk == 0:
                s = jnp.where(c_i >= first_key, s, NEG_INF)
            snk = sinks_ref[hh] * LOG2E
            m = jnp.maximum(jnp.max(s, axis=-1, keepdims=True), snk)
            p = jnp.exp2(s - m)
            dens.append(jnp.sum(p, axis=-1, keepdims=True) + jnp.exp2(snk - m))
            probs.append(p.astype(BF16))
            yield
        o_all = [_dot(jnp.concatenate(probs[g * GQA:(g + 1) * GQA], axis=0), vbuf[r0:r0 + 2 * Q_BLK, :])
                 for g in range(N_KV_HEADS)]
        out = lambda hh: o_all[hh // GQA][(hh % GQA) * Q_BLK:(hh % GQA + 1) * Q_BLK, :] / dens[hh]
        for j in range(GQA):
            yatt_s[r0:r0 + Q_BLK, j * LANES:(j + 1) * LANES] = jnp.where(lo, out(j), out(GQA + j)).astype(BF16)
        yield
    kwin_o[0] = proj_s[TT - WINDOW:TT, D_SSM + D_ATTN:D_SSM + D_ATTN + D_KV]
    vwin_o[0] = proj_s[TT - WINDOW:TT, D_SSM + D_ATTN + D_KV:D_SSM + D_ATTN + 2 * D_KV]
    kbuf[0:WINDOW, :] = kbuf[TT:TT + WINDOW, :]
    vbuf[0:WINDOW, :] = vbuf[TT:TT + WINDOW, :]
    yield

def _pool_chain(t, r, res):
    proj_s, pbuf, lvl_s, plast_o, wpool_ref, pscale_ref = (
        r.proj_s, r.pbuf, r.lvl_s, r.plast_o, r.wpool_ref, r.pscale_ref)
    p0 = D_SMALL - D_POOL
    cur = POOL_PAD + POOL_HIST
    end = cur + TT
    pbuf[cur:end, :] = proj_s[:, p0:D_SMALL]
    plast_o[0] = proj_s[TT - POOL_HIST:TT, p0:D_SMALL]
    pos1 = (t * TT + 1 + lax.broadcasted_iota(jnp.int32, (TT, LANES), 0)).astype(F32)
    lo_t = _lane_lo((TT, LANES))
    a_s, b_s = lvl_s.at[0], lvl_s.at[1]
    mixed = []
    for col, (w_lo, w_hi) in enumerate(((POOL_WINDOWS[0], POOL_WINDOWS[1]), (POOL_WINDOWS[2], POOL_WINDOWS[3]))):
        cs = slice(col * LANES, (col + 1) * LANES)
        pf = pbuf[cur:end, cs]
        a_s[POOL_PAD:end, :] = pbuf[POOL_PAD:end, cs] + pbuf[POOL_PAD - 1:end - 1, cs]
        have = 2
        while have < w_lo:
            b_s[POOL_PAD:end, :] = a_s[POOL_PAD:end, :] + a_s[POOL_PAD - have:end - have, :]
            a_s, b_s, have = b_s, a_s, 2 * have
        acc_lo = a_s[cur:end, :]
        while 2 * have < w_hi:
            b_s[POOL_PAD:end, :] = a_s[POOL_PAD:end, :] + a_s[POOL_PAD - have:end - have, :]
            a_s, b_s, have = b_s, a_s, 2 * have
        acc_hi = a_s[cur:end, :] + a_s[cur - have:end - have, :]
        cnt = jnp.where(lo_t, jnp.minimum(float(w_lo), pos1), jnp.minimum(float(w_hi), pos1))
        mixed.append(jnp.where(lo_t, acc_lo, acc_hi) / cnt - pf)
        yield
    ypool = _halves(_dot, jnp.concatenate(mixed, axis=1).astype(BF16), wpool_ref[...]) * pscale_ref[...]
    pbuf[POOL_PAD:cur, :] = pbuf[end - POOL_HIST:end, :]
    res['ypool'] = ypool
    yield


def _merge_tail(r, res):
    sg_s, yssm_s, yatt_s, wus_ref, wua_ref, wup_ref, wout_ref, x_ref, g_ref, npost_ref, xo_ref = (
        r.sg_s, r.yssm_s, r.yatt_s, r.wus_ref, r.wua_ref, r.wup_ref, r.wout_ref, r.x_ref, r.g_ref, r.npost_ref,
        r.xo_ref)
    ypool = res['ypool']
    per_branch = D_MODEL // GATE_CHUNK
    gate = lambda b: jnp.concatenate([sg_s[b * per_branch + c] for c in range(per_branch)], axis=1)
    yssm = jnp.concatenate([yssm_s[c] for c in range(D_SSM // LANES)], axis=1)
    merged = gate(0) * _dot(yssm.astype(BF16), wus_ref[...])
    yield
    merged += gate(1) * _dot(yatt_s[...], wua_ref[...])
    yield
    merged += gate(2) * _dot(ypool.astype(BF16), wup_ref[...])
    yield
    out = _dot(merged.astype(BF16), wout_ref[...])
    yield
    xo_ref[0] = x_ref[0] + _rms(out, g_ref[0] * npost_ref[...])


def _mixer_call(x, mods3, lp):
    b, t_len, d = x.shape
    nt = t_len // TT
    nsamp = mods3.shape[0] - b
    sps = SEQ_PER_STEP
    mod_spec = lambda col: pl.BlockSpec((sps, 1, D_MODEL), lambda i, j: (nsamp // sps + i, 0, col))
    x_spec = pl.BlockSpec((sps, TT, d), lambda i, j: (i, j, 0))
    consts = [lp['norm_mix_pre'], lp['norm_mix_post'], lp['w_in_a'], lp['w_in_g'],
              lp['w_bu'], lp['ab_re'], lp['ab_im'], lp['pw_re'], lp['pw_im'], lp['c_mat'],
              lp['ssm_d'], lp['ssm_w_glu'], lp['ssm_b_glu']]
    consts2 = [lp['pool_wbd'], lp['pool_scale'], lp['w_up_ssm'], lp['w_up_attn'], lp['w_up_pool'], lp['w_out']]
    in_specs = ([x_spec, mod_spec(0), mod_spec(1), mod_spec(2)]
                + [_const_spec(a.shape) for a in consts]
                + [pl.BlockSpec(memory_space=pltpu.SMEM)]
                + [_const_spec(a.shape) for a in consts2])
    per_b = lambda r, c: pl.BlockSpec((sps, r, c), lambda i, j: (i, 0, 0))
    out_shape = (jax.ShapeDtypeStruct((b, t_len, d), F32),
                 jax.ShapeDtypeStruct((b, 1, D_STATE), F32), jax.ShapeDtypeStruct((b, 1, D_STATE), F32),
                 jax.ShapeDtypeStruct((b, WINDOW, D_KV), F32), jax.ShapeDtypeStruct((b, WINDOW, D_KV), F32),
                 jax.ShapeDtypeStruct((b, POOL_HIST, D_POOL), F32))
    out_specs = (x_spec, per_b(1, D_STATE), per_b(1, D_STATE), per_b(WINDOW, D_KV), per_b(WINDOW, D_KV),
                 per_b(POOL_HIST, D_POOL))
    per_seq = lambda shape, dtype: pltpu.VMEM((sps,) + shape, dtype)
    scratch = [per_seq((TT, d), BF16), per_seq((TT, D_SMALL), F32), per_seq((TT, 2 * D_STATE), F32),
               per_seq((TT, 2 * D_STATE), BF16), per_seq((D_GATES // GATE_CHUNK, TT, GATE_CHUNK), F32),
               per_seq((D_SSM // LANES, TT, LANES), F32), per_seq((TT, D_SSM), F32),
               per_seq((TT, D_SSM), F32), per_seq((D_SSM // LANES, TT, LANES), F32),
               per_seq((TT, D_ATTN), BF16), per_seq((2, POOL_PAD + POOL_HIST + TT, LANES), F32),
               per_seq((WINDOW + TT, D_KV), BF16), per_seq((WINDOW + TT, D_KV), BF16),
               per_seq((POOL_PAD + POOL_HIST + TT, D_POOL), F32), per_seq((SUBLANES, D_STATE), F32),
               pltpu.VMEM((N_HEADS, Q_BLK, 2 * Q_BLK), F32),
               pltpu.VMEM((D_GATES // GATE_CHUNK, d, GATE_CHUNK), BF16)]
    return pl.pallas_call(
        _mixer_kernel,
        out_shape=out_shape,
        grid=(b // sps, nt),
        in_specs=in_specs,
        out_specs=out_specs,
        scratch_shapes=scratch,
        compiler_params=_params(("arbitrary", "arbitrary")),
        name="prompt_mixer",
    )(x, mods3, mods3, mods3, *consts, lp['attn_sinks'], *consts2)


def _mod_rows(ref, per_row):
    return ref[...] if per_row else ref[0]


def _ffn_kernel(per_row, x_ref, sh_ref, sc_ref, g_ref, npre_ref, npost_ref, wg_ref, wu_ref, wd_ref, o_ref):
    x = x_ref[...]
    h = (_rms(x, npre_ref[...] * (1.0 + _mod_rows(sc_ref, per_row))) + _mod_rows(sh_ref, per_row)).astype(BF16)
    f = None
    for c0, c1 in FF_CHUNKS:
        act = _silu(_dot(h, wg_ref[:, c0:c1])) * _dot(h, wu_ref[:, c0:c1])
        part = _dot(act.astype(BF16), wd_ref[c0:c1, :])
        f = part if f is None else f + part
    o_ref[...] = x + _rms(f, _mod_rows(g_ref, per_row) * npost_ref[...])


def _mod_specs(mods3, mods2, n_rows, tm, rows_per_seq, per_row):
    nsamp = mods3.shape[0] - (n_rows // rows_per_seq if not per_row else 0)
    if per_row:
        specs = [pl.BlockSpec((tm, D_MODEL), functools.partial(lambda col, i, *_: (i, col), col)) for col in (3, 4, 5)]
        return specs, [mods2] * 3
    tiles_per_seq = rows_per_seq // tm
    specs = [pl.BlockSpec((1, 1, D_MODEL),
                          functools.partial(lambda col, i, *_: (nsamp + i // tiles_per_seq, 0, col), col))
             for col in (3, 4, 5)]
    return specs, [mods3] * 3


def _ffn_call(x2, mods3, mods2, lp, rows_per_seq, per_row):
    n = x2.shape[0]
    tm = n if per_row else FFN_TM
    mspecs, mops = _mod_specs(mods3, mods2, n, tm, rows_per_seq, per_row)
    consts = [lp['norm_ffn_pre'], lp['norm_ffn_post'], lp['ffn_w_gate'], lp['ffn_w_up'], lp['ffn_w_down']]
    x_spec = pl.BlockSpec((tm, D_MODEL), lambda i: (i, 0))
    return pl.pallas_call(
        functools.partial(_ffn_kernel, per_row),
        out_shape=jax.ShapeDtypeStruct(x2.shape, F32),
        grid=(n // tm,),
        in_specs=[x_spec] + mspecs + [_const_spec(a.shape) for a in consts],
        out_specs=x_spec,
        compiler_params=_params(("arbitrary",)),
        name="ffn_rows" if per_row else "ffn_seq",
    )(x2, *mops, *consts)


def _moe_kernel(per_row, x_ref, sh_ref, sc_ref, g_ref, npre_ref, npost_ref, wr_ref, br_ref,
                wg_ref, wu_ref, wd_ref, o_ref, h_s, gates_s, acc_s):
    e = pl.program_id(1)
    lane_e = lax.broadcasted_iota(jnp.int32, gates_s.shape, 1)

    @pl.when(e == 0)
    def _():
        x = x_ref[...]
        h = _rms(x, npre_ref[...]) * (1.0 + _mod_rows(sc_ref, per_row)) + _mod_rows(sh_ref, per_row)
        hb = h.astype(BF16)
        h_s[...] = hb
        logits = _dot(hb, wr_ref[...]) + br_ref[...]
        gates_s[...], _ = _top2_gates(logits, lane_e.astype(F32))
        acc_s[...] = jnp.zeros_like(acc_s)

    h = h_s[...]
    act = _silu(_dot(h, wg_ref[0, 0])) * _dot(h, wu_ref[0, 0])
    y = _dot(act.astype(BF16), wd_ref[0, 0])
    ge = jnp.sum(jnp.where(lane_e == e, gates_s[...], 0.0), axis=-1, keepdims=True)
    acc_s[...] += ge * y

    @pl.when(e == N_EXPERTS - 1)
    def _():
        o_ref[...] = x_ref[...] + _mod_rows(g_ref, per_row) * _rms(acc_s[...], npost_ref[...])


def _moe_call(x2, mods3, mods2, lp, rows_per_seq, per_row):
    n = x2.shape[0]
    tm = n if per_row else MOE_TM
    mspecs, mops = _mod_specs(mods3, mods2, n, tm, rows_per_seq, per_row)
    consts = [lp['norm_ffn_pre'], lp['norm_ffn_post'], lp['moe_w_router'], lp['moe_b_router']]
    x_spec = pl.BlockSpec((tm, D_MODEL), lambda i, e: (i, 0))
    w_spec = lambda a: pl.BlockSpec((1, 1) + a.shape[2:], lambda i, e: (0, e, 0, 0))
    experts = [lp['moe_w_gate'], lp['moe_w_up'], lp['moe_w_down']]
    return pl.pallas_call(
        functools.partial(_moe_kernel, per_row),
        out_shape=jax.ShapeDtypeStruct(x2.shape, F32),
        grid=(n // tm, N_EXPERTS),
        in_specs=[x_spec] + mspecs + [_const_spec(a.shape) for a in consts] + [w_spec(a) for a in experts],
        out_specs=x_spec,
        scratch_shapes=[pltpu.VMEM((tm, D_MODEL), BF16), pltpu.VMEM((tm, N_EXPERTS), F32),
                        pltpu.VMEM((tm, D_MODEL), F32)],
        compiler_params=_params(("arbitrary", "arbitrary")),
        name="moe_rows" if per_row else "moe_seq",
    )(x2, *mops, *consts, *experts)


def _top2_gates(logits, lane_f):
    n_lanes = float(logits.shape[-1])
    v1 = jnp.max(logits, axis=-1, keepdims=True)
    i1 = jnp.min(jnp.where(logits == v1, lane_f, n_lanes), axis=-1, keepdims=True)
    rest = jnp.where(lane_f == i1, -jnp.inf, logits)
    v2 = jnp.max(rest, axis=-1, keepdims=True)
    i2 = jnp.min(jnp.where(rest == v2, lane_f, n_lanes), axis=-1, keepdims=True)
    e2 = jnp.exp(v2 - v1)
    den = 1.0 + e2
    gates = jnp.where(lane_f == i1, 1.0 / den, 0.0) + jnp.where(lane_f == i2, e2 / den, 0.0)
    return gates, (lane_f == i1) | (lane_f == i2)


def _moe_routed_kernel(seq_len, n_rows, x_ref, sh0_ref, sh1_ref, sc0_ref, sc1_ref, g0_ref, g1_ref,
                       npre_ref, npost_ref, wr_ref, br_ref, wg_ref, wu_ref, wd_ref, o_ref,
                       h_s, before_s, dest_s, gate_s, acc_s, cnt_s):
    i = pl.program_id(0)
    e = pl.program_id(1)
    tm = x_ref.shape[0]
    row = lax.broadcasted_iota(jnp.int32, (tm, 1), 0) + i * tm
    if n_rows % tm == 0:
        keep = lambda v: v
    else:
        valid = row < n_rows
        keep = lambda v: jnp.where(valid, v, 0.0)
    if seq_len % tm == 0:
        pick = lambda a_ref, b_ref: a_ref[0]
    else:
        second = row >= ((i * tm) // seq_len + 1) * seq_len
        pick = lambda a_ref, b_ref: jnp.where(second, b_ref[0], a_ref[0])

    @pl.when((i == 0) & (e == 0))
    def _():
        r_i = lax.broadcasted_iota(jnp.int32, before_s.shape, 0)
        c_i = lax.broadcasted_iota(jnp.int32, before_s.shape, 1)
        before_s[...] = jnp.where(c_i < r_i, 1.0, 0.0).astype(BF16)

    @pl.when(e == 0)
    def _():
        x = keep(x_ref[...])
        h = _rms(x, npre_ref[...] * (1.0 + pick(sc0_ref, sc1_ref))) + pick(sh0_ref, sh1_ref)
        hb = h.astype(BF16)
        h_s[...] = hb
        lane_f = lax.broadcasted_iota(jnp.int32, (tm, LANES), 1).astype(F32)
        logits = jnp.where(lane_f < N_EXPERTS, _halves(_dot, hb, wr_ref[...]) + br_ref[...], -jnp.inf)
        gates, sel = _top2_gates(logits, lane_f)
        self = keep(jnp.where(sel, 1.0, 0.0))
        counts = jnp.zeros((1, LANES), F32)
        dest_blocks = []
        for k in range(tm // MOE_RANK_BLOCK):
            self_k = self[k * MOE_RANK_BLOCK:(k + 1) * MOE_RANK_BLOCK, :]
            rank_k = _dot(before_s[...], self_k.astype(BF16)) + counts
            dest_blocks.append(jnp.where(self_k > 0.0, rank_k, -1.0))
            counts = counts + jnp.sum(self_k, axis=0, keepdims=True)
        dest = jnp.concatenate(dest_blocks, axis=0)
        dest_s[...] = dest.T[0:SUBLANES, :]
        gate_s[...] = gates.T[0:SUBLANES, :]
        for j in range(N_EXPERTS):
            cnt_s[j] = counts[0, j].astype(jnp.int32)
        acc_s[...] = jnp.zeros_like(acc_s)

    dest_row = dest_s[pl.ds(e, 1), :]
    gate_row = gate_s[pl.ds(e, 1), :]
    def chunk(base, size):
        slot = lax.broadcasted_iota(jnp.int32, (size, tm), 0).astype(F32) + base.astype(F32)
        onehot = jnp.where(slot == dest_row, 1.0, 0.0)
        gate_packed = jnp.sum(onehot * gate_row, axis=-1, keepdims=True)
        pack = onehot.astype(BF16)
        xg = _dot(pack, h_s[...]).astype(BF16)
        act = _silu(_dot(xg, wg_ref[0, 0])) * _dot(xg, wu_ref[0, 0])
        y = _dot(act.astype(BF16), wd_ref[0, 0]) * gate_packed
        acc_s[...] += lax.dot_general(pack, y.astype(BF16), (((0,), (0,)), ((), ())), preferred_element_type=F32)

    shift = MOE_CAP.bit_length() - 1
    cnt = cnt_s[e]
    n_full = lax.shift_right_logical(cnt, shift)
    rem = cnt - lax.shift_left(n_full, shift)
    long_last = (n_full >= 1) & (rem >= 1) & (rem <= MOE_TAIL)
    n_plain = jnp.where(long_last, n_full - 1, n_full + jnp.where(rem >= 1, 1, 0))

    def plain_chunk(c, _):
        chunk(c * MOE_CAP, MOE_CAP)
        return 0

    lax.fori_loop(0, n_plain, plain_chunk, 0)

    @pl.when(long_last)
    def _():
        chunk(n_plain * MOE_CAP, MOE_CAP + MOE_TAIL)

    @pl.when(e == N_EXPERTS - 1)
    def _():
        o_ref[...] = keep(x_ref[...]) + _rms(acc_s[...], pick(g0_ref, g1_ref) * npost_ref[...])


def _moe_routed_call(x2, mods3, lp, seq_len):
    n = x2.shape[0]
    tm = MOE_SP_TM
    assert tm % MOE_RANK_BLOCK == 0
    n_seq = n // seq_len
    nsamp = mods3.shape[0] - n_seq

    def mod_spec(col, nxt):
        return pl.BlockSpec((1, 1, D_MODEL),
                            lambda i, e: (nsamp + jnp.minimum((i * tm) // seq_len + nxt, n_seq - 1), 0, col))

    mspecs = [mod_spec(col, nxt) for col in (3, 4, 5) for nxt in (0, 1)]
    pad = LANES - N_EXPERTS
    consts = [lp['norm_ffn_pre'], lp['norm_ffn_post'],
              jnp.pad(lp['moe_w_router'], ((0, 0), (0, pad))), jnp.pad(lp['moe_b_router'], ((0, 0), (0, pad)))]
    x_spec = pl.BlockSpec((tm, D_MODEL), lambda i, e: (i, 0))
    w_spec = lambda a: pl.BlockSpec((1, 1) + a.shape[2:], lambda i, e: (0, e, 0, 0))
    experts = [lp['moe_w_gate'], lp['moe_w_up'], lp['moe_w_down']]
    return pl.pallas_call(
        functools.partial(_moe_routed_kernel, seq_len, n),
        out_shape=jax.ShapeDtypeStruct(x2.shape, F32),
        grid=(pl.cdiv(n, tm), N_EXPERTS),
        in_specs=[x_spec] + mspecs + [_const_spec(a.shape) for a in consts] + [w_spec(a) for a in experts],
        out_specs=x_spec,
        scratch_shapes=[pltpu.VMEM((tm, D_MODEL), BF16), pltpu.VMEM((MOE_RANK_BLOCK, MOE_RANK_BLOCK), BF16),
                        pltpu.VMEM((SUBLANES, tm), F32), pltpu.VMEM((SUBLANES, tm), F32),
                        pltpu.VMEM((tm, D_MODEL), F32), pltpu.SMEM((N_EXPERTS,), jnp.int32)],
        compiler_params=_params(("arbitrary", "arbitrary")),
        name="moe_routed",
    )(x2, *([mods3] * 6), *consts, *experts)


def _sproj_kernel(x_ref, sh_ref, sc_ref, npre_ref, wa_ref, wg_ref, pa_o, pg_o):
    h = (_rms(x_ref[...], npre_ref[...]) * (1.0 + sc_ref[...]) + sh_ref[...]).astype(BF16)
    pa_o[...] = _dot(h, wa_ref[...])
    pg_o[...] = _dot(h, wg_ref[...])


def _sproj_call(xs, mods2, lp):
    n = xs.shape[0]
    mod_spec = lambda col: pl.BlockSpec((n, D_MODEL), lambda i: (0, col))
    consts = [lp['norm_mix_pre'], lp['w_in_a'], lp['w_in_g']]
    return pl.pallas_call(
        _sproj_kernel,
        out_shape=(jax.ShapeDtypeStruct((n, D_SMALL), F32), jax.ShapeDtypeStruct((n, D_GATES), F32)),
        grid=(1,),
        in_specs=[_const_spec(xs.shape), mod_spec(0), mod_spec(1)] + [_const_spec(a.shape) for a in consts],
        out_specs=(_const_spec((n, D_SMALL)), _const_spec((n, D_GATES))),
        compiler_params=_params(("arbitrary",)),
        name="sample_proj",
    )(xs, mods2, mods2, *consts)


N_SMIX_IN = 22


def _smix_kernel(*refs):
    (u_ref, q_ref, kn_ref, vn_ref, knt_ref, vnt_ref, pin_ref, h0r_ref, h0i_ref, ck_ref, cv_ref, hist_ref,
     wbu_ref, abre_ref, abim_ref, cm_ref, dsk_ref, wglu_ref, bglu_ref, sinks_ref,
     wpool_ref, pscale_ref) = refs[:N_SMIX_IN]
    yssm_o, yatt_o, ypool_o, hr_o, hi_o, ck_o, cv_o, hist_o = refs[-8:]
    u = u_ref[...]
    bu = _dot(u.astype(BF16), wbu_ref[...])
    abr, abi = abre_ref[...], abim_ref[...]
    h0r, h0i = h0r_ref[...], h0i_ref[...]
    hr = bu[:, 0:D_STATE] + (abr * h0r - abi * h0i)
    hi = bu[:, D_STATE:2 * D_STATE] + (abr * h0i + abi * h0r)
    hr_o[...] = hr
    hi_o[...] = hi
    y = _dot(jnp.concatenate([hr, hi], axis=1).astype(BF16), cm_ref[...])
    yssm_o[...] = _ssm_out(y, u, dsk_ref[...], wglu_ref[...], bglu_ref[...])

    q = q_ref[...] * ATTN_SCALE
    kn, vn = kn_ref[...], vn_ref[...]
    kc = ck_ref[...].astype(BF16)
    vc = cv_ref[...].astype(BF16)
    lo = _lane_lo(q.shape)
    j_i = lax.broadcasted_iota(jnp.int32, (1, GQA, 1), 1)
    dist_c = (WINDOW - lax.broadcasted_iota(jnp.int32, (1, 1, WINDOW), 2)).astype(F32)
    halves = []
    for half in range(N_KV_HEADS):
        keep = lo if half == 0 else jnp.logical_not(lo)
        qm = jnp.where(keep, q, 0.0)
        slope = jnp.zeros((1, GQA, 1), F32)
        snk = jnp.zeros((1, GQA, 1), F32)
        for j in range(GQA):
            slope = jnp.where(j_i == j, SLOPES[half * GQA + j], slope)
            snk = jnp.where(j_i == j, sinks_ref[half * GQA + j], snk)
        s = jnp.einsum('bhd,bds->bhs', qm.astype(BF16), kc, preferred_element_type=F32) - slope * dist_c
        s_new = jnp.sum(qm * kn, axis=-1, keepdims=True)
        m = jnp.maximum(jnp.maximum(jnp.max(s, axis=-1, keepdims=True), s_new), snk)
        p = jnp.exp(s - m)
        p_new = jnp.exp(s_new - m)
        den = jnp.sum(p, axis=-1, keepdims=True) + p_new + jnp.exp(snk - m)
        o = jnp.einsum('bhs,bds->bhd', p.astype(BF16), vc, preferred_element_type=F32) + p_new * vn
        halves.append(o / den)
    yatt_o[...] = jnp.where(lo, halves[0], halves[1])

    bc = ck_ref.shape[0]
    lane_w = lax.broadcasted_iota(jnp.int32, (D_KV, WINDOW), 1)
    lane_b = lax.broadcasted_iota(jnp.int32, knt_ref.shape, 1)
    for src, new_t, dst in ((ck_ref, knt_ref, ck_o), (cv_ref, vnt_ref, cv_o)):
        for b in range(bc):
            col = jnp.sum(jnp.where(lane_b == pl.program_id(0) * bc + b, new_t[...], 0.0), axis=-1, keepdims=True)
            dst[b] = jnp.where(lane_w == WINDOW - 1, col, pltpu.roll(src[b], WINDOW - 1, axis=1))

    pin = pin_ref[...]
    lo2 = _lane_lo((pin.shape[0], LANES))
    mixed = []
    for col, (w_lo, w_hi) in enumerate(((POOL_WINDOWS[0], POOL_WINDOWS[1]), (POOL_WINDOWS[2], POOL_WINDOWS[3]))):
        cs = slice(col * LANES, (col + 1) * LANES)
        pf = pin[:, cs]
        acc = pf
        for k in range(1, w_lo):
            acc = acc + hist_ref[POOL_BUF - k, :, cs]
        acc_lo = acc
        for k in range(w_lo, w_hi):
            acc = acc + hist_ref[POOL_BUF - k, :, cs]
        cnt_lo = float(min(w_lo, PAST_LEN + 1))
        cnt_hi = float(min(w_hi, PAST_LEN + 1))
        mixed.append(jnp.where(lo2, acc_lo / cnt_lo, acc / cnt_hi) - pf)
    ypool_o[...] = _dot(jnp.concatenate(mixed, axis=1).astype(BF16), wpool_ref[...]) * pscale_ref[...]
    hist_o[0:POOL_BUF - 1] = hist_ref[1:POOL_BUF]
    hist_o[POOL_BUF - 1] = pin


def _smix_call(layer, u, q3, kn3, vn3, knt, vnt, pin, h0r, h0i, ck, cv, prev_windows, hist_t, lp):
    n = u.shape[0]
    bc = SAMPLE_BC
    rows2 = lambda c: pl.BlockSpec((bc, c), lambda i: (i, 0))
    rows3 = lambda r, c: pl.BlockSpec((bc, r, c), lambda i: (i, 0, 0))
    win_spec = pl.BlockSpec((None, bc, D_KV, WINDOW), lambda i: (layer, i, 0, 0))
    consts = [lp['w_bu'], lp['ab_re'], lp['ab_im'], lp['c_mat'], lp['ssm_d'], lp['ssm_w_glu'], lp['ssm_b_glu']]
    consts2 = [lp['pool_wbd'], lp['pool_scale']]
    hist_spec = pl.BlockSpec((POOL_BUF, bc, D_POOL), lambda i: (0, i, 0))
    in_specs = ([rows2(D_SSM), rows3(GQA, LANES), rows3(1, LANES), rows3(1, LANES),
                 _const_spec(knt.shape), _const_spec(vnt.shape), rows2(D_POOL),
                 rows2(D_STATE), rows2(D_STATE), win_spec, win_spec, hist_spec]
                + [_const_spec(a.shape) for a in consts]
                + [pl.BlockSpec(memory_space=pltpu.SMEM)]
                + [_const_spec(a.shape) for a in consts2]
                + [pl.BlockSpec(memory_space=pl.ANY) for _ in prev_windows])
    assert len(in_specs) == N_SMIX_IN + len(prev_windows)
    aliases = {N_SMIX_IN + k: 5 + k for k in range(len(prev_windows))}
    return pl.pallas_call(
        _smix_kernel,
        out_shape=(jax.ShapeDtypeStruct((n, D_SSM), F32), jax.ShapeDtypeStruct((n, GQA, LANES), F32),
                   jax.ShapeDtypeStruct((n, D_POOL), F32),
                   jax.ShapeDtypeStruct((n, D_STATE), F32), jax.ShapeDtypeStruct((n, D_STATE), F32),
                   jax.ShapeDtypeStruct(ck.shape, F32), jax.ShapeDtypeStruct(cv.shape, F32),
                   jax.ShapeDtypeStruct(hist_t.shape, F32)),
        grid=(n // bc,),
        in_specs=in_specs,
        out_specs=(rows2(D_SSM), rows3(GQA, LANES), rows2(D_POOL), rows2(D_STATE), rows2(D_STATE),
                   win_spec, win_spec, hist_spec),
        input_output_aliases=aliases,
        compiler_params=_params(("arbitrary",)),
        name="sample_mix",
    )(u, q3, kn3, vn3, knt, vnt, pin, h0r, h0i, ck, cv, hist_t, *consts, lp['attn_sinks'], *consts2, *prev_windows)


def _smerge_kernel(x_ref, g_ref, pg_ref, yssm_ref, yatt_ref, ypool_ref, npost_ref,
                   wus_ref, wua_ref, wup_ref, wout_ref, o_ref):
    merged = _sigmoid(pg_ref[:, 0:D_MODEL]) * _dot(yssm_ref[...].astype(BF16), wus_ref[...])
    merged += _sigmoid(pg_ref[:, D_MODEL:2 * D_MODEL]) * _dot(yatt_ref[...].astype(BF16), wua_ref[...])
    merged += _sigmoid(pg_ref[:, 2 * D_MODEL:3 * D_MODEL]) * _dot(ypool_ref[...].astype(BF16), wup_ref[...])
    out = _dot(merged.astype(BF16), wout_ref[...])
    o_ref[...] = x_ref[...] + g_ref[...] * _rms(out, npost_ref[...])


def _smerge_call(xs, mods2, pg, yssm, yatt, ypool, lp):
    n = xs.shape[0]
    ops = [xs, mods2, pg, yssm, yatt, ypool, lp['norm_mix_post'],
           lp['w_up_ssm'], lp['w_up_attn'], lp['w_up_pool'], lp['w_out']]
    in_specs = [_const_spec(a.shape) for a in ops]
    in_specs[1] = pl.BlockSpec((n, D_MODEL), lambda i: (0, 2))
    return pl.pallas_call(
        _smerge_kernel,
        out_shape=jax.ShapeDtypeStruct(xs.shape, F32),
        grid=(1,),
        in_specs=in_specs,
        out_specs=_const_spec(xs.shape),
        compiler_params=_params(("arbitrary",)),
        name="sample_merge",
    )(*ops)


def _block_diag(w):
    g, a, b = w.shape
    eye = jnp.eye(g, dtype=w.dtype)
    return jnp.einsum('gab,gh->gahb', w, eye).reshape(g * a, g * b)


def _pair_heads(w, axis):
    shape = w.shape
    w = w.reshape(shape[:axis] + (N_KV_HEADS, GQA, HEAD_DIM) + shape[axis + 1:])
    return jnp.swapaxes(w, axis, axis + 1).reshape(shape)


def _layer_params(l, p, prep):
    ab_re, ab_im, bb_re, bb_im, pw_re, pw_im = prep
    row = lambda a: a.reshape(1, -1)
    w_in = p['w_in'][l]
    q_cols = _pair_heads(w_in[:, D_SSM:D_SSM + D_ATTN], 1)
    w_in_a = jnp.concatenate([w_in[:, 0:D_SSM], q_cols, w_in[:, D_SSM + D_ATTN:D_SMALL]], axis=1)
    lp = {
        'norm_mix_pre': row(p['norm_mix_pre'][l]), 'norm_mix_post': row(p['norm_mix_post'][l]),
        'norm_ffn_pre': row(p['norm_ffn_pre'][l]), 'norm_ffn_post': row(p['norm_ffn_post'][l]),
        'w_in_a': w_in_a.astype(BF16),
        'w_in_g': w_in[:, D_SMALL:].astype(BF16),
        'w_bu': jnp.concatenate([_block_diag(bb_re[l]), _block_diag(bb_im[l])], axis=1).astype(BF16),
        'ab_re': ab_re[l].reshape(1, D_STATE), 'ab_im': ab_im[l].reshape(1, D_STATE),
        'pw_re': pw_re[l].reshape(SCAN_LEN, D_STATE), 'pw_im': pw_im[l].reshape(SCAN_LEN, D_STATE),
        'c_mat': jnp.concatenate([_block_diag(jnp.swapaxes(p['ssm_c_re'][l], 1, 2)),
                                  _block_diag(jnp.swapaxes(-p['ssm_c_im'][l], 1, 2))], axis=0).astype(BF16),
        'ssm_d': row(p['ssm_d'][l]), 'ssm_w_glu': p['ssm_w_glu'][l].astype(BF16), 'ssm_b_glu': row(p['ssm_b_glu'][l]),
        'attn_sinks': p['attn_sinks'][l],
        'pool_wbd': _block_diag(p['pool_w'][l]).astype(BF16), 'pool_scale': row(p['pool_scale'][l]),
        'w_up_ssm': p['w_up_ssm'][l].astype(BF16), 'w_up_attn': _pair_heads(p['w_up_attn'][l], 0).astype(BF16),
        'w_up_pool': p['w_up_pool'][l].astype(BF16), 'w_out': p['w_out'][l].astype(BF16),
    }
    j = l // 2
    if l % 2 == 1:
        lp.update({'moe_w_router': p['moe_w_router'][j].astype(BF16), 'moe_b_router': row(p['moe_b_router'][j]),
                   'moe_w_gate': p['moe_w_gate'][j:j + 1].astype(BF16), 'moe_w_up': p['moe_w_up'][j:j + 1].astype(BF16),
                   'moe_w_down': p['moe_w_down'][j:j + 1].astype(BF16)})
    else:
        lp.update({'ffn_w_gate': p['ffn_w_gate'][j].astype(BF16), 'ffn_w_up': p['ffn_w_up'][j].astype(BF16),
                   'ffn_w_down': p['ffn_w_down'][j].astype(BF16)})
    return lp


def kernel(x_prompt, x_sample, state_ssm_re, state_ssm_im, cache_win_k, cache_win_v, state_pool,
           c_prompt, c_sample, w_ada, b_ada, norm_mix_pre, norm_mix_post, norm_ffn_pre, norm_ffn_post,
           w_in, ssm_a_re, ssm_a_im, ssm_log_dt, ssm_b_re, ssm_b_im, ssm_c_re, ssm_c_im, ssm_d,
           ssm_w_glu, ssm_b_glu, attn_sinks, pool_w, pool_scale, w_up_ssm, w_up_attn, w_up_pool, w_out,
           ffn_w_gate, ffn_w_up, ffn_w_down, moe_w_router, moe_b_router, moe_w_gate, moe_w_up, moe_w_down):
    p = dict(norm_mix_pre=norm_mix_pre, norm_mix_post=norm_mix_post, norm_ffn_pre=norm_ffn_pre,
             norm_ffn_post=norm_ffn_post, w_in=w_in, ssm_c_re=ssm_c_re, ssm_c_im=ssm_c_im, ssm_d=ssm_d,
             ssm_w_glu=ssm_w_glu, ssm_b_glu=ssm_b_glu, attn_sinks=attn_sinks, pool_w=pool_w, pool_scale=pool_scale,
             w_up_ssm=w_up_ssm, w_up_attn=w_up_attn, w_up_pool=w_up_pool, w_out=w_out,
             ffn_w_gate=ffn_w_gate, ffn_w_up=ffn_w_up, ffn_w_down=ffn_w_down, moe_w_router=moe_w_router,
             moe_b_router=moe_b_router, moe_w_gate=moe_w_gate, moe_w_up=moe_w_up, moe_w_down=moe_w_down)
    depth = w_in.shape[0]
    b, t_len, d = x_prompt.shape
    ns = x_sample.shape[0]

    mods, mods_seq = _ada_call(jnp.concatenate([c_sample, c_prompt], axis=0), b, w_ada, b_ada)
    prep = _ssm_prep_call(ssm_a_re, ssm_a_im, ssm_log_dt, ssm_b_re, ssm_b_im)

    xp = x_prompt
    xs = x_sample.reshape(ns, d)
    p_out = ([], [], [], [], [])
    s_out = ([], [], [], [], [])
    win_minor = lambda c: jnp.transpose(c, (0, 1, 3, 4, 2)).reshape(depth, ns, D_KV, -1)
    win_major = lambda c: jnp.transpose(c.reshape(depth, ns, N_KV_HEADS, HEAD_DIM, -1), (0, 1, 4, 2, 3))
    old_windows = [win_minor(cache_win_k), win_minor(cache_win_v)]
    new_windows = []
    for l in range(depth):
        lp = _layer_params(l, p, prep)
        mods2 = mods[l]
        mods3 = mods_seq[l]

        xp, hre, him, kwin, vwin, plast = _mixer_call(xp, mods3, lp)
        x2 = xp.reshape(b * t_len, d)
        if l % 2 == 1:
            x2 = _moe_routed_call(x2, mods3, lp, t_len)
        else:
            x2 = _ffn_call(x2, mods3, mods2, lp, t_len, False)
        xp = x2.reshape(b, t_len, d)
        p_out[0].append(hre.reshape(b, SSM_GROUPS, SSM_STATE))
        p_out[1].append(him.reshape(b, SSM_GROUPS, SSM_STATE))
        p_out[2].append(kwin.reshape(b, WINDOW, N_KV_HEADS, HEAD_DIM))
        p_out[3].append(vwin.reshape(b, WINDOW, N_KV_HEADS, HEAD_DIM))
        p_out[4].append(plast[:, POOL_HIST - POOL_BUF:, :])

        pa, pg = _sproj_call(xs, mods2, lp)
        u = pa[:, 0:D_SSM]
        q3 = pa[:, D_SSM:D_SSM + D_ATTN].reshape(ns, GQA, LANES)
        k_new = pa[:, D_SSM + D_ATTN:D_SSM + D_ATTN + D_KV]
        v_new = pa[:, D_SSM + D_ATTN + D_KV:D_SSM + D_ATTN + 2 * D_KV]
        pin = pa[:, D_SMALL - D_POOL:]
        yssm, yatt3, ypool, hr, hi, *new_windows, hist_new = _smix_call(
            l, u, q3, k_new.reshape(ns, 1, D_KV), v_new.reshape(ns, 1, D_KV), k_new.T, v_new.T, pin,
            state_ssm_re[l].reshape(ns, D_STATE), state_ssm_im[l].reshape(ns, D_STATE),
            *old_windows, new_windows, jnp.swapaxes(state_pool[l], 0, 1), lp)
        xs = _smerge_call(xs, mods2, pg, yssm, yatt3.reshape(ns, D_ATTN), ypool, lp)
        xs = (_moe_call if l % 2 == 1 else _ffn_call)(xs, mods3, mods2, lp, 1, True)
        s_out[0].append(hr.reshape(ns, SSM_GROUPS, SSM_STATE))
        s_out[1].append(hi.reshape(ns, SSM_GROUPS, SSM_STATE))
        s_out[4].append(jnp.swapaxes(hist_new, 0, 1))

    stack = lambda xs_: [jnp.stack(a, axis=0) for a in xs_]
    p_ssm_re, p_ssm_im, p_win_k, p_win_v, p_pool = stack(p_out)
    s_ssm_re, s_ssm_im, s_pool = stack([s_out[0], s_out[1], s_out[4]])
    s_win_k, s_win_v = [win_major(w) for w in new_windows]
    return (xp, xs.reshape(x_sample.shape), p_ssm_re, p_ssm_im, p_win_k, p_win_v, p_pool,
            s_ssm_re, s_ssm_im, s_win_k, s_win_v, s_pool)
```

```python
import functools
import math
import types

import jax
import jax.numpy as jnp
from jax import lax
from jax.experimental import pallas as pl
from jax.experimental.pallas import tpu as pltpu

F32 = jnp.float32
BF16 = jnp.bfloat16

D_MODEL = 1024
D_SSM = 256
SSM_GROUP = 16
SSM_GROUPS = 16
SSM_STATE = 64
D_STATE = SSM_GROUPS * SSM_STATE
HEAD_DIM = 64
N_HEADS = 8
N_KV_HEADS = 2
GQA = N_HEADS // N_KV_HEADS
D_ATTN = N_HEADS * HEAD_DIM
D_KV = N_KV_HEADS * HEAD_DIM
WINDOW = 128
ATTN_SCALE = HEAD_DIM ** -0.5
NEG_INF = -1e30
LOG2E = math.log2(math.e)
D_POOL = 256
POOL_WINDOWS = (2, 4, 8, 16)
POOL_BUF = 15
POOL_HIST = 16
POOL_PAD = 8
N_MOD = 6
D_FF = 2816
N_EXPERTS = 8
D_EXPERT = 1024
RMS_EPS = 1e-6
PAST_LEN = 16384
D_SMALL = D_SSM + D_ATTN + 2 * D_KV + D_POOL
D_GATES = 3 * D_MODEL
SLOPES = tuple(2.0 ** (-8.0 * (h + 1) / N_HEADS) for h in range(N_HEADS))

SUBLANES = 8
LANES = 128
VMEM_LIMIT = 56 * 1024 * 1024

TT = 512
SEQ_PER_STEP = 1
SCAN_LEN = TT // SUBLANES
Q_BLK = 128
GATE_CHUNK = 512
PROJ_CHUNK = 256
BU_CHUNK = 512
MIXER_LAG = 0
GATE_LOOP_A = 4
GATE_LOOP_B = D_GATES // GATE_CHUNK - GATE_LOOP_A
FFN_TM = 512
MOE_TM = 1024
MOE_SP_TM = 1024
MOE_CAP = 256
MOE_TAIL = 128
MOE_RANK_BLOCK = 256
SAMPLE_BC = 32
ADA_TN = 1536
FF_CHUNKS = ((0, 1024), (1024, 2048), (2048, D_FF))


def _rms(x, gain):
    return x * lax.rsqrt(jnp.mean(x * x, axis=-1, keepdims=True) + RMS_EPS) * gain


def _sigmoid(x):
    return 1.0 / (1.0 + jnp.exp(-x))


def _silu(x):
    return x * _sigmoid(x)


def _dot(a, b):
    return jnp.dot(a, b, preferred_element_type=F32)


def _dot_t(a, b):
    return lax.dot_general(a, b, (((1,), (1,)), ((), ())), preferred_element_type=F32)


def _halves(dot, a, b):
    half = a.shape[0] // 2
    return jnp.concatenate([dot(a[:half], b), dot(a[half:], b)], axis=0)


def _const_spec(shape):
    nd = len(shape)
    return pl.BlockSpec(shape, lambda *_: (0,) * nd)


def _params(sem):
    return pltpu.CompilerParams(dimension_semantics=sem, vmem_limit_bytes=VMEM_LIMIT)


def _ada_kernel(c_ref, w_ref, b_ref, o_ref, oseq_ref):
    c = c_ref[...]
    s = _silu(c).astype(BF16)
    mod = _dot(s, w_ref[0].astype(BF16)) + b_ref[0]
    o_ref[0] = mod
    n_seq = oseq_ref.shape[1]
    first = mod.shape[0] - n_seq
    for s_i in range(n_seq):
        oseq_ref[0, s_i] = mod[first + s_i:first + s_i + 1, :]


def _ada_call(c_all, n_seq, w_ada, b_ada):
    depth = w_ada.shape[0]
    rows = c_all.shape[0]
    n = N_MOD * D_MODEL
    return pl.pallas_call(
        _ada_kernel,
        out_shape=(jax.ShapeDtypeStruct((depth, rows, n), F32), jax.ShapeDtypeStruct((depth, n_seq, 1, n), F32)),
        grid=(depth, n // ADA_TN),
        in_specs=[
            pl.BlockSpec((rows, D_MODEL), lambda l, j: (0, 0)),
            pl.BlockSpec((1, D_MODEL, ADA_TN), lambda l, j: (l, 0, j)),
            pl.BlockSpec((1, 1, ADA_TN), lambda l, j: (l, 0, j)),
        ],
        out_specs=(pl.BlockSpec((1, rows, ADA_TN), lambda l, j: (l, 0, j)),
                   pl.BlockSpec((1, n_seq, 1, ADA_TN), lambda l, j: (l, 0, 0, j))),
        compiler_params=_params(("arbitrary", "arbitrary")),
        name="ada_mod",
    )(c_all, w_ada, b_ada.reshape(depth, 1, n))


def _ssm_prep_kernel(are_ref, aim_ref, ldt_ref, bre_ref, bim_ref,
                     abre_o, abim_o, bbre_o, bbim_o, pwre_o, pwim_o):
    ar = are_ref[0]
    ai = aim_ref[0]
    dt = jnp.exp(ldt_ref[0])
    mag = jnp.exp(ar * dt)
    ab_re = mag * jnp.cos(ai * dt)
    ab_im = mag * jnp.sin(ai * dt)
    den = ar * ar + ai * ai
    n_re = ab_re - 1.0
    f_re = (n_re * ar + ab_im * ai) / den
    f_im = (ab_im * ar - n_re * ai) / den
    br = bre_ref[0]
    bi = bim_ref[0]
    bbre_o[0] = f_re * br - f_im * bi
    bbim_o[0] = f_re * bi + f_im * br
    abre_o[0] = ab_re
    abim_o[0] = ab_im
    cr, ci = ab_re, ab_im
    for t in range(SCAN_LEN):
        pwre_o[0, t] = cr
        pwim_o[0, t] = ci
        cr, ci = cr * ab_re - ci * ab_im, cr * ab_im + ci * ab_re


def _ssm_prep_call(a_re, a_im, log_dt, b_re, b_im):
    depth = a_re.shape[0]
    g, p, c = SSM_GROUPS, SSM_STATE, SSM_GROUP
    a4 = lambda a: a.reshape(depth, g, 1, p)
    spec_a = pl.BlockSpec((1, g, 1, p), lambda l: (l, 0, 0, 0))
    spec_b = pl.BlockSpec((1, g, c, p), lambda l: (l, 0, 0, 0))
    spec_pw = pl.BlockSpec((1, SCAN_LEN, g, 1, p), lambda l: (l, 0, 0, 0, 0))
    return pl.pallas_call(
        _ssm_prep_kernel,
        out_shape=(jax.ShapeDtypeStruct((depth, g, 1, p), F32),) * 2
        + (jax.ShapeDtypeStruct((depth, g, c, p), F32),) * 2
        + (jax.ShapeDtypeStruct((depth, SCAN_LEN, g, 1, p), F32),) * 2,
        grid=(depth,),
        in_specs=[spec_a, spec_a, pl.BlockSpec((1, g, 1, 1), lambda l: (l, 0, 0, 0)), spec_b, spec_b],
        out_specs=(spec_a, spec_a, spec_b, spec_b, spec_pw, spec_pw),
        compiler_params=_params(("arbitrary",)),
        name="ssm_prep",
    )(a4(a_re), a4(a_im), log_dt.reshape(depth, g, 1, 1),
      jnp.swapaxes(b_re, 2, 3), jnp.swapaxes(b_im, 2, 3))


def _ssm_out(y, u, d_skip, w_glu, b_glu):
    z = jax.nn.gelu(y + d_skip * u)
    return z * _sigmoid(_halves(_dot, z.astype(BF16), w_glu) + b_glu)


def _lane_lo(shape):
    return (lax.broadcasted_iota(jnp.int32, shape, len(shape) - 1) % LANES) < HEAD_DIM


N_MIXER_SEQ_IN = 4
N_MIXER_IN = 24
N_MIXER_OUT = 6


def _mixer_kernel(*refs):
    ins = refs[:N_MIXER_IN]
    outs = refs[N_MIXER_IN:N_MIXER_IN + N_MIXER_OUT]
    scratch = refs[N_MIXER_IN + N_MIXER_OUT:]
    bias_s, wg3_s = scratch[-2:]
    wg_ref = ins[MIXER_REF_NAMES.index('wg_ref')]
    t = pl.program_id(1)

    @pl.when((pl.program_id(0) == 0) & (t == 0))
    def _():
        for c in range(D_GATES // GATE_CHUNK):
            wg3_s[c] = wg_ref[:, c * GATE_CHUNK:(c + 1) * GATE_CHUNK]
        r_i = lax.broadcasted_iota(jnp.int32, (Q_BLK, 2 * Q_BLK), 0)
        c_i = lax.broadcasted_iota(jnp.int32, (Q_BLK, 2 * Q_BLK), 1)
        dist = r_i - c_i + Q_BLK
        valid = (dist >= 0) & (dist <= WINDOW)
        for hh in range(N_HEADS):
            bias_s[hh] = jnp.where(valid, (-SLOPES[hh] * LOG2E) * dist.astype(F32), NEG_INF)

    lvl_s, kbuf, vbuf, pbuf, hc_s = scratch[-7:-2]

    @pl.when(t == 0)
    def _():
        hc_s[...] = jnp.zeros_like(hc_s)
        kbuf[:, 0:WINDOW, :] = jnp.zeros((SEQ_PER_STEP, WINDOW, D_KV), BF16)
        vbuf[:, 0:WINDOW, :] = jnp.zeros((SEQ_PER_STEP, WINDOW, D_KV), BF16)
        pbuf[:, 0:POOL_PAD + POOL_HIST, :] = jnp.zeros((SEQ_PER_STEP, POOL_PAD + POOL_HIST, D_POOL), F32)
        lvl_s[:, :, 0:POOL_PAD, :] = jnp.zeros((SEQ_PER_STEP, 2, POOL_PAD, LANES), F32)

    chains = []
    for s in range(SEQ_PER_STEP):
        one = functools.partial(lambda s, r: r.at[s:s + 1], s)
        chains.append(_mixer_body(t, *[one(r) for r in ins[:N_MIXER_SEQ_IN]], *ins[N_MIXER_SEQ_IN:],
                                  *[one(r) for r in outs], *[r.at[s] for r in scratch[:-2]], bias_s, wg3_s))
    for _ in range(MIXER_LAG):
        next(chains[0])
    while chains:
        for chain in list(chains):
            if next(chain, StopIteration) is StopIteration:
                chains.remove(chain)


MIXER_REF_NAMES = (
    'x_ref sh_ref sc_ref g_ref npre_ref npost_ref wa_ref wg_ref '
    'wbu_ref abre_ref abim_ref pwre_ref pwim_ref cm_ref dsk_ref wglu_ref bglu_ref '
    'sinks_ref wpool_ref pscale_ref wus_ref wua_ref wup_ref wout_ref '
    'xo_ref hre_o him_o kwin_o vwin_o plast_o '
    'hb_s proj_s bu_s hst_s sg_s unat_s uperm_s operm_s yssm_s yatt_s lvl_s kbuf vbuf pbuf hc_s '
    'bias_s wg3_s').split()


def _mixer_body(t, *refs):
    r = types.SimpleNamespace(**dict(zip(MIXER_REF_NAMES, refs, strict=True)))
    x = r.x_ref[0]
    h = _rms(x, r.npre_ref[...] * (1.0 + r.sc_ref[0])) + r.sh_ref[0]
    r.hb_s[...] = h.astype(BF16)
    yield
    for c in range(D_SMALL // PROJ_CHUNK):
        cols = slice(c * PROJ_CHUNK, (c + 1) * PROJ_CHUNK)
        r.proj_s[:, cols] = _dot(r.hb_s[...], r.wa_ref[:, cols])
        yield
    res = {}
    yield from _ssm_chain(r)
    yield from _attn_chain(t, r)
    yield from _pool_chain(t, r, res)
    yield from _merge_tail(r, res)


def _ssm_chain(r):
    (proj_s, hb_s, bu_s, hst_s, sg_s, unat_s, uperm_s, operm_s, yssm_s, hc_s, wg_ref, wbu_ref, abre_ref, abim_ref,
     pwre_ref, pwim_ref, cm_ref, dsk_ref, wglu_ref, bglu_ref, hre_o, him_o) = (
        r.proj_s, r.hb_s, r.bu_s, r.hst_s, r.sg_s, r.unat_s, r.uperm_s, r.operm_s, r.yssm_s, r.hc_s, r.wg3_s,
        r.wbu_ref, r.abre_ref, r.abim_ref, r.pwre_ref, r.pwim_ref, r.cm_ref, r.dsk_ref, r.wglu_ref, r.bglu_ref,
        r.hre_o, r.him_o)
    for c in range(D_SSM // LANES):
        unat_s[c] = proj_s[:, c * LANES:(c + 1) * LANES]
        for i in range(SCAN_LEN):
            uperm_s[i * SUBLANES:(i + 1) * SUBLANES, c * LANES:(c + 1) * LANES] = (
                unat_s.at[c][pl.ds(i, SUBLANES, stride=SCAN_LEN), :])
        yield
    for c in range(2 * D_STATE // BU_CHUNK):
        cols = slice(c * BU_CHUNK, (c + 1) * BU_CHUNK)
        bu_s[:, cols] = _dot(uperm_s[...].astype(BF16), wbu_ref[:, cols])
        yield

    abr = jnp.broadcast_to(abre_ref[...], (SUBLANES, D_STATE))
    abi = jnp.broadcast_to(abim_ref[...], (SUBLANES, D_STATE))

    def gate_chunk(c):
        sg_s[c] = _sigmoid(_dot(hb_s[...], wg_ref[c]))

    zero = jnp.zeros((SUBLANES, D_STATE), F32)
    steps_a = SCAN_LEN // GATE_LOOP_A

    er, ei = zero, zero
    for i in range(GATE_LOOP_A):
        gate_chunk(i)
        yield
        for s in range(steps_a):
            rows = slice((i * steps_a + s) * SUBLANES, (i * steps_a + s + 1) * SUBLANES)
            er, ei = (abr * er - abi * ei + bu_s[rows, 0:D_STATE],
                      abr * ei + abi * er + bu_s[rows, D_STATE:2 * D_STATE])
            bu_s[rows, 0:D_STATE] = er
            bu_s[rows, D_STATE:2 * D_STATE] = ei
        yield

    alr = pwre_ref[SCAN_LEN - 1:SCAN_LEN, :]
    ali = pwim_ref[SCAN_LEN - 1:SCAN_LEN, :]
    cr = hc_s[0:1, :]
    ci = hc_s[1:2, :]
    row = lax.broadcasted_iota(jnp.int32, (SUBLANES, D_STATE), 0)
    hin_r = zero
    hin_i = zero
    for j in range(SUBLANES):
        hin_r = jnp.where(row == j, cr, hin_r)
        hin_i = jnp.where(row == j, ci, hin_i)
        cr, ci = (alr * cr - ali * ci + er[j:j + 1, :], alr * ci + ali * cr + ei[j:j + 1, :])
    hc_s[0:1, :] = cr
    hc_s[1:2, :] = ci
    hre_o[0] = cr
    him_o[0] = ci
    yield

    pack = 2 * SUBLANES
    packs_b = TT // pack // GATE_LOOP_B

    for i in range(GATE_LOOP_B):
        gate_chunk(GATE_LOOP_A + i)
        yield
        for s in range(packs_b):
            k = i * packs_b + s
            fixed_r, fixed_i = [], []
            for slab in (2 * k, 2 * k + 1):
                rows = slice(slab * SUBLANES, (slab + 1) * SUBLANES)
                pr = pwre_ref[slab:slab + 1, :]
                pi = pwim_ref[slab:slab + 1, :]
                fixed_r.append(bu_s[rows, 0:D_STATE] + (pr * hin_r - pi * hin_i))
                fixed_i.append(bu_s[rows, D_STATE:2 * D_STATE] + (pr * hin_i + pi * hin_r))
            prow = slice(k * pack, (k + 1) * pack)
            hst_s[prow, 0:D_STATE] = jnp.concatenate(fixed_r, axis=0).astype(BF16)
            hst_s[prow, D_STATE:2 * D_STATE] = jnp.concatenate(fixed_i, axis=0).astype(BF16)
        yield

    y = _halves(_dot, hst_s[...], cm_ref[...])
    yield
    operm_s[...] = _ssm_out(y, uperm_s[...], dsk_ref[...], wglu_ref[...], bglu_ref[...])
    yield
    for c in range(D_SSM // LANES):
        for i in range(SCAN_LEN):
            yssm_s.at[c][pl.ds(i, SUBLANES, stride=SCAN_LEN), :] = (
                operm_s[i * SUBLANES:(i + 1) * SUBLANES, c * LANES:(c + 1) * LANES])
        yield

def _attn_chain(t, r):
    proj_s, kbuf, vbuf, yatt_s, bias_s, sinks_ref, kwin_o, vwin_o = (
        r.proj_s, r.kbuf, r.vbuf, r.yatt_s, r.bias_s, r.sinks_ref, r.kwin_o, r.vwin_o)
    kbuf[WINDOW:WINDOW + TT, :] = proj_s[:, D_SSM + D_ATTN:D_SSM + D_ATTN + D_KV].astype(BF16)
    vbuf[WINDOW:WINDOW + TT, :] = proj_s[:, D_SSM + D_ATTN + D_KV:D_SSM + D_ATTN + 2 * D_KV].astype(BF16)
    lo = _lane_lo((Q_BLK, LANES))
    r_i = lax.broadcasted_iota(jnp.int32, (Q_BLK, 2 * Q_BLK), 0)
    c_i = lax.broadcasted_iota(jnp.int32, (Q_BLK, 2 * Q_BLK), 1)
    first_key = jnp.where(t == 0, Q_BLK, 0)
    def scores(blk):
        r0 = blk * Q_BLK
        pieces = []
        for hh in range(N_HEADS):
            j = hh % GQA
            qp = proj_s[r0:r0 + Q_BLK, D_SSM + j * LANES:D_SSM + (j + 1) * LANES] * (ATTN_SCALE * LOG2E)
            keep = lo if hh < GQA else jnp.logical_not(lo)
            pieces.append(jnp.where(keep, qp, 0.0).astype(BF16))
        return [_dot_t(jnp.concatenate(pieces[g * GQA:(g + 1) * GQA], axis=0), kbuf[r0:r0 + 2 * Q_BLK, :])
                for g in range(N_KV_HEADS)]

    n_blk = TT // Q_BLK
    s_next = scores(0)
    yield
    for blk in range(n_blk):
        r0 = blk * Q_BLK
        s_all = s_next
        if blk + 1 < n_blk:
            s_next = scores(blk + 1)
            yield
        probs, dens = [], []
        for hh in range(N_HEADS):
            s = s_all[hh // GQA][(hh % GQA) * Q_BLK:(hh % GQA + 1) * Q_BLK, :] + bias_s[hh]
            if blk == 0:
                s = jnp.where(c_i >= first_key, s, NEG_INF)
            snk = sinks_ref[hh] * LOG2E
            m = jnp.maximum(jnp.max(s, axis=-1, keepdims=True), snk)
            p = jnp.exp2(s - m)
            dens.append(jnp.sum(p, axis=-1, keepdims=True) + jnp.exp2(snk - m))
            probs.append(p.astype(BF16))
            yield
        o_all = [_dot(jnp.concatenate(probs[g * GQA:(g + 1) * GQA], axis=0), vbuf[r0:r0 + 2 * Q_BLK, :])
                 for g in range(N_KV_HEADS)]
        out = lambda hh: o_all[hh // GQA][(hh % GQA) * Q_BLK:(hh % GQA + 1) * Q_BLK, :] / dens[hh]
        for j in range(GQA):
            yatt_s[r0:r0 + Q_BLK, j * LANES:(j + 1) * LANES] = jnp.where(lo, out(j), out(GQA + j)).astype(BF16)
        yield
    kwin_o[0] = proj_s[TT - WINDOW:TT, D_SSM + D_ATTN:D_SSM + D_ATTN + D_KV]
    vwin_o[0] = proj_s[TT - WINDOW:TT, D_SSM + D_ATTN + D_KV:D_SSM + D_ATTN + 2 * D_KV]
    kbuf[0:WINDOW, :] = kbuf[TT:TT + WINDOW, :]
    vbuf[0:WINDOW, :] = vbuf[TT:TT + WINDOW, :]
    yield

def _pool_chain(t, r, res):
    proj_s, pbuf, lvl_s, plast_o, wpool_ref, pscale_ref = (
        r.proj_s, r.pbuf, r.lvl_s, r.plast_o, r.wpool_ref, r.pscale_ref)
    p0 = D_SMALL - D_POOL
    cur = POOL_PAD + POOL_HIST
    end = cur + TT
    pbuf[cur:end, :] = proj_s[:, p0:D_SMALL]
    plast_o[0] = proj_s[TT - POOL_HIST:TT, p0:D_SMALL]
    pos1 = (t * TT + 1 + lax.broadcasted_iota(jnp.int32, (TT, LANES), 0)).astype(F32)
    lo_t = _lane_lo((TT, LANES))
    a_s, b_s = lvl_s.at[0], lvl_s.at[1]
    mixed = []
    for col, (w_lo, w_hi) in enumerate(((POOL_WINDOWS[0], POOL_WINDOWS[1]), (POOL_WINDOWS[2], POOL_WINDOWS[3]))):
        cs = slice(col * LANES, (col + 1) * LANES)
        pf = pbuf[cur:end, cs]
        a_s[POOL_PAD:end, :] = pbuf[POOL_PAD:end, cs] + pbuf[POOL_PAD - 1:end - 1, cs]
        have = 2
        while have < w_lo:
            b_s[POOL_PAD:end, :] = a_s[POOL_PAD:end, :] + a_s[POOL_PAD - have:end - have, :]
            a_s, b_s, have = b_s, a_s, 2 * have
        acc_lo = a_s[cur:end, :]
        while 2 * have < w_hi:
            b_s[POOL_PAD:end, :] = a_s[POOL_PAD:end, :] + a_s[POOL_PAD - have:end - have, :]
            a_s, b_s, have = b_s, a_s, 2 * have
        acc_hi = a_s[cur:end, :] + a_s[cur - have:end - have, :]
        cnt = jnp.where(lo_t, jnp.minimum(float(w_lo), pos1), jnp.minimum(float(w_hi), pos1))
        mixed.append(jnp.where(lo_t, acc_lo, acc_hi) / cnt - pf)
        yield
    ypool = _halves(_dot, jnp.concatenate(mixed, axis=1).astype(BF16), wpool_ref[...]) * pscale_ref[...]
    pbuf[POOL_PAD:cur, :] = pbuf[end - POOL_HIST:end, :]
    res['ypool'] = ypool
    yield


def _merge_tail(r, res):
    sg_s, yssm_s, yatt_s, wus_ref, wua_ref, wup_ref, wout_ref, x_ref, g_ref, npost_ref, xo_ref = (
        r.sg_s, r.yssm_s, r.yatt_s, r.wus_ref, r.wua_ref, r.wup_ref, r.wout_ref, r.x_ref, r.g_ref, r.npost_ref,
        r.xo_ref)
    ypool = res['ypool']
    per_branch = D_MODEL // GATE_CHUNK
    gate = lambda b: jnp.concatenate([sg_s[b * per_branch + c] for c in range(per_branch)], axis=1)
    yssm = jnp.concatenate([yssm_s[c] for c in range(D_SSM // LANES)], axis=1)
    merged = gate(0) * _dot(yssm.astype(BF16), wus_ref[...])
    yield
    merged += gate(1) * _dot(yatt_s[...], wua_ref[...])
    yield
    merged += gate(2) * _dot(ypool.astype(BF16), wup_ref[...])
    yield
    out = _dot(merged.astype(BF16), wout_ref[...])
    yield
    xo_ref[0] = x_ref[0] + _rms(out, g_ref[0] * npost_ref[...])


def _mixer_call(x, mods3, lp):
    b, t_len, d = x.shape
    nt = t_len // TT
    nsamp = mods3.shape[0] - b
    sps = SEQ_PER_STEP
    mod_spec = lambda col: pl.BlockSpec((sps, 1, D_MODEL), lambda i, j: (nsamp // sps + i, 0, col))
    x_spec = pl.BlockSpec((sps, TT, d), lambda i, j: (i, j, 0))
    consts = [lp['norm_mix_pre'], lp['norm_mix_post'], lp['w_in_a'], lp['w_in_g'],
              lp['w_bu'], lp['ab_re'], lp['ab_im'], lp['pw_re'], lp['pw_im'], lp['c_mat'],
              lp['ssm_d'], lp['ssm_w_glu'], lp['ssm_b_glu']]
    consts2 = [lp['pool_wbd'], lp['pool_scale'], lp['w_up_ssm'], lp['w_up_attn'], lp['w_up_pool'], lp['w_out']]
    in_specs = ([x_spec, mod_spec(0), mod_spec(1), mod_spec(2)]
                + [_const_spec(a.shape) for a in consts]
                + [pl.BlockSpec(memory_space=pltpu.SMEM)]
                + [_const_spec(a.shape) for a in consts2])
    per_b = lambda r, c: pl.BlockSpec((sps, r, c), lambda i, j: (i, 0, 0))
    out_shape = (jax.ShapeDtypeStruct((b, t_len, d), F32),
                 jax.ShapeDtypeStruct((b, 1, D_STATE), F32), jax.ShapeDtypeStruct((b, 1, D_STATE), F32),
                 jax.ShapeDtypeStruct((b, WINDOW, D_KV), F32), jax.ShapeDtypeStruct((b, WINDOW, D_KV), F32),
                 jax.ShapeDtypeStruct((b, POOL_HIST, D_POOL), F32))
    out_specs = (x_spec, per_b(1, D_STATE), per_b(1, D_STATE), per_b(WINDOW, D_KV), per_b(WINDOW, D_KV),
                 per_b(POOL_HIST, D_POOL))
    per_seq = lambda shape, dtype: pltpu.VMEM((sps,) + shape, dtype)
    scratch = [per_seq((TT, d), BF16), per_seq((TT, D_SMALL), F32), per_seq((TT, 2 * D_STATE), F32),
               per_seq((TT, 2 * D_STATE), BF16), per_seq((D_GATES // GATE_CHUNK, TT, GATE_CHUNK), F32),
               per_seq((D_SSM // LANES, TT, LANES), F32), per_seq((TT, D_SSM), F32),
               per_seq((TT, D_SSM), F32), per_seq((D_SSM // LANES, TT, LANES), F32),
               per_seq((TT, D_ATTN), BF16), per_seq((2, POOL_PAD + POOL_HIST + TT, LANES), F32),
               per_seq((WINDOW + TT, D_KV), BF16), per_seq((WINDOW + TT, D_KV), BF16),
               per_seq((POOL_PAD + POOL_HIST + TT, D_POOL), F32), per_seq((SUBLANES, D_STATE), F32),
               pltpu.VMEM((N_HEADS, Q_BLK, 2 * Q_BLK), F32),
               pltpu.VMEM((D_GATES // GATE_CHUNK, d, GATE_CHUNK), BF16)]
    return pl.pallas_call(
        _mixer_kernel,
        out_shape=out_shape,
        grid=(b // sps, nt),
        in_specs=in_specs,
        out_specs=out_specs,
        scratch_shapes=scratch,
        compiler_params=_params(("arbitrary", "arbitrary")),
        name="prompt_mixer",
    )(x, mods3, mods3, mods3, *consts, lp['attn_sinks'], *consts2)


def _mod_rows(ref, per_row):
    return ref[...] if per_row else ref[0]


def _ffn_kernel(per_row, x_ref, sh_ref, sc_ref, g_ref, npre_ref, npost_ref, wg_ref, wu_ref, wd_ref, o_ref):
    x = x_ref[...]
    h = (_rms(x, npre_ref[...] * (1.0 + _mod_rows(sc_ref, per_row))) + _mod_rows(sh_ref, per_row)).astype(BF16)
    f = None
    for c0, c1 in FF_CHUNKS:
        act = _silu(_dot(h, wg_ref[:, c0:c1])) * _dot(h, wu_ref[:, c0:c1])
        part = _dot(act.astype(BF16), wd_ref[c0:c1, :])
        f = part if f is None else f + part
    o_ref[...] = x + _rms(f, _mod_rows(g_ref, per_row) * npost_ref[...])


def _mod_specs(mods3, mods2, n_rows, tm, rows_per_seq, per_row):
    nsamp = mods3.shape[0] - (n_rows // rows_per_seq if not per_row else 0)
    if per_row:
        specs = [pl.BlockSpec((tm, D_MODEL), functools.partial(lambda col, i, *_: (i, col), col)) for col in (3, 4, 5)]
        return specs, [mods2] * 3
    tiles_per_seq = rows_per_seq // tm
    specs = [pl.BlockSpec((1, 1, D_MODEL),
                          functools.partial(lambda col, i, *_: (nsamp + i // tiles_per_seq, 0, col), col))
             for col in (3, 4, 5)]
    return specs, [mods3] * 3


def _ffn_call(x2, mods3, mods2, lp, rows_per_seq, per_row):
    n = x2.shape[0]
    tm = n if per_row else FFN_TM
    mspecs, mops = _mod_specs(mods3, mods2, n, tm, rows_per_seq, per_row)
    consts = [lp['norm_ffn_pre'], lp['norm_ffn_post'], lp['ffn_w_gate'], lp['ffn_w_up'], lp['ffn_w_down']]
    x_spec = pl.BlockSpec((tm, D_MODEL), lambda i: (i, 0))
    return pl.pallas_call(
        functools.partial(_ffn_kernel, per_row),
        out_shape=jax.ShapeDtypeStruct(x2.shape, F32),
        grid=(n // tm,),
        in_specs=[x_spec] + mspecs + [_const_spec(a.shape) for a in consts],
        out_specs=x_spec,
        compiler_params=_params(("parallel",)),
        name="ffn_rows" if per_row else "ffn_seq",
    )(x2, *mops, *consts)


def _moe_kernel(per_row, x_ref, sh_ref, sc_ref, g_ref, npre_ref, npost_ref, wr_ref, br_ref,
                wg_ref, wu_ref, wd_ref, o_ref, h_s, gates_s, acc_s):
    e = pl.program_id(1)
    lane_e = lax.broadcasted_iota(jnp.int32, gates_s.shape, 1)

    @pl.when(e == 0)
    def _():
        x = x_ref[...]
        h = _rms(x, npre_ref[...]) * (1.0 + _mod_rows(sc_ref, per_row)) + _mod_rows(sh_ref, per_row)
        hb = h.astype(BF16)
        h_s[...] = hb
        logits = _dot(hb, wr_ref[...]) + br_ref[...]
        gates_s[...], _ = _top2_gates(logits, lane_e.astype(F32))
        acc_s[...] = jnp.zeros_like(acc_s)

    h = h_s[...]
    act = _silu(_dot(h, wg_ref[0, 0])) * _dot(h, wu_ref[0, 0])
    y = _dot(act.astype(BF16), wd_ref[0, 0])
    ge = jnp.sum(jnp.where(lane_e == e, gates_s[...], 0.0), axis=-1, keepdims=True)
    acc_s[...] += ge * y

    @pl.when(e == N_EXPERTS - 1)
    def _():
        o_ref[...] = x_ref[...] + _mod_rows(g_ref, per_row) * _rms(acc_s[...], npost_ref[...])


def _moe_call(x2, mods3, mods2, lp, rows_per_seq, per_row):
    n = x2.shape[0]
    tm = n if per_row else MOE_TM
    mspecs, mops = _mod_specs(mods3, mods2, n, tm, rows_per_seq, per_row)
    consts = [lp['norm_ffn_pre'], lp['norm_ffn_post'], lp['moe_w_router'], lp['moe_b_router']]
    x_spec = pl.BlockSpec((tm, D_MODEL), lambda i, e: (i, 0))
    w_spec = lambda a: pl.BlockSpec((1, 1) + a.shape[2:], lambda i, e: (0, e, 0, 0))
    experts = [lp['moe_w_gate'], lp['moe_w_up'], lp['moe_w_down']]
    return pl.pallas_call(
        functools.partial(_moe_kernel, per_row),
        out_shape=jax.ShapeDtypeStruct(x2.shape, F32),
        grid=(n // tm, N_EXPERTS),
        in_specs=[x_spec] + mspecs + [_const_spec(a.shape) for a in consts] + [w_spec(a) for a in experts],
        out_specs=x_spec,
        scratch_shapes=[pltpu.VMEM((tm, D_MODEL), BF16), pltpu.VMEM((tm, N_EXPERTS), F32),
                        pltpu.VMEM((tm, D_MODEL), F32)],
        compiler_params=_params(("arbitrary", "arbitrary")),
        name="moe_rows" if per_row else "moe_seq",
    )(x2, *mops, *consts, *experts)


def _top2_gates(logits, lane_f):
    n_lanes = float(logits.shape[-1])
    v1 = jnp.max(logits, axis=-1, keepdims=True)
    i1 = jnp.min(jnp.where(logits == v1, lane_f, n_lanes), axis=-1, keepdims=True)
    rest = jnp.where(lane_f == i1, -jnp.inf, logits)
    v2 = jnp.max(rest, axis=-1, keepdims=True)
    i2 = jnp.min(jnp.where(rest == v2, lane_f, n_lanes), axis=-1, keepdims=True)
    e2 = jnp.exp(v2 - v1)
    den = 1.0 + e2
    gates = jnp.where(lane_f == i1, 1.0 / den, 0.0) + jnp.where(lane_f == i2, e2 / den, 0.0)
    return gates, (lane_f == i1) | (lane_f == i2)


def _moe_routed_kernel(seq_len, n_rows, x_ref, sh0_ref, sh1_ref, sc0_ref, sc1_ref, g0_ref, g1_ref,
                       npre_ref, npost_ref, wr_ref, br_ref, wg_ref, wu_ref, wd_ref, o_ref,
                       h_s, before_s, dest_s, gate_s, acc_s, cnt_s):
    i = pl.program_id(0)
    e = pl.program_id(1)
    tm = x_ref.shape[0]
    row = lax.broadcasted_iota(jnp.int32, (tm, 1), 0) + i * tm
    if n_rows % tm == 0:
        keep = lambda v: v
    else:
        valid = row < n_rows
        keep = lambda v: jnp.where(valid, v, 0.0)
    if seq_len % tm == 0:
        pick = lambda a_ref, b_ref: a_ref[0]
    else:
        second = row >= ((i * tm) // seq_len + 1) * seq_len
        pick = lambda a_ref, b_ref: jnp.where(second, b_ref[0], a_ref[0])

    @pl.when((i == 0) & (e == 0))
    def _():
        r_i = lax.broadcasted_iota(jnp.int32, before_s.shape, 0)
        c_i = lax.broadcasted_iota(jnp.int32, before_s.shape, 1)
        before_s[...] = jnp.where(c_i < r_i, 1.0, 0.0).astype(BF16)

    @pl.when(e == 0)
    def _():
        x = keep(x_ref[...])
        h = _rms(x, npre_ref[...] * (1.0 + pick(sc0_ref, sc1_ref))) + pick(sh0_ref, sh1_ref)
        hb = h.astype(BF16)
        h_s[...] = hb
        lane_f = lax.broadcasted_iota(jnp.int32, (tm, LANES), 1).astype(F32)
        logits = jnp.where(lane_f < N_EXPERTS, _halves(_dot, hb, wr_ref[...]) + br_ref[...], -jnp.inf)
        gates, sel = _top2_gates(logits, lane_f)
        self = keep(jnp.where(sel, 1.0, 0.0))
        counts = jnp.zeros((1, LANES), F32)
        dest_blocks = []
        for k in range(tm // MOE_RANK_BLOCK):
            self_k = self[k * MOE_RANK_BLOCK:(k + 1) * MOE_RANK_BLOCK, :]
            rank_k = _dot(before_s[...], self_k.astype(BF16)) + counts
            dest_blocks.append(jnp.where(self_k > 0.0, rank_k, -1.0))
            counts = counts + jnp.sum(self_k, axis=0, keepdims=True)
        dest = jnp.concatenate(dest_blocks, axis=0)
        dest_s[...] = dest.T[0:SUBLANES, :]
        gate_s[...] = gates.T[0:SUBLANES, :]
        for j in range(N_EXPERTS):
            cnt_s[j] = counts[0, j].astype(jnp.int32)
        acc_s[...] = jnp.zeros_like(acc_s)

    dest_row = dest_s[pl.ds(e, 1), :]
    gate_row = gate_s[pl.ds(e, 1), :]
    def chunk(base, size):
        slot = lax.broadcasted_iota(jnp.int32, (size, tm), 0).astype(F32) + base.astype(F32)
        onehot = jnp.where(slot == dest_row, 1.0, 0.0)
        gate_packed = jnp.sum(onehot * gate_row, axis=-1, keepdims=True)
        pack = onehot.astype(BF16)
        xg = _dot(pack, h_s[...]).astype(BF16)
        act = _silu(_dot(xg, wg_ref[0, 0])) * _dot(xg, wu_ref[0, 0])
        y = _dot(act.astype(BF16), wd_ref[0, 0]) * gate_packed
        acc_s[...] += lax.dot_general(pack, y.astype(BF16), (((0,), (0,)), ((), ())), preferred_element_type=F32)

    shift = MOE_CAP.bit_length() - 1
    cnt = cnt_s[e]
    n_full = lax.shift_right_logical(cnt, shift)
    rem = cnt - lax.shift_left(n_full, shift)
    long_last = (n_full >= 1) & (rem >= 1) & (rem <= MOE_TAIL)
    n_plain = jnp.where(long_last, n_full - 1, n_full + jnp.where(rem >= 1, 1, 0))

    def plain_chunk(c, _):
        chunk(c * MOE_CAP, MOE_CAP)
        return 0

    lax.fori_loop(0, n_plain, plain_chunk, 0)

    @pl.when(long_last)
    def _():
        chunk(n_plain * MOE_CAP, MOE_CAP + MOE_TAIL)

    @pl.when(e == N_EXPERTS - 1)
    def _():
        o_ref[...] = keep(x_ref[...]) + _rms(acc_s[...], pick(g0_ref, g1_ref) * npost_ref[...])


def _moe_routed_call(x2, mods3, lp, seq_len):
    n = x2.shape[0]
    tm = MOE_SP_TM
    assert tm % MOE_RANK_BLOCK == 0
    n_seq = n // seq_len
    nsamp = mods3.shape[0] - n_seq

    def mod_spec(col, nxt):
        return pl.BlockSpec((1, 1, D_MODEL),
                            lambda i, e: (nsamp + jnp.minimum((i * tm) // seq_len + nxt, n_seq - 1), 0, col))

    mspecs = [mod_spec(col, nxt) for col in (3, 4, 5) for nxt in (0, 1)]
    pad = LANES - N_EXPERTS
    consts = [lp['norm_ffn_pre'], lp['norm_ffn_post'],
              jnp.pad(lp['moe_w_router'], ((0, 0), (0, pad))), jnp.pad(lp['moe_b_router'], ((0, 0), (0, pad)))]
    x_spec = pl.BlockSpec((tm, D_MODEL), lambda i, e: (i, 0))
    w_spec = lambda a: pl.BlockSpec((1, 1) + a.shape[2:], lambda i, e: (0, e, 0, 0))
    experts = [lp['moe_w_gate'], lp['moe_w_up'], lp['moe_w_down']]
    return pl.pallas_call(
        functools.partial(_moe_routed_kernel, seq_len, n),
        out_shape=jax.ShapeDtypeStruct(x2.shape, F32),
        grid=(pl.cdiv(n, tm), N_EXPERTS),
        in_specs=[x_spec] + mspecs + [_const_spec(a.shape) for a in consts] + [w_spec(a) for a in experts],
        out_specs=x_spec,
        scratch_shapes=[pltpu.VMEM((tm, D_MODEL), BF16), pltpu.VMEM((MOE_RANK_BLOCK, MOE_RANK_BLOCK), BF16),
                        pltpu.VMEM((SUBLANES, tm), F32), pltpu.VMEM((SUBLANES, tm), F32),
                        pltpu.VMEM((tm, D_MODEL), F32), pltpu.SMEM((N_EXPERTS,), jnp.int32)],
        compiler_params=_params(("arbitrary", "arbitrary")),
        name="moe_routed",
    )(x2, *([mods3] * 6), *consts, *experts)


def _sproj_kernel(x_ref, sh_ref, sc_ref, npre_ref, wa_ref, wg_ref, pa_o, pg_o):
    h = (_rms(x_ref[...], npre_ref[...]) * (1.0 + sc_ref[...]) + sh_ref[...]).astype(BF16)
    pa_o[...] = _dot(h, wa_ref[...])
    pg_o[...] = _dot(h, wg_ref[...])


def _sproj_call(xs, mods2, lp):
    n = xs.shape[0]
    mod_spec = lambda col: pl.BlockSpec((n, D_MODEL), lambda i: (0, col))
    consts = [lp['norm_mix_pre'], lp['w_in_a'], lp['w_in_g']]
    return pl.pallas_call(
        _sproj_kernel,
        out_shape=(jax.ShapeDtypeStruct((n, D_SMALL), F32), jax.ShapeDtypeStruct((n, D_GATES), F32)),
        grid=(1,),
        in_specs=[_const_spec(xs.shape), mod_spec(0), mod_spec(1)] + [_const_spec(a.shape) for a in consts],
        out_specs=(_const_spec((n, D_SMALL)), _const_spec((n, D_GATES))),
        compiler_params=_params(("arbitrary",)),
        name="sample_proj",
    )(xs, mods2, mods2, *consts)


N_SMIX_IN = 22


def _smix_kernel(*refs):
    (u_ref, q_ref, kn_ref, vn_ref, knt_ref, vnt_ref, pin_ref, h0r_ref, h0i_ref, ck_ref, cv_ref, hist_ref,
     wbu_ref, abre_ref, abim_ref, cm_ref, dsk_ref, wglu_ref, bglu_ref, sinks_ref,
     wpool_ref, pscale_ref) = refs[:N_SMIX_IN]
    yssm_o, yatt_o, ypool_o, hr_o, hi_o, ck_o, cv_o, hist_o = refs[-8:]
    u = u_ref[...]
    bu = _dot(u.astype(BF16), wbu_ref[...])
    abr, abi = abre_ref[...], abim_ref[...]
    h0r, h0i = h0r_ref[...], h0i_ref[...]
    hr = bu[:, 0:D_STATE] + (abr * h0r - abi * h0i)
    hi = bu[:, D_STATE:2 * D_STATE] + (abr * h0i + abi * h0r)
    hr_o[...] = hr
    hi_o[...] = hi
    y = _dot(jnp.concatenate([hr, hi], axis=1).astype(BF16), cm_ref[...])
    yssm_o[...] = _ssm_out(y, u, dsk_ref[...], wglu_ref[...], bglu_ref[...])

    q = q_ref[...] * ATTN_SCALE
    kn, vn = kn_ref[...], vn_ref[...]
    kc = ck_ref[...].astype(BF16)
    vc = cv_ref[...].astype(BF16)
    lo = _lane_lo(q.shape)
    j_i = lax.broadcasted_iota(jnp.int32, (1, GQA, 1), 1)
    dist_c = (WINDOW - lax.broadcasted_iota(jnp.int32, (1, 1, WINDOW), 2)).astype(F32)
    halves = []
    for half in range(N_KV_HEADS):
        keep = lo if half == 0 else jnp.logical_not(lo)
        qm = jnp.where(keep, q, 0.0)
        slope = jnp.zeros((1, GQA, 1), F32)
        snk = jnp.zeros((1, GQA, 1), F32)
        for j in range(GQA):
            slope = jnp.where(j_i == j, SLOPES[half * GQA + j], slope)
            snk = jnp.where(j_i == j, sinks_ref[half * GQA + j], snk)
        s = jnp.einsum('bhd,bds->bhs', qm.astype(BF16), kc, preferred_element_type=F32) - slope * dist_c
        s_new = jnp.sum(qm * kn, axis=-1, keepdims=True)
        m = jnp.maximum(jnp.maximum(jnp.max(s, axis=-1, keepdims=True), s_new), snk)
        p = jnp.exp(s - m)
        p_new = jnp.exp(s_new - m)
        den = jnp.sum(p, axis=-1, keepdims=True) + p_new + jnp.exp(snk - m)
        o = jnp.einsum('bhs,bds->bhd', p.astype(BF16), vc, preferred_element_type=F32) + p_new * vn
        halves.append(o / den)
    yatt_o[...] = jnp.where(lo, halves[0], halves[1])

    bc = ck_ref.shape[0]
    lane_w = lax.broadcasted_iota(jnp.int32, (D_KV, WINDOW), 1)
    lane_b = lax.broadcasted_iota(jnp.int32, knt_ref.shape, 1)
    for src, new_t, dst in ((ck_ref, knt_ref, ck_o), (cv_ref, vnt_ref, cv_o)):
        for b in range(bc):
            col = jnp.sum(jnp.where(lane_b == pl.program_id(0) * bc + b, new_t[...], 0.0), axis=-1, keepdims=True)
            dst[b] = jnp.where(lane_w == WINDOW - 1, col, pltpu.roll(src[b], WINDOW - 1, axis=1))

    pin = pin_ref[...]
    lo2 = _lane_lo((pin.shape[0], LANES))
    mixed = []
    for col, (w_lo, w_hi) in enumerate(((POOL_WINDOWS[0], POOL_WINDOWS[1]), (POOL_WINDOWS[2], POOL_WINDOWS[3]))):
        cs = slice(col * LANES, (col + 1) * LANES)
        pf = pin[:, cs]
        acc = pf
        for k in range(1, w_lo):
            acc = acc + hist_ref[POOL_BUF - k, :, cs]
        acc_lo = acc
        for k in range(w_lo, w_hi):
            acc = acc + hist_ref[POOL_BUF - k, :, cs]
        cnt_lo = float(min(w_lo, PAST_LEN + 1))
        cnt_hi = float(min(w_hi, PAST_LEN + 1))
        mixed.append(jnp.where(lo2, acc_lo / cnt_lo, acc / cnt_hi) - pf)
    ypool_o[...] = _dot(jnp.concatenate(mixed, axis=1).astype(BF16), wpool_ref[...]) * pscale_ref[...]
    hist_o[0:POOL_BUF - 1] = hist_ref[1:POOL_BUF]
    hist_o[POOL_BUF - 1] = pin


def _smix_call(layer, u, q3, kn3, vn3, knt, vnt, pin, h0r, h0i, ck, cv, prev_windows, hist_t, lp):
    n = u.shape[0]
    bc = SAMPLE_BC
    rows2 = lambda c: pl.BlockSpec((bc, c), lambda i: (i, 0))
    rows3 = lambda r, c: pl.BlockSpec((bc, r, c), lambda i: (i, 0, 0))
    win_spec = pl.BlockSpec((None, bc, D_KV, WINDOW), lambda i: (layer, i, 0, 0))
    consts = [lp['w_bu'], lp['ab_re'], lp['ab_im'], lp['c_mat'], lp['ssm_d'], lp['ssm_w_glu'], lp['ssm_b_glu']]
    consts2 = [lp['pool_wbd'], lp['pool_scale']]
    hist_spec = pl.BlockSpec((POOL_BUF, bc, D_POOL), lambda i: (0, i, 0))
    in_specs = ([rows2(D_SSM), rows3(GQA, LANES), rows3(1, LANES), rows3(1, LANES),
                 _const_spec(knt.shape), _const_spec(vnt.shape), rows2(D_POOL),
                 rows2(D_STATE), rows2(D_STATE), win_spec, win_spec, hist_spec]
                + [_const_spec(a.shape) for a in consts]
                + [pl.BlockSpec(memory_space=pltpu.SMEM)]
                + [_const_spec(a.shape) for a in consts2]
                + [pl.BlockSpec(memory_space=pl.ANY) for _ in prev_windows])
    assert len(in_specs) == N_SMIX_IN + len(prev_windows)
    aliases = {N_SMIX_IN + k: 5 + k for k in range(len(prev_windows))}
    return pl.pallas_call(
        _smix_kernel,
        out_shape=(jax.ShapeDtypeStruct((n, D_SSM), F32), jax.ShapeDtypeStruct((n, GQA, LANES), F32),
                   jax.ShapeDtypeStruct((n, D_POOL), F32),
                   jax.ShapeDtypeStruct((n, D_STATE), F32), jax.ShapeDtypeStruct((n, D_STATE), F32),
                   jax.ShapeDtypeStruct(ck.shape, F32), jax.ShapeDtypeStruct(cv.shape, F32),
                   jax.ShapeDtypeStruct(hist_t.shape, F32)),
        grid=(n // bc,),
        in_specs=in_specs,
        out_specs=(rows2(D_SSM), rows3(GQA, LANES), rows2(D_POOL), rows2(D_STATE), rows2(D_STATE),
                   win_spec, win_spec, hist_spec),
        input_output_aliases=aliases,
        compiler_params=_params(("arbitrary",)),
        name="sample_mix",
    )(u, q3, kn3, vn3, knt, vnt, pin, h0r, h0i, ck, cv, hist_t, *consts, lp['attn_sinks'], *consts2, *prev_windows)


def _smerge_kernel(x_ref, g_ref, pg_ref, yssm_ref, yatt_ref, ypool_ref, npost_ref,
                   wus_ref, wua_ref, wup_ref, wout_ref, o_ref):
    merged = _sigmoid(pg_ref[:, 0:D_MODEL]) * _dot(yssm_ref[...].astype(BF16), wus_ref[...])
    merged += _sigmoid(pg_ref[:, D_MODEL:2 * D_MODEL]) * _dot(yatt_ref[...].astype(BF16), wua_ref[...])
    merged += _sigmoid(pg_ref[:, 2 * D_MODEL:3 * D_MODEL]) * _dot(ypool_ref[...].astype(BF16), wup_ref[...])
    out = _dot(merged.astype(BF16), wout_ref[...])
    o_ref[...] = x_ref[...] + g_ref[...] * _rms(out, npost_ref[...])


def _smerge_call(xs, mods2, pg, yssm, yatt, ypool, lp):
    n = xs.shape[0]
    ops = [xs, mods2, pg, yssm, yatt, ypool, lp['norm_mix_post'],
           lp['w_up_ssm'], lp['w_up_attn'], lp['w_up_pool'], lp['w_out']]
    in_specs = [_const_spec(a.shape) for a in ops]
    in_specs[1] = pl.BlockSpec((n, D_MODEL), lambda i: (0, 2))
    return pl.pallas_call(
        _smerge_kernel,
        out_shape=jax.ShapeDtypeStruct(xs.shape, F32),
        grid=(1,),
        in_specs=in_specs,
        out_specs=_const_spec(xs.shape),
        compiler_params=_params(("arbitrary",)),
        name="sample_merge",
    )(*ops)


def _block_diag(w):
    g, a, b = w.shape
    eye = jnp.eye(g, dtype=w.dtype)
    return jnp.einsum('gab,gh->gahb', w, eye).reshape(g * a, g * b)


def _pair_heads(w, axis):
    shape = w.shape
    w = w.reshape(shape[:axis] + (N_KV_HEADS, GQA, HEAD_DIM) + shape[axis + 1:])
    return jnp.swapaxes(w, axis, axis + 1).reshape(shape)


def _layer_params(l, p, prep):
    ab_re, ab_im, bb_re, bb_im, pw_re, pw_im = prep
    row = lambda a: a.reshape(1, -1)
    w_in = p['w_in'][l]
    q_cols = _pair_heads(w_in[:, D_SSM:D_SSM + D_ATTN], 1)
    w_in_a = jnp.concatenate([w_in[:, 0:D_SSM], q_cols, w_in[:, D_SSM + D_ATTN:D_SMALL]], axis=1)
    lp = {
        'norm_mix_pre': row(p['norm_mix_pre'][l]), 'norm_mix_post': row(p['norm_mix_post'][l]),
        'norm_ffn_pre': row(p['norm_ffn_pre'][l]), 'norm_ffn_post': row(p['norm_ffn_post'][l]),
        'w_in_a': w_in_a.astype(BF16),
        'w_in_g': w_in[:, D_SMALL:].astype(BF16),
        'w_bu': jnp.concatenate([_block_diag(bb_re[l]), _block_diag(bb_im[l])], axis=1).astype(BF16),
        'ab_re': ab_re[l].reshape(1, D_STATE), 'ab_im': ab_im[l].reshape(1, D_STATE),
        'pw_re': pw_re[l].reshape(SCAN_LEN, D_STATE), 'pw_im': pw_im[l].reshape(SCAN_LEN, D_STATE),
        'c_mat': jnp.concatenate([_block_diag(jnp.swapaxes(p['ssm_c_re'][l], 1, 2)),
                                  _block_diag(jnp.swapaxes(-p['ssm_c_im'][l], 1, 2))], axis=0).astype(BF16),
        'ssm_d': row(p['ssm_d'][l]), 'ssm_w_glu': p['ssm_w_glu'][l].astype(BF16), 'ssm_b_glu': row(p['ssm_b_glu'][l]),
        'attn_sinks': p['attn_sinks'][l],
        'pool_wbd': _block_diag(p['pool_w'][l]).astype(BF16), 'pool_scale': row(p['pool_scale'][l]),
        'w_up_ssm': p['w_up_ssm'][l].astype(BF16), 'w_up_attn': _pair_heads(p['w_up_attn'][l], 0).astype(BF16),
        'w_up_pool': p['w_up_pool'][l].astype(BF16), 'w_out': p['w_out'][l].astype(BF16),
    }
    j = l // 2
    if l % 2 == 1:
        lp.update({'moe_w_router': p['moe_w_router'][j].astype(BF16), 'moe_b_router': row(p['moe_b_router'][j]),
                   'moe_w_gate': p['moe_w_gate'][j:j + 1].astype(BF16), 'moe_w_up': p['moe_w_up'][j:j + 1].astype(BF16),
                   'moe_w_down': p['moe_w_down'][j:j + 1].astype(BF16)})
    else:
        lp.update({'ffn_w_gate': p['ffn_w_gate'][j].astype(BF16), 'ffn_w_up': p['ffn_w_up'][j].astype(BF16),
                   'ffn_w_down': p['ffn_w_down'][j].astype(BF16)})
    return lp


def kernel(x_prompt, x_sample, state_ssm_re, state_ssm_im, cache_win_k, cache_win_v, state_pool,
           c_prompt, c_sample, w_ada, b_ada, norm_mix_pre, norm_mix_post, norm_ffn_pre, norm_ffn_post,
           w_in, ssm_a_re, ssm_a_im, ssm_log_dt, ssm_b_re, ssm_b_im, ssm_c_re, ssm_c_im, ssm_d,
           ssm_w_glu, ssm_b_glu, attn_sinks, pool_w, pool_scale, w_up_ssm, w_up_attn, w_up_pool, w_out,
           ffn_w_gate, ffn_w_up, ffn_w_down, moe_w_router, moe_b_router, moe_w_gate, moe_w_up, moe_w_down):
    p = dict(norm_mix_pre=norm_mix_pre, norm_mix_post=norm_mix_post, norm_ffn_pre=norm_ffn_pre,
             norm_ffn_post=norm_ffn_post, w_in=w_in, ssm_c_re=ssm_c_re, ssm_c_im=ssm_c_im, ssm_d=ssm_d,
             ssm_w_glu=ssm_w_glu, ssm_b_glu=ssm_b_glu, attn_sinks=attn_sinks, pool_w=pool_w, pool_scale=pool_scale,
             w_up_ssm=w_up_ssm, w_up_attn=w_up_attn, w_up_pool=w_up_pool, w_out=w_out,
             ffn_w_gate=ffn_w_gate, ffn_w_up=ffn_w_up, ffn_w_down=ffn_w_down, moe_w_router=moe_w_router,
             moe_b_router=moe_b_router, moe_w_gate=moe_w_gate, moe_w_up=moe_w_up, moe_w_down=moe_w_down)
    depth = w_in.shape[0]
    b, t_len, d = x_prompt.shape
    ns = x_sample.shape[0]

    mods, mods_seq = _ada_call(jnp.concatenate([c_sample, c_prompt], axis=0), b, w_ada, b_ada)
    prep = _ssm_prep_call(ssm_a_re, ssm_a_im, ssm_log_dt, ssm_b_re, ssm_b_im)

    xp = x_prompt
    xs = x_sample.reshape(ns, d)
    p_out = ([], [], [], [], [])
    s_out = ([], [], [], [], [])
    win_minor = lambda c: jnp.transpose(c, (0, 1, 3, 4, 2)).reshape(depth, ns, D_KV, -1)
    win_major = lambda c: jnp.transpose(c.reshape(depth, ns, N_KV_HEADS, HEAD_DIM, -1), (0, 1, 4, 2, 3))
    old_windows = [win_minor(cache_win_k), win_minor(cache_win_v)]
    new_windows = []
    for l in range(depth):
        lp = _layer_params(l, p, prep)
        mods2 = mods[l]
        mods3 = mods_seq[l]

        xp, hre, him, kwin, vwin, plast = _mixer_call(xp, mods3, lp)
        x2 = xp.reshape(b * t_len, d)
        if l % 2 == 1:
            x2 = _moe_routed_call(x2, mods3, lp, t_len)
        else:
            x2 = _ffn_call(x2, mods3, mods2, lp, t_len, False)
        xp = x2.reshape(b, t_len, d)
        p_out[0].append(hre.reshape(b, SSM_GROUPS, SSM_STATE))
        p_out[1].append(him.reshape(b, SSM_GROUPS, SSM_STATE))
        p_out[2].append(kwin.reshape(b, WINDOW, N_KV_HEADS, HEAD_DIM))
        p_out[3].append(vwin.reshape(b, WINDOW, N_KV_HEADS, HEAD_DIM))
        p_out[4].append(plast[:, POOL_HIST - POOL_BUF:, :])

        pa, pg = _sproj_call(xs, mods2, lp)
        u = pa[:, 0:D_SSM]
        q3 = pa[:, D_SSM:D_SSM + D_ATTN].reshape(ns, GQA, LANES)
        k_new = pa[:, D_SSM + D_ATTN:D_SSM + D_ATTN + D_KV]
        v_new = pa[:, D_SSM + D_ATTN + D_KV:D_SSM + D_ATTN + 2 * D_KV]
        pin = pa[:, D_SMALL - D_POOL:]
        yssm, yatt3, ypool, hr, hi, *new_windows, hist_new = _smix_call(
            l, u, q3, k_new.reshape(ns, 1, D_KV), v_new.reshape(ns, 1, D_KV), k_new.T, v_new.T, pin,
            state_ssm_re[l].reshape(ns, D_STATE), state_ssm_im[l].reshape(ns, D_STATE),
            *old_windows, new_windows, jnp.swapaxes(state_pool[l], 0, 1), lp)
        xs = _smerge_call(xs, mods2, pg, yssm, yatt3.reshape(ns, D_ATTN), ypool, lp)
        xs = (_moe_call if l % 2 == 1 else _ffn_call)(xs, mods3, mods2, lp, 1, True)
        s_out[0].append(hr.reshape(ns, SSM_GROUPS, SSM_STATE))
        s_out[1].append(hi.reshape(ns, SSM_GROUPS, SSM_STATE))
        s_out[4].append(jnp.swapaxes(hist_new, 0, 1))

    stack = lambda xs_: [jnp.stack(a, axis=0) for a in xs_]
    p_ssm_re, p_ssm_im, p_win_k, p_win_v, p_pool = stack(p_out)
    s_ssm_re, s_ssm_im, s_pool = stack([s_out[0], s_out[1], s_out[4]])
    s_win_k, s_win_v = [win_major(w) for w in new_windows]
    return (xp, xs.reshape(x_sample.shape), p_ssm_re, p_ssm_im, p_win_k, p_win_v, p_pool,
            s_ssm_re, s_ssm_im, s_win_k, s_win_v, s_pool)
```
